```python
import jax, jax.numpy as jnp
from jax import lax
import numpy as np

D_MODEL = 1024
BATCH = 2
SEQ = 16384
DEPTH = 2

GRID_W = 64
CTX_LEN = 256
HEAD_DIM = 64
ATTN_WIDTH = D_MODEL // 2
RET_WIDTH = D_MODEL // 4
POOL_WIDTH = D_MODEL // 4
ATTN_Q_HEADS = ATTN_WIDTH // HEAD_DIM
ATTN_KV_HEADS = 2
ATTN_GROUP = ATTN_Q_HEADS // ATTN_KV_HEADS
KV_WIDTH = ATTN_KV_HEADS * HEAD_DIM
RET_HEADS = RET_WIDTH // HEAD_DIM
POOL_WINDOWS = (2, 4, 8, 16)
POOL_GROUP = POOL_WIDTH // len(POOL_WINDOWS)
MIX_WIDTH = ATTN_WIDTH + RET_WIDTH + POOL_WIDTH
IN_WIDTH = ATTN_WIDTH + 2 * KV_WIDTH + 4 * RET_WIDTH + POOL_WIDTH
Q_BLOCK = 128
RET_CHUNK = 128
ROPE_THETA = 10000.0
D_FF = ((8 * D_MODEL // 3 + 127) // 128) * 128
N_EXPERTS = 8
TOP_K = 2
N_DENSE = (DEPTH + 1) // 2
N_MOE = DEPTH // 2
NORM_EPS = 1e-6

kernel_name = 'hybrid_attn_retention_pool_moe_dit'


def rms_norm(x, g):
    xf = x.astype(jnp.float32)
    y = xf * lax.rsqrt(jnp.mean(xf * xf, axis=-1, keepdims=True) + NORM_EPS)
    return (y * g.astype(jnp.float32)).astype(x.dtype)


def axial_rope_tables(n_tokens):
    rows = n_tokens // GRID_W
    row_pos = jnp.repeat(jnp.arange(rows, dtype=jnp.float32), GRID_W)
    col_pos = jnp.tile(jnp.arange(GRID_W, dtype=jnp.float32), rows)
    n_freq = HEAD_DIM // 4
    inv = ROPE_THETA ** (-jnp.arange(n_freq, dtype=jnp.float32) / n_freq)
    ang = jnp.stack([row_pos[:, None] * inv, col_pos[:, None] * inv], axis=1)
    return jnp.cos(ang), jnp.sin(ang)


def apply_axial_rope(x, cos, sin):
    xs = x.astype(jnp.float32).reshape(x.shape[:-1] + (2, 2, HEAD_DIM // 4))
    x1 = xs[..., :, 0, :]
    x2 = xs[..., :, 1, :]
    out = jnp.stack([x1 * cos - x2 * sin, x2 * cos + x1 * sin], axis=-2)
    return out.reshape(x.shape).astype(x.dtype)


def _split_projection(p):
    sizes = (ATTN_WIDTH, KV_WIDTH, KV_WIDTH, RET_WIDTH, RET_WIDTH, RET_WIDTH, RET_WIDTH, POOL_WIDTH)
    out, off = [], 0
    for s in sizes:
        out.append(p[..., off:off + s])
        off += s
    return out


def _attn_heads(q, k, v, q_gain, k_gain):
    B, T, _ = q.shape
    q = q.reshape(B, T, ATTN_KV_HEADS, ATTN_GROUP, HEAD_DIM).transpose(0, 2, 3, 1, 4)
    k = k.reshape(B, T, ATTN_KV_HEADS, HEAD_DIM).transpose(0, 2, 1, 3)
    v = v.reshape(B, T, ATTN_KV_HEADS, HEAD_DIM).transpose(0, 2, 1, 3)
    return rms_norm(q, q_gain), rms_norm(k, k_gain), v


def _sdpa(q, k, v):
    s = jnp.einsum('bkgqd,bknd->bkgqn', q, k).astype(jnp.float32) * (HEAD_DIM ** -0.5)
    p = jax.nn.softmax(s, axis=-1).astype(v.dtype)
    return jnp.einsum('bkgqn,bknd->bkgqd', p, v)


def _blocked_attention(q, k, v):
    B, KVH, G, T, d = q.shape
    nb = T // Q_BLOCK
    qb = q.reshape(B, KVH, G, nb, Q_BLOCK, d).transpose(3, 0, 1, 2, 4, 5)
    ob = lax.map(lambda qi: _sdpa(qi, k, v), qb)
    return ob.transpose(1, 2, 3, 0, 4, 5).reshape(B, KVH, G, T, d)


def _merge_attn(o):
    B, KVH, G, T, d = o.shape
    return o.transpose(0, 3, 1, 2, 4).reshape(B, T, ATTN_WIDTH)


def _ret_heads(t):
    B, T, _ = t.shape
    return t.reshape(B, T, RET_HEADS, HEAD_DIM).transpose(0, 2, 1, 3)


def retention_scan(q, k, v, log_gamma, state0):
    B, H, T, d = q.shape
    C = RET_CHUNK
    n = T // C
    qc = q.astype(jnp.float32).reshape(B, H, n, C, d)
    kc = k.astype(jnp.float32).reshape(B, H, n, C, d)
    vc = v.astype(jnp.float32).reshape(B, H, n, C, d)
    pos = jnp.arange(C, dtype=jnp.float32)
    lg = log_gamma.astype(jnp.float32)[:, None]
    diff = pos[:, None] - pos[None, :]
    dmat = jnp.where(diff >= 0, jnp.exp(lg[:, :, None] * jnp.maximum(diff, 0.0)), 0.0)
    scores = jnp.einsum('bhncd,bhnmd->bhncm', qc, kc) * dmat[:, None]
    intra = jnp.einsum('bhncm,bhnme->bhnce', scores, vc)
    zeta = jnp.exp(lg * (C - 1 - pos))
    upd = jnp.einsum('bhncd,hc,bhnce->nbhde', kc, zeta, vc)
    decay_chunk = jnp.exp(lg[:, 0] * C)[None, :, None, None]

    def step(state, u):
        return decay_chunk * state + u, state

    final, prev = lax.scan(step, state0, upd)
    xi = jnp.exp(lg * (pos + 1))
    cross = jnp.einsum('bhncd,hc,nbhde->bhnce', qc, xi, prev)
    return (intra + cross).reshape(B, H, T, d), final


def _bi_retention(q, k, v, lg_fwd, lg_bwd, init_fwd, init_bwd):
    of, sf = retention_scan(q, k, v, lg_fwd, init_fwd)
    fl = lambda t: jnp.flip(t, axis=2)
    ob, sb = retention_scan(fl(q), fl(k), fl(v), lg_bwd, init_bwd)
    return of + fl(ob), sf, sb


def _ret_output(o, g):
    mu = jnp.mean(o, axis=-1, keepdims=True)
    var = jnp.mean(jnp.square(o - mu), axis=-1, keepdims=True)
    y = (o - mu) * lax.rsqrt(var + NORM_EPS)
    B, H, T, d = o.shape
    y = y.transpose(0, 2, 1, 3).reshape(B, T, RET_WIDTH).astype(g.dtype)
    return jax.nn.silu(g) * y


def multiscale_pool(p, w_pool, pool_scale):
    B, T, _ = p.shape
    ng = len(POOL_WINDOWS)
    pg = p.astype(jnp.float32).reshape(B, T, ng, POOL_GROUP)
    cs = jnp.concatenate([jnp.zeros((B, 1, ng, POOL_GROUP), jnp.float32), jnp.cumsum(pg, axis=1)], axis=1)
    t = jnp.arange(T)
    outs = []
    for gi, w in enumerate(POOL_WINDOWS):
        lo = jnp.clip(t - w // 2, 0, T)
        hi = jnp.clip(t + w // 2, 0, T)
        csg = cs[:, :, gi]
        mean = (csg[:, hi] - csg[:, lo]) / (hi - lo).astype(jnp.float32)[None, :, None]
        outs.append(mean - pg[:, :, gi])
    mixed = jnp.stack(outs, axis=2)
    y = jnp.einsum('btgc,gcd->btgd', mixed, w_pool.astype(jnp.float32)).reshape(B, T, POOL_WIDTH)
    return (y * pool_scale.astype(jnp.float32)).astype(p.dtype)


def token_mixers(hx, hc, w_in, w_out, q_gain, k_gain, decay_logit, pool_w, pool_scale, cos, sin, need_ctx):
    B = hx.shape[0]
    aqx, akx, avx, rqx, rkx, rvx, rgx, ppx = _split_projection(hx @ w_in)
    aqc, akc, avc, rqc, rkc, rvc, rgc, ppc = _split_projection(hc @ w_in)

    qx, kx, vx = _attn_heads(aqx, akx, avx, q_gain, k_gain)
    qc, kc, vc = _attn_heads(aqc, akc, avc, q_gain, k_gain)
    qx = apply_axial_rope(qx, cos, sin)
    kx = apply_axial_rope(kx, cos, sin)
    k_all = jnp.concatenate([kc, kx], axis=2)
    v_all = jnp.concatenate([vc, vx], axis=2)
    attn_x = _merge_attn(_blocked_attention(qx, k_all, v_all))

    log_gamma = jax.nn.log_sigmoid(decay_logit.astype(jnp.float32))
    k_scale = HEAD_DIM ** -0.5
    zero = jnp.zeros((B, RET_HEADS, HEAD_DIM, HEAD_DIM), jnp.float32)
    ret_c, s_fwd, s_bwd = _bi_retention(_ret_heads(rqc), _ret_heads(rkc) * k_scale, _ret_heads(rvc),
                                        log_gamma[0], log_gamma[1], zero, zero)
    ret_x, _, _ = _bi_retention(_ret_heads(rqx), _ret_heads(rkx) * k_scale, _ret_heads(rvx),
                                log_gamma[0], log_gamma[1], s_fwd, s_bwd)
    ret_x = _ret_output(ret_x, rgx)

    pool_x = multiscale_pool(ppx, pool_w, pool_scale)

    mx = jnp.concatenate([attn_x.astype(hx.dtype), ret_x.astype(hx.dtype), pool_x], axis=-1) @ w_out
    if not need_ctx:
        return mx, None
    attn_c = _merge_attn(_sdpa(qc, kc, vc))
    ret_c = _ret_output(ret_c, rgc)
    pool_c = multiscale_pool(ppc, pool_w, pool_scale)
    mc = jnp.concatenate([attn_c.astype(hc.dtype), ret_c.astype(hc.dtype), pool_c], axis=-1) @ w_out
    return mx, mc


def swiglu(h, w1, w3, w2):
    return (jax.nn.silu(h @ w1) * (h @ w3)) @ w2


def moe_swiglu(h, router, w1, w3, w2):
    logits = (h @ router).astype(jnp.float32)
    top_vals, top_idx = lax.top_k(logits, TOP_K)
    top_w = jax.nn.softmax(top_vals, axis=-1)
    gates = jnp.sum(jax.nn.one_hot(top_idx, N_EXPERTS, dtype=jnp.float32) * top_w[..., None], axis=-2)
    out = jnp.zeros_like(h)
    for e in range(N_EXPERTS):
        out = out + gates[..., e:e + 1].astype(h.dtype) * swiglu(h, w1[e], w3[e], w2[e])
    return out


def setup_inputs(seed: int = 0) -> dict:
    key = jax.random.key(seed)
    ks = jax.random.split(key, 26)
    nrm = lambda k, shape, s: jax.random.normal(k, shape, jnp.float32) * s
    base_gamma = 1.0 - 2.0 ** (-5.0 - jnp.arange(RET_HEADS, dtype=jnp.float32))
    base_logit = jnp.log(base_gamma) - jnp.log1p(-base_gamma)
    return {
        'x': nrm(ks[0], (BATCH, SEQ, D_MODEL), 1.0),
        'c': nrm(ks[1], (BATCH, D_MODEL), 1.0),
        'ctx': nrm(ks[2], (BATCH, CTX_LEN, D_MODEL), 1.0),
        'c_ctx': nrm(ks[3], (D_MODEL,), 1.0),
        'w_mod': nrm(ks[4], (DEPTH, D_MODEL, 6 * D_MODEL), 0.5 * D_MODEL ** -0.5),
        'b_mod': nrm(ks[5], (DEPTH, 6 * D_MODEL), 0.02),
        'norm_pre_mix': 1.0 + nrm(ks[6], (DEPTH, D_MODEL), 0.05),
        'norm_post_mix': 1.0 + nrm(ks[7], (DEPTH, D_MODEL), 0.05),
        'norm_pre_ffn': 1.0 + nrm(ks[8], (DEPTH, D_MODEL), 0.05),
        'norm_post_ffn': 1.0 + nrm(ks[9], (DEPTH, D_MODEL), 0.05),
        'w_in': nrm(ks[10], (DEPTH, D_MODEL, IN_WIDTH), D_MODEL ** -0.5),
        'w_out': nrm(ks[11], (DEPTH, MIX_WIDTH, D_MODEL), MIX_WIDTH ** -0.5),
        'q_norm': 1.0 + nrm(ks[12], (DEPTH, HEAD_DIM), 0.05),
        'k_norm': 1.0 + nrm(ks[13], (DEPTH, HEAD_DIM), 0.05),
        'ret_decay_logit': jnp.broadcast_to(base_logit, (DEPTH, 2, RET_HEADS)) + nrm(ks[14], (DEPTH, 2, RET_HEADS), 0.1),
        'pool_w': nrm(ks[15], (DEPTH, len(POOL_WINDOWS), POOL_GROUP, POOL_GROUP), POOL_GROUP ** -0.5),
        'pool_scale': 1.0 + nrm(ks[16], (DEPTH, POOL_WIDTH), 0.05),
        'ffn_w1': nrm(ks[17], (N_DENSE, D_MODEL, D_FF), D_MODEL ** -0.5),
        'ffn_w3': nrm(ks[18], (N_DENSE, D_MODEL, D_FF), D_MODEL ** -0.5),
        'ffn_w2': nrm(ks[19], (N_DENSE, D_FF, D_MODEL), D_FF ** -0.5),
        'moe_router': nrm(ks[20], (N_MOE, D_MODEL, N_EXPERTS), D_MODEL ** -0.5),
        'moe_w1': nrm(ks[21], (N_MOE, N_EXPERTS, D_MODEL, D_FF), D_MODEL ** -0.5),
        'moe_w3': nrm(ks[22], (N_MOE, N_EXPERTS, D_MODEL, D_FF), D_MODEL ** -0.5),
        'moe_w2': nrm(ks[23], (N_MOE, N_EXPERTS, D_FF, D_MODEL), D_FF ** -0.5),
    }


def reference(x, c, ctx, c_ctx, w_mod, b_mod, norm_pre_mix, norm_post_mix, norm_pre_ffn, norm_post_ffn,
              w_in, w_out, q_norm, k_norm, ret_decay_logit, pool_w, pool_scale,
              ffn_w1, ffn_w3, ffn_w2, moe_router, moe_w1, moe_w3, moe_w2):
    n_tokens = x.shape[1]
    cos, sin = axial_rope_tables(n_tokens)
    xc = ctx
    silu_c = jax.nn.silu(c)
    silu_cc = jax.nn.silu(c_ctx)
    for i in range(DEPTH):
        need_ctx = i < DEPTH - 1
        mod_x = (silu_c @ w_mod[i] + b_mod[i])[:, None, :]
        mod_c = (silu_cc @ w_mod[i] + b_mod[i])[None, None, :]
        sh1, sc1, g1, sh2, sc2, g2 = jnp.split(mod_x, 6, axis=-1)
        csh1, csc1, cg1, csh2, csc2, cg2 = jnp.split(mod_c, 6, axis=-1)

        hx = rms_norm(x, norm_pre_mix[i]) * (1.0 + sc1) + sh1
        hc = rms_norm(xc, norm_pre_mix[i]) * (1.0 + csc1) + csh1
        mx, mc = token_mixers(hx, hc, w_in[i], w_out[i], q_norm[i], k_norm[i], ret_decay_logit[i],
                              pool_w[i], pool_scale[i], cos, sin, need_ctx)
        x = x + g1 * rms_norm(mx, norm_post_mix[i])
        if need_ctx:
            xc = xc + cg1 * rms_norm(mc, norm_post_mix[i])

        j = i // 2
        if i % 2 == 0:
            ffn = lambda h: swiglu(h, ffn_w1[j], ffn_w3[j], ffn_w2[j])
        else:
            ffn = lambda h: moe_swiglu(h, moe_router[j], moe_w1[j], moe_w3[j], moe_w2[j])
        hx = rms_norm(x, norm_pre_ffn[i]) * (1.0 + sc2) + sh2
        x = x + g2 * rms_norm(ffn(hx), norm_post_ffn[i])
        if need_ctx:
            hc = rms_norm(xc, norm_pre_ffn[i]) * (1.0 + csc2) + csh2
            xc = xc + cg2 * rms_norm(ffn(hc), norm_post_ffn[i])
    return x
```

```python
import functools

import jax
import jax.numpy as jnp
from jax import lax
from jax.experimental import pallas as pl
from jax.experimental.pallas import tpu as pltpu

F32 = jnp.float32
BF16 = jnp.bfloat16
I32 = jnp.int32

D_MODEL = 1024
GRID_W = 64
HEAD_DIM = 64
ATTN_WIDTH = 512
KV_HEADS = 2
ATTN_GROUP = 4
KV_WIDTH = 128
RET_WIDTH = 256
RET_HEADS = 4
POOL_WIDTH = 256
POOL_WINDOWS = (2, 4, 8, 16)
IN_WIDTH = 2048
RET_CHUNK = 128
ROPE_THETA = 10000.0
D_FF = 2816
N_EXPERTS = 8
NORM_EPS = 1e-6
DEPTH = 2

TOKEN_TILE = 512
ATTN_Q_TILE = 256
ATTN_K_CHUNK = 512
POOL_TILE = 256
POOL_HALO = 16
FF_CHUNK = 256
EXPERT_TILE = 512
MOD_ROWS = 8
VMEM_LIMIT = 56 * 1024 * 1024


def _cparams(sem, vmem=None):
    return pltpu.CompilerParams(dimension_semantics=sem, vmem_limit_bytes=vmem)


def _rms(x, gain):
    ms = jnp.mean(x * x, axis=-1, keepdims=True)
    return x * lax.rsqrt(ms + NORM_EPS) * gain


def _silu(x):
    return x * jax.nn.sigmoid(x)


def _mod_kernel(c_ref, w_ref, b_ref, o_ref):
    s = _silu(c_ref[...])
    o_ref[...] = jnp.dot(s, w_ref[...], precision=lax.Precision.HIGHEST,
                         preferred_element_type=F32) + b_ref[...]


def _modulation(cvec, w_mod, b_mod):
    nchunk = 6
    out = pl.pallas_call(
        _mod_kernel,
        out_shape=jax.ShapeDtypeStruct((DEPTH, MOD_ROWS, 6 * D_MODEL), F32),
        grid=(DEPTH, nchunk),
        in_specs=[
            pl.BlockSpec((MOD_ROWS, D_MODEL), lambda l, j: (0, 0)),
            pl.BlockSpec((None, D_MODEL, D_MODEL), lambda l, j: (l, 0, j)),
            pl.BlockSpec((None, 1, D_MODEL), lambda l, j: (l, 0, j)),
        ],
        out_specs=pl.BlockSpec((None, MOD_ROWS, D_MODEL), lambda l, j: (l, 0, j)),
        compiler_params=_cparams(("parallel", "parallel")),
        name="modulation",
    )(cvec, w_mod, b_mod.reshape(DEPTH, 1, 6 * D_MODEL))
    return out.reshape(DEPTH * MOD_ROWS * nchunk, 1, D_MODEL)


def _mod_spec(layer, chunk, ctx):
    base = layer * MOD_ROWS * 6
    if ctx:
        return pl.BlockSpec((None, 1, D_MODEL), lambda b, i: (base + 2 * 6 + chunk, 0, 0))
    return pl.BlockSpec((None, 1, D_MODEL), lambda b, i: (base + b * 6 + chunk, 0, 0))


def _row_spec(width):
    return pl.BlockSpec((1, width), lambda b, i: (0, 0))


def _inproj_kernel(x_ref, g_ref, sc_ref, sh_ref, w_ref, qkv_ref, ret_ref, pp_ref):
    h = _rms(x_ref[...], g_ref[...]) * (1.0 + sc_ref[...]) + sh_ref[...]
    p = jnp.dot(h.astype(BF16), w_ref[...], preferred_element_type=F32)
    qkv_ref[...] = p[:, :768].astype(BF16)
    ret_ref[...] = p[:, 768:1792].astype(BF16)
    pp_ref[...] = p[:, 1792:].astype(BF16)


def _in_projection(x, modr, gain, w_in_bf16, layer, ctx):
    B, T, _ = x.shape
    tm = min(TOKEN_TILE, T)
    tok = lambda w: pl.BlockSpec((None, tm, w), lambda b, i: (b, i, 0))
    return pl.pallas_call(
        _inproj_kernel,
        out_shape=(jax.ShapeDtypeStruct((B, T, 768), BF16),
                   jax.ShapeDtypeStruct((B, T, 1024), BF16),
                   jax.ShapeDtypeStruct((B, T, POOL_WIDTH), BF16)),
        grid=(B, T // tm),
        in_specs=[tok(D_MODEL), _row_spec(D_MODEL), _mod_spec(layer, 1, ctx), _mod_spec(layer, 0, ctx),
                  pl.BlockSpec((D_MODEL, IN_WIDTH), lambda b, i: (0, 0))],
        out_specs=(tok(768), tok(1024), tok(POOL_WIDTH)),
        compiler_params=_cparams(("parallel", "parallel"), VMEM_LIMIT),
        name="in_projection",
    )(x, gain.reshape(1, D_MODEL), modr, modr, w_in_bf16)


def _prep_kernel(qkv_ref, qg_ref, kg_ref, cos_ref, sin_ref, qt_ref, kn_ref, vt_ref, *, rope):
    t = qkv_ref[...].astype(F32).T

    def norm_rope(blk, gain):
        ms = jnp.mean(blk * blk, axis=0, keepdims=True)
        y = blk * lax.rsqrt(ms + NORM_EPS) * gain
        if rope:
            partner = jnp.concatenate([y[16:32], y[0:16], y[48:64], y[32:48]], axis=0)
            y = y * cos_ref[...] + partner * sin_ref[...]
        return y

    for h in range(ATTN_WIDTH // HEAD_DIM):
        lo = h * HEAD_DIM
        qt_ref[lo:lo + HEAD_DIM, :] = (norm_rope(t[lo:lo + HEAD_DIM], qg_ref[...])
                                       * (HEAD_DIM ** -0.5)).astype(BF16)
    kt = jnp.concatenate(
        [norm_rope(t[ATTN_WIDTH + kv * HEAD_DIM:ATTN_WIDTH + (kv + 1) * HEAD_DIM], kg_ref[...])
         for kv in range(KV_HEADS)], axis=0)
    k = kt.T
    for kv in range(KV_HEADS):
        kn_ref[kv] = k[:, kv * HEAD_DIM:(kv + 1) * HEAD_DIM].astype(BF16)
        lo = ATTN_WIDTH + KV_WIDTH + kv * HEAD_DIM
        vt_ref[kv, 0] = t[lo:lo + HEAD_DIM].astype(BF16)


def _attn_prep(qkv, q_gain, k_gain, cos_t, sin_t, rope):
    B, T, _ = qkv.shape
    tm = min(ATTN_K_CHUNK, T)
    nc = T // tm
    return pl.pallas_call(
        functools.partial(_prep_kernel, rope=rope),
        out_shape=(jax.ShapeDtypeStruct((B, ATTN_WIDTH, T), BF16),
                   jax.ShapeDtypeStruct((B, KV_HEADS, T, HEAD_DIM), BF16),
                   jax.ShapeDtypeStruct((B, KV_HEADS, nc, HEAD_DIM, tm), BF16)),
        grid=(B, nc),
        in_specs=[pl.BlockSpec((None, tm, 768), lambda b, i: (b, i, 0)),
                  pl.BlockSpec((HEAD_DIM, 1), lambda b, i: (0, 0)),
                  pl.BlockSpec((HEAD_DIM, 1), lambda b, i: (0, 0)),
                  pl.BlockSpec((HEAD_DIM, tm), lambda b, i: (0, i)),
                  pl.BlockSpec((HEAD_DIM, tm), lambda b, i: (0, i))],
        out_specs=(pl.BlockSpec((None, ATTN_WIDTH, tm), lambda b, i: (b, 0, i)),
                   pl.BlockSpec((None, KV_HEADS, tm, HEAD_DIM), lambda b, i: (b, 0, i, 0)),
                   pl.BlockSpec((None, KV_HEADS, 1, HEAD_DIM, tm), lambda b, i: (b, 0, i, 0, 0))),
        compiler_params=_cparams(("parallel", "parallel")),
        name="attn_prep",
    )(qkv, q_gain.reshape(HEAD_DIM, 1), k_gain.reshape(HEAD_DIM, 1), cos_t, sin_t)


def _rope_tables(T):
    t = jnp.arange(T)
    row = (t // GRID_W).astype(F32)
    col = (t % GRID_W).astype(F32)
    n_freq = HEAD_DIM // 4
    inv = ROPE_THETA ** (-jnp.arange(n_freq, dtype=F32) / n_freq)
    ang_r = row[None, :] * inv[:, None]
    ang_c = col[None, :] * inv[:, None]
    cos_t = jnp.concatenate([jnp.cos(ang_r), jnp.cos(ang_r), jnp.cos(ang_c), jnp.cos(ang_c)], axis=0)
    sin_t = jnp.concatenate([-jnp.sin(ang_r), jnp.sin(ang_r), -jnp.sin(ang_c), jnp.sin(ang_c)], axis=0)
    return cos_t, sin_t


def _attn_kernel(*refs, n_x_chunks, tk):
    if n_x_chunks:
        qt_ref, kc_ref, vtc_ref, kx_ref, vtx_ref, o_ref = refs
    else:
        qt_ref, kc_ref, vtc_ref, o_ref = refs
    qs = [qt_ref[g * HEAD_DIM:(g + 1) * HEAD_DIM, :] for g in range(ATTN_GROUP)]

    def first(q):
        s = jnp.dot(kc_ref[...], q, preferred_element_type=F32)
        m = jnp.max(s, axis=0, keepdims=True)
        p = jnp.exp(s - m)
        l = jnp.sum(p, axis=0, keepdims=True)
        acc = jnp.dot(vtc_ref[0], p.astype(BF16), preferred_element_type=F32)
        return m, l, acc

    state = tuple(first(q) for q in qs)

    if n_x_chunks:
        def body(j, state):
            k = kx_ref[pl.ds(pl.multiple_of(j * tk, tk), tk), :]
            vt = vtx_ref[j]
            new = []
            for q, (m, l, acc) in zip(qs, state):
                s = jnp.dot(k, q, preferred_element_type=F32)
                m_new = jnp.maximum(m, jnp.max(s, axis=0, keepdims=True))
                alpha = jnp.exp(m - m_new)
                p = jnp.exp(s - m_new)
                l = alpha * l + jnp.sum(p, axis=0, keepdims=True)
                acc = alpha * acc + jnp.dot(vt, p.astype(BF16), preferred_element_type=F32)
                new.append((m_new, l, acc))
            return tuple(new)

        state = lax.fori_loop(0, n_x_chunks, body, state)

    o = jnp.concatenate([acc / l for (_, l, acc) in state], axis=0)
    o_ref[...] = o.T.astype(BF16)


def _attention(qt, kc, vtc, kx=None, vtx=None):
    B, _, Tq = qt.shape
    Tc = kc.shape[2]
    tq = min(ATTN_Q_TILE, Tq)
    in_specs = [pl.BlockSpec((None, ATTN_GROUP * HEAD_DIM, tq), lambda b, h, i: (b, h, i)),
                pl.BlockSpec((None, None, Tc, HEAD_DIM), lambda b, h, i: (b, h, 0, 0)),
                pl.BlockSpec((None, None, 1, HEAD_DIM, Tc), lambda b, h, i: (b, h, 0, 0, 0))]
    args = [qt, kc, vtc]
    n_x_chunks, tk = 0, 0
    if kx is not None:
        _, _, n_x_chunks, _, tk = vtx.shape
        T = kx.shape[2]
        in_specs += [pl.BlockSpec((None, None, T, HEAD_DIM), lambda b, h, i: (b, h, 0, 0)),
                     pl.BlockSpec((None, None, n_x_chunks, HEAD_DIM, tk), lambda b, h, i: (b, h, 0, 0, 0))]
        args += [kx, vtx]
    return pl.pallas_call(
        functools.partial(_attn_kernel, n_x_chunks=n_x_chunks, tk=tk),
        out_shape=jax.ShapeDtypeStruct((B, Tq, ATTN_WIDTH), BF16),
        grid=(B, KV_HEADS, Tq // tq),
        in_specs=in_specs,
        out_specs=pl.BlockSpec((None, tq, ATTN_GROUP * HEAD_DIM), lambda b, h, i: (b, i, h)),
        compiler_params=_cparams(("parallel", "parallel", "parallel"), VMEM_LIMIT),
        name="attention",
    )(*args)


def _ret_kernel(lgc_f_ref, lgc_b_ref, lgr_f_ref, lgr_b_ref, s0f_ref, s0b_ref, blk_f_ref, blk_b_ref,
                of_ref, ob_ref, sf_ref, sb_ref, st_f, st_b, dec_f, dec_b):
    C = RET_CHUNK
    W = RET_WIDTH
    n = pl.program_id(1)

    @pl.when(n == 0)
    def _():
        st_f[...] = s0f_ref[...]
        st_b[...] = s0b_ref[...]
        c = jnp.bitwise_and(lax.broadcasted_iota(I32, (RET_HEADS * C, C), 0), C - 1)
        m = lax.broadcasted_iota(I32, (RET_HEADS * C, C), 1)
        diff = (c - m).astype(F32)
        dec_f[...] = jnp.where(diff >= 0, jnp.exp(lgc_f_ref[...] * jnp.maximum(diff, 0.0)), 0.0)
        dec_b[...] = jnp.where(diff <= 0, jnp.exp(lgc_b_ref[...] * jnp.maximum(-diff, 0.0)), 0.0)

    lane_head = jnp.right_shift(lax.broadcasted_iota(I32, (C, W), 1), 6)
    pos = lax.broadcasted_iota(I32, (C, W), 0).astype(F32)
    same_head = (jnp.right_shift(lax.broadcasted_iota(I32, (W, W), 0), 6)
                 == jnp.right_shift(lax.broadcasted_iota(I32, (W, W), 1), 6))

    def direction(blk_ref, dec_ref, st_ref, lgr, forward, out_ref):
        q = blk_ref[:, 0:W].astype(F32)
        kf = blk_ref[:, W:2 * W].astype(F32) * (HEAD_DIM ** -0.5)
        v = blk_ref[:, 2 * W:3 * W]
        if forward:
            zeta = jnp.exp(lgr * (C - 1.0 - pos))
            xi = jnp.exp(lgr * (pos + 1.0))
        else:
            zeta = jnp.exp(lgr * pos)
            xi = jnp.exp(lgr * (C - pos))
        qexp = jnp.concatenate([jnp.where(lane_head == h, q, 0.0) for h in range(RET_HEADS)],
                               axis=0).astype(BF16)
        a = lax.dot_general(qexp, kf.astype(BF16), (((1,), (1,)), ((), ())),
                            preferred_element_type=F32)
        p = (a * dec_ref[...]).astype(BF16)
        full = jnp.dot(p, v, preferred_element_type=F32)
        intra = jnp.zeros((C, W), F32)
        for h in range(RET_HEADS):
            intra = intra + jnp.where(lane_head == h, full[h * C:(h + 1) * C], 0.0)
        state = st_ref[...]
        cross = jnp.dot((q * xi).astype(BF16), state.astype(BF16), preferred_element_type=F32)
        out_ref[...] = intra + cross
        upd = lax.dot_general((kf * zeta).astype(BF16), v, (((0,), (0,)), ((), ())),
                              preferred_element_type=F32)
        st_ref[...] = jnp.where(same_head, state * jnp.exp(lgr * float(C)) + upd, 0.0)

    direction(blk_f_ref, dec_f, st_f, lgr_f_ref[...], True, of_ref)
    direction(blk_b_ref, dec_b, st_b, lgr_b_ref[...], False, ob_ref)

    @pl.when(n == pl.num_programs(1) - 1)
    def _():
        sf_ref[...] = st_f[...]
        sb_ref[...] = st_b[...]


def _retention(ret, log_gamma, s0f, s0b):
    B, T, _ = ret.shape
    C = RET_CHUNK
    nc = T // C
    lgc = lambda d: jnp.repeat(log_gamma[d], C).reshape(RET_HEADS * C, 1)
    lgr = lambda d: jnp.repeat(log_gamma[d], HEAD_DIM).reshape(1, RET_WIDTH)
    const = lambda shape: pl.BlockSpec(shape, lambda b, n: (0,) * len(shape))
    st_spec = pl.BlockSpec((None, RET_WIDTH, RET_WIDTH), lambda b, n: (b, 0, 0))
    return pl.pallas_call(
        _ret_kernel,
        out_shape=(jax.ShapeDtypeStruct((B, T, RET_WIDTH), F32),
                   jax.ShapeDtypeStruct((B, T, RET_WIDTH), F32),
                   jax.ShapeDtypeStruct((B, RET_WIDTH, RET_WIDTH), F32),
                   jax.ShapeDtypeStruct((B, RET_WIDTH, RET_WIDTH), F32)),
        grid=(B, nc),
        in_specs=[const((RET_HEADS * C, 1)), const((RET_HEADS * C, 1)),
                  const((1, RET_WIDTH)), const((1, RET_WIDTH)), st_spec, st_spec,
                  pl.BlockSpec((None, C, 1024), lambda b, n: (b, n, 0)),
                  pl.BlockSpec((None, C, 1024), lambda b, n: (b, nc - 1 - n, 0))],
        out_specs=(pl.BlockSpec((None, C, RET_WIDTH), lambda b, n: (b, n, 0)),
                   pl.BlockSpec((None, C, RET_WIDTH), lambda b, n: (b, nc - 1 - n, 0)),
                   st_spec, st_spec),
        scratch_shapes=[pltpu.VMEM((RET_WIDTH, RET_WIDTH), F32), pltpu.VMEM((RET_WIDTH, RET_WIDTH), F32),
                        pltpu.VMEM((RET_HEADS * C, C), F32), pltpu.VMEM((RET_HEADS * C, C), F32)],
        compiler_params=_cparams(("parallel", "arbitrary")),
        name="retention",
    )(lgc(0), lgc(1), lgr(0), lgr(1), s0f, s0b, ret, ret)


def _pool_kernel(prev_ref, cur_ref, next_ref, w_ref, scale_ref, o_ref, *, seq_len):
    tm = cur_ref.shape[0]
    ext = jnp.concatenate([prev_ref[...], cur_ref[...], next_ref[...]], axis=0)
    t0 = pl.program_id(1) * tm
    tok = t0 + lax.broadcasted_iota(I32, (tm, tm + 2 * POOL_HALO), 0)
    src = t0 - POOL_HALO + lax.broadcasted_iota(I32, (tm, tm + 2 * POOL_HALO), 1)
    tcol = t0 + lax.broadcasted_iota(I32, (tm, 1), 0)
    lane_group = jnp.right_shift(lax.broadcasted_iota(I32, (tm, POOL_WIDTH), 1), 6)
    cur = cur_ref[...].astype(F32)
    mixed = jnp.zeros((tm, POOL_WIDTH), F32)
    for gi, w in enumerate(POOL_WINDOWS):
        lo = jnp.maximum(tok - w // 2, 0)
        hi = jnp.minimum(tok + w // 2, seq_len)
        band = jnp.where((src >= lo) & (src < hi), 1.0, 0.0).astype(BF16)
        total = jnp.dot(band, ext, preferred_element_type=F32)
        cnt = (jnp.minimum(tcol + w // 2, seq_len) - jnp.maximum(tcol - w // 2, 0)).astype(F32)
        mixed = mixed + jnp.where(lane_group == gi, total / cnt - cur, 0.0)
    y = jnp.dot(mixed.astype(BF16), w_ref[...], preferred_element_type=F32)
    o_ref[...] = (y * scale_ref[...]).astype(BF16)


def _pooling(pp, w_blockdiag_bf16, scale):
    B, T, _ = pp.shape
    tm = min(POOL_TILE, T)
    r = tm // POOL_HALO
    last = T // POOL_HALO - 1
    return pl.pallas_call(
        functools.partial(_pool_kernel, seq_len=T),
        out_shape=jax.ShapeDtypeStruct((B, T, POOL_WIDTH), BF16),
        grid=(B, T // tm),
        in_specs=[pl.BlockSpec((None, POOL_HALO, POOL_WIDTH), lambda b, i: (b, jnp.maximum(i * r - 1, 0), 0)),
                  pl.BlockSpec((None, tm, POOL_WIDTH), lambda b, i: (b, i, 0)),
                  pl.BlockSpec((None, POOL_HALO, POOL_WIDTH), lambda b, i: (b, jnp.minimum((i + 1) * r, last), 0)),
                  pl.BlockSpec((POOL_WIDTH, POOL_WIDTH), lambda b, i: (0, 0)),
                  _row_spec(POOL_WIDTH)],
        out_specs=pl.BlockSpec((None, tm, POOL_WIDTH), lambda b, i: (b, i, 0)),
        compiler_params=_cparams(("parallel", "parallel")),
        name="pooling",
    )(pp, pp, pp, w_blockdiag_bf16, scale.reshape(1, POOL_WIDTH))


def _head_mean(x, avg):
    hi = x.astype(BF16)
    lo = (x - hi.astype(F32)).astype(BF16)
    return (jnp.dot(hi, avg, preferred_element_type=F32) + jnp.dot(lo, avg, preferred_element_type=F32))


def _mixout_kernel(x_ref, attn_ref, of_ref, ob_ref, gate_ref, pool_ref, avg_ref, w_ref, npost_ref, g1_ref, o_ref):
    o = of_ref[...] + ob_ref[...]
    avg = avg_ref[...]
    mu = _head_mean(o, avg)
    cen = o - mu
    var = _head_mean(cen * cen, avg)
    y_ret = (_silu(gate_ref[...].astype(F32)) * (cen * lax.rsqrt(var + NORM_EPS))).astype(BF16)
    mx = (jnp.dot(attn_ref[...], w_ref[0:ATTN_WIDTH, :], preferred_element_type=F32)
          + jnp.dot(y_ret, w_ref[ATTN_WIDTH:ATTN_WIDTH + RET_WIDTH, :], preferred_element_type=F32)
          + jnp.dot(pool_ref[...], w_ref[ATTN_WIDTH + RET_WIDTH:, :], preferred_element_type=F32))
    o_ref[...] = x_ref[...] + g1_ref[...] * _rms(mx, npost_ref[...])


def _mix_out(x, attn, of, ob, ret, pool, avg_bf16, w_out_bf16, npost, modr, layer, ctx):
    B, T, _ = x.shape
    tm = min(TOKEN_TILE, T)
    tok = lambda w: pl.BlockSpec((None, tm, w), lambda b, i: (b, i, 0))
    return pl.pallas_call(
        _mixout_kernel,
        out_shape=jax.ShapeDtypeStruct((B, T, D_MODEL), F32),
        grid=(B, T // tm),
        in_specs=[tok(D_MODEL), tok(ATTN_WIDTH), tok(RET_WIDTH), tok(RET_WIDTH),
                  pl.BlockSpec((None, tm, RET_WIDTH), lambda b, i: (b, i, 3)),
                  tok(POOL_WIDTH),
                  pl.BlockSpec((RET_WIDTH, RET_WIDTH), lambda b, i: (0, 0)),
                  pl.BlockSpec((D_MODEL, D_MODEL), lambda b, i: (0, 0)),
                  _row_spec(D_MODEL), _mod_spec(layer, 2, ctx)],
        out_specs=tok(D_MODEL),
        compiler_params=_cparams(("parallel", "parallel"), VMEM_LIMIT),
        name="mix_out",
    )(x, attn, of, ob, ret, pool, avg_bf16, w_out_bf16, npost.reshape(1, D_MODEL), modr)


def _swiglu_tile(h, w1_ref, w3_ref, w2_ref):
    acc = jnp.zeros((h.shape[0], D_MODEL), F32)
    for f in range(0, D_FF, FF_CHUNK):
        a = jnp.dot(h, w1_ref[:, f:f + FF_CHUNK], preferred_element_type=F32)
        b = jnp.dot(h, w3_ref[:, f:f + FF_CHUNK], preferred_element_type=F32)
        u = (_silu(a) * b).astype(BF16)
        acc = acc + jnp.dot(u, w2_ref[f:f + FF_CHUNK, :], preferred_element_type=F32)
    return acc


def _ffn_kernel(x_ref, gpre_ref, sc_ref, sh_ref, w1_ref, w3_ref, w2_ref, npost_ref, g2_ref, o_ref):
    x = x_ref[...]
    h = (_rms(x, gpre_ref[...]) * (1.0 + sc_ref[...]) + sh_ref[...]).astype(BF16)
    y = _swiglu_tile(h, w1_ref, w3_ref, w2_ref)
    o_ref[...] = x + g2_ref[...] * _rms(y, npost_ref[...])


def _dense_ffn(x, gpre, npost, w1, w3, w2, modr, layer, ctx):
    B, T, _ = x.shape
    tm = min(TOKEN_TILE, T)
    tok = pl.BlockSpec((None, tm, D_MODEL), lambda b, i: (b, i, 0))
    wspec = lambda shape: pl.BlockSpec(shape, lambda b, i: (0, 0), pipeline_mode=pl.Buffered(1))
    return pl.pallas_call(
        _ffn_kernel,
        out_shape=jax.ShapeDtypeStruct((B, T, D_MODEL), F32),
        grid=(B, T // tm),
        in_specs=[tok, _row_spec(D_MODEL), _mod_spec(layer, 4, ctx), _mod_spec(layer, 3, ctx),
                  wspec((D_MODEL, D_FF)), wspec((D_MODEL, D_FF)), wspec((D_FF, D_MODEL)),
                  _row_spec(D_MODEL), _mod_spec(layer, 5, ctx)],
        out_specs=tok,
        compiler_params=_cparams(("parallel", "parallel"), VMEM_LIMIT),
        name="dense_ffn",
    )(x, gpre.reshape(1, D_MODEL), modr, modr, w1, w3, w2, npost.reshape(1, D_MODEL), modr)


def _route_kernel(x_ref, gpre_ref, sc_ref, sh_ref, rt_ref, h_ref, idx_ref, gate_ref, rank_ref, cnt_ref, run_ref):
    tm = x_ref.shape[0]
    first = (pl.program_id(0) == 0) & (pl.program_id(1) == 0)

    @pl.when(first)
    def _():
        run_ref[...] = jnp.zeros_like(run_ref)

    h = _rms(x_ref[...], gpre_ref[...]) * (1.0 + sc_ref[...]) + sh_ref[...]
    h_ref[...] = h
    logits = lax.dot_general(rt_ref[...], h, (((1,), (1,)), ((), ())), precision=lax.Precision.HIGHEST,
                             preferred_element_type=F32)
    eid = lax.broadcasted_iota(I32, (N_EXPERTS, tm), 0).astype(F32)
    m1 = jnp.max(logits, axis=0, keepdims=True)
    i1 = jnp.min(jnp.where(logits == m1, eid, float(N_EXPERTS)), axis=0, keepdims=True)
    oh1 = eid == i1
    rest = jnp.where(oh1, -jnp.inf, logits)
    m2 = jnp.max(rest, axis=0, keepdims=True)
    i2 = jnp.min(jnp.where(rest == m2, eid, float(N_EXPERTS)), axis=0, keepdims=True)
    oh2 = eid == i2
    e2 = jnp.exp(m2 - m1)
    gate_ref[0:1, :] = 1.0 / (1.0 + e2)
    gate_ref[1:2, :] = e2 / (1.0 + e2)
    idx_ref[0:1, :] = i1.astype(I32)
    idx_ref[1:2, :] = i2.astype(I32)
    upper = (lax.broadcasted_iota(I32, (tm, tm), 0) < lax.broadcasted_iota(I32, (tm, tm), 1))
    upper = jnp.where(upper, 1.0, 0.0).astype(BF16)
    f1 = jnp.where(oh1, 1.0, 0.0)
    f2 = jnp.where(oh2, 1.0, 0.0)
    before1 = jnp.dot(f1.astype(BF16), upper, preferred_element_type=F32)
    before2 = jnp.dot(f2.astype(BF16), upper, preferred_element_type=F32)
    cnt1 = jnp.sum(f1, axis=1, keepdims=True)
    cnt2 = jnp.sum(f2, axis=1, keepdims=True)
    run = run_ref[:, 0:1]
    rank_ref[0:1, :] = jnp.sum(f1 * (run + before1), axis=0, keepdims=True).astype(I32)
    rank_ref[1:2, :] = jnp.sum(f2 * (run + cnt1 + before2), axis=0, keepdims=True).astype(I32)
    run_new = run_ref[...] + cnt1 + cnt2
    run_ref[...] = run_new
    cnt_ref[...] = run_new


def _route(x, gpre, router_t, modr, layer):
    B, T, _ = x.shape
    tm = min(TOKEN_TILE, T)
    nt = T // tm
    tok = pl.BlockSpec((None, tm, D_MODEL), lambda b, i: (b, i, 0))
    lane = pl.BlockSpec((2, tm), lambda b, i: (0, b * nt + i))
    return pl.pallas_call(
        _route_kernel,
        out_shape=(jax.ShapeDtypeStruct((B, T, D_MODEL), F32),
                   jax.ShapeDtypeStruct((2, B * T), I32),
                   jax.ShapeDtypeStruct((2, B * T), F32),
                   jax.ShapeDtypeStruct((2, B * T), I32),
                   jax.ShapeDtypeStruct((N_EXPERTS, 128), F32)),
        grid=(B, nt),
        in_specs=[tok, _row_spec(D_MODEL), _mod_spec(layer, 4, False), _mod_spec(layer, 3, False),
                  pl.BlockSpec((N_EXPERTS, D_MODEL), lambda b, i: (0, 0))],
        out_specs=(tok, lane, lane, lane, pl.BlockSpec((N_EXPERTS, 128), lambda b, i: (0, 0))),
        scratch_shapes=[pltpu.VMEM((N_EXPERTS, 128), F32)],
        compiler_params=_cparams(("arbitrary", "arbitrary")),
        name="moe_route",
    )(x, gpre.reshape(1, D_MODEL), modr, modr, router_t)


def _dispatch_kernel(slot_ref, h_ref, hs_in_ref, hs_ref, sem):
    del hs_in_ref
    tm = h_ref.shape[0]

    def body(r, carry):
        for k in range(2):
            s = slot_ref[0, k, r]
            pltpu.make_async_copy(h_ref.at[pl.ds(r, 1)], hs_ref.at[pl.ds(s, 1)], sem).start()
        return carry

    lax.fori_loop(0, tm, body, 0)
    for _ in range(2):
        pltpu.make_async_copy(h_ref, hs_ref.at[pl.ds(0, tm)], sem).wait()


def _dispatch(h, slot_tiles, n_slots):
    N = h.shape[0]
    tm = slot_tiles.shape[2]
    return pl.pallas_call(
        _dispatch_kernel,
        out_shape=jax.ShapeDtypeStruct((n_slots, D_MODEL), F32),
        grid=(N // tm,),
        in_specs=[pl.BlockSpec((1, 2, tm), lambda i: (i, 0, 0), memory_space=pltpu.SMEM),
                  pl.BlockSpec((tm, D_MODEL), lambda i: (i, 0)),
                  pl.BlockSpec(memory_space=pl.ANY)],
        out_specs=pl.BlockSpec(memory_space=pl.ANY),
        scratch_shapes=[pltpu.SemaphoreType.DMA(())],
        input_output_aliases={2: 0},
        compiler_params=_cparams(("arbitrary",)),
        name="moe_dispatch",
    )(slot_tiles, h, jnp.zeros((n_slots, D_MODEL), F32))


def _expert_kernel(te_ref, tv_ref, h_ref, w1_ref, w3_ref, w2_ref, y_ref):
    t = pl.program_id(0)

    @pl.when(tv_ref[t] == 1)
    def _():
        y_ref[...] = _swiglu_tile(h_ref[...].astype(BF16), w1_ref, w3_ref, w2_ref)

    @pl.when(tv_ref[t] == 0)
    def _():
        y_ref[...] = jnp.zeros_like(y_ref)


def _experts(hs, tile_expert, tile_valid, w1, w3, w2):
    n_slots = hs.shape[0]
    tm = EXPERT_TILE
    grid_spec = pltpu.PrefetchScalarGridSpec(
        num_scalar_prefetch=2,
        grid=(n_slots // tm,),
        in_specs=[pl.BlockSpec((tm, D_MODEL), lambda t, te, tv: (t, 0)),
                  pl.BlockSpec((None, D_MODEL, D_FF), lambda t, te, tv: (te[t], 0, 0)),
                  pl.BlockSpec((None, D_MODEL, D_FF), lambda t, te, tv: (te[t], 0, 0)),
                  pl.BlockSpec((None, D_FF, D_MODEL), lambda t, te, tv: (te[t], 0, 0))],
        out_specs=pl.BlockSpec((tm, D_MODEL), lambda t, te, tv: (t, 0)),
    )
    return pl.pallas_call(
        _expert_kernel,
        out_shape=jax.ShapeDtypeStruct((n_slots, D_MODEL), F32),
        grid_spec=grid_spec,
        compiler_params=_cparams(("arbitrary",), VMEM_LIMIT),
        name="moe_experts",
    )(tile_expert, tile_valid, hs, w1, w3, w2)


def _combine_kernel(slot_ref, gate_ref, x_ref, npost_ref, g2_ref, ys_ref, o_ref, buf0, buf1, sem):
    tm = x_ref.shape[0]
    bufs = (buf0, buf1)

    def body(r, carry):
        for k in range(2):
            s = slot_ref[0, k, r]
            pltpu.make_async_copy(ys_ref.at[pl.ds(s, 1)], bufs[k].at[pl.ds(r, 1)], sem).start()
        return carry

    lax.fori_loop(0, tm, body, 0)
    for k in range(2):
        pltpu.make_async_copy(ys_ref.at[pl.ds(0, tm)], bufs[k], sem).wait()
    y = gate_ref[:, 0:1] * buf0[...] + gate_ref[:, 1:2] * buf1[...]
    o_ref[...] = x_ref[...] + g2_ref[...] * _rms(y, npost_ref[...])


def _combine(x, ys, slot_tiles, gate_tok, npost, modr, layer):
    B, T, _ = x.shape
    tm = slot_tiles.shape[2]
    nt = T // tm
    tok = pl.BlockSpec((None, tm, D_MODEL), lambda b, i: (b, i, 0))
    return pl.pallas_call(
        _combine_kernel,
        out_shape=jax.ShapeDtypeStruct((B, T, D_MODEL), F32),
        grid=(B, nt),
        in_specs=[pl.BlockSpec((1, 2, tm), lambda b, i: (b * nt + i, 0, 0), memory_space=pltpu.SMEM),
                  pl.BlockSpec((tm, 2), lambda b, i: (b * nt + i, 0)),
                  tok, _row_spec(D_MODEL), _mod_spec(layer, 5, False),
                  pl.BlockSpec(memory_space=pl.ANY)],
        out_specs=tok,
        scratch_shapes=[pltpu.VMEM((tm, D_MODEL), F32), pltpu.VMEM((tm, D_MODEL), F32),
                        pltpu.SemaphoreType.DMA(())],
        compiler_params=_cparams(("arbitrary", "arbitrary")),
        name="moe_combine",
    )(slot_tiles, gate_tok, x, npost.reshape(1, D_MODEL), modr, ys)


def _moe_ffn(x, gpre, npost, router, w1, w3, w2, modr, layer):
    B, T, _ = x.shape
    N = B * T
    tm = min(TOKEN_TILE, T)
    h, idx, gate, rank, cnt = _route(x, gpre, router.T, modr, layer)
    n_slots = 2 * N + N_EXPERTS * EXPERT_TILE
    n_tiles = n_slots // EXPERT_TILE
    counts = cnt[:, 0].astype(I32)
    padded = ((counts + EXPERT_TILE - 1) // EXPERT_TILE) * EXPERT_TILE
    ends = jnp.cumsum(padded)
    starts = ends - padded
    slot = starts[idx] + rank
    slot_tiles = slot.reshape(2, N // tm, tm).transpose(1, 0, 2)
    tile_start = jnp.arange(n_tiles, dtype=I32) * EXPERT_TILE
    tile_valid = (tile_start < ends[-1]).astype(I32)
    tile_expert = jnp.minimum(jnp.searchsorted(ends, tile_start, side="right"), N_EXPERTS - 1).astype(I32)
    last_expert = jnp.max(jnp.where(tile_valid == 1, tile_expert, 0))
    tile_expert = jnp.where(tile_valid == 1, tile_expert, last_expert)
    hs = _dispatch(h.reshape(N, D_MODEL), slot_tiles, n_slots)
    ys = _experts(hs, tile_expert, tile_valid, w1, w3, w2)
    return _combine(x, ys, slot_tiles, gate.T, npost, modr, layer)


def kernel(x, c, ctx, c_ctx, w_mod, b_mod, norm_pre_mix, norm_post_mix, norm_pre_ffn, norm_post_ffn, w_in, w_out, q_norm, k_norm, ret_decay_logit, pool_w, pool_scale, ffn_w1, ffn_w3, ffn_w2, moe_router, moe_w1, moe_w3, moe_w2):
    B, T, _ = x.shape
    cvec = jnp.zeros((MOD_ROWS, D_MODEL), F32).at[0:B].set(c).at[2].set(c_ctx)
    modr = _modulation(cvec, w_mod, b_mod)
    cos_t, sin_t = _rope_tables(T)
    avg = jnp.kron(jnp.eye(RET_HEADS, dtype=F32), jnp.full((HEAD_DIM, HEAD_DIM), 1.0 / HEAD_DIM, F32)).astype(BF16)
    zero_state = jnp.zeros((B, RET_WIDTH, RET_WIDTH), F32)
    xc = ctx
    for i in range(DEPTH):
        need_ctx = i < DEPTH - 1
        w_in_b = w_in[i].astype(BF16)
        w_out_b = w_out[i].astype(BF16)
        pool_bd = jax.scipy.linalg.block_diag(*[pool_w[i, g] for g in range(len(POOL_WINDOWS))]).astype(BF16)
        log_gamma = jax.nn.log_sigmoid(ret_decay_logit[i].astype(F32))

        qkv_c, ret_c, pp_c = _in_projection(xc, modr, norm_pre_mix[i], w_in_b, i, True)
        qt_c, kn_c, vt_c = _attn_prep(qkv_c, q_norm[i], k_norm[i], cos_t, sin_t, False)
        of_c, ob_c, s_fwd, s_bwd = _retention(ret_c, log_gamma, zero_state, zero_state)

        qkv_x, ret_x, pp_x = _in_projection(x, modr, norm_pre_mix[i], w_in_b, i, False)
        qt_x, kn_x, vt_x = _attn_prep(qkv_x, q_norm[i], k_norm[i], cos_t, sin_t, True)
        attn_x = _attention(qt_x, kn_c, vt_c, kn_x, vt_x)
        of_x, ob_x, _, _ = _retention(ret_x, log_gamma, s_fwd, s_bwd)
        pool_x = _pooling(pp_x, pool_bd, pool_scale[i])
        x = _mix_out(x, attn_x, of_x, ob_x, ret_x, pool_x, avg, w_out_b, norm_post_mix[i], modr, i, False)

        if need_ctx:
            attn_c = _attention(qt_c, kn_c, vt_c)
            pool_c = _pooling(pp_c, pool_bd, pool_scale[i])
            xc = _mix_out(xc, attn_c, of_c, ob_c, ret_c, pool_c, avg, w_out_b, norm_post_mix[i], modr, i, True)

        j = i // 2
        if i % 2 == 0:
            w1, w3, w2 = ffn_w1[j].astype(BF16), ffn_w3[j].astype(BF16), ffn_w2[j].astype(BF16)
            x = _dense_ffn(x, norm_pre_ffn[i], norm_post_ffn[i], w1, w3, w2, modr, i, False)
            if need_ctx:
                xc = _dense_ffn(xc, norm_pre_ffn[i], norm_post_ffn[i], w1, w3, w2, modr, i, True)
        else:
            x = _moe_ffn(x, norm_pre_ffn[i], norm_post_ffn[i], moe_router[j], moe_w1[j].astype(BF16),
                         moe_w3[j].astype(BF16), moe_w2[j].astype(BF16), modr, i)
    return x
```

```python
import functools

import jax
import jax.numpy as jnp
from jax import lax
from jax.experimental import pallas as pl
from jax.experimental.pallas import tpu as pltpu

F32 = jnp.float32
BF16 = jnp.bfloat16
I32 = jnp.int32

D_MODEL = 1024
GRID_W = 64
HEAD_DIM = 64
ATTN_WIDTH = 512
KV_HEADS = 2
ATTN_GROUP = 4
KV_WIDTH = 128
RET_WIDTH = 256
RET_HEADS = 4
POOL_WIDTH = 256
POOL_WINDOWS = (2, 4, 8, 16)
IN_WIDTH = 2048
RET_CHUNK = 128
ROPE_THETA = 10000.0
D_FF = 2816
N_EXPERTS = 8
NORM_EPS = 1e-6
DEPTH = 2

TOKEN_TILE = 512
ATTN_Q_TILE = 256
ATTN_K_CHUNK = 2048
POOL_TILE = 256
POOL_HALO = 16
FF_CHUNK = 256
EXPERT_TILE = 512
K_COLS = 128
VT_ROWS = 80
MOD_ROWS = 8
VMEM_LIMIT = 56 * 1024 * 1024


def _cparams(sem, vmem=None):
    return pltpu.CompilerParams(dimension_semantics=sem, vmem_limit_bytes=vmem)


def _rms(x, gain):
    ms = jnp.mean(x * x, axis=-1, keepdims=True)
    return x * lax.rsqrt(ms + NORM_EPS) * gain


def _silu(x):
    return x * jax.nn.sigmoid(x)


def _mod_kernel(c_ref, w_ref, b_ref, o_ref):
    s = _silu(c_ref[...])
    o_ref[...] = jnp.dot(s, w_ref[...], precision=lax.Precision.HIGHEST,
                         preferred_element_type=F32) + b_ref[...]


def _modulation(cvec, w_mod, b_mod):
    nchunk = 6
    out = pl.pallas_call(
        _mod_kernel,
        out_shape=jax.ShapeDtypeStruct((DEPTH, MOD_ROWS, 6 * D_MODEL), F32),
        grid=(DEPTH, nchunk),
        in_specs=[
            pl.BlockSpec((MOD_ROWS, D_MODEL), lambda l, j: (0, 0)),
            pl.BlockSpec((None, D_MODEL, D_MODEL), lambda l, j: (l, 0, j)),
            pl.BlockSpec((None, 1, D_MODEL), lambda l, j: (l, 0, j)),
        ],
        out_specs=pl.BlockSpec((None, MOD_ROWS, D_MODEL), lambda l, j: (l, 0, j)),
        compiler_params=_cparams(("parallel", "parallel")),
        name="modulation",
    )(cvec, w_mod, b_mod.reshape(DEPTH, 1, 6 * D_MODEL))
    return out.reshape(DEPTH * MOD_ROWS * nchunk, 1, D_MODEL)


def _mod_spec(layer, chunk, ctx):
    base = layer * MOD_ROWS * 6
    if ctx:
        return pl.BlockSpec((None, 1, D_MODEL), lambda b, i: (base + 2 * 6 + chunk, 0, 0))
    return pl.BlockSpec((None, 1, D_MODEL), lambda b, i: (base + b * 6 + chunk, 0, 0))


def _row_spec(width):
    return pl.BlockSpec((1, width), lambda b, i: (0, 0))


def _inproj_kernel(x_ref, g_ref, sc_ref, sh_ref, w_ref, qkv_ref, ret_ref, pp_ref):
    h = _rms(x_ref[...], g_ref[...]) * (1.0 + sc_ref[...]) + sh_ref[...]
    p = jnp.dot(h.astype(BF16), w_ref[...], preferred_element_type=F32)
    qkv_ref[...] = p[:, :768].astype(BF16)
    ret_ref[...] = p[:, 768:1792].astype(BF16)
    pp_ref[...] = p[:, 1792:].astype(BF16)


def _in_projection(x, modr, gain, w_in_bf16, layer, ctx):
    B, T, _ = x.shape
    tm = min(TOKEN_TILE, T)
    tok = lambda w: pl.BlockSpec((None, tm, w), lambda b, i: (b, i, 0))
    return pl.pallas_call(
        _inproj_kernel,
        out_shape=(jax.ShapeDtypeStruct((B, T, 768), BF16),
                   jax.ShapeDtypeStruct((B, T, 1024), BF16),
                   jax.ShapeDtypeStruct((B, T, POOL_WIDTH), BF16)),
        grid=(B, T // tm),
        in_specs=[tok(D_MODEL), _row_spec(D_MODEL), _mod_spec(layer, 1, ctx), _mod_spec(layer, 0, ctx),
                  pl.BlockSpec((D_MODEL, IN_WIDTH), lambda b, i: (0, 0))],
        out_specs=(tok(768), tok(1024), tok(POOL_WIDTH)),
        compiler_params=_cparams(("parallel", "parallel"), VMEM_LIMIT),
        name="in_projection",
    )(x, gain.reshape(1, D_MODEL), modr, modr, w_in_bf16)


def _prep_kernel(qkv_ref, qg_ref, kg_ref, cos_ref, sin_ref, qt_ref, kn_ref, vt_ref, *, rope):
    t = qkv_ref[...].astype(F32).T

    def norm_rope(blk, gain):
        ms = jnp.mean(blk * blk, axis=0, keepdims=True)
        y = blk * lax.rsqrt(ms + NORM_EPS) * gain
        if rope:
            partner = jnp.concatenate([y[16:32], y[0:16], y[48:64], y[32:48]], axis=0)
            y = y * cos_ref[...] + partner * sin_ref[...]
        return y

    for h in range(ATTN_WIDTH // HEAD_DIM):
        lo = h * HEAD_DIM
        qt_ref[lo:lo + HEAD_DIM, :] = (norm_rope(t[lo:lo + HEAD_DIM], qg_ref[...])
                                       * (HEAD_DIM ** -0.5)).astype(BF16)
    tm = t.shape[1]
    k_ones = jnp.where(lax.broadcasted_iota(I32, (HEAD_DIM, tm), 0) < 2, 1.0, 0.0)
    v_ones = jnp.where(lax.broadcasted_iota(I32, (VT_ROWS - HEAD_DIM, tm), 0) < 1, 1.0, 0.0)
    for kv in range(KV_HEADS):
        lo = ATTN_WIDTH + kv * HEAD_DIM
        kt = jnp.concatenate([norm_rope(t[lo:lo + HEAD_DIM], kg_ref[...]), k_ones], axis=0)
        kn_ref[kv] = kt.T.astype(BF16)
        lo = ATTN_WIDTH + KV_WIDTH + kv * HEAD_DIM
        vt_ref[kv, 0] = jnp.concatenate([t[lo:lo + HEAD_DIM], v_ones], axis=0).astype(BF16)


def _attn_prep(qkv, q_gain, k_gain, cos_t, sin_t, rope):
    B, T, _ = qkv.shape
    tm = min(ATTN_K_CHUNK, T)
    nc = T // tm
    return pl.pallas_call(
        functools.partial(_prep_kernel, rope=rope),
        out_shape=(jax.ShapeDtypeStruct((B, ATTN_WIDTH, T), BF16),
                   jax.ShapeDtypeStruct((B, KV_HEADS, T, K_COLS), BF16),
                   jax.ShapeDtypeStruct((B, KV_HEADS, nc, VT_ROWS, tm), BF16)),
        grid=(B, nc),
        in_specs=[pl.BlockSpec((None, tm, 768), lambda b, i: (b, i, 0)),
                  pl.BlockSpec((HEAD_DIM, 1), lambda b, i: (0, 0)),
                  pl.BlockSpec((HEAD_DIM, 1), lambda b, i: (0, 0)),
                  pl.BlockSpec((HEAD_DIM, tm), lambda b, i: (0, i)),
                  pl.BlockSpec((HEAD_DIM, tm), lambda b, i: (0, i))],
        out_specs=(pl.BlockSpec((None, ATTN_WIDTH, tm), lambda b, i: (b, 0, i)),
                   pl.BlockSpec((None, KV_HEADS, tm, K_COLS), lambda b, i: (b, 0, i, 0)),
                   pl.BlockSpec((None, KV_HEADS, 1, VT_ROWS, tm), lambda b, i: (b, 0, i, 0, 0))),
        compiler_params=_cparams(("parallel", "parallel")),
        name="attn_prep",
    )(qkv, q_gain.reshape(HEAD_DIM, 1), k_gain.reshape(HEAD_DIM, 1), cos_t, sin_t)


def _rope_tables(T):
    t = jnp.arange(T)
    row = (t // GRID_W).astype(F32)
    col = (t % GRID_W).astype(F32)
    n_freq = HEAD_DIM // 4
    inv = ROPE_THETA ** (-jnp.arange(n_freq, dtype=F32) / n_freq)
    ang_r = row[None, :] * inv[:, None]
    ang_c = col[None, :] * inv[:, None]
    cos_t = jnp.concatenate([jnp.cos(ang_r), jnp.cos(ang_r), jnp.cos(ang_c), jnp.cos(ang_c)], axis=0)
    sin_t = jnp.concatenate([-jnp.sin(ang_r), jnp.sin(ang_r), -jnp.sin(ang_c), jnp.sin(ang_c)], axis=0)
    return cos_t, sin_t


def _attn_kernel(*refs, n_x_chunks, tk):
    if n_x_chunks:
        qt_ref, kc_ref, vtc_ref, kx_ref, vtx_ref, o_ref, qa_ref, acc_ref = refs
    else:
        qt_ref, kc_ref, vtc_ref, o_ref, qa_ref, acc_ref = refs
    tq = qt_ref.shape[1]
    nq = ATTN_GROUP * tq

    for g in range(ATTN_GROUP):
        qa_ref[0:HEAD_DIM, g * tq:(g + 1) * tq] = qt_ref[g * HEAD_DIM:(g + 1) * HEAD_DIM, :]
    qa_ref[HEAD_DIM:K_COLS, :] = jnp.zeros((K_COLS - HEAD_DIM, nq), BF16)

    def scores(k):
        return jnp.dot(k, qa_ref[...], preferred_element_type=F32)

    def colmax8(s):
        return jnp.max(s.reshape(s.shape[0] // 8, 8, nq), axis=0)

    kc = kc_ref[...]
    mx = colmax8(scores(kc))
    if n_x_chunks:
        def max_body(j, mx):
            k = kx_ref[pl.ds(pl.multiple_of(j * tk, tk), tk), :]
            return jnp.maximum(mx, colmax8(scores(k)))

        mx = lax.fori_loop(0, n_x_chunks, max_body, mx)

    m = jnp.max(mx, axis=0, keepdims=True)
    m_hi = m.astype(BF16).astype(F32)
    m_lo = m - m_hi
    row = lax.broadcasted_iota(I32, (16, nq), 0)
    qa_ref[HEAD_DIM:HEAD_DIM + 16, :] = jnp.where(row == 0, -m_hi, jnp.where(row == 1, -m_lo, 0.0)).astype(BF16)

    def weighted(k, vt):
        p = jnp.exp(scores(k)).astype(BF16)
        return jnp.dot(vt, p, preferred_element_type=F32)

    acc_ref[...] = weighted(kc, vtc_ref[0])
    if n_x_chunks:
        def sum_body(j, carry):
            k = kx_ref[pl.ds(pl.multiple_of(j * tk, tk), tk), :]
            acc_ref[...] += weighted(k, vtx_ref[j])
            return carry

        lax.fori_loop(0, n_x_chunks, sum_body, 0)

    o = acc_ref[0:HEAD_DIM, :] / acc_ref[HEAD_DIM:HEAD_DIM + 1, :]
    o = jnp.concatenate([o[:, g * tq:(g + 1) * tq] for g in range(ATTN_GROUP)], axis=0)
    o_ref[...] = o.T.astype(BF16)


def _attention(qt, kc, vtc, kx=None, vtx=None):
    B, _, Tq = qt.shape
    Tc = kc.shape[2]
    tq = min(ATTN_Q_TILE, Tq)
    in_specs = [pl.BlockSpec((None, ATTN_GROUP * HEAD_DIM, tq), lambda b, h, i: (b, h, i)),
                pl.BlockSpec((None, None, Tc, K_COLS), lambda b, h, i: (b, h, 0, 0)),
                pl.BlockSpec((None, None, 1, VT_ROWS, Tc), lambda b, h, i: (b, h, 0, 0, 0))]
    args = [qt, kc, vtc]
    n_x_chunks, tk = 0, 0
    if kx is not None:
        _, _, n_x_chunks, _, tk = vtx.shape
        T = kx.shape[2]
        in_specs += [pl.BlockSpec((None, None, T, K_COLS), lambda b, h, i: (b, h, 0, 0)),
                     pl.BlockSpec((None, None, n_x_chunks, VT_ROWS, tk), lambda b, h, i: (b, h, 0, 0, 0))]
        args += [kx, vtx]
    return pl.pallas_call(
        functools.partial(_attn_kernel, n_x_chunks=n_x_chunks, tk=tk),
        out_shape=jax.ShapeDtypeStruct((B, Tq, ATTN_WIDTH), BF16),
        grid=(B, KV_HEADS, Tq // tq),
        in_specs=in_specs,
        out_specs=pl.BlockSpec((None, tq, ATTN_GROUP * HEAD_DIM), lambda b, h, i: (b, i, h)),
        scratch_shapes=[pltpu.VMEM((K_COLS, ATTN_GROUP * tq), BF16), pltpu.VMEM((VT_ROWS, ATTN_GROUP * tq), F32)],
        compiler_params=_cparams(("parallel", "parallel", "parallel"), VMEM_LIMIT),
        name="attention",
    )(*args)


def _ret_kernel(lgc_f_ref, lgc_b_ref, lgr_f_ref, lgr_b_ref, s0f_ref, s0b_ref, blk_f_ref, blk_b_ref,
                of_ref, ob_ref, sf_ref, sb_ref, st_f, st_b, dec_f, dec_b):
    C = RET_CHUNK
    W = RET_WIDTH
    n = pl.program_id(1)

    @pl.when(n == 0)
    def _():
        st_f[...] = s0f_ref[...]
        st_b[...] = s0b_ref[...]
        c = jnp.bitwise_and(lax.broadcasted_iota(I32, (RET_HEADS * C, C), 0), C - 1)
        m = lax.broadcasted_iota(I32, (RET_HEADS * C, C), 1)
        diff = (c - m).astype(F32)
        dec_f[...] = jnp.where(diff >= 0, jnp.exp(lgc_f_ref[...] * jnp.maximum(diff, 0.0)), 0.0)
        dec_b[...] = jnp.where(diff <= 0, jnp.exp(lgc_b_ref[...] * jnp.maximum(-diff, 0.0)), 0.0)

    lane_head = jnp.right_shift(lax.broadcasted_iota(I32, (C, W), 1), 6)
    pos = lax.broadcasted_iota(I32, (C, W), 0).astype(F32)
    same_head = (jnp.right_shift(lax.broadcasted_iota(I32, (W, W), 0), 6)
                 == jnp.right_shift(lax.broadcasted_iota(I32, (W, W), 1), 6))

    def direction(blk_ref, dec_ref, st_ref, lgr, forward, out_ref):
        q = blk_ref[:, 0:W].astype(F32)
        kf = blk_ref[:, W:2 * W].astype(F32) * (HEAD_DIM ** -0.5)
        v = blk_ref[:, 2 * W:3 * W]
        if forward:
            zeta = jnp.exp(lgr * (C - 1.0 - pos))
            xi = jnp.exp(lgr * (pos + 1.0))
        else:
            zeta = jnp.exp(lgr * pos)
            xi = jnp.exp(lgr * (C - pos))
        qexp = jnp.concatenate([jnp.where(lane_head == h, q, 0.0) for h in range(RET_HEADS)],
                               axis=0).astype(BF16)
        a = lax.dot_general(qexp, kf.astype(BF16), (((1,), (1,)), ((), ())),
                            preferred_element_type=F32)
        p = (a * dec_ref[...]).astype(BF16)
        full = jnp.dot(p, v, preferred_element_type=F32)
        intra = jnp.zeros((C, W), F32)
        for h in range(RET_HEADS):
            intra = intra + jnp.where(lane_head == h, full[h * C:(h + 1) * C], 0.0)
        state = st_ref[...]
        cross = jnp.dot((q * xi).astype(BF16), state.astype(BF16), preferred_element_type=F32)
        out_ref[...] = intra + cross
        upd = lax.dot_general((kf * zeta).astype(BF16), v, (((0,), (0,)), ((), ())),
                              preferred_element_type=F32)
        st_ref[...] = jnp.where(same_head, state * jnp.exp(lgr * float(C)) + upd, 0.0)

    direction(blk_f_ref, dec_f, st_f, lgr_f_ref[...], True, of_ref)
    direction(blk_b_ref, dec_b, st_b, lgr_b_ref[...], False, ob_ref)

    @pl.when(n == pl.num_programs(1) - 1)
    def _():
        sf_ref[...] = st_f[...]
        sb_ref[...] = st_b[...]


def _retention(ret, log_gamma, s0f, s0b):
    B, T, _ = ret.shape
    C = RET_CHUNK
    nc = T // C
    lgc = lambda d: jnp.repeat(log_gamma[d], C).reshape(RET_HEADS * C, 1)
    lgr = lambda d: jnp.repeat(log_gamma[d], HEAD_DIM).reshape(1, RET_WIDTH)
    const = lambda shape: pl.BlockSpec(shape, lambda b, n: (0,) * len(shape))
    st_spec = pl.BlockSpec((None, RET_WIDTH, RET_WIDTH), lambda b, n: (b, 0, 0))
    return pl.pallas_call(
        _ret_kernel,
        out_shape=(jax.ShapeDtypeStruct((B, T, RET_WIDTH), F32),
                   jax.ShapeDtypeStruct((B, T, RET_WIDTH), F32),
                   jax.ShapeDtypeStruct((B, RET_WIDTH, RET_WIDTH), F32),
                   jax.ShapeDtypeStruct((B, RET_WIDTH, RET_WIDTH), F32)),
        grid=(B, nc),
        in_specs=[const((RET_HEADS * C, 1)), const((RET_HEADS * C, 1)),
                  const((1, RET_WIDTH)), const((1, RET_WIDTH)), st_spec, st_spec,
                  pl.BlockSpec((None, C, 1024), lambda b, n: (b, n, 0)),
                  pl.BlockSpec((None, C, 1024), lambda b, n: (b, nc - 1 - n, 0))],
        out_specs=(pl.BlockSpec((None, C, RET_WIDTH), lambda b, n: (b, n, 0)),
                   pl.BlockSpec((None, C, RET_WIDTH), lambda b, n: (b, nc - 1 - n, 0)),
                   st_spec, st_spec),
        scratch_shapes=[pltpu.VMEM((RET_WIDTH, RET_WIDTH), F32), pltpu.VMEM((RET_WIDTH, RET_WIDTH), F32),
                        pltpu.VMEM((RET_HEADS * C, C), F32), pltpu.VMEM((RET_HEADS * C, C), F32)],
        compiler_params=_cparams(("parallel", "arbitrary")),
        name="retention",
    )(lgc(0), lgc(1), lgr(0), lgr(1), s0f, s0b, ret, ret)


def _pool_kernel(prev_ref, cur_ref, next_ref, w_ref, scale_ref, o_ref, *, seq_len):
    tm = cur_ref.shape[0]
    ext = jnp.concatenate([prev_ref[...], cur_ref[...], next_ref[...]], axis=0)
    t0 = pl.program_id(1) * tm
    tok = t0 + lax.broadcasted_iota(I32, (tm, tm + 2 * POOL_HALO), 0)
    src = t0 - POOL_HALO + lax.broadcasted_iota(I32, (tm, tm + 2 * POOL_HALO), 1)
    tcol = t0 + lax.broadcasted_iota(I32, (tm, 1), 0)
    lane_group = jnp.right_shift(lax.broadcasted_iota(I32, (tm, POOL_WIDTH), 1), 6)
    cur = cur_ref[...].astype(F32)
    mixed = jnp.zeros((tm, POOL_WIDTH), F32)
    for gi, w in enumerate(POOL_WINDOWS):
        lo = jnp.maximum(tok - w // 2, 0)
        hi = jnp.minimum(tok + w // 2, seq_len)
        band = jnp.where((src >= lo) & (src < hi), 1.0, 0.0).astype(BF16)
        total = jnp.dot(band, ext, preferred_element_type=F32)
        cnt = (jnp.minimum(tcol + w // 2, seq_len) - jnp.maximum(tcol - w // 2, 0)).astype(F32)
        mixed = mixed + jnp.where(lane_group == gi, total / cnt - cur, 0.0)
    y = jnp.dot(mixed.astype(BF16), w_ref[...], preferred_element_type=F32)
    o_ref[...] = (y * scale_ref[...]).astype(BF16)


def _pooling(pp, w_blockdiag_bf16, scale):
    B, T, _ = pp.shape
    tm = min(POOL_TILE, T)
    r = tm // POOL_HALO
    last = T // POOL_HALO - 1
    return pl.pallas_call(
        functools.partial(_pool_kernel, seq_len=T),
        out_shape=jax.ShapeDtypeStruct((B, T, POOL_WIDTH), BF16),
        grid=(B, T // tm),
        in_specs=[pl.BlockSpec((None, POOL_HALO, POOL_WIDTH), lambda b, i: (b, jnp.maximum(i * r - 1, 0), 0)),
                  pl.BlockSpec((None, tm, POOL_WIDTH), lambda b, i: (b, i, 0)),
                  pl.BlockSpec((None, POOL_HALO, POOL_WIDTH), lambda b, i: (b, jnp.minimum((i + 1) * r, last), 0)),
                  pl.BlockSpec((POOL_WIDTH, POOL_WIDTH), lambda b, i: (0, 0)),
                  _row_spec(POOL_WIDTH)],
        out_specs=pl.BlockSpec((None, tm, POOL_WIDTH), lambda b, i: (b, i, 0)),
        compiler_params=_cparams(("parallel", "parallel")),
        name="pooling",
    )(pp, pp, pp, w_blockdiag_bf16, scale.reshape(1, POOL_WIDTH))


def _head_mean(x, avg):
    hi = x.astype(BF16)
    lo = (x - hi.astype(F32)).astype(BF16)
    return (jnp.dot(hi, avg, preferred_element_type=F32) + jnp.dot(lo, avg, preferred_element_type=F32))


def _mixout_kernel(x_ref, attn_ref, of_ref, ob_ref, gate_ref, pool_ref, avg_ref, w_ref, npost_ref, g1_ref, o_ref):
    o = of_ref[...] + ob_ref[...]
    avg = avg_ref[...]
    mu = _head_mean(o, avg)
    cen = o - mu
    var = _head_mean(cen * cen, avg)
    y_ret = (_silu(gate_ref[...].astype(F32)) * (cen * lax.rsqrt(var + NORM_EPS))).astype(BF16)
    mx = (jnp.dot(attn_ref[...], w_ref[0:ATTN_WIDTH, :], preferred_element_type=F32)
          + jnp.dot(y_ret, w_ref[ATTN_WIDTH:ATTN_WIDTH + RET_WIDTH, :], preferred_element_type=F32)
          + jnp.dot(pool_ref[...], w_ref[ATTN_WIDTH + RET_WIDTH:, :], preferred_element_type=F32))
    o_ref[...] = x_ref[...] + g1_ref[...] * _rms(mx, npost_ref[...])


def _mix_out(x, attn, of, ob, ret, pool, avg_bf16, w_out_bf16, npost, modr, layer, ctx):
    B, T, _ = x.shape
    tm = min(TOKEN_TILE, T)
    tok = lambda w: pl.BlockSpec((None, tm, w), lambda b, i: (b, i, 0))
    return pl.pallas_call(
        _mixout_kernel,
        out_shape=jax.ShapeDtypeStruct((B, T, D_MODEL), F32),
        grid=(B, T // tm),
        in_specs=[tok(D_MODEL), tok(ATTN_WIDTH), tok(RET_WIDTH), tok(RET_WIDTH),
                  pl.BlockSpec((None, tm, RET_WIDTH), lambda b, i: (b, i, 3)),
                  tok(POOL_WIDTH),
                  pl.BlockSpec((RET_WIDTH, RET_WIDTH), lambda b, i: (0, 0)),
                  pl.BlockSpec((D_MODEL, D_MODEL), lambda b, i: (0, 0)),
                  _row_spec(D_MODEL), _mod_spec(layer, 2, ctx)],
        out_specs=tok(D_MODEL),
        compiler_params=_cparams(("parallel", "parallel"), VMEM_LIMIT),
        name="mix_out",
    )(x, attn, of, ob, ret, pool, avg_bf16, w_out_bf16, npost.reshape(1, D_MODEL), modr)


def _swiglu_tile(h, w1_ref, w3_ref, w2_ref):
    acc = jnp.zeros((h.shape[0], D_MODEL), F32)
    for f in range(0, D_FF, FF_CHUNK):
        a = jnp.dot(h, w1_ref[:, f:f + FF_CHUNK], preferred_element_type=F32)
        b = jnp.dot(h, w3_ref[:, f:f + FF_CHUNK], preferred_element_type=F32)
        u = (_silu(a) * b).astype(BF16)
        acc = acc + jnp.dot(u, w2_ref[f:f + FF_CHUNK, :], preferred_element_type=F32)
    return acc


def _ffn_kernel(x_ref, gpre_ref, sc_ref, sh_ref, w1_ref, w3_ref, w2_ref, npost_ref, g2_ref, o_ref):
    x = x_ref[...]
    h = (_rms(x, gpre_ref[...]) * (1.0 + sc_ref[...]) + sh_ref[...]).astype(BF16)
    y = _swiglu_tile(h, w1_ref, w3_ref, w2_ref)
    o_ref[...] = x + g2_ref[...] * _rms(y, npost_ref[...])


def _dense_ffn(x, gpre, npost, w1, w3, w2, modr, layer, ctx):
    B, T, _ = x.shape
    tm = min(TOKEN_TILE, T)
    tok = pl.BlockSpec((None, tm, D_MODEL), lambda b, i: (b, i, 0))
    wspec = lambda shape: pl.BlockSpec(shape, lambda b, i: (0, 0), pipeline_mode=pl.Buffered(1))
    return pl.pallas_call(
        _ffn_kernel,
        out_shape=jax.ShapeDtypeStruct((B, T, D_MODEL), F32),
        grid=(B, T // tm),
        in_specs=[tok, _row_spec(D_MODEL), _mod_spec(layer, 4, ctx), _mod_spec(layer, 3, ctx),
                  wspec((D_MODEL, D_FF)), wspec((D_MODEL, D_FF)), wspec((D_FF, D_MODEL)),
                  _row_spec(D_MODEL), _mod_spec(layer, 5, ctx)],
        out_specs=tok,
        compiler_params=_cparams(("parallel", "parallel"), VMEM_LIMIT),
        name="dense_ffn",
    )(x, gpre.reshape(1, D_MODEL), modr, modr, w1, w3, w2, npost.reshape(1, D_MODEL), modr)


def _route_kernel(x_ref, gpre_ref, sc_ref, sh_ref, rt_ref, h_ref, idx_ref, gate_ref, rank_ref, cnt_ref, run_ref):
    tm = x_ref.shape[0]
    first = (pl.program_id(0) == 0) & (pl.program_id(1) == 0)

    @pl.when(first)
    def _():
        run_ref[...] = jnp.zeros_like(run_ref)

    h = _rms(x_ref[...], gpre_ref[...]) * (1.0 + sc_ref[...]) + sh_ref[...]
    h_ref[...] = h
    logits = lax.dot_general(rt_ref[...], h, (((1,), (1,)), ((), ())), precision=lax.Precision.HIGHEST,
                             preferred_element_type=F32)
    eid = lax.broadcasted_iota(I32, (N_EXPERTS, tm), 0).astype(F32)
    m1 = jnp.max(logits, axis=0, keepdims=True)
    i1 = jnp.min(jnp.where(logits == m1, eid, float(N_EXPERTS)), axis=0, keepdims=True)
    oh1 = eid == i1
    rest = jnp.where(oh1, -jnp.inf, logits)
    m2 = jnp.max(rest, axis=0, keepdims=True)
    i2 = jnp.min(jnp.where(rest == m2, eid, float(N_EXPERTS)), axis=0, keepdims=True)
    oh2 = eid == i2
    e2 = jnp.exp(m2 - m1)
    gate_ref[0:1, :] = 1.0 / (1.0 + e2)
    gate_ref[1:2, :] = e2 / (1.0 + e2)
    idx_ref[0:1, :] = i1.astype(I32)
    idx_ref[1:2, :] = i2.astype(I32)
    upper = (lax.broadcasted_iota(I32, (tm, tm), 0) < lax.broadcasted_iota(I32, (tm, tm), 1))
    upper = jnp.where(upper, 1.0, 0.0).astype(BF16)
    f1 = jnp.where(oh1, 1.0, 0.0)
    f2 = jnp.where(oh2, 1.0, 0.0)
    before1 = jnp.dot(f1.astype(BF16), upper, preferred_element_type=F32)
    before2 = jnp.dot(f2.astype(BF16), upper, preferred_element_type=F32)
    cnt1 = jnp.sum(f1, axis=1, keepdims=True)
    cnt2 = jnp.sum(f2, axis=1, keepdims=True)
    run = run_ref[:, 0:1]
    rank_ref[0:1, :] = jnp.sum(f1 * (run + before1), axis=0, keepdims=True).astype(I32)
    rank_ref[1:2, :] = jnp.sum(f2 * (run + cnt1 + before2), axis=0, keepdims=True).astype(I32)
    run_new = run_ref[...] + cnt1 + cnt2
    run_ref[...] = run_new
    cnt_ref[...] = run_new


def _route(x, gpre, router_t, modr, layer):
    B, T, _ = x.shape
    tm = min(TOKEN_TILE, T)
    nt = T // tm
    tok = pl.BlockSpec((None, tm, D_MODEL), lambda b, i: (b, i, 0))
    lane = pl.BlockSpec((2, tm), lambda b, i: (0, b * nt + i))
    return pl.pallas_call(
        _route_kernel,
        out_shape=(jax.ShapeDtypeStruct((B, T, D_MODEL), F32),
                   jax.ShapeDtypeStruct((2, B * T), I32),
                   jax.ShapeDtypeStruct((2, B * T), F32),
                   jax.ShapeDtypeStruct((2, B * T), I32),
                   jax.ShapeDtypeStruct((N_EXPERTS, 128), F32)),
        grid=(B, nt),
        in_specs=[tok, _row_spec(D_MODEL), _mod_spec(layer, 4, False), _mod_spec(layer, 3, False),
                  pl.BlockSpec((N_EXPERTS, D_MODEL), lambda b, i: (0, 0))],
        out_specs=(tok, lane, lane, lane, pl.BlockSpec((N_EXPERTS, 128), lambda b, i: (0, 0))),
        scratch_shapes=[pltpu.VMEM((N_EXPERTS, 128), F32)],
        compiler_params=_cparams(("arbitrary", "arbitrary")),
        name="moe_route",
    )(x, gpre.reshape(1, D_MODEL), modr, modr, router_t)


def _dispatch_kernel(slot_ref, h_ref, hs_in_ref, hs_ref, sem):
    del hs_in_ref
    tm = h_ref.shape[0]

    def body(r, carry):
        for k in range(2):
            s = slot_ref[0, k, r]
            pltpu.make_async_copy(h_ref.at[pl.ds(r, 1)], hs_ref.at[pl.ds(s, 1)], sem).start()
        return carry

    lax.fori_loop(0, tm, body, 0)
    for _ in range(2):
        pltpu.make_async_copy(h_ref, hs_ref.at[pl.ds(0, tm)], sem).wait()


def _dispatch(h, slot_tiles, n_slots):
    N = h.shape[0]
    tm = slot_tiles.shape[2]
    return pl.pallas_call(
        _dispatch_kernel,
        out_shape=jax.ShapeDtypeStruct((n_slots, D_MODEL), F32),
        grid=(N // tm,),
        in_specs=[pl.BlockSpec((1, 2, tm), lambda i: (i, 0, 0), memory_space=pltpu.SMEM),
                  pl.BlockSpec((tm, D_MODEL), lambda i: (i, 0)),
                  pl.BlockSpec(memory_space=pl.ANY)],
        out_specs=pl.BlockSpec(memory_space=pl.ANY),
        scratch_shapes=[pltpu.SemaphoreType.DMA(())],
        input_output_aliases={2: 0},
        compiler_params=_cparams(("arbitrary",)),
        name="moe_dispatch",
    )(slot_tiles, h, jnp.zeros((n_slots, D_MODEL), F32))


def _expert_kernel(te_ref, tv_ref, h_ref, w1_ref, w3_ref, w2_ref, y_ref):
    t = pl.program_id(0)

    @pl.when(tv_ref[t] == 1)
    def _():
        y_ref[...] = _swiglu_tile(h_ref[...].astype(BF16), w1_ref, w3_ref, w2_ref)

    @pl.when(tv_ref[t] == 0)
    def _():
        y_ref[...] = jnp.zeros_like(y_ref)


def _experts(hs, tile_expert, tile_valid, w1, w3, w2):
    n_slots = hs.shape[0]
    tm = EXPERT_TILE
    grid_spec = pltpu.PrefetchScalarGridSpec(
        num_scalar_prefetch=2,
        grid=(n_slots // tm,),
        in_specs=[pl.BlockSpec((tm, D_MODEL), lambda t, te, tv: (t, 0)),
                  pl.BlockSpec((None, D_MODEL, D_FF), lambda t, te, tv: (te[t], 0, 0)),
                  pl.BlockSpec((None, D_MODEL, D_FF), lambda t, te, tv: (te[t], 0, 0)),
                  pl.BlockSpec((None, D_FF, D_MODEL), lambda t, te, tv: (te[t], 0, 0))],
        out_specs=pl.BlockSpec((tm, D_MODEL), lambda t, te, tv: (t, 0)),
    )
    return pl.pallas_call(
        _expert_kernel,
        out_shape=jax.ShapeDtypeStruct((n_slots, D_MODEL), F32),
        grid_spec=grid_spec,
        compiler_params=_cparams(("arbitrary",), VMEM_LIMIT),
        name="moe_experts",
    )(tile_expert, tile_valid, hs, w1, w3, w2)


def _combine_kernel(slot_ref, gate_ref, x_ref, npost_ref, g2_ref, ys_ref, o_ref, buf0, buf1, sem):
    tm = x_ref.shape[0]
    bufs = (buf0, buf1)

    def body(r, carry):
        for k in range(2):
            s = slot_ref[0, k, r]
            pltpu.make_async_copy(ys_ref.at[pl.ds(s, 1)], bufs[k].at[pl.ds(r, 1)], sem).start()
        return carry

    lax.fori_loop(0, tm, body, 0)
    for k in range(2):
        pltpu.make_async_copy(ys_ref.at[pl.ds(0, tm)], bufs[k], sem).wait()
    y = gate_ref[:, 0:1] * buf0[...] + gate_ref[:, 1:2] * buf1[...]
    o_ref[...] = x_ref[...] + g2_ref[...] * _rms(y, npost_ref[...])


def _combine(x, ys, slot_tiles, gate_tok, npost, modr, layer):
    B, T, _ = x.shape
    tm = slot_tiles.shape[2]
    nt = T // tm
    tok = pl.BlockSpec((None, tm, D_MODEL), lambda b, i: (b, i, 0))
    return pl.pallas_call(
        _combine_kernel,
        out_shape=jax.ShapeDtypeStruct((B, T, D_MODEL), F32),
        grid=(B, nt),
        in_specs=[pl.BlockSpec((1, 2, tm), lambda b, i: (b * nt + i, 0, 0), memory_space=pltpu.SMEM),
                  pl.BlockSpec((tm, 2), lambda b, i: (b * nt + i, 0)),
                  tok, _row_spec(D_MODEL), _mod_spec(layer, 5, False),
                  pl.BlockSpec(memory_space=pl.ANY)],
        out_specs=tok,
        scratch_shapes=[pltpu.VMEM((tm, D_MODEL), F32), pltpu.VMEM((tm, D_MODEL), F32),
                        pltpu.SemaphoreType.DMA(())],
        compiler_params=_cparams(("arbitrary", "arbitrary")),
        name="moe_combine",
    )(slot_tiles, gate_tok, x, npost.reshape(1, D_MODEL), modr, ys)


def _moe_ffn(x, gpre, npost, router, w1, w3, w2, modr, layer):
    B, T, _ = x.shape
    N = B * T
    tm = min(TOKEN_TILE, T)
    h, idx, gate, rank, cnt = _route(x, gpre, router.T, modr, layer)
    n_slots = 2 * N + N_EXPERTS * EXPERT_TILE
    n_tiles = n_slots // EXPERT_TILE
    counts = cnt[:, 0].astype(I32)
    padded = ((counts + EXPERT_TILE - 1) // EXPERT_TILE) * EXPERT_TILE
    ends = jnp.cumsum(padded)
    starts = ends - padded
    slot = rank
    for e in range(N_EXPERTS):
        slot = slot + jnp.where(idx == e, starts[e], 0)
    slot_tiles = slot.reshape(2, N // tm, tm).transpose(1, 0, 2)
    tile_start = jnp.arange(n_tiles, dtype=I32) * EXPERT_TILE
    tile_valid = (tile_start < ends[-1]).astype(I32)
    tile_expert = jnp.minimum(jnp.sum((tile_start[:, None] >= ends[None, :]).astype(I32), axis=1), N_EXPERTS - 1)
    last_expert = jnp.max(jnp.where(tile_valid == 1, tile_expert, 0))
    tile_expert = jnp.where(tile_valid == 1, tile_expert, last_expert)
    hs = _dispatch(h.reshape(N, D_MODEL), slot_tiles, n_slots)
    ys = _experts(hs, tile_expert, tile_valid, w1, w3, w2)
    return _combine(x, ys, slot_tiles, gate.T, npost, modr, layer)


def kernel(x, c, ctx, c_ctx, w_mod, b_mod, norm_pre_mix, norm_post_mix, norm_pre_ffn, norm_post_ffn, w_in, w_out, q_norm, k_norm, ret_decay_logit, pool_w, pool_scale, ffn_w1, ffn_w3, ffn_w2, moe_router, moe_w1, moe_w3, moe_w2):
    B, T, _ = x.shape
    cvec = jnp.zeros((MOD_ROWS, D_MODEL), F32).at[0:B].set(c).at[2].set(c_ctx)
    modr = _modulation(cvec, w_mod, b_mod)
    cos_t, sin_t = _rope_tables(T)
    avg = jnp.kron(jnp.eye(RET_HEADS, dtype=F32), jnp.full((HEAD_DIM, HEAD_DIM), 1.0 / HEAD_DIM, F32)).astype(BF16)
    zero_state = jnp.zeros((B, RET_WIDTH, RET_WIDTH), F32)
    xc = ctx
    for i in range(DEPTH):
        need_ctx = i < DEPTH - 1
        w_in_b = w_in[i].astype(BF16)
        w_out_b = w_out[i].astype(BF16)
        pool_bd = jax.scipy.linalg.block_diag(*[pool_w[i, g] for g in range(len(POOL_WINDOWS))]).astype(BF16)
        log_gamma = jax.nn.log_sigmoid(ret_decay_logit[i].astype(F32))

        qkv_c, ret_c, pp_c = _in_projection(xc, modr, norm_pre_mix[i], w_in_b, i, True)
        qt_c, kn_c, vt_c = _attn_prep(qkv_c, q_norm[i], k_norm[i], cos_t, sin_t, False)
        of_c, ob_c, s_fwd, s_bwd = _retention(ret_c, log_gamma, zero_state, zero_state)

        qkv_x, ret_x, pp_x = _in_projection(x, modr, norm_pre_mix[i], w_in_b, i, False)
        qt_x, kn_x, vt_x = _attn_prep(qkv_x, q_norm[i], k_norm[i], cos_t, sin_t, True)
        attn_x = _attention(qt_x, kn_c, vt_c, kn_x, vt_x)
        of_x, ob_x, _, _ = _retention(ret_x, log_gamma, s_fwd, s_bwd)
        pool_x = _pooling(pp_x, pool_bd, pool_scale[i])
        x = _mix_out(x, attn_x, of_x, ob_x, ret_x, pool_x, avg, w_out_b, norm_post_mix[i], modr, i, False)

        if need_ctx:
            attn_c = _attention(qt_c, kn_c, vt_c)
            pool_c = _pooling(pp_c, pool_bd, pool_scale[i])
            xc = _mix_out(xc, attn_c, of_c, ob_c, ret_c, pool_c, avg, w_out_b, norm_post_mix[i], modr, i, True)

        j = i // 2
        if i % 2 == 0:
            w1, w3, w2 = ffn_w1[j].astype(BF16), ffn_w3[j].astype(BF16), ffn_w2[j].astype(BF16)
            x = _dense_ffn(x, norm_pre_ffn[i], norm_post_ffn[i], w1, w3, w2, modr, i, False)
            if need_ctx:
                xc = _dense_ffn(xc, norm_pre_ffn[i], norm_post_ffn[i], w1, w3, w2, modr, i, True)
        else:
            x = _moe_ffn(x, norm_pre_ffn[i], norm_post_ffn[i], moe_router[j], moe_w1[j].astype(BF16),
                         moe_w3[j].astype(BF16), moe_w2[j].astype(BF16), modr, i)
    return x
```

```python
import functools

import jax
import jax.numpy as jnp
from jax import lax
from jax.experimental import pallas as pl
from jax.experimental.pallas import tpu as pltpu

F32 = jnp.float32
BF16 = jnp.bfloat16
I32 = jnp.int32

D_MODEL = 1024
GRID_W = 64
HEAD_DIM = 64
ATTN_WIDTH = 512
KV_HEADS = 2
ATTN_GROUP = 4
KV_WIDTH = 128
RET_WIDTH = 256
RET_HEADS = 4
POOL_WIDTH = 256
POOL_WINDOWS = (2, 4, 8, 16)
IN_WIDTH = 2048
RET_CHUNK = 128
ROPE_THETA = 10000.0
D_FF = 2816
N_EXPERTS = 8
NORM_EPS = 1e-6
DEPTH = 2

TOKEN_TILE = 512
ATTN_Q_TILE = 256
ATTN_K_CHUNK = 2048
POOL_TILE = 256
POOL_HALO = 16
FF_CHUNK = 256
EXPERT_TILE = 512
BF16_SUBLANES = 16
K_COLS = 128
VT_ROWS = HEAD_DIM + BF16_SUBLANES
MAX_BOUND_SHIFT = 40.0
SHIFT_MARGIN = 1.01
MOD_ROWS = 8
VMEM_LIMIT = 56 * 1024 * 1024


def _cparams(sem, vmem=None):
    return pltpu.CompilerParams(dimension_semantics=sem, vmem_limit_bytes=vmem)


def _rms(x, gain):
    ms = jnp.mean(x * x, axis=-1, keepdims=True)
    return x * lax.rsqrt(ms + NORM_EPS) * gain


def _silu(x):
    return x * jax.nn.sigmoid(x)


def _mod_kernel(c_ref, w_ref, b_ref, o_ref):
    s = _silu(c_ref[...])
    o_ref[...] = jnp.dot(s, w_ref[...], precision=lax.Precision.HIGHEST,
                         preferred_element_type=F32) + b_ref[...]


def _modulation(cvec, w_mod, b_mod):
    nchunk = 6
    out = pl.pallas_call(
        _mod_kernel,
        out_shape=jax.ShapeDtypeStruct((DEPTH, MOD_ROWS, 6 * D_MODEL), F32),
        grid=(DEPTH, nchunk),
        in_specs=[
            pl.BlockSpec((MOD_ROWS, D_MODEL), lambda l, j: (0, 0)),
            pl.BlockSpec((None, D_MODEL, D_MODEL), lambda l, j: (l, 0, j)),
            pl.BlockSpec((None, 1, D_MODEL), lambda l, j: (l, 0, j)),
        ],
        out_specs=pl.BlockSpec((None, MOD_ROWS, D_MODEL), lambda l, j: (l, 0, j)),
        compiler_params=_cparams(("parallel", "parallel")),
        name="modulation",
    )(cvec, w_mod, b_mod.reshape(DEPTH, 1, 6 * D_MODEL))
    return out.reshape(DEPTH * MOD_ROWS * nchunk, 1, D_MODEL)


def _mod_spec(layer, chunk, ctx):
    base = layer * MOD_ROWS * 6
    if ctx:
        return pl.BlockSpec((None, 1, D_MODEL), lambda b, i: (base + 2 * 6 + chunk, 0, 0))
    return pl.BlockSpec((None, 1, D_MODEL), lambda b, i: (base + b * 6 + chunk, 0, 0))


def _row_spec(width):
    return pl.BlockSpec((1, width), lambda b, i: (0, 0))


def _inproj_kernel(x_ref, g_ref, sc_ref, sh_ref, w_ref, qkv_ref, ret_ref, pp_ref):
    h = _rms(x_ref[...], g_ref[...]) * (1.0 + sc_ref[...]) + sh_ref[...]
    p = jnp.dot(h.astype(BF16), w_ref[...], preferred_element_type=F32)
    qkv_ref[...] = p[:, :768].astype(BF16)
    ret_ref[...] = p[:, 768:1792].astype(BF16)
    pp_ref[...] = p[:, 1792:].astype(BF16)


def _in_projection(x, modr, gain, w_in_bf16, layer, ctx):
    B, T, _ = x.shape
    tm = min(TOKEN_TILE, T)
    tok = lambda w: pl.BlockSpec((None, tm, w), lambda b, i: (b, i, 0))
    return pl.pallas_call(
        _inproj_kernel,
        out_shape=(jax.ShapeDtypeStruct((B, T, 768), BF16),
                   jax.ShapeDtypeStruct((B, T, 1024), BF16),
                   jax.ShapeDtypeStruct((B, T, POOL_WIDTH), BF16)),
        grid=(B, T // tm),
        in_specs=[tok(D_MODEL), _row_spec(D_MODEL), _mod_spec(layer, 1, ctx), _mod_spec(layer, 0, ctx),
                  pl.BlockSpec((D_MODEL, IN_WIDTH), lambda b, i: (0, 0))],
        out_specs=(tok(768), tok(1024), tok(POOL_WIDTH)),
        compiler_params=_cparams(("parallel", "parallel"), VMEM_LIMIT),
        name="in_projection",
    )(x, gain.reshape(1, D_MODEL), modr, modr, w_in_bf16)


def _prep_kernel(qkv_ref, qg_ref, kg_ref, cos_ref, sin_ref, qt_ref, kn_ref, vt_ref, stat_ref, *, rope):
    t = qkv_ref[...].astype(F32).T

    def norm_rope(blk, gain):
        ms = jnp.mean(blk * blk, axis=0, keepdims=True)
        y = blk * lax.rsqrt(ms + NORM_EPS) * gain
        if rope:
            partner = jnp.concatenate([y[16:32], y[0:16], y[48:64], y[32:48]], axis=0)
            y = y * cos_ref[...] + partner * sin_ref[...]
        return y

    def max_sq_norm(y):
        n2 = jnp.sum(y * y, axis=0, keepdims=True)
        return jnp.broadcast_to(jnp.max(n2, axis=1, keepdims=True), (1, 128))

    q_stats = []
    for h in range(ATTN_WIDTH // HEAD_DIM):
        lo = h * HEAD_DIM
        q = norm_rope(t[lo:lo + HEAD_DIM], qg_ref[...]) * (HEAD_DIM ** -0.5)
        qt_ref[lo:lo + HEAD_DIM, :] = q.astype(BF16)
        q_stats.append(max_sq_norm(q))
    tm = t.shape[1]
    k_ones = jnp.where(lax.broadcasted_iota(I32, (HEAD_DIM, tm), 0) < 2, 1.0, 0.0)
    v_ones = jnp.where(lax.broadcasted_iota(I32, (VT_ROWS - HEAD_DIM, tm), 0) < 1, 1.0, 0.0)
    k_stats = []
    for kv in range(KV_HEADS):
        lo = ATTN_WIDTH + kv * HEAD_DIM
        k = norm_rope(t[lo:lo + HEAD_DIM], kg_ref[...])
        k_stats.append(max_sq_norm(k))
        kn_ref[kv] = jnp.concatenate([k, k_ones], axis=0).T.astype(BF16)
        lo = ATTN_WIDTH + KV_WIDTH + kv * HEAD_DIM
        vt_ref[kv, 0] = jnp.concatenate([t[lo:lo + HEAD_DIM], v_ones], axis=0).astype(BF16)
    q_group = [functools.reduce(jnp.maximum, q_stats[kv * ATTN_GROUP:(kv + 1) * ATTN_GROUP])
               for kv in range(KV_HEADS)]
    stat_ref[...] = jnp.concatenate(k_stats + q_group + [jnp.zeros((8 - 2 * KV_HEADS, 128), F32)], axis=0)


def _attn_prep(qkv, q_gain, k_gain, cos_t, sin_t, rope):
    B, T, _ = qkv.shape
    tm = min(ATTN_K_CHUNK, T)
    nc = T // tm
    return pl.pallas_call(
        functools.partial(_prep_kernel, rope=rope),
        out_shape=(jax.ShapeDtypeStruct((B, ATTN_WIDTH, T), BF16),
                   jax.ShapeDtypeStruct((B, KV_HEADS, T, K_COLS), BF16),
                   jax.ShapeDtypeStruct((B, KV_HEADS, nc, VT_ROWS, tm), BF16),
                   jax.ShapeDtypeStruct((B, nc, 8, 128), F32)),
        grid=(B, nc),
        in_specs=[pl.BlockSpec((None, tm, 768), lambda b, i: (b, i, 0)),
                  pl.BlockSpec((HEAD_DIM, 1), lambda b, i: (0, 0)),
                  pl.BlockSpec((HEAD_DIM, 1), lambda b, i: (0, 0)),
                  pl.BlockSpec((HEAD_DIM, tm), lambda b, i: (0, i)),
                  pl.BlockSpec((HEAD_DIM, tm), lambda b, i: (0, i))],
        out_specs=(pl.BlockSpec((None, ATTN_WIDTH, tm), lambda b, i: (b, 0, i)),
                   pl.BlockSpec((None, KV_HEADS, tm, K_COLS), lambda b, i: (b, 0, i, 0)),
                   pl.BlockSpec((None, KV_HEADS, 1, VT_ROWS, tm), lambda b, i: (b, 0, i, 0, 0)),
                   pl.BlockSpec((None, None, 8, 128), lambda b, i: (b, i, 0, 0))),
        compiler_params=_cparams(("parallel", "parallel")),
        name="attn_prep",
    )(qkv, q_gain.reshape(HEAD_DIM, 1), k_gain.reshape(HEAD_DIM, 1), cos_t, sin_t)


def _shift_plan(q_stats, k_stats_list):
    k2 = functools.reduce(jnp.maximum, [st[:, :, 0:KV_HEADS, 0].max(axis=1) for st in k_stats_list])
    q2 = q_stats[:, :, KV_HEADS:2 * KV_HEADS, 0].max(axis=1)
    use_bound = (jnp.sqrt(k2 * q2) * SHIFT_MARGIN <= MAX_BOUND_SHIFT).astype(I32).reshape(-1)
    return use_bound, (jnp.sqrt(k2) * SHIFT_MARGIN).reshape(-1)


def _rope_tables(T):
    t = jnp.arange(T)
    row = (t // GRID_W).astype(F32)
    col = (t % GRID_W).astype(F32)
    n_freq = HEAD_DIM // 4
    inv = ROPE_THETA ** (-jnp.arange(n_freq, dtype=F32) / n_freq)
    ang_r = row[None, :] * inv[:, None]
    ang_c = col[None, :] * inv[:, None]
    cos_t = jnp.concatenate([jnp.cos(ang_r), jnp.cos(ang_r), jnp.cos(ang_c), jnp.cos(ang_c)], axis=0)
    sin_t = jnp.concatenate([-jnp.sin(ang_r), jnp.sin(ang_r), -jnp.sin(ang_c), jnp.sin(ang_c)], axis=0)
    return cos_t, sin_t


def _attn_kernel(*refs, n_x_chunks, tk):
    if n_x_chunks:
        use_bound_ref, kmax_ref, qt_ref, kc_ref, vtc_ref, kx_ref, vtx_ref, o_ref, qa_ref, acc_ref = refs
    else:
        use_bound_ref, kmax_ref, qt_ref, kc_ref, vtc_ref, o_ref, qa_ref, acc_ref = refs
    tq = qt_ref.shape[1]
    nq = ATTN_GROUP * tq
    bh = pl.program_id(0) * KV_HEADS + pl.program_id(1)

    for g in range(ATTN_GROUP):
        qa_ref[0:HEAD_DIM, g * tq:(g + 1) * tq] = qt_ref[g * HEAD_DIM:(g + 1) * HEAD_DIM, :]
    qa_ref[HEAD_DIM:K_COLS, :] = jnp.zeros((K_COLS - HEAD_DIM, nq), BF16)

    def scores(k):
        return jnp.dot(k, qa_ref[...], preferred_element_type=F32)

    def colmax8(s):
        return jnp.max(s.reshape(s.shape[0] // 8, 8, nq), axis=0)

    def set_shift(m):
        m_hi = m.astype(BF16).astype(F32)
        m_lo = m - m_hi
        row = lax.broadcasted_iota(I32, (BF16_SUBLANES, nq), 0)
        qa_ref[HEAD_DIM:HEAD_DIM + BF16_SUBLANES, :] = jnp.where(
            row == 0, -m_hi, jnp.where(row == 1, -m_lo, 0.0)).astype(BF16)

    kc = kc_ref[...]

    @pl.when(use_bound_ref[bh] == 1)
    def _():
        q = qa_ref[0:HEAD_DIM, :].astype(F32)
        set_shift(jnp.sqrt(jnp.sum(q * q, axis=0, keepdims=True)) * kmax_ref[bh])

    @pl.when(use_bound_ref[bh] == 0)
    def _():
        mx = colmax8(scores(kc))
        if n_x_chunks:
            def max_body(j, mx):
                k = kx_ref[pl.ds(pl.multiple_of(j * tk, tk), tk), :]
                return jnp.maximum(mx, colmax8(scores(k)))

            mx = lax.fori_loop(0, n_x_chunks, max_body, mx)
        set_shift(jnp.max(mx, axis=0, keepdims=True))

    def weighted(k, vt):
        p = jnp.exp(scores(k)).astype(BF16)
        return jnp.dot(vt, p, preferred_element_type=F32)

    acc_ref[...] = weighted(kc, vtc_ref[0])
    if n_x_chunks:
        def sum_body(j, carry):
            k = kx_ref[pl.ds(pl.multiple_of(j * tk, tk), tk), :]
            acc_ref[...] += weighted(k, vtx_ref[j])
            return carry

        lax.fori_loop(0, n_x_chunks, sum_body, 0)

    o = acc_ref[0:HEAD_DIM, :] / acc_ref[HEAD_DIM:HEAD_DIM + 1, :]
    o = jnp.concatenate([o[:, g * tq:(g + 1) * tq] for g in range(ATTN_GROUP)], axis=0)
    o_ref[...] = o.T.astype(BF16)


def _attention(plan, qt, kc, vtc, kx=None, vtx=None):
    B, _, Tq = qt.shape
    Tc = kc.shape[2]
    tq = min(ATTN_Q_TILE, Tq)
    in_specs = [pl.BlockSpec((None, ATTN_GROUP * HEAD_DIM, tq), lambda b, h, i, *_: (b, h, i)),
                pl.BlockSpec((None, None, Tc, K_COLS), lambda b, h, i, *_: (b, h, 0, 0)),
                pl.BlockSpec((None, None, 1, VT_ROWS, Tc), lambda b, h, i, *_: (b, h, 0, 0, 0))]
    args = [qt, kc, vtc]
    n_x_chunks, tk = 0, 0
    if kx is not None:
        _, _, n_x_chunks, _, tk = vtx.shape
        T = kx.shape[2]
        in_specs += [pl.BlockSpec((None, None, T, K_COLS), lambda b, h, i, *_: (b, h, 0, 0)),
                     pl.BlockSpec((None, None, n_x_chunks, VT_ROWS, tk), lambda b, h, i, *_: (b, h, 0, 0, 0))]
        args += [kx, vtx]
    grid_spec = pltpu.PrefetchScalarGridSpec(
        num_scalar_prefetch=2,
        grid=(B, KV_HEADS, Tq // tq),
        in_specs=in_specs,
        out_specs=pl.BlockSpec((None, tq, ATTN_GROUP * HEAD_DIM), lambda b, h, i, *_: (b, i, h)),
        scratch_shapes=[pltpu.VMEM((K_COLS, ATTN_GROUP * tq), BF16), pltpu.VMEM((VT_ROWS, ATTN_GROUP * tq), F32)],
    )
    return pl.pallas_call(
        functools.partial(_attn_kernel, n_x_chunks=n_x_chunks, tk=tk),
        out_shape=jax.ShapeDtypeStruct((B, Tq, ATTN_WIDTH), BF16),
        grid_spec=grid_spec,
        compiler_params=_cparams(("parallel", "parallel", "parallel"), VMEM_LIMIT),
        name="attention",
    )(*plan, *args)


def _ret_kernel(lgc_f_ref, lgc_b_ref, lgr_f_ref, lgr_b_ref, s0f_ref, s0b_ref, blk_f_ref, blk_b_ref,
                of_ref, ob_ref, sf_ref, sb_ref, st_f, st_b, dec_f, dec_b):
    C = RET_CHUNK
    W = RET_WIDTH
    n = pl.program_id(1)

    @pl.when(n == 0)
    def _():
        st_f[...] = s0f_ref[...]
        st_b[...] = s0b_ref[...]
        c = jnp.bitwise_and(lax.broadcasted_iota(I32, (RET_HEADS * C, C), 0), C - 1)
        m = lax.broadcasted_iota(I32, (RET_HEADS * C, C), 1)
        diff = (c - m).astype(F32)
        dec_f[...] = jnp.where(diff >= 0, jnp.exp(lgc_f_ref[...] * jnp.maximum(diff, 0.0)), 0.0)
        dec_b[...] = jnp.where(diff <= 0, jnp.exp(lgc_b_ref[...] * jnp.maximum(-diff, 0.0)), 0.0)

    lane_head = jnp.right_shift(lax.broadcasted_iota(I32, (C, W), 1), 6)
    pos = lax.broadcasted_iota(I32, (C, W), 0).astype(F32)
    same_head = (jnp.right_shift(lax.broadcasted_iota(I32, (W, W), 0), 6)
                 == jnp.right_shift(lax.broadcasted_iota(I32, (W, W), 1), 6))

    def direction(blk_ref, dec_ref, st_ref, lgr, forward, out_ref):
        q = blk_ref[:, 0:W].astype(F32)
        kf = blk_ref[:, W:2 * W].astype(F32) * (HEAD_DIM ** -0.5)
        v = blk_ref[:, 2 * W:3 * W]
        if forward:
            zeta = jnp.exp(lgr * (C - 1.0 - pos))
            xi = jnp.exp(lgr * (pos + 1.0))
        else:
            zeta = jnp.exp(lgr * pos)
            xi = jnp.exp(lgr * (C - pos))
        qexp = jnp.concatenate([jnp.where(lane_head == h, q, 0.0) for h in range(RET_HEADS)],
                               axis=0).astype(BF16)
        a = lax.dot_general(qexp, kf.astype(BF16), (((1,), (1,)), ((), ())),
                            preferred_element_type=F32)
        p = (a * dec_ref[...]).astype(BF16)
        full = jnp.dot(p, v, preferred_element_type=F32)
        intra = jnp.zeros((C, W), F32)
        for h in range(RET_HEADS):
            intra = intra + jnp.where(lane_head == h, full[h * C:(h + 1) * C], 0.0)
        state = st_ref[...]
        cross = jnp.dot((q * xi).astype(BF16), state.astype(BF16), preferred_element_type=F32)
        out_ref[...] = intra + cross
        upd = lax.dot_general((kf * zeta).astype(BF16), v, (((0,), (0,)), ((), ())),
                              preferred_element_type=F32)
        st_ref[...] = jnp.where(same_head, state * jnp.exp(lgr * float(C)) + upd, 0.0)

    direction(blk_f_ref, dec_f, st_f, lgr_f_ref[...], True, of_ref)
    direction(blk_b_ref, dec_b, st_b, lgr_b_ref[...], False, ob_ref)

    @pl.when(n == pl.num_programs(1) - 1)
    def _():
        sf_ref[...] = st_f[...]
        sb_ref[...] = st_b[...]


def _retention(ret, log_gamma, s0f, s0b):
    B, T, _ = ret.shape
    C = RET_CHUNK
    nc = T // C
    lgc = lambda d: jnp.repeat(log_gamma[d], C).reshape(RET_HEADS * C, 1)
    lgr = lambda d: jnp.repeat(log_gamma[d], HEAD_DIM).reshape(1, RET_WIDTH)
    const = lambda shape: pl.BlockSpec(shape, lambda b, n: (0,) * len(shape))
    st_spec = pl.BlockSpec((None, RET_WIDTH, RET_WIDTH), lambda b, n: (b, 0, 0))
    return pl.pallas_call(
        _ret_kernel,
        out_shape=(jax.ShapeDtypeStruct((B, T, RET_WIDTH), F32),
                   jax.ShapeDtypeStruct((B, T, RET_WIDTH), F32),
                   jax.ShapeDtypeStruct((B, RET_WIDTH, RET_WIDTH), F32),
                   jax.ShapeDtypeStruct((B, RET_WIDTH, RET_WIDTH), F32)),
        grid=(B, nc),
        in_specs=[const((RET_HEADS * C, 1)), const((RET_HEADS * C, 1)),
                  const((1, RET_WIDTH)), const((1, RET_WIDTH)), st_spec, st_spec,
                  pl.BlockSpec((None, C, 1024), lambda b, n: (b, n, 0)),
                  pl.BlockSpec((None, C, 1024), lambda b, n: (b, nc - 1 - n, 0))],
        out_specs=(pl.BlockSpec((None, C, RET_WIDTH), lambda b, n: (b, n, 0)),
                   pl.BlockSpec((None, C, RET_WIDTH), lambda b, n: (b, nc - 1 - n, 0)),
                   st_spec, st_spec),
        scratch_shapes=[pltpu.VMEM((RET_WIDTH, RET_WIDTH), F32), pltpu.VMEM((RET_WIDTH, RET_WIDTH), F32),
                        pltpu.VMEM((RET_HEADS * C, C), F32), pltpu.VMEM((RET_HEADS * C, C), F32)],
        compiler_params=_cparams(("parallel", "arbitrary")),
        name="retention",
    )(lgc(0), lgc(1), lgr(0), lgr(1), s0f, s0b, ret, ret)


def _pool_kernel(prev_ref, cur_ref, next_ref, w_ref, scale_ref, o_ref, *, seq_len):
    tm = cur_ref.shape[0]
    ext = jnp.concatenate([prev_ref[...], cur_ref[...], next_ref[...]], axis=0)
    t0 = pl.program_id(1) * tm
    tok = t0 + lax.broadcasted_iota(I32, (tm, tm + 2 * POOL_HALO), 0)
    src = t0 - POOL_HALO + lax.broadcasted_iota(I32, (tm, tm + 2 * POOL_HALO), 1)
    tcol = t0 + lax.broadcasted_iota(I32, (tm, 1), 0)
    lane_group = jnp.right_shift(lax.broadcasted_iota(I32, (tm, POOL_WIDTH), 1), 6)
    cur = cur_ref[...].astype(F32)
    mixed = jnp.zeros((tm, POOL_WIDTH), F32)
    for gi, w in enumerate(POOL_WINDOWS):
        lo = jnp.maximum(tok - w // 2, 0)
        hi = jnp.minimum(tok + w // 2, seq_len)
        band = jnp.where((src >= lo) & (src < hi), 1.0, 0.0).astype(BF16)
        total = jnp.dot(band, ext, preferred_element_type=F32)
        cnt = (jnp.minimum(tcol + w // 2, seq_len) - jnp.maximum(tcol - w // 2, 0)).astype(F32)
        mixed = mixed + jnp.where(lane_group == gi, total / cnt - cur, 0.0)
    y = jnp.dot(mixed.astype(BF16), w_ref[...], preferred_element_type=F32)
    o_ref[...] = (y * scale_ref[...]).astype(BF16)


def _pooling(pp, w_blockdiag_bf16, scale):
    B, T, _ = pp.shape
    tm = min(POOL_TILE, T)
    r = tm // POOL_HALO
    last = T // POOL_HALO - 1
    return pl.pallas_call(
        functools.partial(_pool_kernel, seq_len=T),
        out_shape=jax.ShapeDtypeStruct((B, T, POOL_WIDTH), BF16),
        grid=(B, T // tm),
        in_specs=[pl.BlockSpec((None, POOL_HALO, POOL_WIDTH), lambda b, i: (b, jnp.maximum(i * r - 1, 0), 0)),
                  pl.BlockSpec((None, tm, POOL_WIDTH), lambda b, i: (b, i, 0)),
                  pl.BlockSpec((None, POOL_HALO, POOL_WIDTH), lambda b, i: (b, jnp.minimum((i + 1) * r, last), 0)),
                  pl.BlockSpec((POOL_WIDTH, POOL_WIDTH), lambda b, i: (0, 0)),
                  _row_spec(POOL_WIDTH)],
        out_specs=pl.BlockSpec((None, tm, POOL_WIDTH), lambda b, i: (b, i, 0)),
        compiler_params=_cparams(("parallel", "parallel")),
        name="pooling",
    )(pp, pp, pp, w_blockdiag_bf16, scale.reshape(1, POOL_WIDTH))


def _head_mean(x, avg):
    hi = x.astype(BF16)
    lo = (x - hi.astype(F32)).astype(BF16)
    return (jnp.dot(hi, avg, preferred_element_type=F32) + jnp.dot(lo, avg, preferred_element_type=F32))


def _mixout_kernel(x_ref, attn_ref, of_ref, ob_ref, gate_ref, pool_ref, avg_ref, w_ref, npost_ref, g1_ref, o_ref):
    o = of_ref[...] + ob_ref[...]
    avg = avg_ref[...]
    mu = _head_mean(o, avg)
    cen = o - mu
    var = _head_mean(cen * cen, avg)
    y_ret = (_silu(gate_ref[...].astype(F32)) * (cen * lax.rsqrt(var + NORM_EPS))).astype(BF16)
    mx = (jnp.dot(attn_ref[...], w_ref[0:ATTN_WIDTH, :], preferred_element_type=F32)
          + jnp.dot(y_ret, w_ref[ATTN_WIDTH:ATTN_WIDTH + RET_WIDTH, :], preferred_element_type=F32)
          + jnp.dot(pool_ref[...], w_ref[ATTN_WIDTH + RET_WIDTH:, :], preferred_element_type=F32))
    o_ref[...] = x_ref[...] + g1_ref[...] * _rms(mx, npost_ref[...])


def _mix_out(x, attn, of, ob, ret, pool, avg_bf16, w_out_bf16, npost, modr, layer, ctx):
    B, T, _ = x.shape
    tm = min(TOKEN_TILE, T)
    tok = lambda w: pl.BlockSpec((None, tm, w), lambda b, i: (b, i, 0))
    return pl.pallas_call(
        _mixout_kernel,
        out_shape=jax.ShapeDtypeStruct((B, T, D_MODEL), F32),
        grid=(B, T // tm),
        in_specs=[tok(D_MODEL), tok(ATTN_WIDTH), tok(RET_WIDTH), tok(RET_WIDTH),
                  pl.BlockSpec((None, tm, RET_WIDTH), lambda b, i: (b, i, 3)),
                  tok(POOL_WIDTH),
                  pl.BlockSpec((RET_WIDTH, RET_WIDTH), lambda b, i: (0, 0)),
                  pl.BlockSpec((D_MODEL, D_MODEL), lambda b, i: (0, 0)),
                  _row_spec(D_MODEL), _mod_spec(layer, 2, ctx)],
        out_specs=tok(D_MODEL),
        compiler_params=_cparams(("parallel", "parallel"), VMEM_LIMIT),
        name="mix_out",
    )(x, attn, of, ob, ret, pool, avg_bf16, w_out_bf16, npost.reshape(1, D_MODEL), modr)


def _swiglu_tile(h, w1_ref, w3_ref, w2_ref):
    acc = jnp.zeros((h.shape[0], D_MODEL), F32)
    for f in range(0, D_FF, FF_CHUNK):
        a = jnp.dot(h, w1_ref[:, f:f + FF_CHUNK], preferred_element_type=F32)
        b = jnp.dot(h, w3_ref[:, f:f + FF_CHUNK], preferred_element_type=F32)
        u = (_silu(a) * b).astype(BF16)
        acc = acc + jnp.dot(u, w2_ref[f:f + FF_CHUNK, :], preferred_element_type=F32)
    return acc


def _ffn_kernel(x_ref, gpre_ref, sc_ref, sh_ref, w1_ref, w3_ref, w2_ref, npost_ref, g2_ref, o_ref):
    x = x_ref[...]
    h = (_rms(x, gpre_ref[...]) * (1.0 + sc_ref[...]) + sh_ref[...]).astype(BF16)
    y = _swiglu_tile(h, w1_ref, w3_ref, w2_ref)
    o_ref[...] = x + g2_ref[...] * _rms(y, npost_ref[...])


def _dense_ffn(x, gpre, npost, w1, w3, w2, modr, layer, ctx):
    B, T, _ = x.shape
    tm = min(TOKEN_TILE, T)
    tok = pl.BlockSpec((None, tm, D_MODEL), lambda b, i: (b, i, 0))
    wspec = lambda shape: pl.BlockSpec(shape, lambda b, i: (0, 0), pipeline_mode=pl.Buffered(1))
    return pl.pallas_call(
        _ffn_kernel,
        out_shape=jax.ShapeDtypeStruct((B, T, D_MODEL), F32),
        grid=(B, T // tm),
        in_specs=[tok, _row_spec(D_MODEL), _mod_spec(layer, 4, ctx), _mod_spec(layer, 3, ctx),
                  wspec((D_MODEL, D_FF)), wspec((D_MODEL, D_FF)), wspec((D_FF, D_MODEL)),
                  _row_spec(D_MODEL), _mod_spec(layer, 5, ctx)],
        out_specs=tok,
        compiler_params=_cparams(("parallel", "parallel"), VMEM_LIMIT),
        name="dense_ffn",
    )(x, gpre.reshape(1, D_MODEL), modr, modr, w1, w3, w2, npost.reshape(1, D_MODEL), modr)


def _route_kernel(x_ref, gpre_ref, sc_ref, sh_ref, rt_ref, h_ref, idx_ref, gate_ref, rank_ref, cnt_ref, run_ref):
    tm = x_ref.shape[0]
    first = (pl.program_id(0) == 0) & (pl.program_id(1) == 0)

    @pl.when(first)
    def _():
        run_ref[...] = jnp.zeros_like(run_ref)

    h = _rms(x_ref[...], gpre_ref[...]) * (1.0 + sc_ref[...]) + sh_ref[...]
    h_ref[...] = h
    logits = lax.dot_general(rt_ref[...], h, (((1,), (1,)), ((), ())), precision=lax.Precision.HIGHEST,
                             preferred_element_type=F32)
    eid = lax.broadcasted_iota(I32, (N_EXPERTS, tm), 0).astype(F32)
    m1 = jnp.max(logits, axis=0, keepdims=True)
    i1 = jnp.min(jnp.where(logits == m1, eid, float(N_EXPERTS)), axis=0, keepdims=True)
    oh1 = eid == i1
    rest = jnp.where(oh1, -jnp.inf, logits)
    m2 = jnp.max(rest, axis=0, keepdims=True)
    i2 = jnp.min(jnp.where(rest == m2, eid, float(N_EXPERTS)), axis=0, keepdims=True)
    oh2 = eid == i2
    e2 = jnp.exp(m2 - m1)
    gate_ref[0:1, :] = 1.0 / (1.0 + e2)
    gate_ref[1:2, :] = e2 / (1.0 + e2)
    idx_ref[0:1, :] = i1.astype(I32)
    idx_ref[1:2, :] = i2.astype(I32)
    upper = (lax.broadcasted_iota(I32, (tm, tm), 0) < lax.broadcasted_iota(I32, (tm, tm), 1))
    upper = jnp.where(upper, 1.0, 0.0).astype(BF16)
    f1 = jnp.where(oh1, 1.0, 0.0)
    f2 = jnp.where(oh2, 1.0, 0.0)
    before1 = jnp.dot(f1.astype(BF16), upper, preferred_element_type=F32)
    before2 = jnp.dot(f2.astype(BF16), upper, preferred_element_type=F32)
    cnt1 = jnp.sum(f1, axis=1, keepdims=True)
    cnt2 = jnp.sum(f2, axis=1, keepdims=True)
    run = run_ref[:, 0:1]
    rank_ref[0:1, :] = jnp.sum(f1 * (run + before1), axis=0, keepdims=True).astype(I32)
    rank_ref[1:2, :] = jnp.sum(f2 * (run + cnt1 + before2), axis=0, keepdims=True).astype(I32)
    run_new = run_ref[...] + cnt1 + cnt2
    run_ref[...] = run_new
    cnt_ref[...] = run_new


def _route(x, gpre, router_t, modr, layer):
    B, T, _ = x.shape
    tm = min(TOKEN_TILE, T)
    nt = T // tm
    tok = pl.BlockSpec((None, tm, D_MODEL), lambda b, i: (b, i, 0))
    lane = pl.BlockSpec((2, tm), lambda b, i: (0, b * nt + i))
    return pl.pallas_call(
        _route_kernel,
        out_shape=(jax.ShapeDtypeStruct((B, T, D_MODEL), F32),
                   jax.ShapeDtypeStruct((2, B * T), I32),
                   jax.ShapeDtypeStruct((2, B * T), F32),
                   jax.ShapeDtypeStruct((2, B * T), I32),
                   jax.ShapeDtypeStruct((N_EXPERTS, 128), F32)),
        grid=(B, nt),
        in_specs=[tok, _row_spec(D_MODEL), _mod_spec(layer, 4, False), _mod_spec(layer, 3, False),
                  pl.BlockSpec((N_EXPERTS, D_MODEL), lambda b, i: (0, 0))],
        out_specs=(tok, lane, lane, lane, pl.BlockSpec((N_EXPERTS, 128), lambda b, i: (0, 0))),
        scratch_shapes=[pltpu.VMEM((N_EXPERTS, 128), F32)],
        compiler_params=_cparams(("arbitrary", "arbitrary")),
        name="moe_route",
    )(x, gpre.reshape(1, D_MODEL), modr, modr, router_t)


def _dispatch_kernel(slot_ref, h_ref, hs_in_ref, hs_ref, sem):
    del hs_in_ref
    tm = h_ref.shape[0]

    def body(r, carry):
        for k in range(2):
            s = slot_ref[0, k, r]
            pltpu.make_async_copy(h_ref.at[pl.ds(r, 1)], hs_ref.at[pl.ds(s, 1)], sem).start()
        return carry

    lax.fori_loop(0, tm, body, 0)
    for _ in range(2):
        pltpu.make_async_copy(h_ref, hs_ref.at[pl.ds(0, tm)], sem).wait()


def _dispatch(h, slot_tiles, n_slots):
    N = h.shape[0]
    tm = slot_tiles.shape[2]
    return pl.pallas_call(
        _dispatch_kernel,
        out_shape=jax.ShapeDtypeStruct((n_slots, D_MODEL), F32),
        grid=(N // tm,),
        in_specs=[pl.BlockSpec((1, 2, tm), lambda i: (i, 0, 0), memory_space=pltpu.SMEM),
                  pl.BlockSpec((tm, D_MODEL), lambda i: (i, 0)),
                  pl.BlockSpec(memory_space=pl.ANY)],
        out_specs=pl.BlockSpec(memory_space=pl.ANY),
        scratch_shapes=[pltpu.SemaphoreType.DMA(())],
        input_output_aliases={2: 0},
        compiler_params=_cparams(("arbitrary",)),
        name="moe_dispatch",
    )(slot_tiles, h, jnp.zeros((n_slots, D_MODEL), F32))


def _expert_kernel(te_ref, tv_ref, h_ref, w1_ref, w3_ref, w2_ref, y_ref):
    t = pl.program_id(0)

    @pl.when(tv_ref[t] == 1)
    def _():
        y_ref[...] = _swiglu_tile(h_ref[...].astype(BF16), w1_ref, w3_ref, w2_ref)

    @pl.when(tv_ref[t] == 0)
    def _():
        y_ref[...] = jnp.zeros_like(y_ref)


def _experts(hs, tile_expert, tile_valid, w1, w3, w2):
    n_slots = hs.shape[0]
    tm = EXPERT_TILE
    grid_spec = pltpu.PrefetchScalarGridSpec(
        num_scalar_prefetch=2,
        grid=(n_slots // tm,),
        in_specs=[pl.BlockSpec((tm, D_MODEL), lambda t, te, tv: (t, 0)),
                  pl.BlockSpec((None, D_MODEL, D_FF), lambda t, te, tv: (te[t], 0, 0)),
                  pl.BlockSpec((None, D_MODEL, D_FF), lambda t, te, tv: (te[t], 0, 0)),
                  pl.BlockSpec((None, D_FF, D_MODEL), lambda t, te, tv: (te[t], 0, 0))],
        out_specs=pl.BlockSpec((tm, D_MODEL), lambda t, te, tv: (t, 0)),
    )
    return pl.pallas_call(
        _expert_kernel,
        out_shape=jax.ShapeDtypeStruct((n_slots, D_MODEL), F32),
        grid_spec=grid_spec,
        compiler_params=_cparams(("arbitrary",), VMEM_LIMIT),
        name="moe_experts",
    )(tile_expert, tile_valid, hs, w1, w3, w2)


def _combine_kernel(slot_ref, gate_ref, x_ref, npost_ref, g2_ref, ys_ref, o_ref, buf0, buf1, sem):
    tm = x_ref.shape[0]
    bufs = (buf0, buf1)

    def body(r, carry):
        for k in range(2):
            s = slot_ref[0, k, r]
            pltpu.make_async_copy(ys_ref.at[pl.ds(s, 1)], bufs[k].at[pl.ds(r, 1)], sem).start()
        return carry

    lax.fori_loop(0, tm, body, 0)
    for k in range(2):
        pltpu.make_async_copy(ys_ref.at[pl.ds(0, tm)], bufs[k], sem).wait()
    y = gate_ref[:, 0:1] * buf0[...] + gate_ref[:, 1:2] * buf1[...]
    o_ref[...] = x_ref[...] + g2_ref[...] * _rms(y, npost_ref[...])


def _combine(x, ys, slot_tiles, gate_tok, npost, modr, layer):
    B, T, _ = x.shape
    tm = slot_tiles.shape[2]
    nt = T // tm
    tok = pl.BlockSpec((None, tm, D_MODEL), lambda b, i: (b, i, 0))
    return pl.pallas_call(
        _combine_kernel,
        out_shape=jax.ShapeDtypeStruct((B, T, D_MODEL), F32),
        grid=(B, nt),
        in_specs=[pl.BlockSpec((1, 2, tm), lambda b, i: (b * nt + i, 0, 0), memory_space=pltpu.SMEM),
                  pl.BlockSpec((tm, 2), lambda b, i: (b * nt + i, 0)),
                  tok, _row_spec(D_MODEL), _mod_spec(layer, 5, False),
                  pl.BlockSpec(memory_space=pl.ANY)],
        out_specs=tok,
        scratch_shapes=[pltpu.VMEM((tm, D_MODEL), F32), pltpu.VMEM((tm, D_MODEL), F32),
                        pltpu.SemaphoreType.DMA(())],
        compiler_params=_cparams(("arbitrary", "arbitrary")),
        name="moe_combine",
    )(slot_tiles, gate_tok, x, npost.reshape(1, D_MODEL), modr, ys)


def _moe_ffn(x, gpre, npost, router, w1, w3, w2, modr, layer):
    B, T, _ = x.shape
    N = B * T
    tm = min(TOKEN_TILE, T)
    h, idx, gate, rank, cnt = _route(x, gpre, router.T, modr, layer)
    n_slots = 2 * N + N_EXPERTS * EXPERT_TILE
    n_tiles = n_slots // EXPERT_TILE
    counts = cnt[:, 0].astype(I32)
    padded = ((counts + EXPERT_TILE - 1) // EXPERT_TILE) * EXPERT_TILE
    ends = jnp.cumsum(padded)
    starts = ends - padded
    slot = rank
    for e in range(N_EXPERTS):
        slot = slot + jnp.where(idx == e, starts[e], 0)
    slot_tiles = slot.reshape(2, N // tm, tm).transpose(1, 0, 2)
    tile_start = jnp.arange(n_tiles, dtype=I32) * EXPERT_TILE
    tile_valid = (tile_start < ends[-1]).astype(I32)
    tile_expert = jnp.minimum(jnp.sum((tile_start[:, None] >= ends[None, :]).astype(I32), axis=1), N_EXPERTS - 1)
    last_expert = jnp.max(jnp.where(tile_valid == 1, tile_expert, 0))
    tile_expert = jnp.where(tile_valid == 1, tile_expert, last_expert)
    hs = _dispatch(h.reshape(N, D_MODEL), slot_tiles, n_slots)
    ys = _experts(hs, tile_expert, tile_valid, w1, w3, w2)
    return _combine(x, ys, slot_tiles, gate.T, npost, modr, layer)


def kernel(x, c, ctx, c_ctx, w_mod, b_mod, norm_pre_mix, norm_post_mix, norm_pre_ffn, norm_post_ffn, w_in, w_out, q_norm, k_norm, ret_decay_logit, pool_w, pool_scale, ffn_w1, ffn_w3, ffn_w2, moe_router, moe_w1, moe_w3, moe_w2):
    B, T, _ = x.shape
    cvec = jnp.zeros((MOD_ROWS, D_MODEL), F32).at[0:B].set(c).at[2].set(c_ctx)
    modr = _modulation(cvec, w_mod, b_mod)
    cos_t, sin_t = _rope_tables(T)
    avg = jnp.kron(jnp.eye(RET_HEADS, dtype=F32), jnp.full((HEAD_DIM, HEAD_DIM), 1.0 / HEAD_DIM, F32)).astype(BF16)
    zero_state = jnp.zeros((B, RET_WIDTH, RET_WIDTH), F32)
    xc = ctx
    for i in range(DEPTH):
        need_ctx = i < DEPTH - 1
        w_in_b = w_in[i].astype(BF16)
        w_out_b = w_out[i].astype(BF16)
        pool_bd = jax.scipy.linalg.block_diag(*[pool_w[i, g] for g in range(len(POOL_WINDOWS))]).astype(BF16)
        log_gamma = jax.nn.log_sigmoid(ret_decay_logit[i].astype(F32))

        qkv_c, ret_c, pp_c = _in_projection(xc, modr, norm_pre_mix[i], w_in_b, i, True)
        qt_c, kn_c, vt_c, st_c = _attn_prep(qkv_c, q_norm[i], k_norm[i], cos_t, sin_t, False)
        of_c, ob_c, s_fwd, s_bwd = _retention(ret_c, log_gamma, zero_state, zero_state)

        qkv_x, ret_x, pp_x = _in_projection(x, modr, norm_pre_mix[i], w_in_b, i, False)
        qt_x, kn_x, vt_x, st_x = _attn_prep(qkv_x, q_norm[i], k_norm[i], cos_t, sin_t, True)
        attn_x = _attention(_shift_plan(st_x, [st_x, st_c]), qt_x, kn_c, vt_c, kn_x, vt_x)
        of_x, ob_x, _, _ = _retention(ret_x, log_gamma, s_fwd, s_bwd)
        pool_x = _pooling(pp_x, pool_bd, pool_scale[i])
        x = _mix_out(x, attn_x, of_x, ob_x, ret_x, pool_x, avg, w_out_b, norm_post_mix[i], modr, i, False)

        if need_ctx:
            attn_c = _attention(_shift_plan(st_c, [st_c]), qt_c, kn_c, vt_c)
            pool_c = _pooling(pp_c, pool_bd, pool_scale[i])
            xc = _mix_out(xc, attn_c, of_c, ob_c, ret_c, pool_c, avg, w_out_b, norm_post_mix[i], modr, i, True)

        j = i // 2
        if i % 2 == 0:
            w1, w3, w2 = ffn_w1[j].astype(BF16), ffn_w3[j].astype(BF16), ffn_w2[j].astype(BF16)
            x = _dense_ffn(x, norm_pre_ffn[i], norm_post_ffn[i], w1, w3, w2, modr, i, False)
            if need_ctx:
                xc = _dense_ffn(xc, norm_pre_ffn[i], norm_post_ffn[i], w1, w3, w2, modr, i, True)
        else:
            x = _moe_ffn(x, norm_pre_ffn[i], norm_post_ffn[i], moe_router[j], moe_w1[j].astype(BF16),
                         moe_w3[j].astype(BF16), moe_w2[j].astype(BF16), modr, i)
    return x
```

```python
import functools

import jax
import jax.numpy as jnp
from jax import lax
from jax.experimental import pallas as pl
from jax.experimental.pallas import tpu as pltpu

F32 = jnp.float32
BF16 = jnp.bfloat16
I32 = jnp.int32

D_MODEL = 1024
GRID_W = 64
HEAD_DIM = 64
ATTN_WIDTH = 512
KV_HEADS = 2
ATTN_GROUP = 4
KV_WIDTH = 128
RET_WIDTH = 256
RET_HEADS = 4
POOL_WIDTH = 256
POOL_WINDOWS = (2, 4, 8, 16)
IN_WIDTH = 2048
RET_CHUNK = 128
ROPE_THETA = 10000.0
D_FF = 2816
N_EXPERTS = 8
NORM_EPS = 1e-6
DEPTH = 2

TOKEN_TILE = 512
ATTN_Q_TILE = 512
ATTN_K_CHUNK = 2048
RET_TILE = 512
POOL_TILE = 256
POOL_HALO = 16
FF_CHUNK = 256
EXPERT_TILE = 512
DMA_UNROLL = 8
BF16_SUBLANES = 16
K_COLS = 128
VT_ROWS = HEAD_DIM + BF16_SUBLANES
MAX_BOUND_SHIFT = 40.0
SHIFT_MARGIN = 1.01
MOD_ROWS = 8
VMEM_LIMIT = 56 * 1024 * 1024


def _cparams(sem, vmem=None):
    return pltpu.CompilerParams(dimension_semantics=sem, vmem_limit_bytes=vmem)


def _rms(x, gain):
    ms = jnp.mean(x * x, axis=-1, keepdims=True)
    return x * lax.rsqrt(ms + NORM_EPS) * gain


def _silu(x):
    return x * jax.nn.sigmoid(x)


def _mod_kernel(c_ref, w_ref, b_ref, o_ref):
    s = _silu(c_ref[...])
    o_ref[...] = jnp.dot(s, w_ref[...], precision=lax.Precision.HIGHEST,
                         preferred_element_type=F32) + b_ref[...]


def _modulation(cvec, w_mod, b_mod):
    nchunk = 6
    out = pl.pallas_call(
        _mod_kernel,
        out_shape=jax.ShapeDtypeStruct((DEPTH, MOD_ROWS, 6 * D_MODEL), F32),
        grid=(DEPTH, nchunk),
        in_specs=[
            pl.BlockSpec((MOD_ROWS, D_MODEL), lambda l, j: (0, 0)),
            pl.BlockSpec((None, D_MODEL, D_MODEL), lambda l, j: (l, 0, j)),
            pl.BlockSpec((None, 1, D_MODEL), lambda l, j: (l, 0, j)),
        ],
        out_specs=pl.BlockSpec((None, MOD_ROWS, D_MODEL), lambda l, j: (l, 0, j)),
        compiler_params=_cparams(("parallel", "parallel")),
        name="modulation",
    )(cvec, w_mod, b_mod.reshape(DEPTH, 1, 6 * D_MODEL))
    return out.reshape(DEPTH * MOD_ROWS * nchunk, 1, D_MODEL)


def _mod_spec(layer, chunk, ctx):
    base = layer * MOD_ROWS * 6
    if ctx:
        return pl.BlockSpec((None, 1, D_MODEL), lambda b, i: (base + 2 * 6 + chunk, 0, 0))
    return pl.BlockSpec((None, 1, D_MODEL), lambda b, i: (base + b * 6 + chunk, 0, 0))


def _row_spec(width):
    return pl.BlockSpec((1, width), lambda b, i: (0, 0))


def _inproj_kernel(x_ref, g_ref, sc_ref, sh_ref, w_ref, qkv_ref, ret_ref, pp_ref):
    h = _rms(x_ref[...], g_ref[...]) * (1.0 + sc_ref[...]) + sh_ref[...]
    p = jnp.dot(h.astype(BF16), w_ref[...], preferred_element_type=F32)
    qkv_ref[...] = p[:, :768].astype(BF16)
    ret_ref[...] = p[:, 768:1792].astype(BF16)
    pp_ref[...] = p[:, 1792:].astype(BF16)


def _in_projection(x, modr, gain, w_in_bf16, layer, ctx):
    B, T, _ = x.shape
    tm = min(TOKEN_TILE, T)
    tok = lambda w: pl.BlockSpec((None, tm, w), lambda b, i: (b, i, 0))
    return pl.pallas_call(
        _inproj_kernel,
        out_shape=(jax.ShapeDtypeStruct((B, T, 768), BF16),
                   jax.ShapeDtypeStruct((B, T, 1024), BF16),
                   jax.ShapeDtypeStruct((B, T, POOL_WIDTH), BF16)),
        grid=(B, T // tm),
        in_specs=[tok(D_MODEL), _row_spec(D_MODEL), _mod_spec(layer, 1, ctx), _mod_spec(layer, 0, ctx),
                  pl.BlockSpec((D_MODEL, IN_WIDTH), lambda b, i: (0, 0))],
        out_specs=(tok(768), tok(1024), tok(POOL_WIDTH)),
        compiler_params=_cparams(("parallel", "parallel"), VMEM_LIMIT),
        name="in_projection",
    )(x, gain.reshape(1, D_MODEL), modr, modr, w_in_bf16)


def _prep_kernel(qkv_ref, qg_ref, kg_ref, cos_ref, sin_ref, qt_ref, kn_ref, vt_ref, stat_ref, *, rope):
    t = qkv_ref[...].astype(F32).T

    def norm_rope(blk, gain):
        ms = jnp.mean(blk * blk, axis=0, keepdims=True)
        y = blk * lax.rsqrt(ms + NORM_EPS) * gain
        if rope:
            partner = jnp.concatenate([y[16:32], y[0:16], y[48:64], y[32:48]], axis=0)
            y = y * cos_ref[...] + partner * sin_ref[...]
        return y

    def max_sq_norm(y):
        n2 = jnp.sum(y * y, axis=0, keepdims=True)
        return jnp.broadcast_to(jnp.max(n2, axis=1, keepdims=True), (1, 128))

    q_stats = []
    for h in range(ATTN_WIDTH // HEAD_DIM):
        lo = h * HEAD_DIM
        q = norm_rope(t[lo:lo + HEAD_DIM], qg_ref[...]) * (HEAD_DIM ** -0.5)
        qt_ref[lo:lo + HEAD_DIM, :] = q.astype(BF16)
        q_stats.append(max_sq_norm(q))
    tm = t.shape[1]
    k_ones = jnp.where(lax.broadcasted_iota(I32, (HEAD_DIM, tm), 0) < 2, 1.0, 0.0)
    v_ones = jnp.where(lax.broadcasted_iota(I32, (VT_ROWS - HEAD_DIM, tm), 0) < 1, 1.0, 0.0)
    k_stats = []
    for kv in range(KV_HEADS):
        lo = ATTN_WIDTH + kv * HEAD_DIM
        k = norm_rope(t[lo:lo + HEAD_DIM], kg_ref[...])
        k_stats.append(max_sq_norm(k))
        kn_ref[kv] = jnp.concatenate([k, k_ones], axis=0).T.astype(BF16)
        lo = ATTN_WIDTH + KV_WIDTH + kv * HEAD_DIM
        vt_ref[kv, 0] = jnp.concatenate([t[lo:lo + HEAD_DIM], v_ones], axis=0).astype(BF16)
    q_group = [functools.reduce(jnp.maximum, q_stats[kv * ATTN_GROUP:(kv + 1) * ATTN_GROUP])
               for kv in range(KV_HEADS)]
    stat_ref[...] = jnp.concatenate(k_stats + q_group + [jnp.zeros((8 - 2 * KV_HEADS, 128), F32)], axis=0)


def _attn_prep(qkv, q_gain, k_gain, cos_t, sin_t, rope):
    B, T, _ = qkv.shape
    tm = min(ATTN_K_CHUNK, T)
    nc = T // tm
    return pl.pallas_call(
        functools.partial(_prep_kernel, rope=rope),
        out_shape=(jax.ShapeDtypeStruct((B, ATTN_WIDTH, T), BF16),
                   jax.ShapeDtypeStruct((B, KV_HEADS, T, K_COLS), BF16),
                   jax.ShapeDtypeStruct((B, KV_HEADS, nc, VT_ROWS, tm), BF16),
                   jax.ShapeDtypeStruct((B, nc, 8, 128), F32)),
        grid=(B, nc),
        in_specs=[pl.BlockSpec((None, tm, 768), lambda b, i: (b, i, 0)),
                  pl.BlockSpec((HEAD_DIM, 1), lambda b, i: (0, 0)),
                  pl.BlockSpec((HEAD_DIM, 1), lambda b, i: (0, 0)),
                  pl.BlockSpec((HEAD_DIM, tm), lambda b, i: (0, i)),
                  pl.BlockSpec((HEAD_DIM, tm), lambda b, i: (0, i))],
        out_specs=(pl.BlockSpec((None, ATTN_WIDTH, tm), lambda b, i: (b, 0, i)),
                   pl.BlockSpec((None, KV_HEADS, tm, K_COLS), lambda b, i: (b, 0, i, 0)),
                   pl.BlockSpec((None, KV_HEADS, 1, VT_ROWS, tm), lambda b, i: (b, 0, i, 0, 0)),
                   pl.BlockSpec((None, None, 8, 128), lambda b, i: (b, i, 0, 0))),
        compiler_params=_cparams(("parallel", "parallel")),
        name="attn_prep",
    )(qkv, q_gain.reshape(HEAD_DIM, 1), k_gain.reshape(HEAD_DIM, 1), cos_t, sin_t)


def _shift_plan(q_stats, k_stats_list):
    k2 = functools.reduce(jnp.maximum, [st[:, :, 0:KV_HEADS, 0].max(axis=1) for st in k_stats_list])
    q2 = q_stats[:, :, KV_HEADS:2 * KV_HEADS, 0].max(axis=1)
    use_bound = (jnp.sqrt(k2 * q2) * SHIFT_MARGIN <= MAX_BOUND_SHIFT).astype(I32).reshape(-1)
    return use_bound, (jnp.sqrt(k2) * SHIFT_MARGIN).reshape(-1)


def _rope_tables(T):
    t = jnp.arange(T)
    row = (t // GRID_W).astype(F32)
    col = (t % GRID_W).astype(F32)
    n_freq = HEAD_DIM // 4
    inv = ROPE_THETA ** (-jnp.arange(n_freq, dtype=F32) / n_freq)
    ang_r = row[None, :] * inv[:, None]
    ang_c = col[None, :] * inv[:, None]
    cos_t = jnp.concatenate([jnp.cos(ang_r), jnp.cos(ang_r), jnp.cos(ang_c), jnp.cos(ang_c)], axis=0)
    sin_t = jnp.concatenate([-jnp.sin(ang_r), jnp.sin(ang_r), -jnp.sin(ang_c), jnp.sin(ang_c)], axis=0)
    return cos_t, sin_t


def _attn_kernel(*refs, n_x_chunks, tk):
    if n_x_chunks:
        use_bound_ref, kmax_ref, qt_ref, kc_ref, vtc_ref, kx_ref, vtx_ref, o_ref, qa_ref, acc_ref = refs
    else:
        use_bound_ref, kmax_ref, qt_ref, kc_ref, vtc_ref, o_ref, qa_ref, acc_ref = refs
    tq = qt_ref.shape[1]
    nq = ATTN_GROUP * tq
    bh = pl.program_id(0) * KV_HEADS + pl.program_id(1)

    for g in range(ATTN_GROUP):
        qa_ref[0:HEAD_DIM, g * tq:(g + 1) * tq] = qt_ref[g * HEAD_DIM:(g + 1) * HEAD_DIM, :]
    qa_ref[HEAD_DIM:K_COLS, :] = jnp.zeros((K_COLS - HEAD_DIM, nq), BF16)

    def scores(k):
        return jnp.dot(k, qa_ref[...], preferred_element_type=F32)

    def colmax8(s):
        return jnp.max(s.reshape(s.shape[0] // 8, 8, nq), axis=0)

    def set_shift(m):
        m_hi = m.astype(BF16).astype(F32)
        m_lo = m - m_hi
        row = lax.broadcasted_iota(I32, (BF16_SUBLANES, nq), 0)
        qa_ref[HEAD_DIM:HEAD_DIM + BF16_SUBLANES, :] = jnp.where(
            row == 0, -m_hi, jnp.where(row == 1, -m_lo, 0.0)).astype(BF16)

    kc = kc_ref[...]

    @pl.when(use_bound_ref[bh] == 1)
    def _():
        q = qa_ref[0:HEAD_DIM, :].astype(F32)
        set_shift(jnp.sqrt(jnp.sum(q * q, axis=0, keepdims=True)) * kmax_ref[bh])

    @pl.when(use_bound_ref[bh] == 0)
    def _():
        mx = colmax8(scores(kc))
        if n_x_chunks:
            def max_body(j, mx):
                k = kx_ref[pl.ds(pl.multiple_of(j * tk, tk), tk), :]
                return jnp.maximum(mx, colmax8(scores(k)))

            mx = lax.fori_loop(0, n_x_chunks, max_body, mx)
        set_shift(jnp.max(mx, axis=0, keepdims=True))

    def weighted(k, vt):
        p = jnp.exp(scores(k)).astype(BF16)
        return jnp.dot(vt, p, preferred_element_type=F32)

    acc_ref[...] = weighted(kc, vtc_ref[0])
    if n_x_chunks:
        def sum_body(j, carry):
            k = kx_ref[pl.ds(pl.multiple_of(j * tk, tk), tk), :]
            acc_ref[...] += weighted(k, vtx_ref[j])
            return carry

        lax.fori_loop(0, n_x_chunks, sum_body, 0)

    o = acc_ref[0:HEAD_DIM, :] / acc_ref[HEAD_DIM:HEAD_DIM + 1, :]
    o = jnp.concatenate([o[:, g * tq:(g + 1) * tq] for g in range(ATTN_GROUP)], axis=0)
    o_ref[...] = o.T.astype(BF16)


def _attention(plan, qt, kc, vtc, kx=None, vtx=None):
    B, _, Tq = qt.shape
    Tc = kc.shape[2]
    tq = min(ATTN_Q_TILE, Tq)
    in_specs = [pl.BlockSpec((None, ATTN_GROUP * HEAD_DIM, tq), lambda b, h, i, *_: (b, h, i)),
                pl.BlockSpec((None, None, Tc, K_COLS), lambda b, h, i, *_: (b, h, 0, 0)),
                pl.BlockSpec((None, None, 1, VT_ROWS, Tc), lambda b, h, i, *_: (b, h, 0, 0, 0))]
    args = [qt, kc, vtc]
    n_x_chunks, tk = 0, 0
    if kx is not None:
        _, _, n_x_chunks, _, tk = vtx.shape
        T = kx.shape[2]
        in_specs += [pl.BlockSpec((None, None, T, K_COLS), lambda b, h, i, *_: (b, h, 0, 0)),
                     pl.BlockSpec((None, None, n_x_chunks, VT_ROWS, tk), lambda b, h, i, *_: (b, h, 0, 0, 0))]
        args += [kx, vtx]
    scratch = [pltpu.VMEM((K_COLS, ATTN_GROUP * tq), BF16), pltpu.VMEM((VT_ROWS, ATTN_GROUP * tq), F32)]
    grid_spec = pltpu.PrefetchScalarGridSpec(
        num_scalar_prefetch=2,
        grid=(B, KV_HEADS, Tq // tq),
        in_specs=in_specs,
        out_specs=pl.BlockSpec((None, tq, ATTN_GROUP * HEAD_DIM), lambda b, h, i, *_: (b, i, h)),
        scratch_shapes=scratch,
    )
    return pl.pallas_call(
        functools.partial(_attn_kernel, n_x_chunks=n_x_chunks, tk=tk),
        out_shape=jax.ShapeDtypeStruct((B, Tq, ATTN_WIDTH), BF16),
        grid_spec=grid_spec,
        compiler_params=_cparams(("parallel", "parallel", "parallel"), VMEM_LIMIT),
        name="attention",
    )(*plan, *args)


def _ret_kernel(lgc_f_ref, lgc_b_ref, lgr_f_ref, lgr_b_ref, s0f_ref, s0b_ref, blk_f_ref, blk_b_ref,
                of_ref, ob_ref, sf_ref, sb_ref, st_f, st_b, dec_f, dec_b):
    C = RET_CHUNK
    W = RET_WIDTH
    n = pl.program_id(1)

    @pl.when(n == 0)
    def _():
        st_f[...] = s0f_ref[...]
        st_b[...] = s0b_ref[...]
        c = jnp.bitwise_and(lax.broadcasted_iota(I32, (RET_HEADS * C, C), 0), C - 1)
        m = lax.broadcasted_iota(I32, (RET_HEADS * C, C), 1)
        diff = (c - m).astype(F32)
        dec_f[...] = jnp.where(diff >= 0, jnp.exp(lgc_f_ref[...] * jnp.maximum(diff, 0.0)), 0.0)
        dec_b[...] = jnp.where(diff <= 0, jnp.exp(lgc_b_ref[...] * jnp.maximum(-diff, 0.0)), 0.0)

    lane_head = jnp.right_shift(lax.broadcasted_iota(I32, (C, W), 1), 6)
    pos = lax.broadcasted_iota(I32, (C, W), 0).astype(F32)
    same_head = (jnp.right_shift(lax.broadcasted_iota(I32, (W, W), 0), 6)
                 == jnp.right_shift(lax.broadcasted_iota(I32, (W, W), 1), 6))

    def direction(blk_ref, dec_ref, st_ref, lgr, forward, out_ref):
        n_sub = blk_ref.shape[0] // C
        state = st_ref[...]
        for sub in (range(n_sub) if forward else reversed(range(n_sub))):
            state = chunk(blk_ref, dec_ref, state, lgr, forward, out_ref, sub * C)
        st_ref[...] = state

    def chunk(blk_ref, dec_ref, state, lgr, forward, out_ref, r0):
        q = blk_ref[r0:r0 + C, 0:W].astype(F32)
        kf = blk_ref[r0:r0 + C, W:2 * W].astype(F32) * (HEAD_DIM ** -0.5)
        v = blk_ref[r0:r0 + C, 2 * W:3 * W]
        if forward:
            zeta = jnp.exp(lgr * (C - 1.0 - pos))
            xi = jnp.exp(lgr * (pos + 1.0))
        else:
            zeta = jnp.exp(lgr * pos)
            xi = jnp.exp(lgr * (C - pos))
        qexp = jnp.concatenate([jnp.where(lane_head == h, q, 0.0) for h in range(RET_HEADS)],
                               axis=0).astype(BF16)
        a = lax.dot_general(qexp, kf.astype(BF16), (((1,), (1,)), ((), ())),
                            preferred_element_type=F32)
        p = (a * dec_ref[...]).astype(BF16)
        full = jnp.dot(p, v, preferred_element_type=F32)
        intra = jnp.zeros((C, W), F32)
        for h in range(RET_HEADS):
            intra = intra + jnp.where(lane_head == h, full[h * C:(h + 1) * C], 0.0)
        cross = jnp.dot((q * xi).astype(BF16), state.astype(BF16), preferred_element_type=F32)
        out_ref[r0:r0 + C, :] = intra + cross
        upd = lax.dot_general((kf * zeta).astype(BF16), v, (((0,), (0,)), ((), ())),
                              preferred_element_type=F32)
        return jnp.where(same_head, state * jnp.exp(lgr * float(C)) + upd, 0.0)

    direction(blk_f_ref, dec_f, st_f, lgr_f_ref[...], True, of_ref)
    direction(blk_b_ref, dec_b, st_b, lgr_b_ref[...], False, ob_ref)

    @pl.when(n == pl.num_programs(1) - 1)
    def _():
        sf_ref[...] = st_f[...]
        sb_ref[...] = st_b[...]


def _retention(ret, log_gamma, s0f, s0b):
    B, T, _ = ret.shape
    C = RET_CHUNK
    tm = min(RET_TILE, T)
    nc = T // tm
    lgc = lambda d: jnp.repeat(log_gamma[d], C).reshape(RET_HEADS * C, 1)
    lgr = lambda d: jnp.repeat(log_gamma[d], HEAD_DIM).reshape(1, RET_WIDTH)
    const = lambda shape: pl.BlockSpec(shape, lambda b, n: (0,) * len(shape))
    st_spec = pl.BlockSpec((None, RET_WIDTH, RET_WIDTH), lambda b, n: (b, 0, 0))
    return pl.pallas_call(
        _ret_kernel,
        out_shape=(jax.ShapeDtypeStruct((B, T, RET_WIDTH), F32),
                   jax.ShapeDtypeStruct((B, T, RET_WIDTH), F32),
                   jax.ShapeDtypeStruct((B, RET_WIDTH, RET_WIDTH), F32),
                   jax.ShapeDtypeStruct((B, RET_WIDTH, RET_WIDTH), F32)),
        grid=(B, nc),
        in_specs=[const((RET_HEADS * C, 1)), const((RET_HEADS * C, 1)),
                  const((1, RET_WIDTH)), const((1, RET_WIDTH)), st_spec, st_spec,
                  pl.BlockSpec((None, tm, 1024), lambda b, n: (b, n, 0)),
                  pl.BlockSpec((None, tm, 1024), lambda b, n: (b, nc - 1 - n, 0))],
        out_specs=(pl.BlockSpec((None, tm, RET_WIDTH), lambda b, n: (b, n, 0)),
                   pl.BlockSpec((None, tm, RET_WIDTH), lambda b, n: (b, nc - 1 - n, 0)),
                   st_spec, st_spec),
        scratch_shapes=[pltpu.VMEM((RET_WIDTH, RET_WIDTH), F32), pltpu.VMEM((RET_WIDTH, RET_WIDTH), F32),
                        pltpu.VMEM((RET_HEADS * C, C), F32), pltpu.VMEM((RET_HEADS * C, C), F32)],
        compiler_params=_cparams(("parallel", "arbitrary")),
        name="retention",
    )(lgc(0), lgc(1), lgr(0), lgr(1), s0f, s0b, ret, ret)


def _pool_kernel(prev_ref, cur_ref, next_ref, w_ref, scale_ref, o_ref, *, seq_len):
    tm = cur_ref.shape[0]
    ext = jnp.concatenate([prev_ref[...], cur_ref[...], next_ref[...]], axis=0)
    t0 = pl.program_id(1) * tm
    tok = t0 + lax.broadcasted_iota(I32, (tm, tm + 2 * POOL_HALO), 0)
    src = t0 - POOL_HALO + lax.broadcasted_iota(I32, (tm, tm + 2 * POOL_HALO), 1)
    tcol = t0 + lax.broadcasted_iota(I32, (tm, 1), 0)
    lane_group = jnp.right_shift(lax.broadcasted_iota(I32, (tm, POOL_WIDTH), 1), 6)
    cur = cur_ref[...].astype(F32)
    mixed = jnp.zeros((tm, POOL_WIDTH), F32)
    for gi, w in enumerate(POOL_WINDOWS):
        lo = jnp.maximum(tok - w // 2, 0)
        hi = jnp.minimum(tok + w // 2, seq_len)
        band = jnp.where((src >= lo) & (src < hi), 1.0, 0.0).astype(BF16)
        total = jnp.dot(band, ext, preferred_element_type=F32)
        cnt = (jnp.minimum(tcol + w // 2, seq_len) - jnp.maximum(tcol - w // 2, 0)).astype(F32)
        mixed = mixed + jnp.where(lane_group == gi, total / cnt - cur, 0.0)
    y = jnp.dot(mixed.astype(BF16), w_ref[...], preferred_element_type=F32)
    o_ref[...] = (y * scale_ref[...]).astype(BF16)


def _pooling(pp, w_blockdiag_bf16, scale):
    B, T, _ = pp.shape
    tm = min(POOL_TILE, T)
    r = tm // POOL_HALO
    last = T // POOL_HALO - 1
    return pl.pallas_call(
        functools.partial(_pool_kernel, seq_len=T),
        out_shape=jax.ShapeDtypeStruct((B, T, POOL_WIDTH), BF16),
        grid=(B, T // tm),
        in_specs=[pl.BlockSpec((None, POOL_HALO, POOL_WIDTH), lambda b, i: (b, jnp.maximum(i * r - 1, 0), 0)),
                  pl.BlockSpec((None, tm, POOL_WIDTH), lambda b, i: (b, i, 0)),
                  pl.BlockSpec((None, POOL_HALO, POOL_WIDTH), lambda b, i: (b, jnp.minimum((i + 1) * r, last), 0)),
                  pl.BlockSpec((POOL_WIDTH, POOL_WIDTH), lambda b, i: (0, 0)),
                  _row_spec(POOL_WIDTH)],
        out_specs=pl.BlockSpec((None, tm, POOL_WIDTH), lambda b, i: (b, i, 0)),
        compiler_params=_cparams(("parallel", "parallel")),
        name="pooling",
    )(pp, pp, pp, w_blockdiag_bf16, scale.reshape(1, POOL_WIDTH))


def _head_mean(x, avg):
    hi = x.astype(BF16)
    lo = (x - hi.astype(F32)).astype(BF16)
    return (jnp.dot(hi, avg, preferred_element_type=F32) + jnp.dot(lo, avg, preferred_element_type=F32))


def _mixout_kernel(x_ref, attn_ref, of_ref, ob_ref, gate_ref, pool_ref, avg_ref, w_ref, npost_ref, g1_ref, o_ref):
    o = of_ref[...] + ob_ref[...]
    avg = avg_ref[...]
    mu = _head_mean(o, avg)
    cen = o - mu
    var = _head_mean(cen * cen, avg)
    y_ret = (_silu(gate_ref[...].astype(F32)) * (cen * lax.rsqrt(var + NORM_EPS))).astype(BF16)
    mx = (jnp.dot(attn_ref[...], w_ref[0:ATTN_WIDTH, :], preferred_element_type=F32)
          + jnp.dot(y_ret, w_ref[ATTN_WIDTH:ATTN_WIDTH + RET_WIDTH, :], preferred_element_type=F32)
          + jnp.dot(pool_ref[...], w_ref[ATTN_WIDTH + RET_WIDTH:, :], preferred_element_type=F32))
    o_ref[...] = x_ref[...] + g1_ref[...] * _rms(mx, npost_ref[...])


def _mix_out(x, attn, of, ob, ret, pool, avg_bf16, w_out_bf16, npost, modr, layer, ctx):
    B, T, _ = x.shape
    tm = min(TOKEN_TILE, T)
    tok = lambda w: pl.BlockSpec((None, tm, w), lambda b, i: (b, i, 0))
    return pl.pallas_call(
        _mixout_kernel,
        out_shape=jax.ShapeDtypeStruct((B, T, D_MODEL), F32),
        grid=(B, T // tm),
        in_specs=[tok(D_MODEL), tok(ATTN_WIDTH), tok(RET_WIDTH), tok(RET_WIDTH),
                  pl.BlockSpec((None, tm, RET_WIDTH), lambda b, i: (b, i, 3)),
                  tok(POOL_WIDTH),
                  pl.BlockSpec((RET_WIDTH, RET_WIDTH), lambda b, i: (0, 0)),
                  pl.BlockSpec((D_MODEL, D_MODEL), lambda b, i: (0, 0)),
                  _row_spec(D_MODEL), _mod_spec(layer, 2, ctx)],
        out_specs=tok(D_MODEL),
        compiler_params=_cparams(("parallel", "parallel"), VMEM_LIMIT),
        name="mix_out",
    )(x, attn, of, ob, ret, pool, avg_bf16, w_out_bf16, npost.reshape(1, D_MODEL), modr)


def _swiglu_tile(h, w1_ref, w3_ref, w2_ref):
    acc = jnp.zeros((h.shape[0], D_MODEL), F32)
    for f in range(0, D_FF, FF_CHUNK):
        a = jnp.dot(h, w1_ref[:, f:f + FF_CHUNK], preferred_element_type=F32)
        b = jnp.dot(h, w3_ref[:, f:f + FF_CHUNK], preferred_element_type=F32)
        u = (_silu(a) * b).astype(BF16)
        acc = acc + jnp.dot(u, w2_ref[f:f + FF_CHUNK, :], preferred_element_type=F32)
    return acc


def _ffn_kernel(x_ref, gpre_ref, sc_ref, sh_ref, w1_ref, w3_ref, w2_ref, npost_ref, g2_ref, o_ref):
    x = x_ref[...]
    h = (_rms(x, gpre_ref[...]) * (1.0 + sc_ref[...]) + sh_ref[...]).astype(BF16)
    y = _swiglu_tile(h, w1_ref, w3_ref, w2_ref)
    o_ref[...] = x + g2_ref[...] * _rms(y, npost_ref[...])


def _dense_ffn(x, gpre, npost, w1, w3, w2, modr, layer, ctx):
    B, T, _ = x.shape
    tm = min(TOKEN_TILE, T)
    tok = pl.BlockSpec((None, tm, D_MODEL), lambda b, i: (b, i, 0))
    wspec = lambda shape: pl.BlockSpec(shape, lambda b, i: (0, 0), pipeline_mode=pl.Buffered(1))
    return pl.pallas_call(
        _ffn_kernel,
        out_shape=jax.ShapeDtypeStruct((B, T, D_MODEL), F32),
        grid=(B, T // tm),
        in_specs=[tok, _row_spec(D_MODEL), _mod_spec(layer, 4, ctx), _mod_spec(layer, 3, ctx),
                  wspec((D_MODEL, D_FF)), wspec((D_MODEL, D_FF)), wspec((D_FF, D_MODEL)),
                  _row_spec(D_MODEL), _mod_spec(layer, 5, ctx)],
        out_specs=tok,
        compiler_params=_cparams(("parallel", "parallel"), VMEM_LIMIT),
        name="dense_ffn",
    )(x, gpre.reshape(1, D_MODEL), modr, modr, w1, w3, w2, npost.reshape(1, D_MODEL), modr)


def _route_kernel(x_ref, gpre_ref, sc_ref, sh_ref, rt_ref, h_ref, idx_ref, gate_ref, rank_ref, cnt_ref, run_ref):
    tm = x_ref.shape[0]
    first = (pl.program_id(0) == 0) & (pl.program_id(1) == 0)

    @pl.when(first)
    def _():
        run_ref[...] = jnp.zeros_like(run_ref)

    h = _rms(x_ref[...], gpre_ref[...]) * (1.0 + sc_ref[...]) + sh_ref[...]
    h_ref[...] = h
    logits = lax.dot_general(rt_ref[...], h, (((1,), (1,)), ((), ())), precision=lax.Precision.HIGHEST,
                             preferred_element_type=F32)
    eid = lax.broadcasted_iota(I32, (N_EXPERTS, tm), 0).astype(F32)
    m1 = jnp.max(logits, axis=0, keepdims=True)
    i1 = jnp.min(jnp.where(logits == m1, eid, float(N_EXPERTS)), axis=0, keepdims=True)
    oh1 = eid == i1
    rest = jnp.where(oh1, -jnp.inf, logits)
    m2 = jnp.max(rest, axis=0, keepdims=True)
    i2 = jnp.min(jnp.where(rest == m2, eid, float(N_EXPERTS)), axis=0, keepdims=True)
    oh2 = eid == i2
    e2 = jnp.exp(m2 - m1)
    gate_ref[0:1, :] = 1.0 / (1.0 + e2)
    gate_ref[1:2, :] = e2 / (1.0 + e2)
    idx_ref[0:1, :] = i1.astype(I32)
    idx_ref[1:2, :] = i2.astype(I32)
    upper = (lax.broadcasted_iota(I32, (tm, tm), 0) < lax.broadcasted_iota(I32, (tm, tm), 1))
    upper = jnp.where(upper, 1.0, 0.0).astype(BF16)
    f1 = jnp.where(oh1, 1.0, 0.0)
    f2 = jnp.where(oh2, 1.0, 0.0)
    before1 = jnp.dot(f1.astype(BF16), upper, preferred_element_type=F32)
    before2 = jnp.dot(f2.astype(BF16), upper, preferred_element_type=F32)
    cnt1 = jnp.sum(f1, axis=1, keepdims=True)
    cnt2 = jnp.sum(f2, axis=1, keepdims=True)
    run = run_ref[:, 0:1]
    rank_ref[0:1, :] = jnp.sum(f1 * (run + before1), axis=0, keepdims=True).astype(I32)
    rank_ref[1:2, :] = jnp.sum(f2 * (run + cnt1 + before2), axis=0, keepdims=True).astype(I32)
    run_new = run_ref[...] + cnt1 + cnt2
    run_ref[...] = run_new
    cnt_ref[...] = run_new


def _route(x, gpre, router_t, modr, layer):
    B, T, _ = x.shape
    tm = min(TOKEN_TILE, T)
    nt = T // tm
    tok = pl.BlockSpec((None, tm, D_MODEL), lambda b, i: (b, i, 0))
    lane = pl.BlockSpec((2, tm), lambda b, i: (0, b * nt + i))
    return pl.pallas_call(
        _route_kernel,
        out_shape=(jax.ShapeDtypeStruct((B, T, D_MODEL), F32),
                   jax.ShapeDtypeStruct((2, B * T), I32),
                   jax.ShapeDtypeStruct((2, B * T), F32),
                   jax.ShapeDtypeStruct((2, B * T), I32),
                   jax.ShapeDtypeStruct((N_EXPERTS, 128), F32)),
        grid=(B, nt),
        in_specs=[tok, _row_spec(D_MODEL), _mod_spec(layer, 4, False), _mod_spec(layer, 3, False),
                  pl.BlockSpec((N_EXPERTS, D_MODEL), lambda b, i: (0, 0))],
        out_specs=(tok, lane, lane, lane,
                   pl.BlockSpec((N_EXPERTS, 128), lambda b, i: (0, 0))),
        scratch_shapes=[pltpu.VMEM((N_EXPERTS, 128), F32)],
        compiler_params=_cparams(("arbitrary", "arbitrary")),
        name="moe_route",
    )(x, gpre.reshape(1, D_MODEL), modr, modr, router_t)


def _dispatch_kernel(zero_tiles_ref, slot_ref, h_ref, hs_ref, zero_ref, sem, zero_sem):
    tm = h_ref.shape[0]

    @pl.when(pl.program_id(0) == 0)
    def _():
        zero_ref[...] = jnp.zeros_like(zero_ref)
        for j in range(2 * N_EXPERTS):
            start = pl.multiple_of(zero_tiles_ref[j] * EXPERT_TILE, EXPERT_TILE)
            fill = pltpu.make_async_copy(zero_ref, hs_ref.at[pl.ds(start, EXPERT_TILE)], zero_sem)
            fill.start()
            fill.wait()

    def body(i, carry):
        for u in range(DMA_UNROLL):
            r = i * DMA_UNROLL + u
            for k in range(2):
                s = slot_ref[0, k, r]
                pltpu.make_async_copy(h_ref.at[pl.ds(r, 1)], hs_ref.at[pl.ds(s, 1)], sem).start(priority=k)
        return carry

    lax.fori_loop(0, tm // DMA_UNROLL, body, 0)
    for _ in range(2):
        pltpu.make_async_copy(h_ref, hs_ref.at[pl.ds(0, tm)], sem).wait()


def _dispatch(h, slot_tiles, zero_tiles, n_slots):
    N = h.shape[0]
    tm = slot_tiles.shape[2]
    grid_spec = pltpu.PrefetchScalarGridSpec(
        num_scalar_prefetch=1,
        grid=(N // tm,),
        in_specs=[pl.BlockSpec((1, 2, tm), lambda i, zt: (i, 0, 0), memory_space=pltpu.SMEM),
                  pl.BlockSpec((tm, D_MODEL), lambda i, zt: (i, 0))],
        out_specs=pl.BlockSpec(memory_space=pl.ANY),
        scratch_shapes=[pltpu.VMEM((EXPERT_TILE, D_MODEL), F32),
                        pltpu.SemaphoreType.DMA(()), pltpu.SemaphoreType.DMA(())],
    )
    return pl.pallas_call(
        _dispatch_kernel,
        out_shape=jax.ShapeDtypeStruct((n_slots, D_MODEL), F32),
        grid_spec=grid_spec,
        compiler_params=_cparams(("arbitrary",)),
        name="moe_dispatch",
    )(zero_tiles, slot_tiles, h)


def _expert_kernel(te_ref, tv_ref, h_ref, w1_ref, w3_ref, w2_ref, y_ref):
    t = pl.program_id(0)

    @pl.when(tv_ref[t] == 1)
    def _():
        y_ref[...] = _swiglu_tile(h_ref[...].astype(BF16), w1_ref, w3_ref, w2_ref)

    @pl.when(tv_ref[t] == 0)
    def _():
        y_ref[...] = jnp.zeros_like(y_ref)


def _experts(hs, tile_expert, tile_valid, w1, w3, w2):
    n_slots = hs.shape[0]
    tm = EXPERT_TILE
    grid_spec = pltpu.PrefetchScalarGridSpec(
        num_scalar_prefetch=2,
        grid=(n_slots // tm,),
        in_specs=[pl.BlockSpec((tm, D_MODEL), lambda t, te, tv: (t, 0)),
                  pl.BlockSpec((None, D_MODEL, D_FF), lambda t, te, tv: (te[t], 0, 0)),
                  pl.BlockSpec((None, D_MODEL, D_FF), lambda t, te, tv: (te[t], 0, 0)),
                  pl.BlockSpec((None, D_FF, D_MODEL), lambda t, te, tv: (te[t], 0, 0))],
        out_specs=pl.BlockSpec((tm, D_MODEL), lambda t, te, tv: (t, 0)),
    )
    return pl.pallas_call(
        _expert_kernel,
        out_shape=jax.ShapeDtypeStruct((n_slots, D_MODEL), F32),
        grid_spec=grid_spec,
        compiler_params=_cparams(("arbitrary",), VMEM_LIMIT),
        name="moe_experts",
    )(tile_expert, tile_valid, hs, w1, w3, w2)


def _combine_kernel(slot_ref, gate_ref, x_ref, npost_ref, g2_ref, ys_ref, o_ref, buf0, buf1, sem):
    tm = x_ref.shape[0]
    bufs = (buf0, buf1)

    def body(i, carry):
        for u in range(DMA_UNROLL):
            r = i * DMA_UNROLL + u
            for k in range(2):
                s = slot_ref[0, k, r]
                pltpu.make_async_copy(ys_ref.at[pl.ds(s, 1)], bufs[k].at[pl.ds(r, 1)], sem).start(priority=k)
        return carry

    lax.fori_loop(0, tm // DMA_UNROLL, body, 0)
    for k in range(2):
        pltpu.make_async_copy(ys_ref.at[pl.ds(0, tm)], bufs[k], sem).wait()
    y = gate_ref[:, 0:1] * buf0[...] + gate_ref[:, 1:2] * buf1[...]
    o_ref[...] = x_ref[...] + g2_ref[...] * _rms(y, npost_ref[...])


def _combine(x, ys, slot_tiles, gate_tok, npost, modr, layer):
    B, T, _ = x.shape
    tm = slot_tiles.shape[2]
    nt = T // tm
    tok = pl.BlockSpec((None, tm, D_MODEL), lambda b, i: (b, i, 0))
    return pl.pallas_call(
        _combine_kernel,
        out_shape=jax.ShapeDtypeStruct((B, T, D_MODEL), F32),
        grid=(B, nt),
        in_specs=[pl.BlockSpec((1, 2, tm), lambda b, i: (b * nt + i, 0, 0), memory_space=pltpu.SMEM),
                  pl.BlockSpec((tm, 2), lambda b, i: (b * nt + i, 0)),
                  tok, _row_spec(D_MODEL), _mod_spec(layer, 5, False),
                  pl.BlockSpec(memory_space=pl.ANY)],
        out_specs=tok,
        scratch_shapes=[pltpu.VMEM((tm, D_MODEL), F32), pltpu.VMEM((tm, D_MODEL), F32),
                        pltpu.SemaphoreType.DMA(())],
        compiler_params=_cparams(("arbitrary", "arbitrary")),
        name="moe_combine",
    )(slot_tiles, gate_tok, x, npost.reshape(1, D_MODEL), modr, ys)


def _moe_ffn(x, gpre, npost, router, w1, w3, w2, modr, layer):
    B, T, _ = x.shape
    N = B * T
    tm = min(TOKEN_TILE, T)
    h, idx, gate, rank, cnt = _route(x, gpre, router.T, modr, layer)
    n_slots = 2 * N + N_EXPERTS * EXPERT_TILE
    n_tiles = n_slots // EXPERT_TILE
    counts = cnt[:, 0].astype(I32)
    padded = ((counts + EXPERT_TILE - 1) // EXPERT_TILE) * EXPERT_TILE
    ends = jnp.cumsum(padded)
    starts = ends - padded
    slot = rank
    for e in range(N_EXPERTS):
        slot = slot + jnp.where(idx == e, starts[e], 0)
    slot_tiles = slot.reshape(2, N // tm, tm).transpose(1, 0, 2)
    tile_start = jnp.arange(n_tiles, dtype=I32) * EXPERT_TILE
    tile_valid = (tile_start < ends[-1]).astype(I32)
    tile_expert = jnp.minimum(jnp.sum((tile_start[:, None] >= ends[None, :]).astype(I32), axis=1), N_EXPERTS - 1)
    last_expert = jnp.max(jnp.where(tile_valid == 1, tile_expert, 0))
    tile_expert = jnp.where(tile_valid == 1, tile_expert, last_expert)
    last_tile = jnp.where(padded > 0, ends // EXPERT_TILE - 1, n_tiles - 1)
    tail_tile = jnp.minimum(ends[-1] // EXPERT_TILE + jnp.arange(N_EXPERTS, dtype=I32), n_tiles - 1)
    zero_tiles = jnp.concatenate([last_tile, tail_tile]).astype(I32)
    hs = _dispatch(h.reshape(N, D_MODEL), slot_tiles, zero_tiles, n_slots)
    ys = _experts(hs, tile_expert, tile_valid, w1, w3, w2)
    return _combine(x, ys, slot_tiles, gate.T, npost, modr, layer)


def kernel(x, c, ctx, c_ctx, w_mod, b_mod, norm_pre_mix, norm_post_mix, norm_pre_ffn, norm_post_ffn, w_in, w_out, q_norm, k_norm, ret_decay_logit, pool_w, pool_scale, ffn_w1, ffn_w3, ffn_w2, moe_router, moe_w1, moe_w3, moe_w2):
    B, T, _ = x.shape
    cvec = jnp.zeros((MOD_ROWS, D_MODEL), F32).at[0:B].set(c).at[2].set(c_ctx)
    modr = _modulation(cvec, w_mod, b_mod)
    cos_t, sin_t = _rope_tables(T)
    avg = jnp.kron(jnp.eye(RET_HEADS, dtype=F32), jnp.full((HEAD_DIM, HEAD_DIM), 1.0 / HEAD_DIM, F32)).astype(BF16)
    zero_state = jnp.zeros((B, RET_WIDTH, RET_WIDTH), F32)
    xc = ctx
    for i in range(DEPTH):
        need_ctx = i < DEPTH - 1
        w_in_b = w_in[i].astype(BF16)
        w_out_b = w_out[i].astype(BF16)
        pool_bd = jax.scipy.linalg.block_diag(*[pool_w[i, g] for g in range(len(POOL_WINDOWS))]).astype(BF16)
        log_gamma = jax.nn.log_sigmoid(ret_decay_logit[i].astype(F32))

        qkv_c, ret_c, pp_c = _in_projection(xc, modr, norm_pre_mix[i], w_in_b, i, True)
        qt_c, kn_c, vt_c, st_c = _attn_prep(qkv_c, q_norm[i], k_norm[i], cos_t, sin_t, False)
        of_c, ob_c, s_fwd, s_bwd = _retention(ret_c, log_gamma, zero_state, zero_state)

        qkv_x, ret_x, pp_x = _in_projection(x, modr, norm_pre_mix[i], w_in_b, i, False)
        qt_x, kn_x, vt_x, st_x = _attn_prep(qkv_x, q_norm[i], k_norm[i], cos_t, sin_t, True)
        attn_x = _attention(_shift_plan(st_x, [st_x, st_c]), qt_x, kn_c, vt_c, kn_x, vt_x)
        of_x, ob_x, _, _ = _retention(ret_x, log_gamma, s_fwd, s_bwd)
        pool_x = _pooling(pp_x, pool_bd, pool_scale[i])
        x = _mix_out(x, attn_x, of_x, ob_x, ret_x, pool_x, avg, w_out_b, norm_post_mix[i], modr, i, False)

        if need_ctx:
            attn_c = _attention(_shift_plan(st_c, [st_c]), qt_c, kn_c, vt_c)
            pool_c = _pooling(pp_c, pool_bd, pool_scale[i])
            xc = _mix_out(xc, attn_c, of_c, ob_c, ret_c, pool_c, avg, w_out_b, norm_post_mix[i], modr, i, True)

        j = i // 2
        if i % 2 == 0:
            w1, w3, w2 = ffn_w1[j].astype(BF16), ffn_w3[j].astype(BF16), ffn_w2[j].astype(BF16)
            x = _dense_ffn(x, norm_pre_ffn[i], norm_post_ffn[i], w1, w3, w2, modr, i, False)
            if need_ctx:
                xc = _dense_ffn(xc, norm_pre_ffn[i], norm_post_ffn[i], w1, w3, w2, modr, i, True)
        else:
            x = _moe_ffn(x, norm_pre_ffn[i], norm_post_ffn[i], moe_router[j], moe_w1[j].astype(BF16),
                         moe_w3[j].astype(BF16), moe_w2[j].astype(BF16), modr, i)
    return x
```

```python
import functools

import jax
import jax.numpy as jnp
from jax import lax
from jax.experimental import pallas as pl
from jax.experimental.pallas import tpu as pltpu

F32 = jnp.float32
BF16 = jnp.bfloat16
I32 = jnp.int32

D_MODEL = 1024
GRID_W = 64
HEAD_DIM = 64
ATTN_WIDTH = 512
KV_HEADS = 2
ATTN_GROUP = 4
KV_WIDTH = 128
RET_WIDTH = 256
RET_HEADS = 4
POOL_WIDTH = 256
POOL_WINDOWS = (2, 4, 8, 16)
IN_WIDTH = 2048
RET_CHUNK = 128
ROPE_THETA = 10000.0
D_FF = 2816
N_EXPERTS = 8
NORM_EPS = 1e-6
DEPTH = 2

TOKEN_TILE = 512
ATTN_Q_TILE = 512
ATTN_K_CHUNK = 2048
RET_TILE = 512
POOL_TILE = 256
POOL_HALO = 16
FF_CHUNK = 256
EXPERT_TILE = 512
DMA_UNROLL = 8
BF16_SUBLANES = 16
K_COLS = 128
VT_ROWS = HEAD_DIM + BF16_SUBLANES
MAX_BOUND_SHIFT = 40.0
SHIFT_MARGIN = 1.01
MOD_ROWS = 8
VMEM_LIMIT = 56 * 1024 * 1024


def _cparams(sem, vmem=None):
    return pltpu.CompilerParams(dimension_semantics=sem, vmem_limit_bytes=vmem)


def _rms(x, gain):
    ms = jnp.mean(x * x, axis=-1, keepdims=True)
    return x * lax.rsqrt(ms + NORM_EPS) * gain


def _silu(x):
    return x * jax.nn.sigmoid(x)


def _mod_kernel(c_ref, w_ref, b_ref, o_ref):
    s = _silu(c_ref[...])
    o_ref[...] = jnp.dot(s, w_ref[...], precision=lax.Precision.HIGHEST,
                         preferred_element_type=F32) + b_ref[...]


def _modulation(cvec, w_mod, b_mod):
    nchunk = 6
    out = pl.pallas_call(
        _mod_kernel,
        out_shape=jax.ShapeDtypeStruct((DEPTH, MOD_ROWS, 6 * D_MODEL), F32),
        grid=(DEPTH, nchunk),
        in_specs=[
            pl.BlockSpec((MOD_ROWS, D_MODEL), lambda l, j: (0, 0)),
            pl.BlockSpec((None, D_MODEL, D_MODEL), lambda l, j: (l, 0, j)),
            pl.BlockSpec((None, 1, D_MODEL), lambda l, j: (l, 0, j)),
        ],
        out_specs=pl.BlockSpec((None, MOD_ROWS, D_MODEL), lambda l, j: (l, 0, j)),
        compiler_params=_cparams(("parallel", "parallel")),
        name="modulation",
    )(cvec, w_mod, b_mod.reshape(DEPTH, 1, 6 * D_MODEL))
    return out.reshape(DEPTH * MOD_ROWS * nchunk, 1, D_MODEL)


def _mod_spec(layer, chunk, ctx):
    base = layer * MOD_ROWS * 6
    if ctx:
        return pl.BlockSpec((None, 1, D_MODEL), lambda b, i: (base + 2 * 6 + chunk, 0, 0))
    return pl.BlockSpec((None, 1, D_MODEL), lambda b, i: (base + b * 6 + chunk, 0, 0))


def _row_spec(width):
    return pl.BlockSpec((1, width), lambda b, i: (0, 0))


def _inproj_kernel(x_ref, g_ref, sc_ref, sh_ref, w_ref, qkv_ref, ret_ref, pp_ref):
    h = _rms(x_ref[...], g_ref[...]) * (1.0 + sc_ref[...]) + sh_ref[...]
    p = jnp.dot(h.astype(BF16), w_ref[...], preferred_element_type=F32)
    qkv_ref[...] = p[:, :768].astype(BF16)
    ret_ref[...] = p[:, 768:1792].astype(BF16)
    pp_ref[...] = p[:, 1792:].astype(BF16)


def _in_projection(x, modr, gain, w_in_bf16, layer, ctx):
    B, T, _ = x.shape
    tm = min(TOKEN_TILE, T)
    tok = lambda w: pl.BlockSpec((None, tm, w), lambda b, i: (b, i, 0))
    return pl.pallas_call(
        _inproj_kernel,
        out_shape=(jax.ShapeDtypeStruct((B, T, 768), BF16),
                   jax.ShapeDtypeStruct((B, T, 1024), BF16),
                   jax.ShapeDtypeStruct((B, T, POOL_WIDTH), BF16)),
        grid=(B, T // tm),
        in_specs=[tok(D_MODEL), _row_spec(D_MODEL), _mod_spec(layer, 1, ctx), _mod_spec(layer, 0, ctx),
                  pl.BlockSpec((D_MODEL, IN_WIDTH), lambda b, i: (0, 0))],
        out_specs=(tok(768), tok(1024), tok(POOL_WIDTH)),
        compiler_params=_cparams(("parallel", "parallel"), VMEM_LIMIT),
        name="in_projection",
    )(x, gain.reshape(1, D_MODEL), modr, modr, w_in_bf16)


def _prep_kernel(qkv_ref, qg_ref, kg_ref, cos_ref, sin_ref, qt_ref, kn_ref, vt_ref, stat_ref, *, rope):
    t = qkv_ref[...].astype(F32).T

    def norm_rope(blk, gain):
        ms = jnp.mean(blk * blk, axis=0, keepdims=True)
        y = blk * lax.rsqrt(ms + NORM_EPS) * gain
        if rope:
            partner = jnp.concatenate([y[16:32], y[0:16], y[48:64], y[32:48]], axis=0)
            y = y * cos_ref[...] + partner * sin_ref[...]
        return y

    def max_sq_norm(y):
        n2 = jnp.sum(y * y, axis=0, keepdims=True)
        return jnp.broadcast_to(jnp.max(n2, axis=1, keepdims=True), (1, 128))

    q_stats = []
    for h in range(ATTN_WIDTH // HEAD_DIM):
        lo = h * HEAD_DIM
        q = norm_rope(t[lo:lo + HEAD_DIM], qg_ref[...]) * (HEAD_DIM ** -0.5)
        qt_ref[lo:lo + HEAD_DIM, :] = q.astype(BF16)
        q_stats.append(max_sq_norm(q))
    tm = t.shape[1]
    k_ones = jnp.where(lax.broadcasted_iota(I32, (HEAD_DIM, tm), 0) < 2, 1.0, 0.0)
    v_ones = jnp.where(lax.broadcasted_iota(I32, (VT_ROWS - HEAD_DIM, tm), 0) < 1, 1.0, 0.0)
    k_stats = []
    for kv in range(KV_HEADS):
        lo = ATTN_WIDTH + kv * HEAD_DIM
        k = norm_rope(t[lo:lo + HEAD_DIM], kg_ref[...])
        k_stats.append(max_sq_norm(k))
        kn_ref[kv] = jnp.concatenate([k, k_ones], axis=0).T.astype(BF16)
        lo = ATTN_WIDTH + KV_WIDTH + kv * HEAD_DIM
        vt_ref[kv, 0] = jnp.concatenate([t[lo:lo + HEAD_DIM], v_ones], axis=0).astype(BF16)
    q_group = [functools.reduce(jnp.maximum, q_stats[kv * ATTN_GROUP:(kv + 1) * ATTN_GROUP])
               for kv in range(KV_HEADS)]
    stat_ref[...] = jnp.concatenate(k_stats + q_group + [jnp.zeros((8 - 2 * KV_HEADS, 128), F32)], axis=0)


def _attn_prep(qkv, q_gain, k_gain, cos_t, sin_t, rope):
    B, T, _ = qkv.shape
    tm = min(ATTN_K_CHUNK, T)
    nc = T // tm
    return pl.pallas_call(
        functools.partial(_prep_kernel, rope=rope),
        out_shape=(jax.ShapeDtypeStruct((B, ATTN_WIDTH, T), BF16),
                   jax.ShapeDtypeStruct((B, KV_HEADS, T, K_COLS), BF16),
                   jax.ShapeDtypeStruct((B, KV_HEADS, nc, VT_ROWS, tm), BF16),
                   jax.ShapeDtypeStruct((B, nc, 8, 128), F32)),
        grid=(B, nc),
        in_specs=[pl.BlockSpec((None, tm, 768), lambda b, i: (b, i, 0)),
                  pl.BlockSpec((HEAD_DIM, 1), lambda b, i: (0, 0)),
                  pl.BlockSpec((HEAD_DIM, 1), lambda b, i: (0, 0)),
                  pl.BlockSpec((HEAD_DIM, tm), lambda b, i: (0, i)),
                  pl.BlockSpec((HEAD_DIM, tm), lambda b, i: (0, i))],
        out_specs=(pl.BlockSpec((None, ATTN_WIDTH, tm), lambda b, i: (b, 0, i)),
                   pl.BlockSpec((None, KV_HEADS, tm, K_COLS), lambda b, i: (b, 0, i, 0)),
                   pl.BlockSpec((None, KV_HEADS, 1, VT_ROWS, tm), lambda b, i: (b, 0, i, 0, 0)),
                   pl.BlockSpec((None, None, 8, 128), lambda b, i: (b, i, 0, 0))),
        compiler_params=_cparams(("parallel", "parallel")),
        name="attn_prep",
    )(qkv, q_gain.reshape(HEAD_DIM, 1), k_gain.reshape(HEAD_DIM, 1), cos_t, sin_t)


def _shift_plan(q_stats, k_stats_list):
    k2 = functools.reduce(jnp.maximum, [st[:, :, 0:KV_HEADS, 0].max(axis=1) for st in k_stats_list])
    q2 = q_stats[:, :, KV_HEADS:2 * KV_HEADS, 0].max(axis=1)
    use_bound = (jnp.sqrt(k2 * q2) * SHIFT_MARGIN <= MAX_BOUND_SHIFT).astype(I32).reshape(-1)
    return use_bound, (jnp.sqrt(k2) * SHIFT_MARGIN).reshape(-1)


def _rope_tables(T):
    t = jnp.arange(T)
    row = (t // GRID_W).astype(F32)
    col = (t % GRID_W).astype(F32)
    n_freq = HEAD_DIM // 4
    inv = ROPE_THETA ** (-jnp.arange(n_freq, dtype=F32) / n_freq)
    ang_r = row[None, :] * inv[:, None]
    ang_c = col[None, :] * inv[:, None]
    cos_t = jnp.concatenate([jnp.cos(ang_r), jnp.cos(ang_r), jnp.cos(ang_c), jnp.cos(ang_c)], axis=0)
    sin_t = jnp.concatenate([-jnp.sin(ang_r), jnp.sin(ang_r), -jnp.sin(ang_c), jnp.sin(ang_c)], axis=0)
    return cos_t, sin_t


def _attn_kernel(*refs, n_x_chunks, tk):
    if n_x_chunks:
        use_bound_ref, kmax_ref, qt_ref, kc_ref, vtc_ref, kx_ref, vtx_ref, o_ref, qa_ref, acc_ref = refs
    else:
        use_bound_ref, kmax_ref, qt_ref, kc_ref, vtc_ref, o_ref, qa_ref, acc_ref = refs
    tq = qt_ref.shape[1]
    nq = ATTN_GROUP * tq
    bh = pl.program_id(0) * KV_HEADS + pl.program_id(1)

    for g in range(ATTN_GROUP):
        qa_ref[0:HEAD_DIM, g * tq:(g + 1) * tq] = qt_ref[g * HEAD_DIM:(g + 1) * HEAD_DIM, :]
    qa_ref[HEAD_DIM:K_COLS, :] = jnp.zeros((K_COLS - HEAD_DIM, nq), BF16)

    def scores(k):
        return jnp.dot(k, qa_ref[...], preferred_element_type=F32)

    def colmax8(s):
        return jnp.max(s.reshape(s.shape[0] // 8, 8, nq), axis=0)

    def set_shift(m):
        m_hi = m.astype(BF16).astype(F32)
        m_lo = m - m_hi
        row = lax.broadcasted_iota(I32, (BF16_SUBLANES, nq), 0)
        qa_ref[HEAD_DIM:HEAD_DIM + BF16_SUBLANES, :] = jnp.where(
            row == 0, -m_hi, jnp.where(row == 1, -m_lo, 0.0)).astype(BF16)

    kc = kc_ref[...]

    @pl.when(use_bound_ref[bh] == 1)
    def _():
        q = qa_ref[0:HEAD_DIM, :].astype(F32)
        set_shift(jnp.sqrt(jnp.sum(q * q, axis=0, keepdims=True)) * kmax_ref[bh])

    @pl.when(use_bound_ref[bh] == 0)
    def _():
        mx = colmax8(scores(kc))
        if n_x_chunks:
            def max_body(j, mx):
                k = kx_ref[pl.ds(pl.multiple_of(j * tk, tk), tk), :]
                return jnp.maximum(mx, colmax8(scores(k)))

            mx = lax.fori_loop(0, n_x_chunks, max_body, mx)
        set_shift(jnp.max(mx, axis=0, keepdims=True))

    def weighted(k, vt):
        p = jnp.exp(scores(k)).astype(BF16)
        return jnp.dot(vt, p, preferred_element_type=F32)

    acc_ref[...] = weighted(kc, vtc_ref[0])
    if n_x_chunks:
        def sum_body(j, carry):
            k = kx_ref[pl.ds(pl.multiple_of(j * tk, tk), tk), :]
            acc_ref[...] += weighted(k, vtx_ref[j])
            return carry

        lax.fori_loop(0, n_x_chunks, sum_body, 0)

    o = acc_ref[0:HEAD_DIM, :] / acc_ref[HEAD_DIM:HEAD_DIM + 1, :]
    o = jnp.concatenate([o[:, g * tq:(g + 1) * tq] for g in range(ATTN_GROUP)], axis=0)
    o_ref[...] = o.T.astype(BF16)


def _attention(plan, qt, kc, vtc, kx=None, vtx=None):
    B, _, Tq = qt.shape
    Tc = kc.shape[2]
    tq = min(ATTN_Q_TILE, Tq)
    in_specs = [pl.BlockSpec((None, ATTN_GROUP * HEAD_DIM, tq), lambda b, h, i, *_: (b, h, i)),
                pl.BlockSpec((None, None, Tc, K_COLS), lambda b, h, i, *_: (b, h, 0, 0)),
                pl.BlockSpec((None, None, 1, VT_ROWS, Tc), lambda b, h, i, *_: (b, h, 0, 0, 0))]
    args = [qt, kc, vtc]
    n_x_chunks, tk = 0, 0
    if kx is not None:
        _, _, n_x_chunks, _, tk = vtx.shape
        T = kx.shape[2]
        in_specs += [pl.BlockSpec((None, None, T, K_COLS), lambda b, h, i, *_: (b, h, 0, 0)),
                     pl.BlockSpec((None, None, n_x_chunks, VT_ROWS, tk), lambda b, h, i, *_: (b, h, 0, 0, 0))]
        args += [kx, vtx]
    scratch = [pltpu.VMEM((K_COLS, ATTN_GROUP * tq), BF16), pltpu.VMEM((VT_ROWS, ATTN_GROUP * tq), F32)]
    grid_spec = pltpu.PrefetchScalarGridSpec(
        num_scalar_prefetch=2,
        grid=(B, KV_HEADS, Tq // tq),
        in_specs=in_specs,
        out_specs=pl.BlockSpec((None, tq, ATTN_GROUP * HEAD_DIM), lambda b, h, i, *_: (b, i, h)),
        scratch_shapes=scratch,
    )
    return pl.pallas_call(
        functools.partial(_attn_kernel, n_x_chunks=n_x_chunks, tk=tk),
        out_shape=jax.ShapeDtypeStruct((B, Tq, ATTN_WIDTH), BF16),
        grid_spec=grid_spec,
        compiler_params=_cparams(("parallel", "parallel", "parallel"), VMEM_LIMIT),
        name="attention",
    )(*plan, *args)


def _ret_kernel(lgc_f_ref, lgc_b_ref, lgr_f_ref, lgr_b_ref, s0f_ref, s0b_ref, blk_f_ref, blk_b_ref,
                of_ref, ob_ref, sf_ref, sb_ref, st_f, st_b, dec_f, dec_b):
    C = RET_CHUNK
    W = RET_WIDTH
    n = pl.program_id(1)

    @pl.when(n == 0)
    def _():
        st_f[...] = s0f_ref[...]
        st_b[...] = s0b_ref[...]
        c = jnp.bitwise_and(lax.broadcasted_iota(I32, (RET_HEADS * C, C), 0), C - 1)
        m = lax.broadcasted_iota(I32, (RET_HEADS * C, C), 1)
        diff = (c - m).astype(F32)
        dec_f[...] = jnp.where(diff >= 0, jnp.exp(lgc_f_ref[...] * jnp.maximum(diff, 0.0)), 0.0)
        dec_b[...] = jnp.where(diff <= 0, jnp.exp(lgc_b_ref[...] * jnp.maximum(-diff, 0.0)), 0.0)

    lane_head = jnp.right_shift(lax.broadcasted_iota(I32, (C, W), 1), 6)
    pos = lax.broadcasted_iota(I32, (C, W), 0).astype(F32)
    same_head = (jnp.right_shift(lax.broadcasted_iota(I32, (W, W), 0), 6)
                 == jnp.right_shift(lax.broadcasted_iota(I32, (W, W), 1), 6))

    def direction(blk_ref, dec_ref, st_ref, lgr, forward, out_ref):
        n_sub = blk_ref.shape[0] // C
        state = st_ref[...]
        for sub in (range(n_sub) if forward else reversed(range(n_sub))):
            state = chunk(blk_ref, dec_ref, state, lgr, forward, out_ref, sub * C)
        st_ref[...] = state

    def chunk(blk_ref, dec_ref, state, lgr, forward, out_ref, r0):
        q = blk_ref[r0:r0 + C, 0:W].astype(F32)
        kf = blk_ref[r0:r0 + C, W:2 * W].astype(F32) * (HEAD_DIM ** -0.5)
        v = blk_ref[r0:r0 + C, 2 * W:3 * W]
        if forward:
            zeta = jnp.exp(lgr * (C - 1.0 - pos))
            xi = jnp.exp(lgr * (pos + 1.0))
        else:
            zeta = jnp.exp(lgr * pos)
            xi = jnp.exp(lgr * (C - pos))
        qexp = jnp.concatenate([jnp.where(lane_head == h, q, 0.0) for h in range(RET_HEADS)],
                               axis=0).astype(BF16)
        a = lax.dot_general(qexp, kf.astype(BF16), (((1,), (1,)), ((), ())),
                            preferred_element_type=F32)
        p = (a * dec_ref[...]).astype(BF16)
        full = jnp.dot(p, v, preferred_element_type=F32)
        intra = jnp.zeros((C, W), F32)
        for h in range(RET_HEADS):
            intra = intra + jnp.where(lane_head == h, full[h * C:(h + 1) * C], 0.0)
        cross = jnp.dot((q * xi).astype(BF16), state.astype(BF16), preferred_element_type=F32)
        out_ref[r0:r0 + C, :] = intra + cross
        upd = lax.dot_general((kf * zeta).astype(BF16), v, (((0,), (0,)), ((), ())),
                              preferred_element_type=F32)
        return jnp.where(same_head, state * jnp.exp(lgr * float(C)) + upd, 0.0)

    direction(blk_f_ref, dec_f, st_f, lgr_f_ref[...], True, of_ref)
    direction(blk_b_ref, dec_b, st_b, lgr_b_ref[...], False, ob_ref)

    @pl.when(n == pl.num_programs(1) - 1)
    def _():
        sf_ref[...] = st_f[...]
        sb_ref[...] = st_b[...]


def _retention(ret, log_gamma, s0f, s0b):
    B, T, _ = ret.shape
    C = RET_CHUNK
    tm = min(RET_TILE, T)
    nc = T // tm
    lgc = lambda d: jnp.repeat(log_gamma[d], C).reshape(RET_HEADS * C, 1)
    lgr = lambda d: jnp.repeat(log_gamma[d], HEAD_DIM).reshape(1, RET_WIDTH)
    const = lambda shape: pl.BlockSpec(shape, lambda b, n: (0,) * len(shape))
    st_spec = pl.BlockSpec((None, RET_WIDTH, RET_WIDTH), lambda b, n: (b, 0, 0))
    return pl.pallas_call(
        _ret_kernel,
        out_shape=(jax.ShapeDtypeStruct((B, T, RET_WIDTH), F32),
                   jax.ShapeDtypeStruct((B, T, RET_WIDTH), F32),
                   jax.ShapeDtypeStruct((B, RET_WIDTH, RET_WIDTH), F32),
                   jax.ShapeDtypeStruct((B, RET_WIDTH, RET_WIDTH), F32)),
        grid=(B, nc),
        in_specs=[const((RET_HEADS * C, 1)), const((RET_HEADS * C, 1)),
                  const((1, RET_WIDTH)), const((1, RET_WIDTH)), st_spec, st_spec,
                  pl.BlockSpec((None, tm, 1024), lambda b, n: (b, n, 0)),
                  pl.BlockSpec((None, tm, 1024), lambda b, n: (b, nc - 1 - n, 0))],
        out_specs=(pl.BlockSpec((None, tm, RET_WIDTH), lambda b, n: (b, n, 0)),
                   pl.BlockSpec((None, tm, RET_WIDTH), lambda b, n: (b, nc - 1 - n, 0)),
                   st_spec, st_spec),
        scratch_shapes=[pltpu.VMEM((RET_WIDTH, RET_WIDTH), F32), pltpu.VMEM((RET_WIDTH, RET_WIDTH), F32),
                        pltpu.VMEM((RET_HEADS * C, C), F32), pltpu.VMEM((RET_HEADS * C, C), F32)],
        compiler_params=_cparams(("parallel", "arbitrary")),
        name="retention",
    )(lgc(0), lgc(1), lgr(0), lgr(1), s0f, s0b, ret, ret)


def _pool_kernel(prev_ref, cur_ref, next_ref, w_ref, scale_ref, o_ref, band_ref, *, seq_len):
    tm = cur_ref.shape[0]
    i = pl.program_id(1)

    @pl.when(i == 0)
    def _():
        tok = lax.broadcasted_iota(I32, (tm, tm + 2 * POOL_HALO), 0)
        src = lax.broadcasted_iota(I32, (tm, tm + 2 * POOL_HALO), 1) - POOL_HALO
        for gi, w in enumerate(POOL_WINDOWS):
            inside = (src >= tok - w // 2) & (src < tok + w // 2)
            band_ref[gi] = jnp.where(inside, 1.0, 0.0).astype(BF16)

    prev = jnp.where(i > 0, prev_ref[...], jnp.zeros_like(prev_ref))
    nxt = jnp.where(i < pl.num_programs(1) - 1, next_ref[...], jnp.zeros_like(next_ref))
    ext = jnp.concatenate([prev, cur_ref[...], nxt], axis=0)
    tcol = i * tm + lax.broadcasted_iota(I32, (tm, 1), 0)
    lane_group = jnp.right_shift(lax.broadcasted_iota(I32, (tm, POOL_WIDTH), 1), 6)
    cur = cur_ref[...].astype(F32)
    mixed = jnp.zeros((tm, POOL_WIDTH), F32)
    for gi, w in enumerate(POOL_WINDOWS):
        total = jnp.dot(band_ref[gi], ext, preferred_element_type=F32)
        cnt = (jnp.minimum(tcol + w // 2, seq_len) - jnp.maximum(tcol - w // 2, 0)).astype(F32)
        mixed = mixed + jnp.where(lane_group == gi, total / cnt - cur, 0.0)
    y = jnp.dot(mixed.astype(BF16), w_ref[...], preferred_element_type=F32)
    o_ref[...] = (y * scale_ref[...]).astype(BF16)


def _pooling(pp, w_blockdiag_bf16, scale):
    B, T, _ = pp.shape
    tm = min(POOL_TILE, T)
    r = tm // POOL_HALO
    last = T // POOL_HALO - 1
    return pl.pallas_call(
        functools.partial(_pool_kernel, seq_len=T),
        out_shape=jax.ShapeDtypeStruct((B, T, POOL_WIDTH), BF16),
        grid=(B, T // tm),
        in_specs=[pl.BlockSpec((None, POOL_HALO, POOL_WIDTH), lambda b, i: (b, jnp.maximum(i * r - 1, 0), 0)),
                  pl.BlockSpec((None, tm, POOL_WIDTH), lambda b, i: (b, i, 0)),
                  pl.BlockSpec((None, POOL_HALO, POOL_WIDTH), lambda b, i: (b, jnp.minimum((i + 1) * r, last), 0)),
                  pl.BlockSpec((POOL_WIDTH, POOL_WIDTH), lambda b, i: (0, 0)),
                  _row_spec(POOL_WIDTH)],
        out_specs=pl.BlockSpec((None, tm, POOL_WIDTH), lambda b, i: (b, i, 0)),
        scratch_shapes=[pltpu.VMEM((len(POOL_WINDOWS), tm, tm + 2 * POOL_HALO), BF16)],
        compiler_params=_cparams(("parallel", "arbitrary")),
        name="pooling",
    )(pp, pp, pp, w_blockdiag_bf16, scale.reshape(1, POOL_WIDTH))


def _head_mean(x, avg):
    hi = x.astype(BF16)
    lo = (x - hi.astype(F32)).astype(BF16)
    return (jnp.dot(hi, avg, preferred_element_type=F32) + jnp.dot(lo, avg, preferred_element_type=F32))


def _mix_tile(x_ref, attn_ref, of_ref, ob_ref, gate_ref, pool_ref, avg_ref, w_ref, npost_ref, g1_ref):
    o = of_ref[...] + ob_ref[...]
    avg = avg_ref[...]
    mu = _head_mean(o, avg)
    cen = o - mu
    var = _head_mean(cen * cen, avg)
    y_ret = (_silu(gate_ref[...].astype(F32)) * (cen * lax.rsqrt(var + NORM_EPS))).astype(BF16)
    mx = (jnp.dot(attn_ref[...], w_ref[0:ATTN_WIDTH, :], preferred_element_type=F32)
          + jnp.dot(y_ret, w_ref[ATTN_WIDTH:ATTN_WIDTH + RET_WIDTH, :], preferred_element_type=F32)
          + jnp.dot(pool_ref[...], w_ref[ATTN_WIDTH + RET_WIDTH:, :], preferred_element_type=F32))
    return x_ref[...] + g1_ref[...] * _rms(mx, npost_ref[...])


N_MIX_INPUTS = 10


def _mix_inputs(mix, tm, layer, ctx):
    x, attn, of, ob, ret, pool, avg_bf16, w_out_bf16, npost, modr = mix
    tok = lambda w: pl.BlockSpec((None, tm, w), lambda b, i: (b, i, 0))
    specs = [tok(D_MODEL), tok(ATTN_WIDTH), tok(RET_WIDTH), tok(RET_WIDTH),
             pl.BlockSpec((None, tm, RET_WIDTH), lambda b, i: (b, i, 3)),
             tok(POOL_WIDTH),
             pl.BlockSpec((RET_WIDTH, RET_WIDTH), lambda b, i: (0, 0)),
             pl.BlockSpec((D_MODEL, D_MODEL), lambda b, i: (0, 0)),
             _row_spec(D_MODEL), _mod_spec(layer, 2, ctx)]
    return [x, attn, of, ob, ret, pool, avg_bf16, w_out_bf16, npost.reshape(1, D_MODEL), modr], specs


def _swiglu_tile(h, w1_ref, w3_ref, w2_ref):
    acc = jnp.zeros((h.shape[0], D_MODEL), F32)
    for f in range(0, D_FF, FF_CHUNK):
        a = jnp.dot(h, w1_ref[:, f:f + FF_CHUNK], preferred_element_type=F32)
        b = jnp.dot(h, w3_ref[:, f:f + FF_CHUNK], preferred_element_type=F32)
        u = (_silu(a) * b).astype(BF16)
        acc = acc + jnp.dot(u, w2_ref[f:f + FF_CHUNK, :], preferred_element_type=F32)
    return acc


def _mix_ffn_kernel(*refs):
    gpre_ref, sc_ref, sh_ref, w1_ref, w3_ref, w2_ref, npost_ref, g2_ref, o_ref = refs[N_MIX_INPUTS:]
    x = _mix_tile(*refs[:N_MIX_INPUTS])
    h = (_rms(x, gpre_ref[...]) * (1.0 + sc_ref[...]) + sh_ref[...]).astype(BF16)
    y = _swiglu_tile(h, w1_ref, w3_ref, w2_ref)
    o_ref[...] = x + g2_ref[...] * _rms(y, npost_ref[...])


def _mix_ffn(mix, gpre, npost, w1, w3, w2, layer, ctx):
    x, modr = mix[0], mix[-1]
    B, T, _ = x.shape
    tm = min(TOKEN_TILE, T)
    args, specs = _mix_inputs(mix, tm, layer, ctx)
    wspec = lambda shape: pl.BlockSpec(shape, lambda b, i: (0, 0), pipeline_mode=pl.Buffered(1))
    return pl.pallas_call(
        _mix_ffn_kernel,
        out_shape=jax.ShapeDtypeStruct((B, T, D_MODEL), F32),
        grid=(B, T // tm),
        in_specs=specs + [_row_spec(D_MODEL), _mod_spec(layer, 4, ctx), _mod_spec(layer, 3, ctx),
                          wspec((D_MODEL, D_FF)), wspec((D_MODEL, D_FF)), wspec((D_FF, D_MODEL)),
                          _row_spec(D_MODEL), _mod_spec(layer, 5, ctx)],
        out_specs=pl.BlockSpec((None, tm, D_MODEL), lambda b, i: (b, i, 0)),
        compiler_params=_cparams(("parallel", "parallel"), VMEM_LIMIT),
        name="mix_ffn",
    )(*args, gpre.reshape(1, D_MODEL), modr, modr, w1, w3, w2, npost.reshape(1, D_MODEL), modr)


def _mix_route_kernel(*refs):
    (gpre_ref, sc_ref, sh_ref, rt_ref, x_ref, h_ref, idx_ref, gate_ref, rank_ref, cnt_ref,
     run_ref, upper_ref) = refs[N_MIX_INPUTS:]
    tm = x_ref.shape[0]
    first = (pl.program_id(0) == 0) & (pl.program_id(1) == 0)

    @pl.when(first)
    def _():
        run_ref[...] = jnp.zeros_like(run_ref)
        earlier = lax.broadcasted_iota(I32, (tm, tm), 0) < lax.broadcasted_iota(I32, (tm, tm), 1)
        upper_ref[...] = jnp.where(earlier, 1.0, 0.0).astype(BF16)

    x = _mix_tile(*refs[:N_MIX_INPUTS])
    x_ref[...] = x
    h = _rms(x, gpre_ref[...]) * (1.0 + sc_ref[...]) + sh_ref[...]
    h_ref[...] = h
    logits = lax.dot_general(rt_ref[...], h, (((1,), (1,)), ((), ())), precision=lax.Precision.HIGHEST,
                             preferred_element_type=F32)
    eid = lax.broadcasted_iota(I32, (N_EXPERTS, tm), 0).astype(F32)
    m1 = jnp.max(logits, axis=0, keepdims=True)
    i1 = jnp.min(jnp.where(logits == m1, eid, float(N_EXPERTS)), axis=0, keepdims=True)
    oh1 = eid == i1
    rest = jnp.where(oh1, -jnp.inf, logits)
    m2 = jnp.max(rest, axis=0, keepdims=True)
    i2 = jnp.min(jnp.where(rest == m2, eid, float(N_EXPERTS)), axis=0, keepdims=True)
    oh2 = eid == i2
    e2 = jnp.exp(m2 - m1)
    gate_ref[0:1, :] = 1.0 / (1.0 + e2)
    gate_ref[1:2, :] = e2 / (1.0 + e2)
    idx_ref[0:1, :] = i1.astype(I32)
    idx_ref[1:2, :] = i2.astype(I32)
    upper = upper_ref[...]
    f1 = jnp.where(oh1, 1.0, 0.0)
    f2 = jnp.where(oh2, 1.0, 0.0)
    before1 = jnp.dot(f1.astype(BF16), upper, preferred_element_type=F32)
    before2 = jnp.dot(f2.astype(BF16), upper, preferred_element_type=F32)
    cnt1 = jnp.sum(f1, axis=1, keepdims=True)
    cnt2 = jnp.sum(f2, axis=1, keepdims=True)
    run = run_ref[:, 0:1]
    rank_ref[0:1, :] = jnp.sum(f1 * (run + before1), axis=0, keepdims=True).astype(I32)
    rank_ref[1:2, :] = jnp.sum(f2 * (run + cnt1 + before2), axis=0, keepdims=True).astype(I32)
    run_new = run_ref[...] + cnt1 + cnt2
    run_ref[...] = run_new
    cnt_ref[...] = run_new


def _mix_route(mix, gpre, router_t, layer):
    x, modr = mix[0], mix[-1]
    B, T, _ = x.shape
    tm = min(TOKEN_TILE, T)
    nt = T // tm
    args, specs = _mix_inputs(mix, tm, layer, False)
    tok = pl.BlockSpec((None, tm, D_MODEL), lambda b, i: (b, i, 0))
    lane = pl.BlockSpec((2, tm), lambda b, i: (0, b * nt + i))
    return pl.pallas_call(
        _mix_route_kernel,
        out_shape=(jax.ShapeDtypeStruct((B, T, D_MODEL), F32),
                   jax.ShapeDtypeStruct((B, T, D_MODEL), F32),
                   jax.ShapeDtypeStruct((2, B * T), I32),
                   jax.ShapeDtypeStruct((2, B * T), F32),
                   jax.ShapeDtypeStruct((2, B * T), I32),
                   jax.ShapeDtypeStruct((N_EXPERTS, 128), F32)),
        grid=(B, nt),
        in_specs=specs + [_row_spec(D_MODEL), _mod_spec(layer, 4, False), _mod_spec(layer, 3, False),
                          pl.BlockSpec((N_EXPERTS, D_MODEL), lambda b, i: (0, 0))],
        out_specs=(tok, tok, lane, lane, lane,
                   pl.BlockSpec((N_EXPERTS, 128), lambda b, i: (0, 0))),
        scratch_shapes=[pltpu.VMEM((N_EXPERTS, 128), F32), pltpu.VMEM((tm, tm), BF16)],
        compiler_params=_cparams(("arbitrary", "arbitrary"), VMEM_LIMIT),
        name="mix_route",
    )(*args, gpre.reshape(1, D_MODEL), modr, modr, router_t)


def _dispatch_kernel(zero_tiles_ref, slot_ref, h_ref, hs_ref, zero_ref, sem, zero_sem):
    tm = h_ref.shape[0]

    @pl.when(pl.program_id(0) == 0)
    def _():
        zero_ref[...] = jnp.zeros_like(zero_ref)
        for j in range(2 * N_EXPERTS):
            start = pl.multiple_of(zero_tiles_ref[j] * EXPERT_TILE, EXPERT_TILE)
            fill = pltpu.make_async_copy(zero_ref, hs_ref.at[pl.ds(start, EXPERT_TILE)], zero_sem)
            fill.start()
            fill.wait()

    def body(i, carry):
        for u in range(DMA_UNROLL):
            r = i * DMA_UNROLL + u
            for k in range(2):
                s = slot_ref[0, k, r]
                pltpu.make_async_copy(h_ref.at[pl.ds(r, 1)], hs_ref.at[pl.ds(s, 1)], sem).start(priority=k)
        return carry

    lax.fori_loop(0, tm // DMA_UNROLL, body, 0)
    for _ in range(2):
        pltpu.make_async_copy(h_ref, hs_ref.at[pl.ds(0, tm)], sem).wait()


def _dispatch(h, slot_tiles, zero_tiles, n_slots):
    N = h.shape[0]
    tm = slot_tiles.shape[2]
    grid_spec = pltpu.PrefetchScalarGridSpec(
        num_scalar_prefetch=1,
        grid=(N // tm,),
        in_specs=[pl.BlockSpec((1, 2, tm), lambda i, zt: (i, 0, 0), memory_space=pltpu.SMEM),
                  pl.BlockSpec((tm, D_MODEL), lambda i, zt: (i, 0))],
        out_specs=pl.BlockSpec(memory_space=pl.ANY),
        scratch_shapes=[pltpu.VMEM((EXPERT_TILE, D_MODEL), F32),
                        pltpu.SemaphoreType.DMA(()), pltpu.SemaphoreType.DMA(())],
    )
    return pl.pallas_call(
        _dispatch_kernel,
        out_shape=jax.ShapeDtypeStruct((n_slots, D_MODEL), F32),
        grid_spec=grid_spec,
        compiler_params=_cparams(("arbitrary",)),
        name="moe_dispatch",
    )(zero_tiles, slot_tiles, h)


def _expert_kernel(te_ref, tv_ref, h_ref, w1_ref, w3_ref, w2_ref, y_ref):
    t = pl.program_id(0)

    @pl.when(tv_ref[t] == 1)
    def _():
        y_ref[...] = _swiglu_tile(h_ref[...].astype(BF16), w1_ref, w3_ref, w2_ref)

    @pl.when(tv_ref[t] == 0)
    def _():
        y_ref[...] = jnp.zeros_like(y_ref)


def _experts(hs, tile_expert, tile_valid, w1, w3, w2):
    n_slots = hs.shape[0]
    tm = EXPERT_TILE
    grid_spec = pltpu.PrefetchScalarGridSpec(
        num_scalar_prefetch=2,
        grid=(n_slots // tm,),
        in_specs=[pl.BlockSpec((tm, D_MODEL), lambda t, te, tv: (t, 0)),
                  pl.BlockSpec((None, D_MODEL, D_FF), lambda t, te, tv: (te[t], 0, 0)),
                  pl.BlockSpec((None, D_MODEL, D_FF), lambda t, te, tv: (te[t], 0, 0)),
                  pl.BlockSpec((None, D_FF, D_MODEL), lambda t, te, tv: (te[t], 0, 0))],
        out_specs=pl.BlockSpec((tm, D_MODEL), lambda t, te, tv: (t, 0)),
    )
    return pl.pallas_call(
        _expert_kernel,
        out_shape=jax.ShapeDtypeStruct((n_slots, D_MODEL), F32),
        grid_spec=grid_spec,
        compiler_params=_cparams(("arbitrary",), VMEM_LIMIT),
        name="moe_experts",
    )(tile_expert, tile_valid, hs, w1, w3, w2)


def _combine_kernel(slot_ref, gate_ref, x_ref, npost_ref, g2_ref, ys_ref, o_ref, buf0, buf1, sem):
    tm = x_ref.shape[0]
    bufs = (buf0, buf1)

    def body(i, carry):
        for u in range(DMA_UNROLL):
            r = i * DMA_UNROLL + u
            for k in range(2):
                s = slot_ref[0, k, r]
                pltpu.make_async_copy(ys_ref.at[pl.ds(s, 1)], bufs[k].at[pl.ds(r, 1)], sem).start(priority=k)
        return carry

    lax.fori_loop(0, tm // DMA_UNROLL, body, 0)
    for k in range(2):
        pltpu.make_async_copy(ys_ref.at[pl.ds(0, tm)], bufs[k], sem).wait()
    y = gate_ref[:, 0:1] * buf0[...] + gate_ref[:, 1:2] * buf1[...]
    o_ref[...] = x_ref[...] + g2_ref[...] * _rms(y, npost_ref[...])


def _combine(x, ys, slot_tiles, gate_tok, npost, modr, layer):
    B, T, _ = x.shape
    tm = slot_tiles.shape[2]
    nt = T // tm
    tok = pl.BlockSpec((None, tm, D_MODEL), lambda b, i: (b, i, 0))
    return pl.pallas_call(
        _combine_kernel,
        out_shape=jax.ShapeDtypeStruct((B, T, D_MODEL), F32),
        grid=(B, nt),
        in_specs=[pl.BlockSpec((1, 2, tm), lambda b, i: (b * nt + i, 0, 0), memory_space=pltpu.SMEM),
                  pl.BlockSpec((tm, 2), lambda b, i: (b * nt + i, 0)),
                  tok, _row_spec(D_MODEL), _mod_spec(layer, 5, False),
                  pl.BlockSpec(memory_space=pl.ANY)],
        out_specs=tok,
        scratch_shapes=[pltpu.VMEM((tm, D_MODEL), F32), pltpu.VMEM((tm, D_MODEL), F32),
                        pltpu.SemaphoreType.DMA(())],
        compiler_params=_cparams(("arbitrary", "arbitrary")),
        name="moe_combine",
    )(slot_tiles, gate_tok, x, npost.reshape(1, D_MODEL), modr, ys)


def _mix_moe_ffn(mix, gpre, npost, router, w1, w3, w2, layer):
    modr = mix[-1]
    B, T, _ = mix[0].shape
    N = B * T
    tm = min(TOKEN_TILE, T)
    x, h, idx, gate, rank, cnt = _mix_route(mix, gpre, router.T, layer)
    n_slots = 2 * N + N_EXPERTS * EXPERT_TILE
    n_tiles = n_slots // EXPERT_TILE
    counts = cnt[:, 0].astype(I32)
    padded = ((counts + EXPERT_TILE - 1) // EXPERT_TILE) * EXPERT_TILE
    ends = jnp.cumsum(padded)
    starts = ends - padded
    slot = rank
    for e in range(N_EXPERTS):
        slot = slot + jnp.where(idx == e, starts[e], 0)
    slot_tiles = slot.reshape(2, N // tm, tm).transpose(1, 0, 2)
    tile_start = jnp.arange(n_tiles, dtype=I32) * EXPERT_TILE
    tile_valid = (tile_start < ends[-1]).astype(I32)
    tile_expert = jnp.minimum(jnp.sum((tile_start[:, None] >= ends[None, :]).astype(I32), axis=1), N_EXPERTS - 1)
    last_expert = jnp.max(jnp.where(tile_valid == 1, tile_expert, 0))
    tile_expert = jnp.where(tile_valid == 1, tile_expert, last_expert)
    last_tile = jnp.where(padded > 0, ends // EXPERT_TILE - 1, n_tiles - 1)
    tail_tile = jnp.minimum(ends[-1] // EXPERT_TILE + jnp.arange(N_EXPERTS, dtype=I32), n_tiles - 1)
    zero_tiles = jnp.concatenate([last_tile, tail_tile]).astype(I32)
    hs = _dispatch(h.reshape(N, D_MODEL), slot_tiles, zero_tiles, n_slots)
    ys = _experts(hs, tile_expert, tile_valid, w1, w3, w2)
    return _combine(x, ys, slot_tiles, gate.T, npost, modr, layer)


def kernel(x, c, ctx, c_ctx, w_mod, b_mod, norm_pre_mix, norm_post_mix, norm_pre_ffn, norm_post_ffn, w_in, w_out, q_norm, k_norm, ret_decay_logit, pool_w, pool_scale, ffn_w1, ffn_w3, ffn_w2, moe_router, moe_w1, moe_w3, moe_w2):
    B, T, _ = x.shape
    cvec = jnp.zeros((MOD_ROWS, D_MODEL), F32).at[0:B].set(c).at[2].set(c_ctx)
    modr = _modulation(cvec, w_mod, b_mod)
    cos_t, sin_t = _rope_tables(T)
    avg = jnp.kron(jnp.eye(RET_HEADS, dtype=F32), jnp.full((HEAD_DIM, HEAD_DIM), 1.0 / HEAD_DIM, F32)).astype(BF16)
    zero_state = jnp.zeros((B, RET_WIDTH, RET_WIDTH), F32)
    xc = ctx
    for i in range(DEPTH):
        need_ctx = i < DEPTH - 1
        w_in_b = w_in[i].astype(BF16)
        w_out_b = w_out[i].astype(BF16)
        pool_bd = jax.scipy.linalg.block_diag(*[pool_w[i, g] for g in range(len(POOL_WINDOWS))]).astype(BF16)
        log_gamma = jax.nn.log_sigmoid(ret_decay_logit[i].astype(F32))

        qkv_c, ret_c, pp_c = _in_projection(xc, modr, norm_pre_mix[i], w_in_b, i, True)
        qt_c, kn_c, vt_c, st_c = _attn_prep(qkv_c, q_norm[i], k_norm[i], cos_t, sin_t, False)
        of_c, ob_c, s_fwd, s_bwd = _retention(ret_c, log_gamma, zero_state, zero_state)

        qkv_x, ret_x, pp_x = _in_projection(x, modr, norm_pre_mix[i], w_in_b, i, False)
        qt_x, kn_x, vt_x, st_x = _attn_prep(qkv_x, q_norm[i], k_norm[i], cos_t, sin_t, True)
        attn_x = _attention(_shift_plan(st_x, [st_x, st_c]), qt_x, kn_c, vt_c, kn_x, vt_x)
        of_x, ob_x, _, _ = _retention(ret_x, log_gamma, s_fwd, s_bwd)
        pool_x = _pooling(pp_x, pool_bd, pool_scale[i])
        mix_x = (x, attn_x, of_x, ob_x, ret_x, pool_x, avg, w_out_b, norm_post_mix[i], modr)

        j = i // 2
        if i % 2 == 0:
            w1, w3, w2 = ffn_w1[j].astype(BF16), ffn_w3[j].astype(BF16), ffn_w2[j].astype(BF16)
            x = _mix_ffn(mix_x, norm_pre_ffn[i], norm_post_ffn[i], w1, w3, w2, i, False)
            if need_ctx:
                attn_c = _attention(_shift_plan(st_c, [st_c]), qt_c, kn_c, vt_c)
                pool_c = _pooling(pp_c, pool_bd, pool_scale[i])
                mix_c = (xc, attn_c, of_c, ob_c, ret_c, pool_c, avg, w_out_b, norm_post_mix[i], modr)
                xc = _mix_ffn(mix_c, norm_pre_ffn[i], norm_post_ffn[i], w1, w3, w2, i, True)
        else:
            assert not need_ctx
            x = _mix_moe_ffn(mix_x, norm_pre_ffn[i], norm_post_ffn[i], moe_router[j], moe_w1[j].astype(BF16),
                             moe_w3[j].astype(BF16), moe_w2[j].astype(BF16), i)
    return x
```

```python
import functools

import jax
import jax.numpy as jnp
from jax import lax
from jax.experimental import pallas as pl
from jax.experimental.pallas import tpu as pltpu

F32 = jnp.float32
BF16 = jnp.bfloat16
I32 = jnp.int32

D_MODEL = 1024
GRID_W = 64
HEAD_DIM = 64
ATTN_WIDTH = 512
KV_HEADS = 2
ATTN_GROUP = 4
KV_WIDTH = 128
RET_WIDTH = 256
RET_HEADS = 4
POOL_WIDTH = 256
POOL_WINDOWS = (2, 4, 8, 16)
IN_WIDTH = 2048
RET_CHUNK = 128
ROPE_THETA = 10000.0
D_FF = 2816
N_EXPERTS = 8
NORM_EPS = 1e-6
DEPTH = 2

TOKEN_TILE = 512
ATTN_Q_TILE = 512
ATTN_K_CHUNK = 2048
RET_TILE = 512
POOL_TILE = 256
POOL_HALO = 16
FF_CHUNK = 256
EXPERT_TILE = 512
DMA_UNROLL = 8
BF16_SUBLANES = 16
MXU_DIM = 256
K_COLS = MXU_DIM
KEY_BLOCK = MXU_DIM
ATTN_UNROLL = 8
N_MXU = 2
SCORE_REG, VALUE_REG = 1, 0
SCORE_BASE = 64
VT_ROWS = HEAD_DIM + BF16_SUBLANES
MAX_BOUND_SHIFT = 40.0
SHIFT_MARGIN = 1.01
MOD_ROWS = 8
VMEM_LIMIT = 56 * 1024 * 1024


def _cparams(sem, vmem=None):
    return pltpu.CompilerParams(dimension_semantics=sem, vmem_limit_bytes=vmem)


def _rms(x, gain):
    ms = jnp.mean(x * x, axis=-1, keepdims=True)
    return x * lax.rsqrt(ms + NORM_EPS) * gain


def _silu(x):
    return x * jax.nn.sigmoid(x)


def _mod_kernel(c_ref, w_ref, b_ref, o_ref):
    s = _silu(c_ref[...])
    o_ref[...] = jnp.dot(s, w_ref[...], precision=lax.Precision.HIGHEST,
                         preferred_element_type=F32) + b_ref[...]


def _modulation(cvec, w_mod, b_mod):
    nchunk = 6
    out = pl.pallas_call(
        _mod_kernel,
        out_shape=jax.ShapeDtypeStruct((DEPTH, MOD_ROWS, 6 * D_MODEL), F32),
        grid=(DEPTH, nchunk),
        in_specs=[
            pl.BlockSpec((MOD_ROWS, D_MODEL), lambda l, j: (0, 0)),
            pl.BlockSpec((None, D_MODEL, D_MODEL), lambda l, j: (l, 0, j)),
            pl.BlockSpec((None, 1, D_MODEL), lambda l, j: (l, 0, j)),
        ],
        out_specs=pl.BlockSpec((None, MOD_ROWS, D_MODEL), lambda l, j: (l, 0, j)),
        compiler_params=_cparams(("parallel", "parallel")),
        name="modulation",
    )(cvec, w_mod, b_mod.reshape(DEPTH, 1, 6 * D_MODEL))
    return out.reshape(DEPTH * MOD_ROWS * nchunk, 1, D_MODEL)


def _mod_spec(layer, chunk, ctx):
    base = layer * MOD_ROWS * 6
    if ctx:
        return pl.BlockSpec((None, 1, D_MODEL), lambda b, i: (base + 2 * 6 + chunk, 0, 0))
    return pl.BlockSpec((None, 1, D_MODEL), lambda b, i: (base + b * 6 + chunk, 0, 0))


def _row_spec(width):
    return pl.BlockSpec((1, width), lambda b, i: (0, 0))


def _inproj_kernel(x_ref, g_ref, sc_ref, sh_ref, w_ref, qkv_ref, ret_ref, pp_ref):
    h = _rms(x_ref[...], g_ref[...]) * (1.0 + sc_ref[...]) + sh_ref[...]
    p = jnp.dot(h.astype(BF16), w_ref[...], preferred_element_type=F32)
    qkv_ref[...] = p[:, :768].astype(BF16)
    ret_ref[...] = p[:, 768:1792].astype(BF16)
    pp_ref[...] = p[:, 1792:].astype(BF16)


def _in_projection(x, modr, gain, w_in_bf16, layer, ctx):
    B, T, _ = x.shape
    tm = min(TOKEN_TILE, T)
    tok = lambda w: pl.BlockSpec((None, tm, w), lambda b, i: (b, i, 0))
    return pl.pallas_call(
        _inproj_kernel,
        out_shape=(jax.ShapeDtypeStruct((B, T, 768), BF16),
                   jax.ShapeDtypeStruct((B, T, 1024), BF16),
                   jax.ShapeDtypeStruct((B, T, POOL_WIDTH), BF16)),
        grid=(B, T // tm),
        in_specs=[tok(D_MODEL), _row_spec(D_MODEL), _mod_spec(layer, 1, ctx), _mod_spec(layer, 0, ctx),
                  pl.BlockSpec((D_MODEL, IN_WIDTH), lambda b, i: (0, 0))],
        out_specs=(tok(768), tok(1024), tok(POOL_WIDTH)),
        compiler_params=_cparams(("parallel", "parallel"), VMEM_LIMIT),
        name="in_projection",
    )(x, gain.reshape(1, D_MODEL), modr, modr, w_in_bf16)


def _prep_kernel(qkv_ref, qg_ref, kg_ref, cos_ref, sin_ref, qt_ref, kn_ref, vt_ref, qn_ref, stat_ref, *, rope):
    t = qkv_ref[...].astype(F32).T

    def norm_rope(blk, gain):
        ms = jnp.mean(blk * blk, axis=0, keepdims=True)
        y = blk * lax.rsqrt(ms + NORM_EPS) * gain
        if rope:
            partner = jnp.concatenate([y[16:32], y[0:16], y[48:64], y[32:48]], axis=0)
            y = y * cos_ref[...] + partner * sin_ref[...]
        return y

    def sq_norm(y):
        return jnp.sum(y * y, axis=0, keepdims=True)

    def row_max(n2):
        return jnp.broadcast_to(jnp.max(n2, axis=1, keepdims=True), (1, 128))

    q_stats = []
    for h in range(ATTN_WIDTH // HEAD_DIM):
        lo = h * HEAD_DIM
        q = norm_rope(t[lo:lo + HEAD_DIM], qg_ref[...]) * (HEAD_DIM ** -0.5)
        qt_ref[lo:lo + HEAD_DIM, :] = q.astype(BF16)
        n2 = sq_norm(q)
        qn_ref[h // ATTN_GROUP, h % ATTN_GROUP:h % ATTN_GROUP + 1, :] = jnp.sqrt(n2)
        q_stats.append(row_max(n2))
    tm = t.shape[1]
    k_pad = jnp.where(lax.broadcasted_iota(I32, (K_COLS - HEAD_DIM, tm), 0) < 2, 1.0, 0.0)
    v_pad = jnp.where(lax.broadcasted_iota(I32, (VT_ROWS - HEAD_DIM, tm), 0) < 1, 1.0, 0.0)
    k_stats = []
    for kv in range(KV_HEADS):
        lo = ATTN_WIDTH + kv * HEAD_DIM
        k = norm_rope(t[lo:lo + HEAD_DIM], kg_ref[...])
        k_stats.append(row_max(sq_norm(k)))
        kn_ref[kv] = jnp.concatenate([k, k_pad], axis=0).T.astype(BF16)
        lo = ATTN_WIDTH + KV_WIDTH + kv * HEAD_DIM
        vt = jnp.concatenate([t[lo:lo + HEAD_DIM], v_pad], axis=0).astype(BF16)
        for j in range(tm // KEY_BLOCK):
            vt_ref[kv, j] = vt[:, j * KEY_BLOCK:(j + 1) * KEY_BLOCK]
    q_group = [functools.reduce(jnp.maximum, q_stats[kv * ATTN_GROUP:(kv + 1) * ATTN_GROUP])
               for kv in range(KV_HEADS)]
    stat_ref[...] = jnp.concatenate(k_stats + q_group + [jnp.zeros((8 - 2 * KV_HEADS, 128), F32)], axis=0)


def _attn_prep(qkv, q_gain, k_gain, cos_t, sin_t, rope):
    B, T, _ = qkv.shape
    tm = min(ATTN_K_CHUNK, T)
    nc = T // tm
    return pl.pallas_call(
        functools.partial(_prep_kernel, rope=rope),
        out_shape=(jax.ShapeDtypeStruct((B, ATTN_WIDTH, T), BF16),
                   jax.ShapeDtypeStruct((B, KV_HEADS, T, K_COLS), BF16),
                   jax.ShapeDtypeStruct((B, KV_HEADS, T // KEY_BLOCK, VT_ROWS, KEY_BLOCK), BF16),
                   jax.ShapeDtypeStruct((B, KV_HEADS, ATTN_GROUP, T), F32),
                   jax.ShapeDtypeStruct((B, nc, 8, 128), F32)),
        grid=(B, nc),
        in_specs=[pl.BlockSpec((None, tm, 768), lambda b, i: (b, i, 0)),
                  pl.BlockSpec((HEAD_DIM, 1), lambda b, i: (0, 0)),
                  pl.BlockSpec((HEAD_DIM, 1), lambda b, i: (0, 0)),
                  pl.BlockSpec((HEAD_DIM, tm), lambda b, i: (0, i)),
                  pl.BlockSpec((HEAD_DIM, tm), lambda b, i: (0, i))],
        out_specs=(pl.BlockSpec((None, ATTN_WIDTH, tm), lambda b, i: (b, 0, i)),
                   pl.BlockSpec((None, KV_HEADS, tm, K_COLS), lambda b, i: (b, 0, i, 0)),
                   pl.BlockSpec((None, KV_HEADS, tm // KEY_BLOCK, VT_ROWS, KEY_BLOCK), lambda b, i: (b, 0, i, 0, 0)),
                   pl.BlockSpec((None, KV_HEADS, ATTN_GROUP, tm), lambda b, i: (b, 0, 0, i)),
                   pl.BlockSpec((None, None, 8, 128), lambda b, i: (b, i, 0, 0))),
        compiler_params=_cparams(("parallel", "parallel")),
        name="attn_prep",
    )(qkv, q_gain.reshape(HEAD_DIM, 1), k_gain.reshape(HEAD_DIM, 1), cos_t, sin_t)


def _rope_tables(T):
    t = jnp.arange(T)
    row = (t // GRID_W).astype(F32)
    col = (t % GRID_W).astype(F32)
    n_freq = HEAD_DIM // 4
    inv = ROPE_THETA ** (-jnp.arange(n_freq, dtype=F32) / n_freq)
    ang_r = row[None, :] * inv[:, None]
    ang_c = col[None, :] * inv[:, None]
    cos_t = jnp.concatenate([jnp.cos(ang_r), jnp.cos(ang_r), jnp.cos(ang_c), jnp.cos(ang_c)], axis=0)
    sin_t = jnp.concatenate([-jnp.sin(ang_r), jnp.sin(ang_r), -jnp.sin(ang_c), jnp.sin(ang_c)], axis=0)
    return cos_t, sin_t


def _score_max_kernel(qt_ref, k_ref, m_ref, qa_ref):
    tq = qt_ref.shape[1]
    nq = ATTN_GROUP * tq
    for g in range(ATTN_GROUP):
        qa_ref[0:HEAD_DIM, g * tq:(g + 1) * tq] = qt_ref[g * HEAD_DIM:(g + 1) * HEAD_DIM, :]
    qa_ref[HEAD_DIM:K_COLS, :] = jnp.zeros((K_COLS - HEAD_DIM, nq), BF16)
    tk = min(ATTN_K_CHUNK, k_ref.shape[0])

    def colmax8(j):
        k = k_ref[pl.ds(pl.multiple_of(j * tk, tk), tk), :]
        s = jnp.dot(k, qa_ref[...], preferred_element_type=F32)
        return jnp.max(s.reshape(tk // 8, 8, nq), axis=0)

    mx = lax.fori_loop(1, k_ref.shape[0] // tk, lambda j, mx: jnp.maximum(mx, colmax8(j)), colmax8(0))
    m = jnp.max(mx, axis=0, keepdims=True)
    for g in range(ATTN_GROUP):
        m_ref[g:g + 1, :] = m[:, g * tq:(g + 1) * tq]


def _score_max(qt, k_all):
    B, _, Tq = qt.shape
    Tk = k_all.shape[2]
    tq = min(ATTN_Q_TILE, Tq)
    return pl.pallas_call(
        _score_max_kernel,
        out_shape=jax.ShapeDtypeStruct((B, KV_HEADS, ATTN_GROUP, Tq), F32),
        grid=(B, KV_HEADS, Tq // tq),
        in_specs=[pl.BlockSpec((None, ATTN_GROUP * HEAD_DIM, tq), lambda b, h, i: (b, h, i)),
                  pl.BlockSpec((None, None, Tk, K_COLS), lambda b, h, i: (b, h, 0, 0))],
        out_specs=pl.BlockSpec((None, None, ATTN_GROUP, tq), lambda b, h, i: (b, h, 0, i)),
        scratch_shapes=[pltpu.VMEM((K_COLS, ATTN_GROUP * tq), BF16)],
        compiler_params=_cparams(("parallel", "parallel", "parallel"), VMEM_LIMIT),
        name="score_max",
    )(qt, k_all)


def _softmax_shift(q_norm, q_stats, k_stats_list, qt, k_all):
    B = qt.shape[0]
    k2 = functools.reduce(jnp.maximum, [st[:, :, 0:KV_HEADS, 0].max(axis=1) for st in k_stats_list])
    q2 = q_stats[:, :, KV_HEADS:2 * KV_HEADS, 0].max(axis=1)
    use_bound = (jnp.sqrt(k2 * q2) * SHIFT_MARGIN <= MAX_BOUND_SHIFT).reshape(B, KV_HEADS, 1, 1)
    bound = q_norm * (jnp.sqrt(k2) * SHIFT_MARGIN).reshape(B, KV_HEADS, 1, 1)
    return lax.cond(jnp.all(use_bound), lambda: bound,
                    lambda: jnp.where(use_bound, bound, _score_max(qt, k_all)))


def _attn_kernel(qt_ref, shift_ref, k_ref, vt_ref, o_ref, qa_ref):
    tq = qt_ref.shape[1]
    n_blocks = k_ref.shape[0] // KEY_BLOCK
    heads_per_mxu = ATTN_GROUP // N_MXU

    row = lax.broadcasted_iota(I32, (BF16_SUBLANES, tq), 0)
    for g in range(ATTN_GROUP):
        m = shift_ref[g:g + 1, :]
        m_hi = m.astype(BF16).astype(F32)
        qa_ref[g, 0:HEAD_DIM, :] = qt_ref[g * HEAD_DIM:(g + 1) * HEAD_DIM, :]
        qa_ref[g, HEAD_DIM:HEAD_DIM + BF16_SUBLANES, :] = jnp.where(
            row == 0, -m_hi, jnp.where(row == 1, m_hi - m, 0.0)).astype(BF16)
        qa_ref[g, HEAD_DIM + BF16_SUBLANES:K_COLS, :] = jnp.zeros((K_COLS - HEAD_DIM - BF16_SUBLANES, tq), BF16)

    def score_addr(slot):
        return SCORE_BASE + slot * (KEY_BLOCK // 4)

    def out_addr(slot):
        return slot * (VT_ROWS // 4)

    def keys(blk):
        return k_ref[pl.ds(pl.multiple_of(blk * KEY_BLOCK, KEY_BLOCK), KEY_BLOCK), :]

    def stage_q(slot):
        for mxu in range(N_MXU):
            pltpu.matmul_push_rhs(qa_ref[mxu * heads_per_mxu + slot], SCORE_REG, mxu)

    def issue_scores(blk, slot):
        k = keys(blk)
        for mxu in range(N_MXU):
            pltpu.matmul_acc_lhs(score_addr(slot), k, mxu, load_staged_rhs=SCORE_REG)

    def pop_probs(slot):
        return [jnp.exp(pltpu.matmul_pop(score_addr(slot), (KEY_BLOCK, tq), F32, mxu)).astype(BF16)
                for mxu in range(N_MXU)]

    def unit(blk, slot, stage_next, first):
        p = None if first else pop_probs(slot)
        issue_scores(blk, slot)
        if p is not None:
            for mxu in range(N_MXU):
                pltpu.matmul_push_rhs(p[mxu], VALUE_REG, mxu)
        if stage_next:
            stage_q((slot + 1) % heads_per_mxu)
        if p is not None:
            vt = vt_ref[blk - 1]
            for mxu in range(N_MXU):
                pltpu.matmul_acc_lhs(out_addr(slot), vt, mxu, load_staged_rhs=VALUE_REG)

    def block(blk, last=False, first=False):
        for slot in range(heads_per_mxu):
            unit(blk, slot, stage_next=not (last and slot == heads_per_mxu - 1), first=first)

    stage_q(0)
    block(0, last=n_blocks == 1, first=True)
    if n_blocks > 1:
        n_loop = n_blocks - 2
        unroll = max([u for u in range(1, ATTN_UNROLL + 1) if n_loop % u == 0] or [1])

        def body(i, carry):
            for u in range(unroll):
                block(1 + i * unroll + u)
            return carry

        if n_loop:
            lax.fori_loop(0, n_loop // unroll, body, 0)
        block(n_blocks - 1, last=True)
    for slot in range(heads_per_mxu):
        p = pop_probs(slot)
        vt = vt_ref[n_blocks - 1]
        for mxu in range(N_MXU):
            pltpu.matmul_push_rhs(p[mxu], VALUE_REG, mxu)
            pltpu.matmul_acc_lhs(out_addr(slot), vt, mxu, load_staged_rhs=VALUE_REG)

    outs = []
    for mxu in range(N_MXU):
        for slot in range(heads_per_mxu):
            acc = pltpu.matmul_pop(out_addr(slot), (VT_ROWS, tq), F32, mxu)
            outs.append(acc[0:HEAD_DIM, :] / acc[HEAD_DIM:HEAD_DIM + 1, :])
    o_ref[...] = jnp.concatenate(outs, axis=0).T.astype(BF16)


def _attention(qt, shift, k_all, vt_all):
    B, _, Tq = qt.shape
    Tk = k_all.shape[2]
    tq = MXU_DIM
    return pl.pallas_call(
        _attn_kernel,
        out_shape=jax.ShapeDtypeStruct((B, Tq, ATTN_WIDTH), BF16),
        grid=(B, KV_HEADS, Tq // tq),
        in_specs=[pl.BlockSpec((None, ATTN_GROUP * HEAD_DIM, tq), lambda b, h, i: (b, h, i)),
                  pl.BlockSpec((None, None, ATTN_GROUP, tq), lambda b, h, i: (b, h, 0, i)),
                  pl.BlockSpec((None, None, Tk, K_COLS), lambda b, h, i: (b, h, 0, 0)),
                  pl.BlockSpec((None, None, Tk // KEY_BLOCK, VT_ROWS, KEY_BLOCK), lambda b, h, i: (b, h, 0, 0, 0))],
        out_specs=pl.BlockSpec((None, tq, ATTN_GROUP * HEAD_DIM), lambda b, h, i: (b, i, h)),
        scratch_shapes=[pltpu.VMEM((ATTN_GROUP, K_COLS, tq), BF16)],
        compiler_params=_cparams(("parallel", "parallel", "parallel"), VMEM_LIMIT),
        name="attention",
    )(qt, shift, k_all, vt_all)


def _ret_kernel(lgc_f_ref, lgc_b_ref, lgr_f_ref, lgr_b_ref, s0f_ref, s0b_ref, blk_f_ref, blk_b_ref,
                of_ref, ob_ref, sf_ref, sb_ref, st_f, st_b, dec_f, dec_b):
    C = RET_CHUNK
    W = RET_WIDTH
    n = pl.program_id(1)

    @pl.when(n == 0)
    def _():
        st_f[...] = s0f_ref[...]
        st_b[...] = s0b_ref[...]
        c = jnp.bitwise_and(lax.broadcasted_iota(I32, (RET_HEADS * C, C), 0), C - 1)
        m = lax.broadcasted_iota(I32, (RET_HEADS * C, C), 1)
        diff = (c - m).astype(F32)
        dec_f[...] = jnp.where(diff >= 0, jnp.exp(lgc_f_ref[...] * jnp.maximum(diff, 0.0)), 0.0)
        dec_b[...] = jnp.where(diff <= 0, jnp.exp(lgc_b_ref[...] * jnp.maximum(-diff, 0.0)), 0.0)

    lane_head = jnp.right_shift(lax.broadcasted_iota(I32, (C, W), 1), 6)
    pos = lax.broadcasted_iota(I32, (C, W), 0).astype(F32)
    same_head = (jnp.right_shift(lax.broadcasted_iota(I32, (W, W), 0), 6)
                 == jnp.right_shift(lax.broadcasted_iota(I32, (W, W), 1), 6))

    def direction(blk_ref, dec_ref, st_ref, lgr, forward, out_ref):
        n_sub = blk_ref.shape[0] // C
        state = st_ref[...]
        for sub in (range(n_sub) if forward else reversed(range(n_sub))):
            state = chunk(blk_ref, dec_ref, state, lgr, forward, out_ref, sub * C)
        st_ref[...] = state

    def chunk(blk_ref, dec_ref, state, lgr, forward, out_ref, r0):
        q = blk_ref[r0:r0 + C, 0:W].astype(F32)
        kf = blk_ref[r0:r0 + C, W:2 * W].astype(F32) * (HEAD_DIM ** -0.5)
        v = blk_ref[r0:r0 + C, 2 * W:3 * W]
        if forward:
            zeta = jnp.exp(lgr * (C - 1.0 - pos))
            xi = jnp.exp(lgr * (pos + 1.0))
        else:
            zeta = jnp.exp(lgr * pos)
            xi = jnp.exp(lgr * (C - pos))
        qexp = jnp.concatenate([jnp.where(lane_head == h, q, 0.0) for h in range(RET_HEADS)],
                               axis=0).astype(BF16)
        a = lax.dot_general(qexp, kf.astype(BF16), (((1,), (1,)), ((), ())),
                            preferred_element_type=F32)
        p = (a * dec_ref[...]).astype(BF16)
        full = jnp.dot(p, v, preferred_element_type=F32)
        intra = jnp.zeros((C, W), F32)
        for h in range(RET_HEADS):
            intra = intra + jnp.where(lane_head == h, full[h * C:(h + 1) * C], 0.0)
        cross = jnp.dot((q * xi).astype(BF16), state.astype(BF16), preferred_element_type=F32)
        out_ref[r0:r0 + C, :] = intra + cross
        upd = lax.dot_general((kf * zeta).astype(BF16), v, (((0,), (0,)), ((), ())),
                              preferred_element_type=F32)
        return jnp.where(same_head, state * jnp.exp(lgr * float(C)) + upd, 0.0)

    direction(blk_f_ref, dec_f, st_f, lgr_f_ref[...], True, of_ref)
    direction(blk_b_ref, dec_b, st_b, lgr_b_ref[...], False, ob_ref)

    @pl.when(n == pl.num_programs(1) - 1)
    def _():
        sf_ref[...] = st_f[...]
        sb_ref[...] = st_b[...]


def _retention(ret, log_gamma, s0f, s0b):
    B, T, _ = ret.shape
    C = RET_CHUNK
    tm = min(RET_TILE, T)
    nc = T // tm
    lgc = lambda d: jnp.repeat(log_gamma[d], C).reshape(RET_HEADS * C, 1)
    lgr = lambda d: jnp.repeat(log_gamma[d], HEAD_DIM).reshape(1, RET_WIDTH)
    const = lambda shape: pl.BlockSpec(shape, lambda b, n: (0,) * len(shape))
    st_spec = pl.BlockSpec((None, RET_WIDTH, RET_WIDTH), lambda b, n: (b, 0, 0))
    return pl.pallas_call(
        _ret_kernel,
        out_shape=(jax.ShapeDtypeStruct((B, T, RET_WIDTH), F32),
                   jax.ShapeDtypeStruct((B, T, RET_WIDTH), F32),
                   jax.ShapeDtypeStruct((B, RET_WIDTH, RET_WIDTH), F32),
                   jax.ShapeDtypeStruct((B, RET_WIDTH, RET_WIDTH), F32)),
        grid=(B, nc),
        in_specs=[const((RET_HEADS * C, 1)), const((RET_HEADS * C, 1)),
                  const((1, RET_WIDTH)), const((1, RET_WIDTH)), st_spec, st_spec,
                  pl.BlockSpec((None, tm, 1024), lambda b, n: (b, n, 0)),
                  pl.BlockSpec((None, tm, 1024), lambda b, n: (b, nc - 1 - n, 0))],
        out_specs=(pl.BlockSpec((None, tm, RET_WIDTH), lambda b, n: (b, n, 0)),
                   pl.BlockSpec((None, tm, RET_WIDTH), lambda b, n: (b, nc - 1 - n, 0)),
                   st_spec, st_spec),
        scratch_shapes=[pltpu.VMEM((RET_WIDTH, RET_WIDTH), F32), pltpu.VMEM((RET_WIDTH, RET_WIDTH), F32),
                        pltpu.VMEM((RET_HEADS * C, C), F32), pltpu.VMEM((RET_HEADS * C, C), F32)],
        compiler_params=_cparams(("parallel", "arbitrary")),
        name="retention",
    )(lgc(0), lgc(1), lgr(0), lgr(1), s0f, s0b, ret, ret)


def _pool_kernel(prev_ref, cur_ref, next_ref, w_ref, scale_ref, o_ref, band_ref, *, seq_len):
    tm = cur_ref.shape[0]
    i = pl.program_id(1)

    @pl.when(i == 0)
    def _():
        tok = lax.broadcasted_iota(I32, (tm, tm + 2 * POOL_HALO), 0)
        src = lax.broadcasted_iota(I32, (tm, tm + 2 * POOL_HALO), 1) - POOL_HALO
        for gi, w in enumerate(POOL_WINDOWS):
            inside = (src >= tok - w // 2) & (src < tok + w // 2)
            band_ref[gi] = jnp.where(inside, 1.0, 0.0).astype(BF16)

    prev = jnp.where(i > 0, prev_ref[...], jnp.zeros_like(prev_ref))
    nxt = jnp.where(i < pl.num_programs(1) - 1, next_ref[...], jnp.zeros_like(next_ref))
    ext = jnp.concatenate([prev, cur_ref[...], nxt], axis=0)
    tcol = i * tm + lax.broadcasted_iota(I32, (tm, 1), 0)
    lane_group = jnp.right_shift(lax.broadcasted_iota(I32, (tm, POOL_WIDTH), 1), 6)
    cur = cur_ref[...].astype(F32)
    mixed = jnp.zeros((tm, POOL_WIDTH), F32)
    for gi, w in enumerate(POOL_WINDOWS):
        total = jnp.dot(band_ref[gi], ext, preferred_element_type=F32)
        cnt = (jnp.minimum(tcol + w // 2, seq_len) - jnp.maximum(tcol - w // 2, 0)).astype(F32)
        mixed = mixed + jnp.where(lane_group == gi, total / cnt - cur, 0.0)
    y = jnp.dot(mixed.astype(BF16), w_ref[...], preferred_element_type=F32)
    o_ref[...] = (y * scale_ref[...]).astype(BF16)


def _pooling(pp, w_blockdiag_bf16, scale):
    B, T, _ = pp.shape
    tm = min(POOL_TILE, T)
    r = tm // POOL_HALO
    last = T // POOL_HALO - 1
    return pl.pallas_call(
        functools.partial(_pool_kernel, seq_len=T),
        out_shape=jax.ShapeDtypeStruct((B, T, POOL_WIDTH), BF16),
        grid=(B, T // tm),
        in_specs=[pl.BlockSpec((None, POOL_HALO, POOL_WIDTH), lambda b, i: (b, jnp.maximum(i * r - 1, 0), 0)),
                  pl.BlockSpec((None, tm, POOL_WIDTH), lambda b, i: (b, i, 0)),
                  pl.BlockSpec((None, POOL_HALO, POOL_WIDTH), lambda b, i: (b, jnp.minimum((i + 1) * r, last), 0)),
                  pl.BlockSpec((POOL_WIDTH, POOL_WIDTH), lambda b, i: (0, 0)),
                  _row_spec(POOL_WIDTH)],
        out_specs=pl.BlockSpec((None, tm, POOL_WIDTH), lambda b, i: (b, i, 0)),
        scratch_shapes=[pltpu.VMEM((len(POOL_WINDOWS), tm, tm + 2 * POOL_HALO), BF16)],
        compiler_params=_cparams(("parallel", "arbitrary")),
        name="pooling",
    )(pp, pp, pp, w_blockdiag_bf16, scale.reshape(1, POOL_WIDTH))


def _head_mean(x, avg):
    hi = x.astype(BF16)
    lo = (x - hi.astype(F32)).astype(BF16)
    return (jnp.dot(hi, avg, preferred_element_type=F32) + jnp.dot(lo, avg, preferred_element_type=F32))


def _mix_tile(x_ref, attn_ref, of_ref, ob_ref, gate_ref, pool_ref, avg_ref, w_ref, npost_ref, g1_ref):
    o = of_ref[...] + ob_ref[...]
    avg = avg_ref[...]
    mu = _head_mean(o, avg)
    cen = o - mu
    var = _head_mean(cen * cen, avg)
    y_ret = (_silu(gate_ref[...].astype(F32)) * (cen * lax.rsqrt(var + NORM_EPS))).astype(BF16)
    mx = (jnp.dot(attn_ref[...], w_ref[0:ATTN_WIDTH, :], preferred_element_type=F32)
          + jnp.dot(y_ret, w_ref[ATTN_WIDTH:ATTN_WIDTH + RET_WIDTH, :], preferred_element_type=F32)
          + jnp.dot(pool_ref[...], w_ref[ATTN_WIDTH + RET_WIDTH:, :], preferred_element_type=F32))
    return x_ref[...] + g1_ref[...] * _rms(mx, npost_ref[...])


N_MIX_INPUTS = 10


def _mix_inputs(mix, tm, layer, ctx):
    x, attn, of, ob, ret, pool, avg_bf16, w_out_bf16, npost, modr = mix
    tok = lambda w: pl.BlockSpec((None, tm, w), lambda b, i: (b, i, 0))
    specs = [tok(D_MODEL), tok(ATTN_WIDTH), tok(RET_WIDTH), tok(RET_WIDTH),
             pl.BlockSpec((None, tm, RET_WIDTH), lambda b, i: (b, i, 3)),
             tok(POOL_WIDTH),
             pl.BlockSpec((RET_WIDTH, RET_WIDTH), lambda b, i: (0, 0)),
             pl.BlockSpec((D_MODEL, D_MODEL), lambda b, i: (0, 0)),
             _row_spec(D_MODEL), _mod_spec(layer, 2, ctx)]
    return [x, attn, of, ob, ret, pool, avg_bf16, w_out_bf16, npost.reshape(1, D_MODEL), modr], specs


def _swiglu_tile(h, w1_ref, w3_ref, w2_ref):
    acc = jnp.zeros((h.shape[0], D_MODEL), F32)
    for f in range(0, D_FF, FF_CHUNK):
        a = jnp.dot(h, w1_ref[:, f:f + FF_CHUNK], preferred_element_type=F32)
        b = jnp.dot(h, w3_ref[:, f:f + FF_CHUNK], preferred_element_type=F32)
        u = (_silu(a) * b).astype(BF16)
        acc = acc + jnp.dot(u, w2_ref[f:f + FF_CHUNK, :], preferred_element_type=F32)
    return acc


def _mix_ffn_kernel(*refs):
    gpre_ref, sc_ref, sh_ref, w1_ref, w3_ref, w2_ref, npost_ref, g2_ref, o_ref = refs[N_MIX_INPUTS:]
    x = _mix_tile(*refs[:N_MIX_INPUTS])
    h = (_rms(x, gpre_ref[...]) * (1.0 + sc_ref[...]) + sh_ref[...]).astype(BF16)
    y = _swiglu_tile(h, w1_ref, w3_ref, w2_ref)
    o_ref[...] = x + g2_ref[...] * _rms(y, npost_ref[...])


def _mix_ffn(mix, gpre, npost, w1, w3, w2, layer, ctx):
    x, modr = mix[0], mix[-1]
    B, T, _ = x.shape
    tm = min(TOKEN_TILE, T)
    args, specs = _mix_inputs(mix, tm, layer, ctx)
    wspec = lambda shape: pl.BlockSpec(shape, lambda b, i: (0, 0), pipeline_mode=pl.Buffered(1))
    return pl.pallas_call(
        _mix_ffn_kernel,
        out_shape=jax.ShapeDtypeStruct((B, T, D_MODEL), F32),
        grid=(B, T // tm),
        in_specs=specs + [_row_spec(D_MODEL), _mod_spec(layer, 4, ctx), _mod_spec(layer, 3, ctx),
                          wspec((D_MODEL, D_FF)), wspec((D_MODEL, D_FF)), wspec((D_FF, D_MODEL)),
                          _row_spec(D_MODEL), _mod_spec(layer, 5, ctx)],
        out_specs=pl.BlockSpec((None, tm, D_MODEL), lambda b, i: (b, i, 0)),
        compiler_params=_cparams(("parallel", "parallel"), VMEM_LIMIT),
        name="mix_ffn",
    )(*args, gpre.reshape(1, D_MODEL), modr, modr, w1, w3, w2, npost.reshape(1, D_MODEL), modr)


def _mix_route_kernel(*refs):
    (gpre_ref, sc_ref, sh_ref, rt_ref, x_ref, h_ref, idx_ref, gate_ref, rank_ref, cnt_ref,
     run_ref, upper_ref) = refs[N_MIX_INPUTS:]
    tm = x_ref.shape[0]
    first = (pl.program_id(0) == 0) & (pl.program_id(1) == 0)

    @pl.when(first)
    def _():
        run_ref[...] = jnp.zeros_like(run_ref)
        earlier = lax.broadcasted_iota(I32, (tm, tm), 0) < lax.broadcasted_iota(I32, (tm, tm), 1)
        upper_ref[...] = jnp.where(earlier, 1.0, 0.0).astype(BF16)

    x = _mix_tile(*refs[:N_MIX_INPUTS])
    x_ref[...] = x
    h = _rms(x, gpre_ref[...]) * (1.0 + sc_ref[...]) + sh_ref[...]
    h_ref[...] = h
    logits = lax.dot_general(rt_ref[...], h, (((1,), (1,)), ((), ())), precision=lax.Precision.HIGHEST,
                             preferred_element_type=F32)
    eid = lax.broadcasted_iota(I32, (N_EXPERTS, tm), 0).astype(F32)
    m1 = jnp.max(logits, axis=0, keepdims=True)
    i1 = jnp.min(jnp.where(logits == m1, eid, float(N_EXPERTS)), axis=0, keepdims=True)
    oh1 = eid == i1
    rest = jnp.where(oh1, -jnp.inf, logits)
    m2 = jnp.max(rest, axis=0, keepdims=True)
    i2 = jnp.min(jnp.where(rest == m2, eid, float(N_EXPERTS)), axis=0, keepdims=True)
    oh2 = eid == i2
    e2 = jnp.exp(m2 - m1)
    gate_ref[0:1, :] = 1.0 / (1.0 + e2)
    gate_ref[1:2, :] = e2 / (1.0 + e2)
    idx_ref[0:1, :] = i1.astype(I32)
    idx_ref[1:2, :] = i2.astype(I32)
    upper = upper_ref[...]
    f1 = jnp.where(oh1, 1.0, 0.0)
    f2 = jnp.where(oh2, 1.0, 0.0)
    before1 = jnp.dot(f1.astype(BF16), upper, preferred_element_type=F32)
    before2 = jnp.dot(f2.astype(BF16), upper, preferred_element_type=F32)
    cnt1 = jnp.sum(f1, axis=1, keepdims=True)
    cnt2 = jnp.sum(f2, axis=1, keepdims=True)
    run = run_ref[:, 0:1]
    rank_ref[0:1, :] = jnp.sum(f1 * (run + before1), axis=0, keepdims=True).astype(I32)
    rank_ref[1:2, :] = jnp.sum(f2 * (run + cnt1 + before2), axis=0, keepdims=True).astype(I32)
    run_new = run_ref[...] + cnt1 + cnt2
    run_ref[...] = run_new
    cnt_ref[...] = run_new


def _mix_route(mix, gpre, router_t, layer):
    x, modr = mix[0], mix[-1]
    B, T, _ = x.shape
    tm = min(TOKEN_TILE, T)
    nt = T // tm
    args, specs = _mix_inputs(mix, tm, layer, False)
    tok = pl.BlockSpec((None, tm, D_MODEL), lambda b, i: (b, i, 0))
    lane = pl.BlockSpec((2, tm), lambda b, i: (0, b * nt + i))
    return pl.pallas_call(
        _mix_route_kernel,
        out_shape=(jax.ShapeDtypeStruct((B, T, D_MODEL), F32),
                   jax.ShapeDtypeStruct((B, T, D_MODEL), F32),
                   jax.ShapeDtypeStruct((2, B * T), I32),
                   jax.ShapeDtypeStruct((2, B * T), F32),
                   jax.ShapeDtypeStruct((2, B * T), I32),
                   jax.ShapeDtypeStruct((N_EXPERTS, 128), F32)),
        grid=(B, nt),
        in_specs=specs + [_row_spec(D_MODEL), _mod_spec(layer, 4, False), _mod_spec(layer, 3, False),
                          pl.BlockSpec((N_EXPERTS, D_MODEL), lambda b, i: (0, 0))],
        out_specs=(tok, tok, lane, lane, lane,
                   pl.BlockSpec((N_EXPERTS, 128), lambda b, i: (0, 0))),
        scratch_shapes=[pltpu.VMEM((N_EXPERTS, 128), F32), pltpu.VMEM((tm, tm), BF16)],
        compiler_params=_cparams(("arbitrary", "arbitrary"), VMEM_LIMIT),
        name="mix_route",
    )(*args, gpre.reshape(1, D_MODEL), modr, modr, router_t)


def _dispatch_kernel(zero_tiles_ref, slot_ref, h_ref, hs_ref, zero_ref, sem, zero_sem):
    tm = h_ref.shape[0]

    @pl.when(pl.program_id(0) == 0)
    def _():
        zero_ref[...] = jnp.zeros_like(zero_ref)
        for j in range(2 * N_EXPERTS):
            start = pl.multiple_of(zero_tiles_ref[j] * EXPERT_TILE, EXPERT_TILE)
            fill = pltpu.make_async_copy(zero_ref, hs_ref.at[pl.ds(start, EXPERT_TILE)], zero_sem)
            fill.start()
            fill.wait()

    def body(i, carry):
        for u in range(DMA_UNROLL):
            r = i * DMA_UNROLL + u
            for k in range(2):
                s = slot_ref[0, k, r]
                pltpu.make_async_copy(h_ref.at[pl.ds(r, 1)], hs_ref.at[pl.ds(s, 1)], sem).start(priority=k)
        return carry

    lax.fori_loop(0, tm // DMA_UNROLL, body, 0)
    for _ in range(2):
        pltpu.make_async_copy(h_ref, hs_ref.at[pl.ds(0, tm)], sem).wait()


def _dispatch(h, slot_tiles, zero_tiles, n_slots):
    N = h.shape[0]
    tm = slot_tiles.shape[2]
    grid_spec = pltpu.PrefetchScalarGridSpec(
        num_scalar_prefetch=1,
        grid=(N // tm,),
        in_specs=[pl.BlockSpec((1, 2, tm), lambda i, zt: (i, 0, 0), memory_space=pltpu.SMEM),
                  pl.BlockSpec((tm, D_MODEL), lambda i, zt: (i, 0))],
        out_specs=pl.BlockSpec(memory_space=pl.ANY),
        scratch_shapes=[pltpu.VMEM((EXPERT_TILE, D_MODEL), F32),
                        pltpu.SemaphoreType.DMA(()), pltpu.SemaphoreType.DMA(())],
    )
    return pl.pallas_call(
        _dispatch_kernel,
        out_shape=jax.ShapeDtypeStruct((n_slots, D_MODEL), F32),
        grid_spec=grid_spec,
        compiler_params=_cparams(("arbitrary",)),
        name="moe_dispatch",
    )(zero_tiles, slot_tiles, h)


def _expert_kernel(te_ref, tv_ref, h_ref, w1_ref, w3_ref, w2_ref, y_ref):
    t = pl.program_id(0)

    @pl.when(tv_ref[t] == 1)
    def _():
        y_ref[...] = _swiglu_tile(h_ref[...].astype(BF16), w1_ref, w3_ref, w2_ref)

    @pl.when(tv_ref[t] == 0)
    def _():
        y_ref[...] = jnp.zeros_like(y_ref)


def _experts(hs, tile_expert, tile_valid, w1, w3, w2):
    n_slots = hs.shape[0]
    tm = EXPERT_TILE
    grid_spec = pltpu.PrefetchScalarGridSpec(
        num_scalar_prefetch=2,
        grid=(n_slots // tm,),
        in_specs=[pl.BlockSpec((tm, D_MODEL), lambda t, te, tv: (t, 0)),
                  pl.BlockSpec((None, D_MODEL, D_FF), lambda t, te, tv: (te[t], 0, 0)),
                  pl.BlockSpec((None, D_MODEL, D_FF), lambda t, te, tv: (te[t], 0, 0)),
                  pl.BlockSpec((None, D_FF, D_MODEL), lambda t, te, tv: (te[t], 0, 0))],
        out_specs=pl.BlockSpec((tm, D_MODEL), lambda t, te, tv: (t, 0)),
    )
    return pl.pallas_call(
        _expert_kernel,
        out_shape=jax.ShapeDtypeStruct((n_slots, D_MODEL), F32),
        grid_spec=grid_spec,
        compiler_params=_cparams(("arbitrary",), VMEM_LIMIT),
        name="moe_experts",
    )(tile_expert, tile_valid, hs, w1, w3, w2)


def _combine_kernel(slot_ref, gate_ref, x_ref, npost_ref, g2_ref, ys_ref, o_ref, buf0, buf1, sem):
    tm = x_ref.shape[0]
    bufs = (buf0, buf1)

    def body(i, carry):
        for u in range(DMA_UNROLL):
            r = i * DMA_UNROLL + u
            for k in range(2):
                s = slot_ref[0, k, r]
                pltpu.make_async_copy(ys_ref.at[pl.ds(s, 1)], bufs[k].at[pl.ds(r, 1)], sem).start(priority=k)
        return carry

    lax.fori_loop(0, tm // DMA_UNROLL, body, 0)
    for k in range(2):
        pltpu.make_async_copy(ys_ref.at[pl.ds(0, tm)], bufs[k], sem).wait()
    y = gate_ref[:, 0:1] * buf0[...] + gate_ref[:, 1:2] * buf1[...]
    o_ref[...] = x_ref[...] + g2_ref[...] * _rms(y, npost_ref[...])


def _combine(x, ys, slot_tiles, gate_tok, npost, modr, layer):
    B, T, _ = x.shape
    tm = slot_tiles.shape[2]
    nt = T // tm
    tok = pl.BlockSpec((None, tm, D_MODEL), lambda b, i: (b, i, 0))
    return pl.pallas_call(
        _combine_kernel,
        out_shape=jax.ShapeDtypeStruct((B, T, D_MODEL), F32),
        grid=(B, nt),
        in_specs=[pl.BlockSpec((1, 2, tm), lambda b, i: (b * nt + i, 0, 0), memory_space=pltpu.SMEM),
                  pl.BlockSpec((tm, 2), lambda b, i: (b * nt + i, 0)),
                  tok, _row_spec(D_MODEL), _mod_spec(layer, 5, False),
                  pl.BlockSpec(memory_space=pl.ANY)],
        out_specs=tok,
        scratch_shapes=[pltpu.VMEM((tm, D_MODEL), F32), pltpu.VMEM((tm, D_MODEL), F32),
                        pltpu.SemaphoreType.DMA(())],
        compiler_params=_cparams(("arbitrary", "arbitrary")),
        name="moe_combine",
    )(slot_tiles, gate_tok, x, npost.reshape(1, D_MODEL), modr, ys)


def _mix_moe_ffn(mix, gpre, npost, router, w1, w3, w2, layer):
    modr = mix[-1]
    B, T, _ = mix[0].shape
    N = B * T
    tm = min(TOKEN_TILE, T)
    x, h, idx, gate, rank, cnt = _mix_route(mix, gpre, router.T, layer)
    n_slots = 2 * N + N_EXPERTS * EXPERT_TILE
    n_tiles = n_slots // EXPERT_TILE
    counts = cnt[:, 0].astype(I32)
    padded = ((counts + EXPERT_TILE - 1) // EXPERT_TILE) * EXPERT_TILE
    ends = jnp.cumsum(padded)
    starts = ends - padded
    slot = rank
    for e in range(N_EXPERTS):
        slot = slot + jnp.where(idx == e, starts[e], 0)
    slot_tiles = slot.reshape(2, N // tm, tm).transpose(1, 0, 2)
    tile_start = jnp.arange(n_tiles, dtype=I32) * EXPERT_TILE
    tile_valid = (tile_start < ends[-1]).astype(I32)
    tile_expert = jnp.minimum(jnp.sum((tile_start[:, None] >= ends[None, :]).astype(I32), axis=1), N_EXPERTS - 1)
    last_expert = jnp.max(jnp.where(tile_valid == 1, tile_expert, 0))
    tile_expert = jnp.where(tile_valid == 1, tile_expert, last_expert)
    last_tile = jnp.where(padded > 0, ends // EXPERT_TILE - 1, n_tiles - 1)
    tail_tile = jnp.minimum(ends[-1] // EXPERT_TILE + jnp.arange(N_EXPERTS, dtype=I32), n_tiles - 1)
    zero_tiles = jnp.concatenate([last_tile, tail_tile]).astype(I32)
    hs = _dispatch(h.reshape(N, D_MODEL), slot_tiles, zero_tiles, n_slots)
    ys = _experts(hs, tile_expert, tile_valid, w1, w3, w2)
    return _combine(x, ys, slot_tiles, gate.T, npost, modr, layer)


def kernel(x, c, ctx, c_ctx, w_mod, b_mod, norm_pre_mix, norm_post_mix, norm_pre_ffn, norm_post_ffn, w_in, w_out, q_norm, k_norm, ret_decay_logit, pool_w, pool_scale, ffn_w1, ffn_w3, ffn_w2, moe_router, moe_w1, moe_w3, moe_w2):
    B, T, _ = x.shape
    cvec = jnp.zeros((MOD_ROWS, D_MODEL), F32).at[0:B].set(c).at[2].set(c_ctx)
    modr = _modulation(cvec, w_mod, b_mod)
    cos_t, sin_t = _rope_tables(T)
    avg = jnp.kron(jnp.eye(RET_HEADS, dtype=F32), jnp.full((HEAD_DIM, HEAD_DIM), 1.0 / HEAD_DIM, F32)).astype(BF16)
    zero_state = jnp.zeros((B, RET_WIDTH, RET_WIDTH), F32)
    xc = ctx
    for i in range(DEPTH):
        need_ctx = i < DEPTH - 1
        w_in_b = w_in[i].astype(BF16)
        w_out_b = w_out[i].astype(BF16)
        pool_bd = jax.scipy.linalg.block_diag(*[pool_w[i, g] for g in range(len(POOL_WINDOWS))]).astype(BF16)
        log_gamma = jax.nn.log_sigmoid(ret_decay_logit[i].astype(F32))

        qkv_c, ret_c, pp_c = _in_projection(xc, modr, norm_pre_mix[i], w_in_b, i, True)
        qt_c, kn_c, vt_c, qn_c, st_c = _attn_prep(qkv_c, q_norm[i], k_norm[i], cos_t, sin_t, False)
        of_c, ob_c, s_fwd, s_bwd = _retention(ret_c, log_gamma, zero_state, zero_state)

        qkv_x, ret_x, pp_x = _in_projection(x, modr, norm_pre_mix[i], w_in_b, i, False)
        qt_x, kn_x, vt_x, qn_x, st_x = _attn_prep(qkv_x, q_norm[i], k_norm[i], cos_t, sin_t, True)
        k_all = jnp.concatenate([kn_c, kn_x], axis=2)
        vt_all = jnp.concatenate([vt_c, vt_x], axis=2)
        attn_x = _attention(qt_x, _softmax_shift(qn_x, st_x, [st_x, st_c], qt_x, k_all), k_all, vt_all)
        of_x, ob_x, _, _ = _retention(ret_x, log_gamma, s_fwd, s_bwd)
        pool_x = _pooling(pp_x, pool_bd, pool_scale[i])
        mix_x = (x, attn_x, of_x, ob_x, ret_x, pool_x, avg, w_out_b, norm_post_mix[i], modr)

        j = i // 2
        if i % 2 == 0:
            w1, w3, w2 = ffn_w1[j].astype(BF16), ffn_w3[j].astype(BF16), ffn_w2[j].astype(BF16)
            x = _mix_ffn(mix_x, norm_pre_ffn[i], norm_post_ffn[i], w1, w3, w2, i, False)
            if need_ctx:
                attn_c = _attention(qt_c, _softmax_shift(qn_c, st_c, [st_c], qt_c, kn_c), kn_c, vt_c)
                pool_c = _pooling(pp_c, pool_bd, pool_scale[i])
                mix_c = (xc, attn_c, of_c, ob_c, ret_c, pool_c, avg, w_out_b, norm_post_mix[i], modr)
                xc = _mix_ffn(mix_c, norm_pre_ffn[i], norm_post_ffn[i], w1, w3, w2, i, True)
        else:
            assert not need_ctx
            x = _mix_moe_ffn(mix_x, norm_pre_ffn[i], norm_post_ffn[i], moe_router[j], moe_w1[j].astype(BF16),
                             moe_w3[j].astype(BF16), moe_w2[j].astype(BF16), i)
    return x
```

```python
import functools

import jax
import jax.numpy as jnp
from jax import lax
from jax.experimental import pallas as pl
from jax.experimental.pallas import tpu as pltpu

F32 = jnp.float32
BF16 = jnp.bfloat16
I32 = jnp.int32

D_MODEL = 1024
GRID_W = 64
HEAD_DIM = 64
ATTN_WIDTH = 512
KV_HEADS = 2
ATTN_GROUP = 4
KV_WIDTH = 128
RET_WIDTH = 256
RET_HEADS = 4
POOL_WIDTH = 256
POOL_WINDOWS = (2, 4, 8, 16)
IN_WIDTH = 2048
RET_CHUNK = 128
ROPE_THETA = 10000.0
D_FF = 2816
N_EXPERTS = 8
NORM_EPS = 1e-6
DEPTH = 2

TOKEN_TILE = 512
ATTN_Q_TILE = 512
ATTN_K_CHUNK = 2048
PREP_TILE = 2048
RET_TILE = 512
POOL_TILE = 256
POOL_HALO = 16
FF_CHUNK = 256
EXPERT_TILE = 512
DMA_UNROLL = 8
BF16_SUBLANES = 16
MXU_DIM = 256
K_COLS = MXU_DIM
KEY_BLOCK = MXU_DIM
ATTN_UNROLL = 8
N_MXU = 2
SCORE_REG, VALUE_REG = 1, 0
SCORE_BASE = 64
VT_ROWS = HEAD_DIM
MAX_BOUND_SHIFT = 40.0
SHIFT_MARGIN = 1.01
MOD_ROWS = 8
VMEM_LIMIT = 56 * 1024 * 1024


def _cparams(sem, vmem=None):
    return pltpu.CompilerParams(dimension_semantics=sem, vmem_limit_bytes=vmem)


def _rms(x, gain):
    ms = jnp.mean(x * x, axis=-1, keepdims=True)
    return x * lax.rsqrt(ms + NORM_EPS) * gain


def _silu(x):
    return x * jax.nn.sigmoid(x)


def _mod_kernel(c_ref, w_ref, b_ref, o_ref):
    s = _silu(c_ref[...])
    o_ref[...] = jnp.dot(s, w_ref[...], precision=lax.Precision.HIGHEST,
                         preferred_element_type=F32) + b_ref[...]


def _modulation(cvec, w_mod, b_mod):
    nchunk = 6
    out = pl.pallas_call(
        _mod_kernel,
        out_shape=jax.ShapeDtypeStruct((DEPTH, MOD_ROWS, 6 * D_MODEL), F32),
        grid=(DEPTH, nchunk),
        in_specs=[
            pl.BlockSpec((MOD_ROWS, D_MODEL), lambda l, j: (0, 0)),
            pl.BlockSpec((None, D_MODEL, D_MODEL), lambda l, j: (l, 0, j)),
            pl.BlockSpec((None, 1, D_MODEL), lambda l, j: (l, 0, j)),
        ],
        out_specs=pl.BlockSpec((None, MOD_ROWS, D_MODEL), lambda l, j: (l, 0, j)),
        compiler_params=_cparams(("parallel", "parallel")),
        name="modulation",
    )(cvec, w_mod, b_mod.reshape(DEPTH, 1, 6 * D_MODEL))
    return out.reshape(DEPTH * MOD_ROWS * nchunk, 1, D_MODEL)


def _mod_spec(layer, chunk, ctx):
    base = layer * MOD_ROWS * 6
    if ctx:
        return pl.BlockSpec((None, 1, D_MODEL), lambda b, i: (base + 2 * 6 + chunk, 0, 0))
    return pl.BlockSpec((None, 1, D_MODEL), lambda b, i: (base + b * 6 + chunk, 0, 0))


def _row_spec(width):
    return pl.BlockSpec((1, width), lambda b, i: (0, 0))


def _inproj_kernel(x_ref, g_ref, sc_ref, sh_ref, w_ref, qkv_ref, ret_ref, pp_ref):
    h = _rms(x_ref[...], g_ref[...]) * (1.0 + sc_ref[...]) + sh_ref[...]
    p = jnp.dot(h.astype(BF16), w_ref[...], preferred_element_type=F32)
    qkv_ref[...] = p[:, :768].astype(BF16)
    ret_ref[...] = p[:, 768:1792].astype(BF16)
    pp_ref[...] = p[:, 1792:].astype(BF16)


def _in_projection(x, modr, gain, w_in_bf16, layer, ctx):
    B, T, _ = x.shape
    tm = min(TOKEN_TILE, T)
    tok = lambda w: pl.BlockSpec((None, tm, w), lambda b, i: (b, i, 0))
    return pl.pallas_call(
        _inproj_kernel,
        out_shape=(jax.ShapeDtypeStruct((B, T, 768), BF16),
                   jax.ShapeDtypeStruct((B, T, 1024), BF16),
                   jax.ShapeDtypeStruct((B, T, POOL_WIDTH), BF16)),
        grid=(B, T // tm),
        in_specs=[tok(D_MODEL), _row_spec(D_MODEL), _mod_spec(layer, 1, ctx), _mod_spec(layer, 0, ctx),
                  pl.BlockSpec((D_MODEL, IN_WIDTH), lambda b, i: (0, 0))],
        out_specs=(tok(768), tok(1024), tok(POOL_WIDTH)),
        compiler_params=_cparams(("parallel", "parallel"), VMEM_LIMIT),
        name="in_projection",
    )(x, gain.reshape(1, D_MODEL), modr, modr, w_in_bf16)


def _prep_kernel(qkv_ref, qg_ref, kg_ref, cos_ref, sin_ref, qt_ref, kn_ref, vt_ref, qn_ref, stat_ref, *, rope):
    t = qkv_ref[...].astype(F32).T

    def norm_rope(blk, gain):
        ms = jnp.mean(blk * blk, axis=0, keepdims=True)
        y = blk * lax.rsqrt(ms + NORM_EPS) * gain
        if rope:
            partner = jnp.concatenate([y[16:32], y[0:16], y[48:64], y[32:48]], axis=0)
            y = y * cos_ref[...] + partner * sin_ref[...]
        return y

    def sq_norm(y):
        return jnp.sum(y * y, axis=0, keepdims=True)

    def row_max(n2):
        return jnp.broadcast_to(jnp.max(n2, axis=1, keepdims=True), (1, 128))

    q_stats = []
    for h in range(ATTN_WIDTH // HEAD_DIM):
        lo = h * HEAD_DIM
        q = norm_rope(t[lo:lo + HEAD_DIM], qg_ref[...]) * (HEAD_DIM ** -0.5)
        qt_ref[lo:lo + HEAD_DIM, :] = q.astype(BF16)
        n2 = sq_norm(q)
        qn_ref[h // ATTN_GROUP, h % ATTN_GROUP:h % ATTN_GROUP + 1, :] = jnp.sqrt(n2)
        q_stats.append(row_max(n2))
    tm = t.shape[1]
    k_pad = jnp.where(lax.broadcasted_iota(I32, (K_COLS - HEAD_DIM, tm), 0) < 2, 1.0, 0.0)
    k_stats = []
    for kv in range(KV_HEADS):
        lo = ATTN_WIDTH + kv * HEAD_DIM
        k = norm_rope(t[lo:lo + HEAD_DIM], kg_ref[...])
        k_stats.append(row_max(sq_norm(k)))
        kn_ref[kv] = jnp.concatenate([k, k_pad], axis=0).T.astype(BF16)
        lo = ATTN_WIDTH + KV_WIDTH + kv * HEAD_DIM
        vt = t[lo:lo + HEAD_DIM].astype(BF16)
        for j in range(tm // KEY_BLOCK):
            vt_ref[kv, j] = vt[:, j * KEY_BLOCK:(j + 1) * KEY_BLOCK]
    q_group = [functools.reduce(jnp.maximum, q_stats[kv * ATTN_GROUP:(kv + 1) * ATTN_GROUP])
               for kv in range(KV_HEADS)]
    stat_ref[...] = jnp.concatenate(k_stats + q_group + [jnp.zeros((8 - 2 * KV_HEADS, 128), F32)], axis=0)


def _attn_prep(qkv, q_gain, k_gain, cos_t, sin_t, rope):
    B, T, _ = qkv.shape
    tm = min(PREP_TILE, T)
    nc = T // tm
    return pl.pallas_call(
        functools.partial(_prep_kernel, rope=rope),
        out_shape=(jax.ShapeDtypeStruct((B, ATTN_WIDTH, T), BF16),
                   jax.ShapeDtypeStruct((B, KV_HEADS, T, K_COLS), BF16),
                   jax.ShapeDtypeStruct((B, KV_HEADS, T // KEY_BLOCK, VT_ROWS, KEY_BLOCK), BF16),
                   jax.ShapeDtypeStruct((B, KV_HEADS, ATTN_GROUP, T), F32),
                   jax.ShapeDtypeStruct((B, nc, 8, 128), F32)),
        grid=(B, nc),
        in_specs=[pl.BlockSpec((None, tm, 768), lambda b, i: (b, i, 0)),
                  pl.BlockSpec((HEAD_DIM, 1), lambda b, i: (0, 0)),
                  pl.BlockSpec((HEAD_DIM, 1), lambda b, i: (0, 0)),
                  pl.BlockSpec((HEAD_DIM, tm), lambda b, i: (0, i)),
                  pl.BlockSpec((HEAD_DIM, tm), lambda b, i: (0, i))],
        out_specs=(pl.BlockSpec((None, ATTN_WIDTH, tm), lambda b, i: (b, 0, i)),
                   pl.BlockSpec((None, KV_HEADS, tm, K_COLS), lambda b, i: (b, 0, i, 0)),
                   pl.BlockSpec((None, KV_HEADS, tm // KEY_BLOCK, VT_ROWS, KEY_BLOCK), lambda b, i: (b, 0, i, 0, 0)),
                   pl.BlockSpec((None, KV_HEADS, ATTN_GROUP, tm), lambda b, i: (b, 0, 0, i)),
                   pl.BlockSpec((None, None, 8, 128), lambda b, i: (b, i, 0, 0))),
        compiler_params=_cparams(("parallel", "parallel")),
        name="attn_prep",
    )(qkv, q_gain.reshape(HEAD_DIM, 1), k_gain.reshape(HEAD_DIM, 1), cos_t, sin_t)


def _rope_tables(T):
    t = jnp.arange(T)
    row = (t // GRID_W).astype(F32)
    col = (t % GRID_W).astype(F32)
    n_freq = HEAD_DIM // 4
    inv = ROPE_THETA ** (-jnp.arange(n_freq, dtype=F32) / n_freq)
    ang_r = row[None, :] * inv[:, None]
    ang_c = col[None, :] * inv[:, None]
    cos_t = jnp.concatenate([jnp.cos(ang_r), jnp.cos(ang_r), jnp.cos(ang_c), jnp.cos(ang_c)], axis=0)
    sin_t = jnp.concatenate([-jnp.sin(ang_r), jnp.sin(ang_r), -jnp.sin(ang_c), jnp.sin(ang_c)], axis=0)
    return cos_t, sin_t


def _score_max_kernel(qt_ref, k_ref, m_ref, qa_ref):
    tq = qt_ref.shape[1]
    nq = ATTN_GROUP * tq
    for g in range(ATTN_GROUP):
        qa_ref[0:HEAD_DIM, g * tq:(g + 1) * tq] = qt_ref[g * HEAD_DIM:(g + 1) * HEAD_DIM, :]
    qa_ref[HEAD_DIM:K_COLS, :] = jnp.zeros((K_COLS - HEAD_DIM, nq), BF16)
    tk = min(ATTN_K_CHUNK, k_ref.shape[0])

    def colmax8(j):
        k = k_ref[pl.ds(pl.multiple_of(j * tk, tk), tk), :]
        s = jnp.dot(k, qa_ref[...], preferred_element_type=F32)
        return jnp.max(s.reshape(tk // 8, 8, nq), axis=0)

    mx = lax.fori_loop(1, k_ref.shape[0] // tk, lambda j, mx: jnp.maximum(mx, colmax8(j)), colmax8(0))
    m = jnp.max(mx, axis=0, keepdims=True)
    for g in range(ATTN_GROUP):
        m_ref[g:g + 1, :] = m[:, g * tq:(g + 1) * tq]


def _score_max(qt, k_all):
    B, _, Tq = qt.shape
    Tk = k_all.shape[2]
    tq = min(ATTN_Q_TILE, Tq)
    return pl.pallas_call(
        _score_max_kernel,
        out_shape=jax.ShapeDtypeStruct((B, KV_HEADS, ATTN_GROUP, Tq), F32),
        grid=(B, KV_HEADS, Tq // tq),
        in_specs=[pl.BlockSpec((None, ATTN_GROUP * HEAD_DIM, tq), lambda b, h, i: (b, h, i)),
                  pl.BlockSpec((None, None, Tk, K_COLS), lambda b, h, i: (b, h, 0, 0))],
        out_specs=pl.BlockSpec((None, None, ATTN_GROUP, tq), lambda b, h, i: (b, h, 0, i)),
        scratch_shapes=[pltpu.VMEM((K_COLS, ATTN_GROUP * tq), BF16)],
        compiler_params=_cparams(("parallel", "parallel", "parallel"), VMEM_LIMIT),
        name="score_max",
    )(qt, k_all)


def _softmax_shift(q_norm, q_stats, k_stats_list, qt, key_sets):
    B = qt.shape[0]
    k2 = functools.reduce(jnp.maximum, [st[:, :, 0:KV_HEADS, 0].max(axis=1) for st in k_stats_list])
    q2 = q_stats[:, :, KV_HEADS:2 * KV_HEADS, 0].max(axis=1)
    use_bound = (jnp.sqrt(k2 * q2) * SHIFT_MARGIN <= MAX_BOUND_SHIFT).reshape(B, KV_HEADS, 1, 1)
    bound = q_norm * (jnp.sqrt(k2) * SHIFT_MARGIN).reshape(B, KV_HEADS, 1, 1)
    exact = lambda: _score_max(qt, jnp.concatenate([k for k, _ in key_sets], axis=2))
    return lax.cond(jnp.all(use_bound), lambda: bound, lambda: jnp.where(use_bound, bound, exact()))


def _attn_kernel(qt_ref, shift_ref, *refs):
    *kv_refs, o_ref, qa_ref = refs
    sources = [(kv_refs[i], kv_refs[i + 1]) for i in range(0, len(kv_refs), 2)]
    counts = [k_ref.shape[0] // KEY_BLOCK for k_ref, _ in sources]
    tq = qt_ref.shape[1]
    n_blocks = sum(counts)
    heads_per_mxu = ATTN_GROUP // N_MXU

    def locate(blk):
        if not isinstance(blk, int):
            return 0, blk
        src = 0
        while blk >= counts[src]:
            blk -= counts[src]
            src += 1
        return src, blk

    row = lax.broadcasted_iota(I32, (BF16_SUBLANES, tq), 0)
    for g in range(ATTN_GROUP):
        m = shift_ref[g:g + 1, :]
        m_hi = m.astype(BF16).astype(F32)
        qa_ref[g, 0:HEAD_DIM, :] = qt_ref[g * HEAD_DIM:(g + 1) * HEAD_DIM, :]
        qa_ref[g, HEAD_DIM:HEAD_DIM + BF16_SUBLANES, :] = jnp.where(
            row == 0, -m_hi, jnp.where(row == 1, m_hi - m, 0.0)).astype(BF16)
        qa_ref[g, HEAD_DIM + BF16_SUBLANES:K_COLS, :] = jnp.zeros((K_COLS - HEAD_DIM - BF16_SUBLANES, tq), BF16)

    def score_addr(slot):
        return SCORE_BASE + slot * (KEY_BLOCK // 4)

    def out_addr(slot):
        return slot * (VT_ROWS // 4)

    def keys(blk):
        src, j = locate(blk)
        return sources[src][0][pl.ds(pl.multiple_of(j * KEY_BLOCK, KEY_BLOCK), KEY_BLOCK), :]

    def stage_q(slot):
        for mxu in range(N_MXU):
            pltpu.matmul_push_rhs(qa_ref[mxu * heads_per_mxu + slot], SCORE_REG, mxu)

    def issue_scores(blk, slot):
        k = keys(blk)
        for mxu in range(N_MXU):
            pltpu.matmul_acc_lhs(score_addr(slot), k, mxu, load_staged_rhs=SCORE_REG)

    def pop_probs(slot, den):
        den = list(den)
        probs = []
        for mxu in range(N_MXU):
            e = jnp.exp(pltpu.matmul_pop(score_addr(slot), (KEY_BLOCK, tq), F32, mxu))
            den[mxu * heads_per_mxu + slot] += e.reshape(KEY_BLOCK // 8, 8, tq).sum(axis=0)
            probs.append(e.astype(BF16))
        return probs, tuple(den)

    def issue_values(blk, slot, p):
        src, j = locate(blk)
        vt = sources[src][1][j]
        for mxu in range(N_MXU):
            pltpu.matmul_acc_lhs(out_addr(slot), vt, mxu, load_staged_rhs=VALUE_REG)

    def unit(blk, slot, den, stage_next, first):
        if not first:
            p, den = pop_probs(slot, den)
        issue_scores(blk, slot)
        if not first:
            for mxu in range(N_MXU):
                pltpu.matmul_push_rhs(p[mxu], VALUE_REG, mxu)
        if stage_next:
            stage_q((slot + 1) % heads_per_mxu)
        if not first:
            issue_values(blk - 1, slot, p)
        return den

    def block(blk, den, last=False, first=False):
        for slot in range(heads_per_mxu):
            den = unit(blk, slot, den, stage_next=not (last and slot == heads_per_mxu - 1), first=first)
        return den

    den = tuple(jnp.zeros((8, tq), F32) for _ in range(ATTN_GROUP))
    stage_q(0)
    den = block(0, den, last=n_blocks == 1, first=True)
    if n_blocks > 1:
        n_loop = n_blocks - 2
        assert n_loop + 1 <= counts[0]
        unroll = max([u for u in range(1, ATTN_UNROLL + 1) if n_loop % u == 0] or [1])

        def body(i, den):
            for u in range(unroll):
                den = block(1 + i * unroll + u, den)
            return den

        if n_loop:
            den = lax.fori_loop(0, n_loop // unroll, body, den)
        den = block(n_blocks - 1, den, last=True)
    for slot in range(heads_per_mxu):
        p, den = pop_probs(slot, den)
        for mxu in range(N_MXU):
            pltpu.matmul_push_rhs(p[mxu], VALUE_REG, mxu)
        issue_values(n_blocks - 1, slot, p)

    outs = []
    for mxu in range(N_MXU):
        for slot in range(heads_per_mxu):
            acc = pltpu.matmul_pop(out_addr(slot), (VT_ROWS, tq), F32, mxu)
            outs.append(acc / jnp.sum(den[mxu * heads_per_mxu + slot], axis=0, keepdims=True))
    o_ref[...] = jnp.concatenate(outs, axis=0).T.astype(BF16)


def _attention(qt, shift, key_sets):
    B, _, Tq = qt.shape
    tq = MXU_DIM
    in_specs = [pl.BlockSpec((None, ATTN_GROUP * HEAD_DIM, tq), lambda b, h, i: (b, h, i)),
                pl.BlockSpec((None, None, ATTN_GROUP, tq), lambda b, h, i: (b, h, 0, i))]
    args = [qt, shift]
    for k, vt in key_sets:
        Tk = k.shape[2]
        in_specs += [pl.BlockSpec((None, None, Tk, K_COLS), lambda b, h, i: (b, h, 0, 0)),
                     pl.BlockSpec((None, None, Tk // KEY_BLOCK, VT_ROWS, KEY_BLOCK), lambda b, h, i: (b, h, 0, 0, 0))]
        args += [k, vt]
    return pl.pallas_call(
        _attn_kernel,
        out_shape=jax.ShapeDtypeStruct((B, Tq, ATTN_WIDTH), BF16),
        grid=(B, KV_HEADS, Tq // tq),
        in_specs=in_specs,
        out_specs=pl.BlockSpec((None, tq, ATTN_GROUP * HEAD_DIM), lambda b, h, i: (b, i, h)),
        scratch_shapes=[pltpu.VMEM((ATTN_GROUP, K_COLS, tq), BF16)],
        compiler_params=_cparams(("parallel", "parallel", "parallel"), VMEM_LIMIT),
        name="attention",
    )(*args)


def _ret_kernel(lgc_f_ref, lgc_b_ref, lgr_f_ref, lgr_b_ref, s0f_ref, s0b_ref, blk_f_ref, blk_b_ref,
                of_ref, ob_ref, sf_ref, sb_ref, st_f, st_b, dec_f, dec_b):
    C = RET_CHUNK
    W = RET_WIDTH
    n = pl.program_id(1)

    @pl.when(n == 0)
    def _():
        st_f[...] = s0f_ref[...]
        st_b[...] = s0b_ref[...]
        c = jnp.bitwise_and(lax.broadcasted_iota(I32, (RET_HEADS * C, C), 0), C - 1)
        m = lax.broadcasted_iota(I32, (RET_HEADS * C, C), 1)
        diff = (c - m).astype(F32)
        dec_f[...] = jnp.where(diff >= 0, jnp.exp(lgc_f_ref[...] * jnp.maximum(diff, 0.0)), 0.0)
        dec_b[...] = jnp.where(diff <= 0, jnp.exp(lgc_b_ref[...] * jnp.maximum(-diff, 0.0)), 0.0)

    lane_head = jnp.right_shift(lax.broadcasted_iota(I32, (C, W), 1), 6)
    pos = lax.broadcasted_iota(I32, (C, W), 0).astype(F32)
    same_head = (jnp.right_shift(lax.broadcasted_iota(I32, (W, W), 0), 6)
                 == jnp.right_shift(lax.broadcasted_iota(I32, (W, W), 1), 6))

    def direction(blk_ref, dec_ref, st_ref, lgr, forward, out_ref):
        n_sub = blk_ref.shape[0] // C
        state = st_ref[...]
        for sub in (range(n_sub) if forward else reversed(range(n_sub))):
            state = chunk(blk_ref, dec_ref, state, lgr, forward, out_ref, sub * C)
        st_ref[...] = state

    def chunk(blk_ref, dec_ref, state, lgr, forward, out_ref, r0):
        q = blk_ref[r0:r0 + C, 0:W].astype(F32)
        kf = blk_ref[r0:r0 + C, W:2 * W].astype(F32) * (HEAD_DIM ** -0.5)
        v = blk_ref[r0:r0 + C, 2 * W:3 * W]
        if forward:
            zeta = jnp.exp(lgr * (C - 1.0 - pos))
            xi = jnp.exp(lgr * (pos + 1.0))
        else:
            zeta = jnp.exp(lgr * pos)
            xi = jnp.exp(lgr * (C - pos))
        qexp = jnp.concatenate([jnp.where(lane_head == h, q, 0.0) for h in range(RET_HEADS)],
                               axis=0).astype(BF16)
        a = lax.dot_general(qexp, kf.astype(BF16), (((1,), (1,)), ((), ())),
                            preferred_element_type=F32)
        p = (a * dec_ref[...]).astype(BF16)
        full = jnp.dot(p, v, preferred_element_type=F32)
        intra = jnp.zeros((C, W), F32)
        for h in range(RET_HEADS):
            intra = intra + jnp.where(lane_head == h, full[h * C:(h + 1) * C], 0.0)
        cross = jnp.dot((q * xi).astype(BF16), state.astype(BF16), preferred_element_type=F32)
        out_ref[r0:r0 + C, :] = intra + cross
        upd = lax.dot_general((kf * zeta).astype(BF16), v, (((0,), (0,)), ((), ())),
                              preferred_element_type=F32)
        return jnp.where(same_head, state * jnp.exp(lgr * float(C)) + upd, 0.0)

    direction(blk_f_ref, dec_f, st_f, lgr_f_ref[...], True, of_ref)
    direction(blk_b_ref, dec_b, st_b, lgr_b_ref[...], False, ob_ref)

    @pl.when(n == pl.num_programs(1) - 1)
    def _():
        sf_ref[...] = st_f[...]
        sb_ref[...] = st_b[...]


def _retention(ret, log_gamma, s0f, s0b):
    B, T, _ = ret.shape
    C = RET_CHUNK
    tm = min(RET_TILE, T)
    nc = T // tm
    lgc = lambda d: jnp.repeat(log_gamma[d], C).reshape(RET_HEADS * C, 1)
    lgr = lambda d: jnp.repeat(log_gamma[d], HEAD_DIM).reshape(1, RET_WIDTH)
    const = lambda shape: pl.BlockSpec(shape, lambda b, n: (0,) * len(shape))
    st_spec = pl.BlockSpec((None, RET_WIDTH, RET_WIDTH), lambda b, n: (b, 0, 0))
    return pl.pallas_call(
        _ret_kernel,
        out_shape=(jax.ShapeDtypeStruct((B, T, RET_WIDTH), F32),
                   jax.ShapeDtypeStruct((B, T, RET_WIDTH), F32),
                   jax.ShapeDtypeStruct((B, RET_WIDTH, RET_WIDTH), F32),
                   jax.ShapeDtypeStruct((B, RET_WIDTH, RET_WIDTH), F32)),
        grid=(B, nc),
        in_specs=[const((RET_HEADS * C, 1)), const((RET_HEADS * C, 1)),
                  const((1, RET_WIDTH)), const((1, RET_WIDTH)), st_spec, st_spec,
                  pl.BlockSpec((None, tm, 1024), lambda b, n: (b, n, 0)),
                  pl.BlockSpec((None, tm, 1024), lambda b, n: (b, nc - 1 - n, 0))],
        out_specs=(pl.BlockSpec((None, tm, RET_WIDTH), lambda b, n: (b, n, 0)),
                   pl.BlockSpec((None, tm, RET_WIDTH), lambda b, n: (b, nc - 1 - n, 0)),
                   st_spec, st_spec),
        scratch_shapes=[pltpu.VMEM((RET_WIDTH, RET_WIDTH), F32), pltpu.VMEM((RET_WIDTH, RET_WIDTH), F32),
                        pltpu.VMEM((RET_HEADS * C, C), F32), pltpu.VMEM((RET_HEADS * C, C), F32)],
        compiler_params=_cparams(("parallel", "arbitrary")),
        name="retention",
    )(lgc(0), lgc(1), lgr(0), lgr(1), s0f, s0b, ret, ret)


def _pool_kernel(prev_ref, cur_ref, next_ref, w_ref, scale_ref, o_ref, band_ref, *, seq_len):
    tm = cur_ref.shape[0]
    i = pl.program_id(1)

    @pl.when(i == 0)
    def _():
        tok = lax.broadcasted_iota(I32, (tm, tm + 2 * POOL_HALO), 0)
        src = lax.broadcasted_iota(I32, (tm, tm + 2 * POOL_HALO), 1) - POOL_HALO
        for gi, w in enumerate(POOL_WINDOWS):
            inside = (src >= tok - w // 2) & (src < tok + w // 2)
            band_ref[gi] = jnp.where(inside, 1.0, 0.0).astype(BF16)

    prev = jnp.where(i > 0, prev_ref[...], jnp.zeros_like(prev_ref))
    nxt = jnp.where(i < pl.num_programs(1) - 1, next_ref[...], jnp.zeros_like(next_ref))
    ext = jnp.concatenate([prev, cur_ref[...], nxt], axis=0)
    tcol = i * tm + lax.broadcasted_iota(I32, (tm, 1), 0)
    lane_group = jnp.right_shift(lax.broadcasted_iota(I32, (tm, POOL_WIDTH), 1), 6)
    cur = cur_ref[...].astype(F32)
    mixed = jnp.zeros((tm, POOL_WIDTH), F32)
    for gi, w in enumerate(POOL_WINDOWS):
        total = jnp.dot(band_ref[gi], ext, preferred_element_type=F32)
        cnt = (jnp.minimum(tcol + w // 2, seq_len) - jnp.maximum(tcol - w // 2, 0)).astype(F32)
        mixed = mixed + jnp.where(lane_group == gi, total / cnt - cur, 0.0)
    y = jnp.dot(mixed.astype(BF16), w_ref[...], preferred_element_type=F32)
    o_ref[...] = (y * scale_ref[...]).astype(BF16)


def _pooling(pp, w_blockdiag_bf16, scale):
    B, T, _ = pp.shape
    tm = min(POOL_TILE, T)
    r = tm // POOL_HALO
    last = T // POOL_HALO - 1
    return pl.pallas_call(
        functools.partial(_pool_kernel, seq_len=T),
        out_shape=jax.ShapeDtypeStruct((B, T, POOL_WIDTH), BF16),
        grid=(B, T // tm),
        in_specs=[pl.BlockSpec((None, POOL_HALO, POOL_WIDTH), lambda b, i: (b, jnp.maximum(i * r - 1, 0), 0)),
                  pl.BlockSpec((None, tm, POOL_WIDTH), lambda b, i: (b, i, 0)),
                  pl.BlockSpec((None, POOL_HALO, POOL_WIDTH), lambda b, i: (b, jnp.minimum((i + 1) * r, last), 0)),
                  pl.BlockSpec((POOL_WIDTH, POOL_WIDTH), lambda b, i: (0, 0)),
                  _row_spec(POOL_WIDTH)],
        out_specs=pl.BlockSpec((None, tm, POOL_WIDTH), lambda b, i: (b, i, 0)),
        scratch_shapes=[pltpu.VMEM((len(POOL_WINDOWS), tm, tm + 2 * POOL_HALO), BF16)],
        compiler_params=_cparams(("parallel", "arbitrary")),
        name="pooling",
    )(pp, pp, pp, w_blockdiag_bf16, scale.reshape(1, POOL_WIDTH))


def _head_mean(x, avg):
    hi = x.astype(BF16)
    lo = (x - hi.astype(F32)).astype(BF16)
    return (jnp.dot(hi, avg, preferred_element_type=F32) + jnp.dot(lo, avg, preferred_element_type=F32))


def _mix_tile(x_ref, attn_ref, of_ref, ob_ref, gate_ref, pool_ref, avg_ref, w_ref, npost_ref, g1_ref):
    o = of_ref[...] + ob_ref[...]
    avg = avg_ref[...]
    mu = _head_mean(o, avg)
    cen = o - mu
    var = _head_mean(cen * cen, avg)
    y_ret = (_silu(gate_ref[...].astype(F32)) * (cen * lax.rsqrt(var + NORM_EPS))).astype(BF16)
    mx = (jnp.dot(attn_ref[...], w_ref[0:ATTN_WIDTH, :], preferred_element_type=F32)
          + jnp.dot(y_ret, w_ref[ATTN_WIDTH:ATTN_WIDTH + RET_WIDTH, :], preferred_element_type=F32)
          + jnp.dot(pool_ref[...], w_ref[ATTN_WIDTH + RET_WIDTH:, :], preferred_element_type=F32))
    return x_ref[...] + g1_ref[...] * _rms(mx, npost_ref[...])


N_MIX_INPUTS = 10


def _mix_inputs(mix, tm, layer, ctx):
    x, attn, of, ob, ret, pool, avg_bf16, w_out_bf16, npost, modr = mix
    tok = lambda w: pl.BlockSpec((None, tm, w), lambda b, i: (b, i, 0))
    specs = [tok(D_MODEL), tok(ATTN_WIDTH), tok(RET_WIDTH), tok(RET_WIDTH),
             pl.BlockSpec((None, tm, RET_WIDTH), lambda b, i: (b, i, 3)),
             tok(POOL_WIDTH),
             pl.BlockSpec((RET_WIDTH, RET_WIDTH), lambda b, i: (0, 0)),
             pl.BlockSpec((D_MODEL, D_MODEL), lambda b, i: (0, 0)),
             _row_spec(D_MODEL), _mod_spec(layer, 2, ctx)]
    return [x, attn, of, ob, ret, pool, avg_bf16, w_out_bf16, npost.reshape(1, D_MODEL), modr], specs


def _swiglu_tile(h, w1_ref, w3_ref, w2_ref):
    acc = jnp.zeros((h.shape[0], D_MODEL), F32)
    for f in range(0, D_FF, FF_CHUNK):
        a = jnp.dot(h, w1_ref[:, f:f + FF_CHUNK], preferred_element_type=F32)
        b = jnp.dot(h, w3_ref[:, f:f + FF_CHUNK], preferred_element_type=F32)
        u = (_silu(a) * b).astype(BF16)
        acc = acc + jnp.dot(u, w2_ref[f:f + FF_CHUNK, :], preferred_element_type=F32)
    return acc


def _mix_ffn_kernel(*refs):
    gpre_ref, sc_ref, sh_ref, w1_ref, w3_ref, w2_ref, npost_ref, g2_ref, o_ref = refs[N_MIX_INPUTS:]
    x = _mix_tile(*refs[:N_MIX_INPUTS])
    h = (_rms(x, gpre_ref[...]) * (1.0 + sc_ref[...]) + sh_ref[...]).astype(BF16)
    y = _swiglu_tile(h, w1_ref, w3_ref, w2_ref)
    o_ref[...] = x + g2_ref[...] * _rms(y, npost_ref[...])


def _mix_ffn(mix, gpre, npost, w1, w3, w2, layer, ctx):
    x, modr = mix[0], mix[-1]
    B, T, _ = x.shape
    tm = min(TOKEN_TILE, T)
    args, specs = _mix_inputs(mix, tm, layer, ctx)
    wspec = lambda shape: pl.BlockSpec(shape, lambda b, i: (0, 0), pipeline_mode=pl.Buffered(1))
    return pl.pallas_call(
        _mix_ffn_kernel,
        out_shape=jax.ShapeDtypeStruct((B, T, D_MODEL), F32),
        grid=(B, T // tm),
        in_specs=specs + [_row_spec(D_MODEL), _mod_spec(layer, 4, ctx), _mod_spec(layer, 3, ctx),
                          wspec((D_MODEL, D_FF)), wspec((D_MODEL, D_FF)), wspec((D_FF, D_MODEL)),
                          _row_spec(D_MODEL), _mod_spec(layer, 5, ctx)],
        out_specs=pl.BlockSpec((None, tm, D_MODEL), lambda b, i: (b, i, 0)),
        compiler_params=_cparams(("parallel", "parallel"), VMEM_LIMIT),
        name="mix_ffn",
    )(*args, gpre.reshape(1, D_MODEL), modr, modr, w1, w3, w2, npost.reshape(1, D_MODEL), modr)


def _mix_route_kernel(*refs):
    (gpre_ref, sc_ref, sh_ref, rt_ref, x_ref, h_ref, idx_ref, gate_ref, rank_ref, cnt_ref,
     run_ref, upper_ref) = refs[N_MIX_INPUTS:]
    tm = x_ref.shape[0]
    first = (pl.program_id(0) == 0) & (pl.program_id(1) == 0)

    @pl.when(first)
    def _():
        run_ref[...] = jnp.zeros_like(run_ref)
        earlier = lax.broadcasted_iota(I32, (tm, tm), 0) < lax.broadcasted_iota(I32, (tm, tm), 1)
        upper_ref[...] = jnp.where(earlier, 1.0, 0.0).astype(BF16)

    x = _mix_tile(*refs[:N_MIX_INPUTS])
    x_ref[...] = x
    h = _rms(x, gpre_ref[...]) * (1.0 + sc_ref[...]) + sh_ref[...]
    h_ref[...] = h
    logits = lax.dot_general(rt_ref[...], h, (((1,), (1,)), ((), ())), precision=lax.Precision.HIGHEST,
                             preferred_element_type=F32)
    eid = lax.broadcasted_iota(I32, (N_EXPERTS, tm), 0).astype(F32)
    m1 = jnp.max(logits, axis=0, keepdims=True)
    i1 = jnp.min(jnp.where(logits == m1, eid, float(N_EXPERTS)), axis=0, keepdims=True)
    oh1 = eid == i1
    rest = jnp.where(oh1, -jnp.inf, logits)
    m2 = jnp.max(rest, axis=0, keepdims=True)
    i2 = jnp.min(jnp.where(rest == m2, eid, float(N_EXPERTS)), axis=0, keepdims=True)
    oh2 = eid == i2
    e2 = jnp.exp(m2 - m1)
    gate_ref[0:1, :] = 1.0 / (1.0 + e2)
    gate_ref[1:2, :] = e2 / (1.0 + e2)
    idx_ref[0:1, :] = i1.astype(I32)
    idx_ref[1:2, :] = i2.astype(I32)
    upper = upper_ref[...]
    f1 = jnp.where(oh1, 1.0, 0.0)
    f2 = jnp.where(oh2, 1.0, 0.0)
    before1 = jnp.dot(f1.astype(BF16), upper, preferred_element_type=F32)
    before2 = jnp.dot(f2.astype(BF16), upper, preferred_element_type=F32)
    cnt1 = jnp.sum(f1, axis=1, keepdims=True)
    cnt2 = jnp.sum(f2, axis=1, keepdims=True)
    run = run_ref[:, 0:1]
    rank_ref[0:1, :] = jnp.sum(f1 * (run + before1), axis=0, keepdims=True).astype(I32)
    rank_ref[1:2, :] = jnp.sum(f2 * (run + cnt1 + before2), axis=0, keepdims=True).astype(I32)
    run_new = run_ref[...] + cnt1 + cnt2
    run_ref[...] = run_new
    cnt_ref[...] = run_new


def _mix_route(mix, gpre, router_t, layer):
    x, modr = mix[0], mix[-1]
    B, T, _ = x.shape
    tm = min(TOKEN_TILE, T)
    nt = T // tm
    args, specs = _mix_inputs(mix, tm, layer, False)
    tok = pl.BlockSpec((None, tm, D_MODEL), lambda b, i: (b, i, 0))
    lane = pl.BlockSpec((2, tm), lambda b, i: (0, b * nt + i))
    return pl.pallas_call(
        _mix_route_kernel,
        out_shape=(jax.ShapeDtypeStruct((B, T, D_MODEL), F32),
                   jax.ShapeDtypeStruct((B, T, D_MODEL), F32),
                   jax.ShapeDtypeStruct((2, B * T), I32),
                   jax.ShapeDtypeStruct((2, B * T), F32),
                   jax.ShapeDtypeStruct((2, B * T), I32),
                   jax.ShapeDtypeStruct((N_EXPERTS, 128), F32)),
        grid=(B, nt),
        in_specs=specs + [_row_spec(D_MODEL), _mod_spec(layer, 4, False), _mod_spec(layer, 3, False),
                          pl.BlockSpec((N_EXPERTS, D_MODEL), lambda b, i: (0, 0))],
        out_specs=(tok, tok, lane, lane, lane,
                   pl.BlockSpec((N_EXPERTS, 128), lambda b, i: (0, 0))),
        scratch_shapes=[pltpu.VMEM((N_EXPERTS, 128), F32), pltpu.VMEM((tm, tm), BF16)],
        compiler_params=_cparams(("arbitrary", "arbitrary"), VMEM_LIMIT),
        name="mix_route",
    )(*args, gpre.reshape(1, D_MODEL), modr, modr, router_t)


def _dispatch_kernel(zero_tiles_ref, slot_ref, h_ref, hs_ref, zero_ref, sem, zero_sem):
    tm = h_ref.shape[0]

    @pl.when(pl.program_id(0) == 0)
    def _():
        zero_ref[...] = jnp.zeros_like(zero_ref)
        for j in range(2 * N_EXPERTS):
            start = pl.multiple_of(zero_tiles_ref[j] * EXPERT_TILE, EXPERT_TILE)
            fill = pltpu.make_async_copy(zero_ref, hs_ref.at[pl.ds(start, EXPERT_TILE)], zero_sem)
            fill.start()
            fill.wait()

    def body(i, carry):
        for u in range(DMA_UNROLL):
            r = i * DMA_UNROLL + u
            for k in range(2):
                s = slot_ref[0, k, r]
                pltpu.make_async_copy(h_ref.at[pl.ds(r, 1)], hs_ref.at[pl.ds(s, 1)], sem).start(priority=k)
        return carry

    lax.fori_loop(0, tm // DMA_UNROLL, body, 0)
    for _ in range(2):
        pltpu.make_async_copy(h_ref, hs_ref.at[pl.ds(0, tm)], sem).wait()


def _dispatch(h, slot_tiles, zero_tiles, n_slots):
    N = h.shape[0]
    tm = slot_tiles.shape[2]
    grid_spec = pltpu.PrefetchScalarGridSpec(
        num_scalar_prefetch=1,
        grid=(N // tm,),
        in_specs=[pl.BlockSpec((1, 2, tm), lambda i, zt: (i, 0, 0), memory_space=pltpu.SMEM),
                  pl.BlockSpec((tm, D_MODEL), lambda i, zt: (i, 0))],
        out_specs=pl.BlockSpec(memory_space=pl.ANY),
        scratch_shapes=[pltpu.VMEM((EXPERT_TILE, D_MODEL), F32),
                        pltpu.SemaphoreType.DMA(()), pltpu.SemaphoreType.DMA(())],
    )
    return pl.pallas_call(
        _dispatch_kernel,
        out_shape=jax.ShapeDtypeStruct((n_slots, D_MODEL), F32),
        grid_spec=grid_spec,
        compiler_params=_cparams(("arbitrary",)),
        name="moe_dispatch",
    )(zero_tiles, slot_tiles, h)


def _expert_kernel(te_ref, tv_ref, h_ref, w1_ref, w3_ref, w2_ref, y_ref):
    t = pl.program_id(0)

    @pl.when(tv_ref[t] == 1)
    def _():
        y_ref[...] = _swiglu_tile(h_ref[...].astype(BF16), w1_ref, w3_ref, w2_ref)

    @pl.when(tv_ref[t] == 0)
    def _():
        y_ref[...] = jnp.zeros_like(y_ref)


def _experts(hs, tile_expert, tile_valid, w1, w3, w2):
    n_slots = hs.shape[0]
    tm = EXPERT_TILE
    grid_spec = pltpu.PrefetchScalarGridSpec(
        num_scalar_prefetch=2,
        grid=(n_slots // tm,),
        in_specs=[pl.BlockSpec((tm, D_MODEL), lambda t, te, tv: (t, 0)),
                  pl.BlockSpec((None, D_MODEL, D_FF), lambda t, te, tv: (te[t], 0, 0)),
                  pl.BlockSpec((None, D_MODEL, D_FF), lambda t, te, tv: (te[t], 0, 0)),
                  pl.BlockSpec((None, D_FF, D_MODEL), lambda t, te, tv: (te[t], 0, 0))],
        out_specs=pl.BlockSpec((tm, D_MODEL), lambda t, te, tv: (t, 0)),
    )
    return pl.pallas_call(
        _expert_kernel,
        out_shape=jax.ShapeDtypeStruct((n_slots, D_MODEL), F32),
        grid_spec=grid_spec,
        compiler_params=_cparams(("arbitrary",), VMEM_LIMIT),
        name="moe_experts",
    )(tile_expert, tile_valid, hs, w1, w3, w2)


def _combine_kernel(slot_ref, gate_ref, x_ref, npost_ref, g2_ref, ys_ref, o_ref, buf0, buf1, sem):
    tm = x_ref.shape[0]
    bufs = (buf0, buf1)

    def body(i, carry):
        for u in range(DMA_UNROLL):
            r = i * DMA_UNROLL + u
            for k in range(2):
                s = slot_ref[0, k, r]
                pltpu.make_async_copy(ys_ref.at[pl.ds(s, 1)], bufs[k].at[pl.ds(r, 1)], sem).start(priority=k)
        return carry

    lax.fori_loop(0, tm // DMA_UNROLL, body, 0)
    for k in range(2):
        pltpu.make_async_copy(ys_ref.at[pl.ds(0, tm)], bufs[k], sem).wait()
    y = gate_ref[:, 0:1] * buf0[...] + gate_ref[:, 1:2] * buf1[...]
    o_ref[...] = x_ref[...] + g2_ref[...] * _rms(y, npost_ref[...])


def _combine(x, ys, slot_tiles, gate_tok, npost, modr, layer):
    B, T, _ = x.shape
    tm = slot_tiles.shape[2]
    nt = T // tm
    tok = pl.BlockSpec((None, tm, D_MODEL), lambda b, i: (b, i, 0))
    return pl.pallas_call(
        _combine_kernel,
        out_shape=jax.ShapeDtypeStruct((B, T, D_MODEL), F32),
        grid=(B, nt),
        in_specs=[pl.BlockSpec((1, 2, tm), lambda b, i: (b * nt + i, 0, 0), memory_space=pltpu.SMEM),
                  pl.BlockSpec((tm, 2), lambda b, i: (b * nt + i, 0)),
                  tok, _row_spec(D_MODEL), _mod_spec(layer, 5, False),
                  pl.BlockSpec(memory_space=pl.ANY)],
        out_specs=tok,
        scratch_shapes=[pltpu.VMEM((tm, D_MODEL), F32), pltpu.VMEM((tm, D_MODEL), F32),
                        pltpu.SemaphoreType.DMA(())],
        compiler_params=_cparams(("arbitrary", "arbitrary")),
        name="moe_combine",
    )(slot_tiles, gate_tok, x, npost.reshape(1, D_MODEL), modr, ys)


def _mix_moe_ffn(mix, gpre, npost, router, w1, w3, w2, layer):
    modr = mix[-1]
    B, T, _ = mix[0].shape
    N = B * T
    tm = min(TOKEN_TILE, T)
    x, h, idx, gate, rank, cnt = _mix_route(mix, gpre, router.T, layer)
    n_slots = 2 * N + N_EXPERTS * EXPERT_TILE
    n_tiles = n_slots // EXPERT_TILE
    counts = cnt[:, 0].astype(I32)
    padded = ((counts + EXPERT_TILE - 1) // EXPERT_TILE) * EXPERT_TILE
    ends = jnp.cumsum(padded)
    starts = ends - padded
    slot = rank
    for e in range(N_EXPERTS):
        slot = slot + jnp.where(idx == e, starts[e], 0)
    slot_tiles = slot.reshape(2, N // tm, tm).transpose(1, 0, 2)
    tile_start = jnp.arange(n_tiles, dtype=I32) * EXPERT_TILE
    tile_valid = (tile_start < ends[-1]).astype(I32)
    tile_expert = jnp.minimum(jnp.sum((tile_start[:, None] >= ends[None, :]).astype(I32), axis=1), N_EXPERTS - 1)
    last_expert = jnp.max(jnp.where(tile_valid == 1, tile_expert, 0))
    tile_expert = jnp.where(tile_valid == 1, tile_expert, last_expert)
    last_tile = jnp.where(padded > 0, ends // EXPERT_TILE - 1, n_tiles - 1)
    tail_tile = jnp.minimum(ends[-1] // EXPERT_TILE + jnp.arange(N_EXPERTS, dtype=I32), n_tiles - 1)
    zero_tiles = jnp.concatenate([last_tile, tail_tile]).astype(I32)
    hs = _dispatch(h.reshape(N, D_MODEL), slot_tiles, zero_tiles, n_slots)
    ys = _experts(hs, tile_expert, tile_valid, w1, w3, w2)
    return _combine(x, ys, slot_tiles, gate.T, npost, modr, layer)


def kernel(x, c, ctx, c_ctx, w_mod, b_mod, norm_pre_mix, norm_post_mix, norm_pre_ffn, norm_post_ffn, w_in, w_out, q_norm, k_norm, ret_decay_logit, pool_w, pool_scale, ffn_w1, ffn_w3, ffn_w2, moe_router, moe_w1, moe_w3, moe_w2):
    B, T, _ = x.shape
    cvec = jnp.zeros((MOD_ROWS, D_MODEL), F32).at[0:B].set(c).at[2].set(c_ctx)
    modr = _modulation(cvec, w_mod, b_mod)
    cos_t, sin_t = _rope_tables(T)
    avg = jnp.kron(jnp.eye(RET_HEADS, dtype=F32), jnp.full((HEAD_DIM, HEAD_DIM), 1.0 / HEAD_DIM, F32)).astype(BF16)
    zero_state = jnp.zeros((B, RET_WIDTH, RET_WIDTH), F32)
    xc = ctx
    for i in range(DEPTH):
        need_ctx = i < DEPTH - 1
        w_in_b = w_in[i].astype(BF16)
        w_out_b = w_out[i].astype(BF16)
        pool_bd = jax.scipy.linalg.block_diag(*[pool_w[i, g] for g in range(len(POOL_WINDOWS))]).astype(BF16)
        log_gamma = jax.nn.log_sigmoid(ret_decay_logit[i].astype(F32))

        qkv_c, ret_c, pp_c = _in_projection(xc, modr, norm_pre_mix[i], w_in_b, i, True)
        qt_c, kn_c, vt_c, qn_c, st_c = _attn_prep(qkv_c, q_norm[i], k_norm[i], cos_t, sin_t, False)
        of_c, ob_c, s_fwd, s_bwd = _retention(ret_c, log_gamma, zero_state, zero_state)

        qkv_x, ret_x, pp_x = _in_projection(x, modr, norm_pre_mix[i], w_in_b, i, False)
        qt_x, kn_x, vt_x, qn_x, st_x = _attn_prep(qkv_x, q_norm[i], k_norm[i], cos_t, sin_t, True)
        keys_x = [(kn_x, vt_x), (kn_c, vt_c)]
        attn_x = _attention(qt_x, _softmax_shift(qn_x, st_x, [st_x, st_c], qt_x, keys_x), keys_x)
        of_x, ob_x, _, _ = _retention(ret_x, log_gamma, s_fwd, s_bwd)
        pool_x = _pooling(pp_x, pool_bd, pool_scale[i])
        mix_x = (x, attn_x, of_x, ob_x, ret_x, pool_x, avg, w_out_b, norm_post_mix[i], modr)

        j = i // 2
        if i % 2 == 0:
            w1, w3, w2 = ffn_w1[j].astype(BF16), ffn_w3[j].astype(BF16), ffn_w2[j].astype(BF16)
            x = _mix_ffn(mix_x, norm_pre_ffn[i], norm_post_ffn[i], w1, w3, w2, i, False)
            if need_ctx:
                keys_c = [(kn_c, vt_c)]
                attn_c = _attention(qt_c, _softmax_shift(qn_c, st_c, [st_c], qt_c, keys_c), keys_c)
                pool_c = _pooling(pp_c, pool_bd, pool_scale[i])
                mix_c = (xc, attn_c, of_c, ob_c, ret_c, pool_c, avg, w_out_b, norm_post_mix[i], modr)
                xc = _mix_ffn(mix_c, norm_pre_ffn[i], norm_post_ffn[i], w1, w3, w2, i, True)
        else:
            assert not need_ctx
            x = _mix_moe_ffn(mix_x, norm_pre_ffn[i], norm_post_ffn[i], moe_router[j], moe_w1[j].astype(BF16),
                             moe_w3[j].astype(BF16), moe_w2[j].astype(BF16), i)
    return x
```

```python
import functools

import jax
import jax.numpy as jnp
from jax import lax
from jax.experimental import pallas as pl
from jax.experimental.pallas import tpu as pltpu

F32 = jnp.float32
BF16 = jnp.bfloat16
I32 = jnp.int32

D_MODEL = 1024
GRID_W = 64
HEAD_DIM = 64
ATTN_WIDTH = 512
KV_HEADS = 2
ATTN_GROUP = 4
KV_WIDTH = 128
RET_WIDTH = 256
RET_HEADS = 4
POOL_WIDTH = 256
POOL_WINDOWS = (2, 4, 8, 16)
IN_WIDTH = 2048
RET_CHUNK = 128
ROPE_THETA = 10000.0
D_FF = 2816
N_EXPERTS = 8
NORM_EPS = 1e-6
DEPTH = 2

TOKEN_TILE = 512
ATTN_Q_TILE = 512
ATTN_K_CHUNK = 2048
PREP_TILE = 2048
RET_TILE = 512
POOL_TILE = 256
POOL_HALO = 16
FF_CHUNK = 256
EXPERT_TILE = 512
DMA_UNROLL = 8
BF16_SUBLANES = 16
MXU_DIM = 256
K_COLS = MXU_DIM
KEY_BLOCK = MXU_DIM
ATTN_UNROLL = 9
N_MXU = 2
SCORE_REG, VALUE_REG = 1, 0
SCORE_BASE = 64
VT_ROWS = HEAD_DIM + BF16_SUBLANES
MAX_BOUND_SHIFT = 40.0
SHIFT_MARGIN = 1.01
MOD_ROWS = 8
VMEM_LIMIT = 56 * 1024 * 1024


def _cparams(sem, vmem=None):
    return pltpu.CompilerParams(dimension_semantics=sem, vmem_limit_bytes=vmem)


def _rms(x, gain):
    ms = jnp.mean(x * x, axis=-1, keepdims=True)
    return x * lax.rsqrt(ms + NORM_EPS) * gain


def _silu(x):
    return x * jax.nn.sigmoid(x)


def _mod_kernel(c_ref, w_ref, b_ref, o_ref):
    s = _silu(c_ref[...])
    o_ref[...] = jnp.dot(s, w_ref[...], precision=lax.Precision.HIGHEST,
                         preferred_element_type=F32) + b_ref[...]


def _modulation(cvec, w_mod, b_mod):
    nchunk = 6
    out = pl.pallas_call(
        _mod_kernel,
        out_shape=jax.ShapeDtypeStruct((DEPTH, MOD_ROWS, 6 * D_MODEL), F32),
        grid=(DEPTH, nchunk),
        in_specs=[
            pl.BlockSpec((MOD_ROWS, D_MODEL), lambda l, j: (0, 0)),
            pl.BlockSpec((None, D_MODEL, D_MODEL), lambda l, j: (l, 0, j)),
            pl.BlockSpec((None, 1, D_MODEL), lambda l, j: (l, 0, j)),
        ],
        out_specs=pl.BlockSpec((None, MOD_ROWS, D_MODEL), lambda l, j: (l, 0, j)),
        compiler_params=_cparams(("parallel", "parallel")),
        name="modulation",
    )(cvec, w_mod, b_mod.reshape(DEPTH, 1, 6 * D_MODEL))
    return out.reshape(DEPTH * MOD_ROWS * nchunk, 1, D_MODEL)


def _mod_spec(layer, chunk, ctx):
    base = layer * MOD_ROWS * 6
    if ctx:
        return pl.BlockSpec((None, 1, D_MODEL), lambda b, i: (base + 2 * 6 + chunk, 0, 0))
    return pl.BlockSpec((None, 1, D_MODEL), lambda b, i: (base + b * 6 + chunk, 0, 0))


def _row_spec(width):
    return pl.BlockSpec((1, width), lambda b, i: (0, 0))


def _inproj_kernel(x_ref, g_ref, sc_ref, sh_ref, w_ref, qkv_ref, ret_ref, pp_ref):
    h = _rms(x_ref[...], g_ref[...]) * (1.0 + sc_ref[...]) + sh_ref[...]
    p = jnp.dot(h.astype(BF16), w_ref[...], preferred_element_type=F32)
    qkv_ref[...] = p[:, :768].astype(BF16)
    ret_ref[...] = p[:, 768:1792].astype(BF16)
    pp_ref[...] = p[:, 1792:].astype(BF16)


def _in_projection(x, modr, gain, w_in_bf16, layer, ctx):
    B, T, _ = x.shape
    tm = min(TOKEN_TILE, T)
    tok = lambda w: pl.BlockSpec((None, tm, w), lambda b, i: (b, i, 0))
    return pl.pallas_call(
        _inproj_kernel,
        out_shape=(jax.ShapeDtypeStruct((B, T, 768), BF16),
                   jax.ShapeDtypeStruct((B, T, 1024), BF16),
                   jax.ShapeDtypeStruct((B, T, POOL_WIDTH), BF16)),
        grid=(B, T // tm),
        in_specs=[tok(D_MODEL), _row_spec(D_MODEL), _mod_spec(layer, 1, ctx), _mod_spec(layer, 0, ctx),
                  pl.BlockSpec((D_MODEL, IN_WIDTH), lambda b, i: (0, 0))],
        out_specs=(tok(768), tok(1024), tok(POOL_WIDTH)),
        compiler_params=_cparams(("parallel", "parallel"), VMEM_LIMIT),
        name="in_projection",
    )(x, gain.reshape(1, D_MODEL), modr, modr, w_in_bf16)


def _prep_kernel(qkv_ref, qg_ref, kg_ref, cos_ref, sin_ref, qt_ref, kn_ref, vt_ref, qn_ref, stat_ref, *, rope):
    t = qkv_ref[...].astype(F32).T

    def norm_rope(blk, gain):
        ms = jnp.mean(blk * blk, axis=0, keepdims=True)
        y = blk * lax.rsqrt(ms + NORM_EPS) * gain
        if rope:
            partner = jnp.concatenate([y[16:32], y[0:16], y[48:64], y[32:48]], axis=0)
            y = y * cos_ref[...] + partner * sin_ref[...]
        return y

    def sq_norm(y):
        return jnp.sum(y * y, axis=0, keepdims=True)

    def row_max(n2):
        return jnp.broadcast_to(jnp.max(n2, axis=1, keepdims=True), (1, 128))

    q_stats = []
    for h in range(ATTN_WIDTH // HEAD_DIM):
        lo = h * HEAD_DIM
        q = norm_rope(t[lo:lo + HEAD_DIM], qg_ref[...]) * (HEAD_DIM ** -0.5)
        qt_ref[lo:lo + HEAD_DIM, :] = q.astype(BF16)
        n2 = sq_norm(q)
        qn_ref[h // ATTN_GROUP, h % ATTN_GROUP:h % ATTN_GROUP + 1, :] = jnp.sqrt(n2)
        q_stats.append(row_max(n2))
    tm = t.shape[1]
    k_pad = jnp.where(lax.broadcasted_iota(I32, (K_COLS - HEAD_DIM, tm), 0) < 2, 1.0, 0.0)
    v_pad = jnp.where(lax.broadcasted_iota(I32, (VT_ROWS - HEAD_DIM, tm), 0) < 1, 1.0, 0.0)
    k_stats = []
    for kv in range(KV_HEADS):
        lo = ATTN_WIDTH + kv * HEAD_DIM
        k = norm_rope(t[lo:lo + HEAD_DIM], kg_ref[...])
        k_stats.append(row_max(sq_norm(k)))
        kn_ref[kv] = jnp.concatenate([k, k_pad], axis=0).T.astype(BF16)
        lo = ATTN_WIDTH + KV_WIDTH + kv * HEAD_DIM
        vt = jnp.concatenate([t[lo:lo + HEAD_DIM], v_pad], axis=0).astype(BF16)
        for j in range(tm // KEY_BLOCK):
            vt_ref[kv, j] = vt[:, j * KEY_BLOCK:(j + 1) * KEY_BLOCK]
    q_group = [functools.reduce(jnp.maximum, q_stats[kv * ATTN_GROUP:(kv + 1) * ATTN_GROUP])
               for kv in range(KV_HEADS)]
    stat_ref[...] = jnp.concatenate(k_stats + q_group + [jnp.zeros((8 - 2 * KV_HEADS, 128), F32)], axis=0)


def _attn_prep(qkv, q_gain, k_gain, cos_t, sin_t, rope):
    B, T, _ = qkv.shape
    tm = min(PREP_TILE, T)
    nc = T // tm
    return pl.pallas_call(
        functools.partial(_prep_kernel, rope=rope),
        out_shape=(jax.ShapeDtypeStruct((B, ATTN_WIDTH, T), BF16),
                   jax.ShapeDtypeStruct((B, KV_HEADS, T, K_COLS), BF16),
                   jax.ShapeDtypeStruct((B, KV_HEADS, T // KEY_BLOCK, VT_ROWS, KEY_BLOCK), BF16),
                   jax.ShapeDtypeStruct((B, KV_HEADS, ATTN_GROUP, T), F32),
                   jax.ShapeDtypeStruct((B, nc, 8, 128), F32)),
        grid=(B, nc),
        in_specs=[pl.BlockSpec((None, tm, 768), lambda b, i: (b, i, 0)),
                  pl.BlockSpec((HEAD_DIM, 1), lambda b, i: (0, 0)),
                  pl.BlockSpec((HEAD_DIM, 1), lambda b, i: (0, 0)),
                  pl.BlockSpec((HEAD_DIM, tm), lambda b, i: (0, i)),
                  pl.BlockSpec((HEAD_DIM, tm), lambda b, i: (0, i))],
        out_specs=(pl.BlockSpec((None, ATTN_WIDTH, tm), lambda b, i: (b, 0, i)),
                   pl.BlockSpec((None, KV_HEADS, tm, K_COLS), lambda b, i: (b, 0, i, 0)),
                   pl.BlockSpec((None, KV_HEADS, tm // KEY_BLOCK, VT_ROWS, KEY_BLOCK), lambda b, i: (b, 0, i, 0, 0)),
                   pl.BlockSpec((None, KV_HEADS, ATTN_GROUP, tm), lambda b, i: (b, 0, 0, i)),
                   pl.BlockSpec((None, None, 8, 128), lambda b, i: (b, i, 0, 0))),
        compiler_params=_cparams(("parallel", "parallel")),
        name="attn_prep",
    )(qkv, q_gain.reshape(HEAD_DIM, 1), k_gain.reshape(HEAD_DIM, 1), cos_t, sin_t)


def _rope_tables(T):
    t = jnp.arange(T)
    row = (t // GRID_W).astype(F32)
    col = (t % GRID_W).astype(F32)
    n_freq = HEAD_DIM // 4
    inv = ROPE_THETA ** (-jnp.arange(n_freq, dtype=F32) / n_freq)
    ang_r = row[None, :] * inv[:, None]
    ang_c = col[None, :] * inv[:, None]
    cos_t = jnp.concatenate([jnp.cos(ang_r), jnp.cos(ang_r), jnp.cos(ang_c), jnp.cos(ang_c)], axis=0)
    sin_t = jnp.concatenate([-jnp.sin(ang_r), jnp.sin(ang_r), -jnp.sin(ang_c), jnp.sin(ang_c)], axis=0)
    return cos_t, sin_t


def _score_max_kernel(qt_ref, k_ref, m_ref, qa_ref):
    tq = qt_ref.shape[1]
    nq = ATTN_GROUP * tq
    for g in range(ATTN_GROUP):
        qa_ref[0:HEAD_DIM, g * tq:(g + 1) * tq] = qt_ref[g * HEAD_DIM:(g + 1) * HEAD_DIM, :]
    qa_ref[HEAD_DIM:K_COLS, :] = jnp.zeros((K_COLS - HEAD_DIM, nq), BF16)
    tk = min(ATTN_K_CHUNK, k_ref.shape[0])

    def colmax8(j):
        k = k_ref[pl.ds(pl.multiple_of(j * tk, tk), tk), :]
        s = jnp.dot(k, qa_ref[...], preferred_element_type=F32)
        return jnp.max(s.reshape(tk // 8, 8, nq), axis=0)

    mx = lax.fori_loop(1, k_ref.shape[0] // tk, lambda j, mx: jnp.maximum(mx, colmax8(j)), colmax8(0))
    m = jnp.max(mx, axis=0, keepdims=True)
    for g in range(ATTN_GROUP):
        m_ref[g:g + 1, :] = m[:, g * tq:(g + 1) * tq]


def _score_max(qt, k_all):
    B, _, Tq = qt.shape
    Tk = k_all.shape[2]
    tq = min(ATTN_Q_TILE, Tq)
    return pl.pallas_call(
        _score_max_kernel,
        out_shape=jax.ShapeDtypeStruct((B, KV_HEADS, ATTN_GROUP, Tq), F32),
        grid=(B, KV_HEADS, Tq // tq),
        in_specs=[pl.BlockSpec((None, ATTN_GROUP * HEAD_DIM, tq), lambda b, h, i: (b, h, i)),
                  pl.BlockSpec((None, None, Tk, K_COLS), lambda b, h, i: (b, h, 0, 0))],
        out_specs=pl.BlockSpec((None, None, ATTN_GROUP, tq), lambda b, h, i: (b, h, 0, i)),
        scratch_shapes=[pltpu.VMEM((K_COLS, ATTN_GROUP * tq), BF16)],
        compiler_params=_cparams(("parallel", "parallel", "parallel"), VMEM_LIMIT),
        name="score_max",
    )(qt, k_all)


def _softmax_shift(q_norm, q_stats, k_stats_list, qt, key_sets):
    B = qt.shape[0]
    k2 = functools.reduce(jnp.maximum, [st[:, :, 0:KV_HEADS, 0].max(axis=1) for st in k_stats_list])
    q2 = q_stats[:, :, KV_HEADS:2 * KV_HEADS, 0].max(axis=1)
    use_bound = (jnp.sqrt(k2 * q2) * SHIFT_MARGIN <= MAX_BOUND_SHIFT).reshape(B, KV_HEADS, 1, 1)
    bound = q_norm * (jnp.sqrt(k2) * SHIFT_MARGIN).reshape(B, KV_HEADS, 1, 1)
    exact = lambda: _score_max(qt, jnp.concatenate([k for k, _ in key_sets], axis=2))
    return lax.cond(jnp.all(use_bound), lambda: bound, lambda: jnp.where(use_bound, bound, exact()))


def _attn_kernel(qt_ref, shift_ref, *refs):
    *kv_refs, o_ref, qa_ref = refs
    sources = [(kv_refs[i], kv_refs[i + 1]) for i in range(0, len(kv_refs), 2)]
    counts = [k_ref.shape[0] // KEY_BLOCK for k_ref, _ in sources]
    tq = qt_ref.shape[1]
    n_blocks = sum(counts)
    heads_per_mxu = ATTN_GROUP // N_MXU

    def locate(blk):
        if not isinstance(blk, int):
            return 0, blk
        src = 0
        while blk >= counts[src]:
            blk -= counts[src]
            src += 1
        return src, blk

    row = lax.broadcasted_iota(I32, (BF16_SUBLANES, tq), 0)
    for g in range(ATTN_GROUP):
        m = shift_ref[g:g + 1, :]
        m_hi = m.astype(BF16).astype(F32)
        qa_ref[g, 0:HEAD_DIM, :] = qt_ref[g * HEAD_DIM:(g + 1) * HEAD_DIM, :]
        qa_ref[g, HEAD_DIM:HEAD_DIM + BF16_SUBLANES, :] = jnp.where(
            row == 0, -m_hi, jnp.where(row == 1, m_hi - m, 0.0)).astype(BF16)
        qa_ref[g, HEAD_DIM + BF16_SUBLANES:K_COLS, :] = jnp.zeros((K_COLS - HEAD_DIM - BF16_SUBLANES, tq), BF16)

    def score_addr(slot):
        return SCORE_BASE + slot * (KEY_BLOCK // 4)

    def out_addr(slot):
        return slot * (VT_ROWS // 4)

    def keys(blk):
        src, j = locate(blk)
        return sources[src][0][pl.ds(pl.multiple_of(j * KEY_BLOCK, KEY_BLOCK), KEY_BLOCK), :]

    def stage_q(slot):
        for mxu in range(N_MXU):
            pltpu.matmul_push_rhs(qa_ref[mxu * heads_per_mxu + slot], SCORE_REG, mxu)

    def issue_scores(blk, slot):
        k = keys(blk)
        for mxu in range(N_MXU):
            pltpu.matmul_acc_lhs(score_addr(slot), k, mxu, load_staged_rhs=SCORE_REG)

    def pop_probs(slot):
        return [jnp.exp(pltpu.matmul_pop(score_addr(slot), (KEY_BLOCK, tq), F32, mxu)).astype(BF16)
                for mxu in range(N_MXU)]

    def push_probs(blk, slot, p):
        src, j = locate(blk)
        vt = sources[src][1][j]
        for mxu in range(N_MXU):
            pltpu.matmul_push_rhs(p[mxu], VALUE_REG, mxu)
        return vt

    def unit(blk, slot, stage_next, first):
        if not first:
            p = pop_probs(slot)
        issue_scores(blk, slot)
        if not first:
            vt = push_probs(blk - 1, slot, p)
        if stage_next:
            stage_q((slot + 1) % heads_per_mxu)
        if not first:
            for mxu in range(N_MXU):
                pltpu.matmul_acc_lhs(out_addr(slot), vt, mxu, load_staged_rhs=VALUE_REG)

    def block(blk, last=False, first=False):
        for slot in range(heads_per_mxu):
            unit(blk, slot, stage_next=not (last and slot == heads_per_mxu - 1), first=first)

    stage_q(0)
    block(0, last=n_blocks == 1, first=True)
    if n_blocks > 1:
        n_loop = n_blocks - 2
        assert n_loop + 1 <= counts[0]
        unroll = max([u for u in range(1, ATTN_UNROLL + 1) if n_loop % u == 0] or [1])

        def body(i, carry):
            for u in range(unroll):
                block(1 + i * unroll + u)
            return carry

        if n_loop:
            lax.fori_loop(0, n_loop // unroll, body, 0)
        block(n_blocks - 1, last=True)
    for slot in range(heads_per_mxu):
        vt = push_probs(n_blocks - 1, slot, pop_probs(slot))
        for mxu in range(N_MXU):
            pltpu.matmul_acc_lhs(out_addr(slot), vt, mxu, load_staged_rhs=VALUE_REG)

    outs = []
    for mxu in range(N_MXU):
        for slot in range(heads_per_mxu):
            acc = pltpu.matmul_pop(out_addr(slot), (VT_ROWS, tq), F32, mxu)
            outs.append(acc[0:HEAD_DIM, :] / acc[HEAD_DIM:HEAD_DIM + 1, :])
    o_ref[...] = jnp.concatenate(outs, axis=0).T.astype(BF16)


def _attention(qt, shift, key_sets):
    B, _, Tq = qt.shape
    tq = MXU_DIM
    in_specs = [pl.BlockSpec((None, ATTN_GROUP * HEAD_DIM, tq), lambda b, h, i: (b, h, i)),
                pl.BlockSpec((None, None, ATTN_GROUP, tq), lambda b, h, i: (b, h, 0, i))]
    args = [qt, shift]
    for k, vt in key_sets:
        Tk = k.shape[2]
        in_specs += [pl.BlockSpec((None, None, Tk, K_COLS), lambda b, h, i: (b, h, 0, 0)),
                     pl.BlockSpec((None, None, Tk // KEY_BLOCK, VT_ROWS, KEY_BLOCK), lambda b, h, i: (b, h, 0, 0, 0))]
        args += [k, vt]
    return pl.pallas_call(
        _attn_kernel,
        out_shape=jax.ShapeDtypeStruct((B, Tq, ATTN_WIDTH), BF16),
        grid=(B, KV_HEADS, Tq // tq),
        in_specs=in_specs,
        out_specs=pl.BlockSpec((None, tq, ATTN_GROUP * HEAD_DIM), lambda b, h, i: (b, i, h)),
        scratch_shapes=[pltpu.VMEM((ATTN_GROUP, K_COLS, tq), BF16)],
        compiler_params=_cparams(("parallel", "parallel", "parallel"), VMEM_LIMIT),
        name="attention",
    )(*args)


def _ret_kernel(lgc_f_ref, lgc_b_ref, lgr_f_ref, lgr_b_ref, s0f_ref, s0b_ref, blk_f_ref, blk_b_ref,
                of_ref, ob_ref, sf_ref, sb_ref, st_f, st_b, dec_f, dec_b):
    C = RET_CHUNK
    W = RET_WIDTH
    n = pl.program_id(1)

    @pl.when(n == 0)
    def _():
        st_f[...] = s0f_ref[...]
        st_b[...] = s0b_ref[...]
        c = jnp.bitwise_and(lax.broadcasted_iota(I32, (RET_HEADS * C, C), 0), C - 1)
        m = lax.broadcasted_iota(I32, (RET_HEADS * C, C), 1)
        diff = (c - m).astype(F32)
        dec_f[...] = jnp.where(diff >= 0, jnp.exp(lgc_f_ref[...] * jnp.maximum(diff, 0.0)), 0.0)
        dec_b[...] = jnp.where(diff <= 0, jnp.exp(lgc_b_ref[...] * jnp.maximum(-diff, 0.0)), 0.0)

    lane_head = jnp.right_shift(lax.broadcasted_iota(I32, (C, W), 1), 6)
    pos = lax.broadcasted_iota(I32, (C, W), 0).astype(F32)
    same_head = (jnp.right_shift(lax.broadcasted_iota(I32, (W, W), 0), 6)
                 == jnp.right_shift(lax.broadcasted_iota(I32, (W, W), 1), 6))

    def direction(blk_ref, dec_ref, st_ref, lgr, forward, out_ref):
        n_sub = blk_ref.shape[0] // C
        state = st_ref[...]
        for sub in (range(n_sub) if forward else reversed(range(n_sub))):
            state = chunk(blk_ref, dec_ref, state, lgr, forward, out_ref, sub * C)
        st_ref[...] = state

    def chunk(blk_ref, dec_ref, state, lgr, forward, out_ref, r0):
        q = blk_ref[r0:r0 + C, 0:W].astype(F32)
        kf = blk_ref[r0:r0 + C, W:2 * W].astype(F32) * (HEAD_DIM ** -0.5)
        v = blk_ref[r0:r0 + C, 2 * W:3 * W]
        if forward:
            zeta = jnp.exp(lgr * (C - 1.0 - pos))
            xi = jnp.exp(lgr * (pos + 1.0))
        else:
            zeta = jnp.exp(lgr * pos)
            xi = jnp.exp(lgr * (C - pos))
        qexp = jnp.concatenate([jnp.where(lane_head == h, q, 0.0) for h in range(RET_HEADS)],
                               axis=0).astype(BF16)
        a = lax.dot_general(qexp, kf.astype(BF16), (((1,), (1,)), ((), ())),
                            preferred_element_type=F32)
        p = (a * dec_ref[...]).astype(BF16)
        full = jnp.dot(p, v, preferred_element_type=F32)
        intra = jnp.zeros((C, W), F32)
        for h in range(RET_HEADS):
            intra = intra + jnp.where(lane_head == h, full[h * C:(h + 1) * C], 0.0)
        cross = jnp.dot((q * xi).astype(BF16), state.astype(BF16), preferred_element_type=F32)
        out_ref[r0:r0 + C, :] = intra + cross
        upd = lax.dot_general((kf * zeta).astype(BF16), v, (((0,), (0,)), ((), ())),
                              preferred_element_type=F32)
        return jnp.where(same_head, state * jnp.exp(lgr * float(C)) + upd, 0.0)

    direction(blk_f_ref, dec_f, st_f, lgr_f_ref[...], True, of_ref)
    direction(blk_b_ref, dec_b, st_b, lgr_b_ref[...], False, ob_ref)

    @pl.when(n == pl.num_programs(1) - 1)
    def _():
        sf_ref[...] = st_f[...]
        sb_ref[...] = st_b[...]


def _retention(ret, log_gamma, s0f, s0b):
    B, T, _ = ret.shape
    C = RET_CHUNK
    tm = min(RET_TILE, T)
    nc = T // tm
    lgc = lambda d: jnp.repeat(log_gamma[d], C).reshape(RET_HEADS * C, 1)
    lgr = lambda d: jnp.repeat(log_gamma[d], HEAD_DIM).reshape(1, RET_WIDTH)
    const = lambda shape: pl.BlockSpec(shape, lambda b, n: (0,) * len(shape))
    st_spec = pl.BlockSpec((None, RET_WIDTH, RET_WIDTH), lambda b, n: (b, 0, 0))
    return pl.pallas_call(
        _ret_kernel,
        out_shape=(jax.ShapeDtypeStruct((B, T, RET_WIDTH), F32),
                   jax.ShapeDtypeStruct((B, T, RET_WIDTH), F32),
                   jax.ShapeDtypeStruct((B, RET_WIDTH, RET_WIDTH), F32),
                   jax.ShapeDtypeStruct((B, RET_WIDTH, RET_WIDTH), F32)),
        grid=(B, nc),
        in_specs=[const((RET_HEADS * C, 1)), const((RET_HEADS * C, 1)),
                  const((1, RET_WIDTH)), const((1, RET_WIDTH)), st_spec, st_spec,
                  pl.BlockSpec((None, tm, 1024), lambda b, n: (b, n, 0)),
                  pl.BlockSpec((None, tm, 1024), lambda b, n: (b, nc - 1 - n, 0))],
        out_specs=(pl.BlockSpec((None, tm, RET_WIDTH), lambda b, n: (b, n, 0)),
                   pl.BlockSpec((None, tm, RET_WIDTH), lambda b, n: (b, nc - 1 - n, 0)),
                   st_spec, st_spec),
        scratch_shapes=[pltpu.VMEM((RET_WIDTH, RET_WIDTH), F32), pltpu.VMEM((RET_WIDTH, RET_WIDTH), F32),
                        pltpu.VMEM((RET_HEADS * C, C), F32), pltpu.VMEM((RET_HEADS * C, C), F32)],
        compiler_params=_cparams(("parallel", "arbitrary")),
        name="retention",
    )(lgc(0), lgc(1), lgr(0), lgr(1), s0f, s0b, ret, ret)


def _init_pool_bands(band_ref):
    tok = lax.broadcasted_iota(I32, (POOL_TILE, POOL_TILE + 2 * POOL_HALO), 0)
    src = lax.broadcasted_iota(I32, (POOL_TILE, POOL_TILE + 2 * POOL_HALO), 1) - POOL_HALO
    for gi, w in enumerate(POOL_WINDOWS):
        inside = (src >= tok - w // 2) & (src < tok + w // 2)
        band_ref[gi] = jnp.where(inside, 1.0, 0.0).astype(BF16)


def _pool_tile(prev_ref, cur_ref, next_ref, w_ref, scale_ref, band_ref, seq_len):
    tm = cur_ref.shape[0]
    i = pl.program_id(1)
    prev = jnp.where(i > 0, prev_ref[...], jnp.zeros_like(prev_ref))
    nxt = jnp.where(i < pl.num_programs(1) - 1, next_ref[...], jnp.zeros_like(next_ref))
    ext = jnp.concatenate([prev, cur_ref[...], nxt], axis=0)
    sub = min(POOL_TILE, tm)
    lane_group = jnp.right_shift(lax.broadcasted_iota(I32, (sub, POOL_WIDTH), 1), 6)
    parts = []
    for r0 in range(0, tm, sub):
        window = ext[r0:r0 + sub + 2 * POOL_HALO]
        cur = cur_ref[r0:r0 + sub, :].astype(F32)
        tcol = i * tm + r0 + lax.broadcasted_iota(I32, (sub, 1), 0)
        mixed = jnp.zeros((sub, POOL_WIDTH), F32)
        for gi, w in enumerate(POOL_WINDOWS):
            total = jnp.dot(band_ref[gi], window, preferred_element_type=F32)
            cnt = (jnp.minimum(tcol + w // 2, seq_len) - jnp.maximum(tcol - w // 2, 0)).astype(F32)
            mixed = mixed + jnp.where(lane_group == gi, total / cnt - cur, 0.0)
        y = jnp.dot(mixed.astype(BF16), w_ref[...], preferred_element_type=F32)
        parts.append((y * scale_ref[...]).astype(BF16))
    return jnp.concatenate(parts, axis=0)


def _head_mean(x, avg):
    hi = x.astype(BF16)
    lo = (x - hi.astype(F32)).astype(BF16)
    return (jnp.dot(hi, avg, preferred_element_type=F32) + jnp.dot(lo, avg, preferred_element_type=F32))


def _mix_tile(x_ref, attn_ref, of_ref, ob_ref, gate_ref, pp_prev_ref, pp_ref, pp_next_ref, pool_w_ref,
              pool_scale_ref, avg_ref, w_ref, npost_ref, g1_ref, band_ref, seq_len):
    pool = _pool_tile(pp_prev_ref, pp_ref, pp_next_ref, pool_w_ref, pool_scale_ref, band_ref, seq_len)
    o = of_ref[...] + ob_ref[...]
    avg = avg_ref[...]
    mu = _head_mean(o, avg)
    cen = o - mu
    var = _head_mean(cen * cen, avg)
    y_ret = (_silu(gate_ref[...].astype(F32)) * (cen * lax.rsqrt(var + NORM_EPS))).astype(BF16)
    mx = (jnp.dot(attn_ref[...], w_ref[0:ATTN_WIDTH, :], preferred_element_type=F32)
          + jnp.dot(y_ret, w_ref[ATTN_WIDTH:ATTN_WIDTH + RET_WIDTH, :], preferred_element_type=F32)
          + jnp.dot(pool, w_ref[ATTN_WIDTH + RET_WIDTH:, :], preferred_element_type=F32))
    return x_ref[...] + g1_ref[...] * _rms(mx, npost_ref[...])


N_MIX_INPUTS = 14
POOL_BANDS = pltpu.VMEM((len(POOL_WINDOWS), POOL_TILE, POOL_TILE + 2 * POOL_HALO), BF16)


def _mix_inputs(mix, tm, layer, ctx):
    x, attn, of, ob, ret, pp, pool_w, pool_scale, avg_bf16, w_out_bf16, npost, modr = mix
    T = x.shape[1]
    r = tm // POOL_HALO
    last = T // POOL_HALO - 1
    tok = lambda w: pl.BlockSpec((None, tm, w), lambda b, i: (b, i, 0))
    specs = [tok(D_MODEL), tok(ATTN_WIDTH), tok(RET_WIDTH), tok(RET_WIDTH),
             pl.BlockSpec((None, tm, RET_WIDTH), lambda b, i: (b, i, 3)),
             pl.BlockSpec((None, POOL_HALO, POOL_WIDTH), lambda b, i: (b, jnp.maximum(i * r - 1, 0), 0)),
             tok(POOL_WIDTH),
             pl.BlockSpec((None, POOL_HALO, POOL_WIDTH), lambda b, i: (b, jnp.minimum((i + 1) * r, last), 0)),
             pl.BlockSpec((POOL_WIDTH, POOL_WIDTH), lambda b, i: (0, 0)), _row_spec(POOL_WIDTH),
             pl.BlockSpec((RET_WIDTH, RET_WIDTH), lambda b, i: (0, 0)),
             pl.BlockSpec((D_MODEL, D_MODEL), lambda b, i: (0, 0)),
             _row_spec(D_MODEL), _mod_spec(layer, 2, ctx)]
    args = [x, attn, of, ob, ret, pp, pp, pp, pool_w, pool_scale.reshape(1, POOL_WIDTH), avg_bf16, w_out_bf16,
            npost.reshape(1, D_MODEL), modr]
    return args, specs


def _swiglu_tile(h, w1_ref, w3_ref, w2_ref):
    acc = jnp.zeros((h.shape[0], D_MODEL), F32)
    for f in range(0, D_FF, FF_CHUNK):
        a = jnp.dot(h, w1_ref[:, f:f + FF_CHUNK], preferred_element_type=F32)
        b = jnp.dot(h, w3_ref[:, f:f + FF_CHUNK], preferred_element_type=F32)
        u = (_silu(a) * b).astype(BF16)
        acc = acc + jnp.dot(u, w2_ref[f:f + FF_CHUNK, :], preferred_element_type=F32)
    return acc


def _mix_ffn_kernel(*refs, seq_len):
    gpre_ref, sc_ref, sh_ref, w1_ref, w3_ref, w2_ref, npost_ref, g2_ref, o_ref, band_ref = refs[N_MIX_INPUTS:]

    @pl.when(pl.program_id(1) == 0)
    def _():
        _init_pool_bands(band_ref)

    x = _mix_tile(*refs[:N_MIX_INPUTS], band_ref, seq_len)
    h = (_rms(x, gpre_ref[...]) * (1.0 + sc_ref[...]) + sh_ref[...]).astype(BF16)
    y = _swiglu_tile(h, w1_ref, w3_ref, w2_ref)
    o_ref[...] = x + g2_ref[...] * _rms(y, npost_ref[...])


def _mix_ffn(mix, gpre, npost, w1, w3, w2, layer, ctx):
    x, modr = mix[0], mix[-1]
    B, T, _ = x.shape
    tm = min(TOKEN_TILE, T)
    args, specs = _mix_inputs(mix, tm, layer, ctx)
    wspec = lambda shape: pl.BlockSpec(shape, lambda b, i: (0, 0), pipeline_mode=pl.Buffered(1))
    return pl.pallas_call(
        functools.partial(_mix_ffn_kernel, seq_len=T),
        out_shape=jax.ShapeDtypeStruct((B, T, D_MODEL), F32),
        grid=(B, T // tm),
        in_specs=specs + [_row_spec(D_MODEL), _mod_spec(layer, 4, ctx), _mod_spec(layer, 3, ctx),
                          wspec((D_MODEL, D_FF)), wspec((D_MODEL, D_FF)), wspec((D_FF, D_MODEL)),
                          _row_spec(D_MODEL), _mod_spec(layer, 5, ctx)],
        out_specs=pl.BlockSpec((None, tm, D_MODEL), lambda b, i: (b, i, 0)),
        scratch_shapes=[POOL_BANDS],
        compiler_params=_cparams(("parallel", "arbitrary"), VMEM_LIMIT),
        name="mix_ffn",
    )(*args, gpre.reshape(1, D_MODEL), modr, modr, w1, w3, w2, npost.reshape(1, D_MODEL), modr)


def _mix_route_kernel(*refs, seq_len):
    (gpre_ref, sc_ref, sh_ref, rt_ref, x_ref, h_ref, idx_ref, gate_ref, rank_ref, cnt_ref,
     run_ref, upper_ref, band_ref) = refs[N_MIX_INPUTS:]
    tm = x_ref.shape[0]
    first = (pl.program_id(0) == 0) & (pl.program_id(1) == 0)

    @pl.when(first)
    def _():
        run_ref[...] = jnp.zeros_like(run_ref)
        earlier = lax.broadcasted_iota(I32, (tm, tm), 0) < lax.broadcasted_iota(I32, (tm, tm), 1)
        upper_ref[...] = jnp.where(earlier, 1.0, 0.0).astype(BF16)
        _init_pool_bands(band_ref)

    x = _mix_tile(*refs[:N_MIX_INPUTS], band_ref, seq_len)
    x_ref[...] = x
    h = _rms(x, gpre_ref[...]) * (1.0 + sc_ref[...]) + sh_ref[...]
    h_ref[...] = h
    logits = lax.dot_general(rt_ref[...], h, (((1,), (1,)), ((), ())), precision=lax.Precision.HIGHEST,
                             preferred_element_type=F32)
    eid = lax.broadcasted_iota(I32, (N_EXPERTS, tm), 0).astype(F32)
    m1 = jnp.max(logits, axis=0, keepdims=True)
    i1 = jnp.min(jnp.where(logits == m1, eid, float(N_EXPERTS)), axis=0, keepdims=True)
    oh1 = eid == i1
    rest = jnp.where(oh1, -jnp.inf, logits)
    m2 = jnp.max(rest, axis=0, keepdims=True)
    i2 = jnp.min(jnp.where(rest == m2, eid, float(N_EXPERTS)), axis=0, keepdims=True)
    oh2 = eid == i2
    e2 = jnp.exp(m2 - m1)
    gate_ref[0:1, :] = 1.0 / (1.0 + e2)
    gate_ref[1:2, :] = e2 / (1.0 + e2)
    idx_ref[0:1, :] = i1.astype(I32)
    idx_ref[1:2, :] = i2.astype(I32)
    upper = upper_ref[...]
    f1 = jnp.where(oh1, 1.0, 0.0)
    f2 = jnp.where(oh2, 1.0, 0.0)
    before1 = jnp.dot(f1.astype(BF16), upper, preferred_element_type=F32)
    before2 = jnp.dot(f2.astype(BF16), upper, preferred_element_type=F32)
    cnt1 = jnp.sum(f1, axis=1, keepdims=True)
    cnt2 = jnp.sum(f2, axis=1, keepdims=True)
    run = run_ref[:, 0:1]
    rank_ref[0:1, :] = jnp.sum(f1 * (run + before1), axis=0, keepdims=True).astype(I32)
    rank_ref[1:2, :] = jnp.sum(f2 * (run + cnt1 + before2), axis=0, keepdims=True).astype(I32)
    run_new = run_ref[...] + cnt1 + cnt2
    run_ref[...] = run_new
    cnt_ref[...] = run_new


def _mix_route(mix, gpre, router_t, layer):
    x, modr = mix[0], mix[-1]
    B, T, _ = x.shape
    tm = min(TOKEN_TILE, T)
    nt = T // tm
    args, specs = _mix_inputs(mix, tm, layer, False)
    tok = pl.BlockSpec((None, tm, D_MODEL), lambda b, i: (b, i, 0))
    lane = pl.BlockSpec((2, tm), lambda b, i: (0, b * nt + i))
    return pl.pallas_call(
        functools.partial(_mix_route_kernel, seq_len=T),
        out_shape=(jax.ShapeDtypeStruct((B, T, D_MODEL), F32),
                   jax.ShapeDtypeStruct((B, T, D_MODEL), F32),
                   jax.ShapeDtypeStruct((2, B * T), I32),
                   jax.ShapeDtypeStruct((2, B * T), F32),
                   jax.ShapeDtypeStruct((2, B * T), I32),
                   jax.ShapeDtypeStruct((N_EXPERTS, 128), F32)),
        grid=(B, nt),
        in_specs=specs + [_row_spec(D_MODEL), _mod_spec(layer, 4, False), _mod_spec(layer, 3, False),
                          pl.BlockSpec((N_EXPERTS, D_MODEL), lambda b, i: (0, 0))],
        out_specs=(tok, tok, lane, lane, lane,
                   pl.BlockSpec((N_EXPERTS, 128), lambda b, i: (0, 0))),
        scratch_shapes=[pltpu.VMEM((N_EXPERTS, 128), F32), pltpu.VMEM((tm, tm), BF16), POOL_BANDS],
        compiler_params=_cparams(("arbitrary", "arbitrary"), VMEM_LIMIT),
        name="mix_route",
    )(*args, gpre.reshape(1, D_MODEL), modr, modr, router_t)


def _dispatch_kernel(zero_tiles_ref, slot_ref, h_ref, hs_ref, zero_ref, sem, zero_sem):
    tm = h_ref.shape[0]

    @pl.when(pl.program_id(0) == 0)
    def _():
        zero_ref[...] = jnp.zeros_like(zero_ref)
        for j in range(2 * N_EXPERTS):
            start = pl.multiple_of(zero_tiles_ref[j] * EXPERT_TILE, EXPERT_TILE)
            fill = pltpu.make_async_copy(zero_ref, hs_ref.at[pl.ds(start, EXPERT_TILE)], zero_sem)
            fill.start()
            fill.wait()

    def body(i, carry):
        for u in range(DMA_UNROLL):
            r = i * DMA_UNROLL + u
            for k in range(2):
                s = slot_ref[0, k, r]
                pltpu.make_async_copy(h_ref.at[pl.ds(r, 1)], hs_ref.at[pl.ds(s, 1)], sem).start(priority=k)
        return carry

    lax.fori_loop(0, tm // DMA_UNROLL, body, 0)
    for _ in range(2):
        pltpu.make_async_copy(h_ref, hs_ref.at[pl.ds(0, tm)], sem).wait()


def _dispatch(h, slot_tiles, zero_tiles, n_slots):
    N = h.shape[0]
    tm = slot_tiles.shape[2]
    grid_spec = pltpu.PrefetchScalarGridSpec(
        num_scalar_prefetch=1,
        grid=(N // tm,),
        in_specs=[pl.BlockSpec((1, 2, tm), lambda i, zt: (i, 0, 0), memory_space=pltpu.SMEM),
                  pl.BlockSpec((tm, D_MODEL), lambda i, zt: (i, 0))],
        out_specs=pl.BlockSpec(memory_space=pl.ANY),
        scratch_shapes=[pltpu.VMEM((EXPERT_TILE, D_MODEL), F32),
                        pltpu.SemaphoreType.DMA(()), pltpu.SemaphoreType.DMA(())],
    )
    return pl.pallas_call(
        _dispatch_kernel,
        out_shape=jax.ShapeDtypeStruct((n_slots, D_MODEL), F32),
        grid_spec=grid_spec,
        compiler_params=_cparams(("arbitrary",)),
        name="moe_dispatch",
    )(zero_tiles, slot_tiles, h)


def _expert_kernel(te_ref, tv_ref, h_ref, w1_ref, w3_ref, w2_ref, y_ref):
    t = pl.program_id(0)

    @pl.when(tv_ref[t] == 1)
    def _():
        y_ref[...] = _swiglu_tile(h_ref[...].astype(BF16), w1_ref, w3_ref, w2_ref)

    @pl.when(tv_ref[t] == 0)
    def _():
        y_ref[...] = jnp.zeros_like(y_ref)


def _experts(hs, tile_expert, tile_valid, w1, w3, w2):
    n_slots = hs.shape[0]
    tm = EXPERT_TILE
    grid_spec = pltpu.PrefetchScalarGridSpec(
        num_scalar_prefetch=2,
        grid=(n_slots // tm,),
        in_specs=[pl.BlockSpec((tm, D_MODEL), lambda t, te, tv: (t, 0)),
                  pl.BlockSpec((None, D_MODEL, D_FF), lambda t, te, tv: (te[t], 0, 0)),
                  pl.BlockSpec((None, D_MODEL, D_FF), lambda t, te, tv: (te[t], 0, 0)),
                  pl.BlockSpec((None, D_FF, D_MODEL), lambda t, te, tv: (te[t], 0, 0))],
        out_specs=pl.BlockSpec((tm, D_MODEL), lambda t, te, tv: (t, 0)),
    )
    return pl.pallas_call(
        _expert_kernel,
        out_shape=jax.ShapeDtypeStruct((n_slots, D_MODEL), F32),
        grid_spec=grid_spec,
        compiler_params=_cparams(("arbitrary",), VMEM_LIMIT),
        name="moe_experts",
    )(tile_expert, tile_valid, hs, w1, w3, w2)


def _combine_kernel(slot_ref, gate_ref, x_ref, npost_ref, g2_ref, ys_ref, o_ref, buf0, buf1, sem):
    tm = x_ref.shape[0]
    bufs = (buf0, buf1)

    def body(i, carry):
        for u in range(DMA_UNROLL):
            r = i * DMA_UNROLL + u
            for k in range(2):
                s = slot_ref[0, k, r]
                pltpu.make_async_copy(ys_ref.at[pl.ds(s, 1)], bufs[k].at[pl.ds(r, 1)], sem).start(priority=k)
        return carry

    lax.fori_loop(0, tm // DMA_UNROLL, body, 0)
    for k in range(2):
        pltpu.make_async_copy(ys_ref.at[pl.ds(0, tm)], bufs[k], sem).wait()
    y = gate_ref[:, 0:1] * buf0[...] + gate_ref[:, 1:2] * buf1[...]
    o_ref[...] = x_ref[...] + g2_ref[...] * _rms(y, npost_ref[...])


def _combine(x, ys, slot_tiles, gate_tok, npost, modr, layer):
    B, T, _ = x.shape
    tm = slot_tiles.shape[2]
    nt = T // tm
    tok = pl.BlockSpec((None, tm, D_MODEL), lambda b, i: (b, i, 0))
    return pl.pallas_call(
        _combine_kernel,
        out_shape=jax.ShapeDtypeStruct((B, T, D_MODEL), F32),
        grid=(B, nt),
        in_specs=[pl.BlockSpec((1, 2, tm), lambda b, i: (b * nt + i, 0, 0), memory_space=pltpu.SMEM),
                  pl.BlockSpec((tm, 2), lambda b, i: (b * nt + i, 0)),
                  tok, _row_spec(D_MODEL), _mod_spec(layer, 5, False),
                  pl.BlockSpec(memory_space=pl.ANY)],
        out_specs=tok,
        scratch_shapes=[pltpu.VMEM((tm, D_MODEL), F32), pltpu.VMEM((tm, D_MODEL), F32),
                        pltpu.SemaphoreType.DMA(())],
        compiler_params=_cparams(("arbitrary", "arbitrary")),
        name="moe_combine",
    )(slot_tiles, gate_tok, x, npost.reshape(1, D_MODEL), modr, ys)


def _mix_moe_ffn(mix, gpre, npost, router, w1, w3, w2, layer):
    modr = mix[-1]
    B, T, _ = mix[0].shape
    N = B * T
    tm = min(TOKEN_TILE, T)
    x, h, idx, gate, rank, cnt = _mix_route(mix, gpre, router.T, layer)
    n_slots = 2 * N + N_EXPERTS * EXPERT_TILE
    n_tiles = n_slots // EXPERT_TILE
    counts = cnt[:, 0].astype(I32)
    padded = ((counts + EXPERT_TILE - 1) // EXPERT_TILE) * EXPERT_TILE
    ends = jnp.cumsum(padded)
    starts = ends - padded
    slot = rank
    for e in range(N_EXPERTS):
        slot = slot + jnp.where(idx == e, starts[e], 0)
    slot_tiles = slot.reshape(2, N // tm, tm).transpose(1, 0, 2)
    tile_start = jnp.arange(n_tiles, dtype=I32) * EXPERT_TILE
    tile_valid = (tile_start < ends[-1]).astype(I32)
    tile_expert = jnp.minimum(jnp.sum((tile_start[:, None] >= ends[None, :]).astype(I32), axis=1), N_EXPERTS - 1)
    last_expert = jnp.max(jnp.where(tile_valid == 1, tile_expert, 0))
    tile_expert = jnp.where(tile_valid == 1, tile_expert, last_expert)
    last_tile = jnp.where(padded > 0, ends // EXPERT_TILE - 1, n_tiles - 1)
    tail_tile = jnp.minimum(ends[-1] // EXPERT_TILE + jnp.arange(N_EXPERTS, dtype=I32), n_tiles - 1)
    zero_tiles = jnp.concatenate([last_tile, tail_tile]).astype(I32)
    hs = _dispatch(h.reshape(N, D_MODEL), slot_tiles, zero_tiles, n_slots)
    ys = _experts(hs, tile_expert, tile_valid, w1, w3, w2)
    return _combine(x, ys, slot_tiles, gate.T, npost, modr, layer)


def kernel(x, c, ctx, c_ctx, w_mod, b_mod, norm_pre_mix, norm_post_mix, norm_pre_ffn, norm_post_ffn, w_in, w_out, q_norm, k_norm, ret_decay_logit, pool_w, pool_scale, ffn_w1, ffn_w3, ffn_w2, moe_router, moe_w1, moe_w3, moe_w2):
    B, T, _ = x.shape
    cvec = jnp.zeros((MOD_ROWS, D_MODEL), F32).at[0:B].set(c).at[2].set(c_ctx)
    modr = _modulation(cvec, w_mod, b_mod)
    cos_t, sin_t = _rope_tables(T)
    avg = jnp.kron(jnp.eye(RET_HEADS, dtype=F32), jnp.full((HEAD_DIM, HEAD_DIM), 1.0 / HEAD_DIM, F32)).astype(BF16)
    zero_state = jnp.zeros((B, RET_WIDTH, RET_WIDTH), F32)
    xc = ctx
    for i in range(DEPTH):
        need_ctx = i < DEPTH - 1
        w_in_b = w_in[i].astype(BF16)
        w_out_b = w_out[i].astype(BF16)
        pool_bd = jax.scipy.linalg.block_diag(*[pool_w[i, g] for g in range(len(POOL_WINDOWS))]).astype(BF16)
        log_gamma = jax.nn.log_sigmoid(ret_decay_logit[i].astype(F32))

        qkv_c, ret_c, pp_c = _in_projection(xc, modr, norm_pre_mix[i], w_in_b, i, True)
        qt_c, kn_c, vt_c, qn_c, st_c = _attn_prep(qkv_c, q_norm[i], k_norm[i], cos_t, sin_t, False)
        of_c, ob_c, s_fwd, s_bwd = _retention(ret_c, log_gamma, zero_state, zero_state)

        qkv_x, ret_x, pp_x = _in_projection(x, modr, norm_pre_mix[i], w_in_b, i, False)
        qt_x, kn_x, vt_x, qn_x, st_x = _attn_prep(qkv_x, q_norm[i], k_norm[i], cos_t, sin_t, True)
        keys_x = [(kn_x, vt_x), (kn_c, vt_c)]
        attn_x = _attention(qt_x, _softmax_shift(qn_x, st_x, [st_x, st_c], qt_x, keys_x), keys_x)
        of_x, ob_x, _, _ = _retention(ret_x, log_gamma, s_fwd, s_bwd)
        mix_x = (x, attn_x, of_x, ob_x, ret_x, pp_x, pool_bd, pool_scale[i], avg, w_out_b, norm_post_mix[i], modr)

        j = i // 2
        if i % 2 == 0:
            w1, w3, w2 = ffn_w1[j].astype(BF16), ffn_w3[j].astype(BF16), ffn_w2[j].astype(BF16)
            x = _mix_ffn(mix_x, norm_pre_ffn[i], norm_post_ffn[i], w1, w3, w2, i, False)
            if need_ctx:
                keys_c = [(kn_c, vt_c)]
                attn_c = _attention(qt_c, _softmax_shift(qn_c, st_c, [st_c], qt_c, keys_c), keys_c)
                mix_c = (xc, attn_c, of_c, ob_c, ret_c, pp_c, pool_bd, pool_scale[i], avg, w_out_b,
                         norm_post_mix[i], modr)
                xc = _mix_ffn(mix_c, norm_pre_ffn[i], norm_post_ffn[i], w1, w3, w2, i, True)
        else:
            assert not need_ctx
            x = _mix_moe_ffn(mix_x, norm_pre_ffn[i], norm_post_ffn[i], moe_router[j], moe_w1[j].astype(BF16),
                             moe_w3[j].astype(BF16), moe_w2[j].astype(BF16), i)
    return x
```

```python
import functools

import jax
import jax.numpy as jnp
from jax import lax
from jax.experimental import pallas as pl
from jax.experimental.pallas import tpu as pltpu

F32 = jnp.float32
BF16 = jnp.bfloat16
I32 = jnp.int32

D_MODEL = 1024
GRID_W = 64
HEAD_DIM = 64
ATTN_WIDTH = 512
KV_HEADS = 2
ATTN_GROUP = 4
KV_WIDTH = 128
RET_WIDTH = 256
RET_HEADS = 4
POOL_WIDTH = 256
POOL_WINDOWS = (2, 4, 8, 16)
IN_WIDTH = 2048
RET_CHUNK = 128
ROPE_THETA = 10000.0
D_FF = 2816
N_EXPERTS = 8
NORM_EPS = 1e-6
DEPTH = 2

TOKEN_TILE = 1024
ATTN_Q_TILE = 512
ATTN_K_CHUNK = 2048
PREP_TILE = 2048
RET_TILE = 512
POOL_TILE = 256
POOL_HALO = 16
FF_CHUNK = 256
EXPERT_TILE = 512
DMA_UNROLL = 8
BF16_SUBLANES = 16
MXU_DIM = 256
K_COLS = MXU_DIM
KEY_BLOCK = MXU_DIM
ATTN_UNROLL = 9
N_MXU = 2
SCORE_REG, VALUE_REG = 1, 0
SCORE_BASE = 64
VT_ROWS = HEAD_DIM + BF16_SUBLANES
MAX_BOUND_SHIFT = 40.0
SHIFT_MARGIN = 1.01
MOD_ROWS = 8
VMEM_LIMIT = 56 * 1024 * 1024


def _cparams(sem, vmem=None):
    return pltpu.CompilerParams(dimension_semantics=sem, vmem_limit_bytes=vmem)


def _rms(x, gain):
    ms = jnp.mean(x * x, axis=-1, keepdims=True)
    return x * lax.rsqrt(ms + NORM_EPS) * gain


def _silu(x):
    return x * jax.nn.sigmoid(x)


def _mod_kernel(c_ref, w_ref, b_ref, o_ref):
    s = _silu(c_ref[...])
    o_ref[...] = jnp.dot(s, w_ref[...], precision=lax.Precision.HIGHEST,
                         preferred_element_type=F32) + b_ref[...]


def _modulation(cvec, w_mod, b_mod):
    nchunk = 6
    out = pl.pallas_call(
        _mod_kernel,
        out_shape=jax.ShapeDtypeStruct((DEPTH, MOD_ROWS, 6 * D_MODEL), F32),
        grid=(DEPTH, nchunk),
        in_specs=[
            pl.BlockSpec((MOD_ROWS, D_MODEL), lambda l, j: (0, 0)),
            pl.BlockSpec((None, D_MODEL, D_MODEL), lambda l, j: (l, 0, j)),
            pl.BlockSpec((None, 1, D_MODEL), lambda l, j: (l, 0, j)),
        ],
        out_specs=pl.BlockSpec((None, MOD_ROWS, D_MODEL), lambda l, j: (l, 0, j)),
        compiler_params=_cparams(("parallel", "parallel")),
        name="modulation",
    )(cvec, w_mod, b_mod.reshape(DEPTH, 1, 6 * D_MODEL))
    return out.reshape(DEPTH * MOD_ROWS * nchunk, 1, D_MODEL)


def _mod_spec(layer, chunk, ctx):
    base = layer * MOD_ROWS * 6
    if ctx:
        return pl.BlockSpec((None, 1, D_MODEL), lambda b, i: (base + 2 * 6 + chunk, 0, 0))
    return pl.BlockSpec((None, 1, D_MODEL), lambda b, i: (base + b * 6 + chunk, 0, 0))


def _row_spec(width):
    return pl.BlockSpec((1, width), lambda b, i: (0, 0))


def _inproj_kernel(x_ref, g_ref, sc_ref, sh_ref, w_ref, qkv_ref, ret_ref, pp_ref):
    h = _rms(x_ref[...], g_ref[...]) * (1.0 + sc_ref[...]) + sh_ref[...]
    p = jnp.dot(h.astype(BF16), w_ref[...], preferred_element_type=F32)
    qkv_ref[...] = p[:, :768].astype(BF16)
    ret_ref[...] = p[:, 768:1792].astype(BF16)
    pp_ref[...] = p[:, 1792:].astype(BF16)


def _in_projection(x, modr, gain, w_in_bf16, layer, ctx):
    B, T, _ = x.shape
    tm = min(TOKEN_TILE, T)
    tok = lambda w: pl.BlockSpec((None, tm, w), lambda b, i: (b, i, 0))
    return pl.pallas_call(
        _inproj_kernel,
        out_shape=(jax.ShapeDtypeStruct((B, T, 768), BF16),
                   jax.ShapeDtypeStruct((B, T, 1024), BF16),
                   jax.ShapeDtypeStruct((B, T, POOL_WIDTH), BF16)),
        grid=(B, T // tm),
        in_specs=[tok(D_MODEL), _row_spec(D_MODEL), _mod_spec(layer, 1, ctx), _mod_spec(layer, 0, ctx),
                  pl.BlockSpec((D_MODEL, IN_WIDTH), lambda b, i: (0, 0))],
        out_specs=(tok(768), tok(1024), tok(POOL_WIDTH)),
        compiler_params=_cparams(("parallel", "parallel"), VMEM_LIMIT),
        name="in_projection",
    )(x, gain.reshape(1, D_MODEL), modr, modr, w_in_bf16)


def _prep_kernel(qkv_ref, qg_ref, kg_ref, cos_ref, sin_ref, qt_ref, kn_ref, vt_ref, qn_ref, stat_ref, *, rope):
    t = qkv_ref[...].astype(F32).T

    def norm_rope(blk, gain):
        ms = jnp.mean(blk * blk, axis=0, keepdims=True)
        y = blk * lax.rsqrt(ms + NORM_EPS) * gain
        if rope:
            partner = jnp.concatenate([y[16:32], y[0:16], y[48:64], y[32:48]], axis=0)
            y = y * cos_ref[...] + partner * sin_ref[...]
        return y

    def sq_norm(y):
        return jnp.sum(y * y, axis=0, keepdims=True)

    def row_max(n2):
        return jnp.broadcast_to(jnp.max(n2, axis=1, keepdims=True), (1, 128))

    q_stats = []
    for h in range(ATTN_WIDTH // HEAD_DIM):
        lo = h * HEAD_DIM
        q = norm_rope(t[lo:lo + HEAD_DIM], qg_ref[...]) * (HEAD_DIM ** -0.5)
        qt_ref[lo:lo + HEAD_DIM, :] = q.astype(BF16)
        n2 = sq_norm(q)
        qn_ref[h // ATTN_GROUP, h % ATTN_GROUP:h % ATTN_GROUP + 1, :] = jnp.sqrt(n2)
        q_stats.append(row_max(n2))
    tm = t.shape[1]
    k_pad = jnp.where(lax.broadcasted_iota(I32, (K_COLS - HEAD_DIM, tm), 0) < 2, 1.0, 0.0)
    v_pad = jnp.where(lax.broadcasted_iota(I32, (VT_ROWS - HEAD_DIM, tm), 0) < 1, 1.0, 0.0)
    k_stats = []
    for kv in range(KV_HEADS):
        lo = ATTN_WIDTH + kv * HEAD_DIM
        k = norm_rope(t[lo:lo + HEAD_DIM], kg_ref[...])
        k_stats.append(row_max(sq_norm(k)))
        kn_ref[kv] = jnp.concatenate([k, k_pad], axis=0).T.astype(BF16)
        lo = ATTN_WIDTH + KV_WIDTH + kv * HEAD_DIM
        vt = jnp.concatenate([t[lo:lo + HEAD_DIM], v_pad], axis=0).astype(BF16)
        for j in range(tm // KEY_BLOCK):
            vt_ref[kv, j] = vt[:, j * KEY_BLOCK:(j + 1) * KEY_BLOCK]
    q_group = [functools.reduce(jnp.maximum, q_stats[kv * ATTN_GROUP:(kv + 1) * ATTN_GROUP])
               for kv in range(KV_HEADS)]
    stat_ref[...] = jnp.concatenate(k_stats + q_group + [jnp.zeros((8 - 2 * KV_HEADS, 128), F32)], axis=0)


def _attn_prep(qkv, q_gain, k_gain, cos_t, sin_t, rope):
    B, T, _ = qkv.shape
    tm = min(PREP_TILE, T)
    nc = T // tm
    return pl.pallas_call(
        functools.partial(_prep_kernel, rope=rope),
        out_shape=(jax.ShapeDtypeStruct((B, ATTN_WIDTH, T), BF16),
                   jax.ShapeDtypeStruct((B, KV_HEADS, T, K_COLS), BF16),
                   jax.ShapeDtypeStruct((B, KV_HEADS, T // KEY_BLOCK, VT_ROWS, KEY_BLOCK), BF16),
                   jax.ShapeDtypeStruct((B, KV_HEADS, ATTN_GROUP, T), F32),
                   jax.ShapeDtypeStruct((B, nc, 8, 128), F32)),
        grid=(B, nc),
        in_specs=[pl.BlockSpec((None, tm, 768), lambda b, i: (b, i, 0)),
                  pl.BlockSpec((HEAD_DIM, 1), lambda b, i: (0, 0)),
                  pl.BlockSpec((HEAD_DIM, 1), lambda b, i: (0, 0)),
                  pl.BlockSpec((HEAD_DIM, tm), lambda b, i: (0, i)),
                  pl.BlockSpec((HEAD_DIM, tm), lambda b, i: (0, i))],
        out_specs=(pl.BlockSpec((None, ATTN_WIDTH, tm), lambda b, i: (b, 0, i)),
                   pl.BlockSpec((None, KV_HEADS, tm, K_COLS), lambda b, i: (b, 0, i, 0)),
                   pl.BlockSpec((None, KV_HEADS, tm // KEY_BLOCK, VT_ROWS, KEY_BLOCK), lambda b, i: (b, 0, i, 0, 0)),
                   pl.BlockSpec((None, KV_HEADS, ATTN_GROUP, tm), lambda b, i: (b, 0, 0, i)),
                   pl.BlockSpec((None, None, 8, 128), lambda b, i: (b, i, 0, 0))),
        compiler_params=_cparams(("parallel", "parallel")),
        name="attn_prep",
    )(qkv, q_gain.reshape(HEAD_DIM, 1), k_gain.reshape(HEAD_DIM, 1), cos_t, sin_t)


def _rope_tables(T):
    t = jnp.arange(T)
    row = (t // GRID_W).astype(F32)
    col = (t % GRID_W).astype(F32)
    n_freq = HEAD_DIM // 4
    inv = ROPE_THETA ** (-jnp.arange(n_freq, dtype=F32) / n_freq)
    ang_r = row[None, :] * inv[:, None]
    ang_c = col[None, :] * inv[:, None]
    cos_t = jnp.concatenate([jnp.cos(ang_r), jnp.cos(ang_r), jnp.cos(ang_c), jnp.cos(ang_c)], axis=0)
    sin_t = jnp.concatenate([-jnp.sin(ang_r), jnp.sin(ang_r), -jnp.sin(ang_c), jnp.sin(ang_c)], axis=0)
    return cos_t, sin_t


def _score_max_kernel(qt_ref, k_ref, m_ref, qa_ref):
    tq = qt_ref.shape[1]
    nq = ATTN_GROUP * tq
    for g in range(ATTN_GROUP):
        qa_ref[0:HEAD_DIM, g * tq:(g + 1) * tq] = qt_ref[g * HEAD_DIM:(g + 1) * HEAD_DIM, :]
    qa_ref[HEAD_DIM:K_COLS, :] = jnp.zeros((K_COLS - HEAD_DIM, nq), BF16)
    tk = min(ATTN_K_CHUNK, k_ref.shape[0])

    def colmax8(j):
        k = k_ref[pl.ds(pl.multiple_of(j * tk, tk), tk), :]
        s = jnp.dot(k, qa_ref[...], preferred_element_type=F32)
        return jnp.max(s.reshape(tk // 8, 8, nq), axis=0)

    mx = lax.fori_loop(1, k_ref.shape[0] // tk, lambda j, mx: jnp.maximum(mx, colmax8(j)), colmax8(0))
    m = jnp.max(mx, axis=0, keepdims=True)
    for g in range(ATTN_GROUP):
        m_ref[g:g + 1, :] = m[:, g * tq:(g + 1) * tq]


def _score_max(qt, k_all):
    B, _, Tq = qt.shape
    Tk = k_all.shape[2]
    tq = min(ATTN_Q_TILE, Tq)
    return pl.pallas_call(
        _score_max_kernel,
        out_shape=jax.ShapeDtypeStruct((B, KV_HEADS, ATTN_GROUP, Tq), F32),
        grid=(B, KV_HEADS, Tq // tq),
        in_specs=[pl.BlockSpec((None, ATTN_GROUP * HEAD_DIM, tq), lambda b, h, i: (b, h, i)),
                  pl.BlockSpec((None, None, Tk, K_COLS), lambda b, h, i: (b, h, 0, 0))],
        out_specs=pl.BlockSpec((None, None, ATTN_GROUP, tq), lambda b, h, i: (b, h, 0, i)),
        scratch_shapes=[pltpu.VMEM((K_COLS, ATTN_GROUP * tq), BF16)],
        compiler_params=_cparams(("parallel", "parallel", "parallel"), VMEM_LIMIT),
        name="score_max",
    )(qt, k_all)


def _softmax_shift(q_norm, q_stats, k_stats_list, qt, key_sets):
    B = qt.shape[0]
    k2 = functools.reduce(jnp.maximum, [st[:, :, 0:KV_HEADS, 0].max(axis=1) for st in k_stats_list])
    q2 = q_stats[:, :, KV_HEADS:2 * KV_HEADS, 0].max(axis=1)
    use_bound = (jnp.sqrt(k2 * q2) * SHIFT_MARGIN <= MAX_BOUND_SHIFT).reshape(B, KV_HEADS, 1, 1)
    bound = q_norm * (jnp.sqrt(k2) * SHIFT_MARGIN).reshape(B, KV_HEADS, 1, 1)
    exact = lambda: _score_max(qt, jnp.concatenate([k for k, _ in key_sets], axis=2))
    return lax.cond(jnp.all(use_bound), lambda: bound, lambda: jnp.where(use_bound, bound, exact()))


def _attn_kernel(qt_ref, shift_ref, *refs):
    *kv_refs, o_ref, qa_ref = refs
    sources = [(kv_refs[i], kv_refs[i + 1]) for i in range(0, len(kv_refs), 2)]
    counts = [k_ref.shape[0] // KEY_BLOCK for k_ref, _ in sources]
    tq = qt_ref.shape[1]
    n_blocks = sum(counts)
    heads_per_mxu = ATTN_GROUP // N_MXU

    def locate(blk):
        if not isinstance(blk, int):
            return 0, blk
        src = 0
        while blk >= counts[src]:
            blk -= counts[src]
            src += 1
        return src, blk

    row = lax.broadcasted_iota(I32, (BF16_SUBLANES, tq), 0)
    for g in range(ATTN_GROUP):
        m = shift_ref[g:g + 1, :]
        m_hi = m.astype(BF16).astype(F32)
        qa_ref[g, 0:HEAD_DIM, :] = qt_ref[g * HEAD_DIM:(g + 1) * HEAD_DIM, :]
        qa_ref[g, HEAD_DIM:HEAD_DIM + BF16_SUBLANES, :] = jnp.where(
            row == 0, -m_hi, jnp.where(row == 1, m_hi - m, 0.0)).astype(BF16)
        qa_ref[g, HEAD_DIM + BF16_SUBLANES:K_COLS, :] = jnp.zeros((K_COLS - HEAD_DIM - BF16_SUBLANES, tq), BF16)

    def score_addr(slot):
        return SCORE_BASE + slot * (KEY_BLOCK // 4)

    def out_addr(slot):
        return slot * (VT_ROWS // 4)

    def keys(blk):
        src, j = locate(blk)
        return sources[src][0][pl.ds(pl.multiple_of(j * KEY_BLOCK, KEY_BLOCK), KEY_BLOCK), :]

    def stage_q(slot):
        for mxu in range(N_MXU):
            pltpu.matmul_push_rhs(qa_ref[mxu * heads_per_mxu + slot], SCORE_REG, mxu)

    def issue_scores(blk, slot):
        k = keys(blk)
        for mxu in range(N_MXU):
            pltpu.matmul_acc_lhs(score_addr(slot), k, mxu, load_staged_rhs=SCORE_REG)

    def pop_probs(slot):
        return [jnp.exp(pltpu.matmul_pop(score_addr(slot), (KEY_BLOCK, tq), F32, mxu)).astype(BF16)
                for mxu in range(N_MXU)]

    def push_probs(blk, slot, p):
        src, j = locate(blk)
        vt = sources[src][1][j]
        for mxu in range(N_MXU):
            pltpu.matmul_push_rhs(p[mxu], VALUE_REG, mxu)
        return vt

    def unit(blk, slot, stage_next, first):
        if not first:
            p = pop_probs(slot)
        issue_scores(blk, slot)
        if not first:
            vt = push_probs(blk - 1, slot, p)
        if stage_next:
            stage_q((slot + 1) % heads_per_mxu)
        if not first:
            for mxu in range(N_MXU):
                pltpu.matmul_acc_lhs(out_addr(slot), vt, mxu, load_staged_rhs=VALUE_REG)

    def block(blk, last=False, first=False):
        for slot in range(heads_per_mxu):
            unit(blk, slot, stage_next=not (last and slot == heads_per_mxu - 1), first=first)

    stage_q(0)
    block(0, last=n_blocks == 1, first=True)
    if n_blocks > 1:
        n_loop = n_blocks - 2
        assert n_loop + 1 <= counts[0]
        unroll = max([u for u in range(1, ATTN_UNROLL + 1) if n_loop % u == 0] or [1])

        def body(i, carry):
            for u in range(unroll):
                block(1 + i * unroll + u)
            return carry

        if n_loop:
            lax.fori_loop(0, n_loop // unroll, body, 0)
        block(n_blocks - 1, last=True)
    for slot in range(heads_per_mxu):
        vt = push_probs(n_blocks - 1, slot, pop_probs(slot))
        for mxu in range(N_MXU):
            pltpu.matmul_acc_lhs(out_addr(slot), vt, mxu, load_staged_rhs=VALUE_REG)

    outs = []
    for mxu in range(N_MXU):
        for slot in range(heads_per_mxu):
            acc = pltpu.matmul_pop(out_addr(slot), (VT_ROWS, tq), F32, mxu)
            outs.append(acc[0:HEAD_DIM, :] / acc[HEAD_DIM:HEAD_DIM + 1, :])
    o_ref[...] = jnp.concatenate(outs, axis=0).T.astype(BF16)


def _attention(qt, shift, key_sets):
    B, _, Tq = qt.shape
    tq = MXU_DIM
    in_specs = [pl.BlockSpec((None, ATTN_GROUP * HEAD_DIM, tq), lambda b, h, i: (b, h, i)),
                pl.BlockSpec((None, None, ATTN_GROUP, tq), lambda b, h, i: (b, h, 0, i))]
    args = [qt, shift]
    for k, vt in key_sets:
        Tk = k.shape[2]
        in_specs += [pl.BlockSpec((None, None, Tk, K_COLS), lambda b, h, i: (b, h, 0, 0)),
                     pl.BlockSpec((None, None, Tk // KEY_BLOCK, VT_ROWS, KEY_BLOCK), lambda b, h, i: (b, h, 0, 0, 0))]
        args += [k, vt]
    return pl.pallas_call(
        _attn_kernel,
        out_shape=jax.ShapeDtypeStruct((B, Tq, ATTN_WIDTH), BF16),
        grid=(B, KV_HEADS, Tq // tq),
        in_specs=in_specs,
        out_specs=pl.BlockSpec((None, tq, ATTN_GROUP * HEAD_DIM), lambda b, h, i: (b, i, h)),
        scratch_shapes=[pltpu.VMEM((ATTN_GROUP, K_COLS, tq), BF16)],
        compiler_params=_cparams(("parallel", "parallel", "parallel"), VMEM_LIMIT),
        name="attention",
    )(*args)


def _ret_kernel(lgc_f_ref, lgc_b_ref, lgr_f_ref, lgr_b_ref, s0f_ref, s0b_ref, blk_f_ref, blk_b_ref,
                of_ref, ob_ref, sf_ref, sb_ref, st_f, st_b, dec_f, dec_b):
    C = RET_CHUNK
    W = RET_WIDTH
    n = pl.program_id(1)

    @pl.when(n == 0)
    def _():
        st_f[...] = s0f_ref[...]
        st_b[...] = s0b_ref[...]
        c = jnp.bitwise_and(lax.broadcasted_iota(I32, (RET_HEADS * C, C), 0), C - 1)
        m = lax.broadcasted_iota(I32, (RET_HEADS * C, C), 1)
        diff = (c - m).astype(F32)
        dec_f[...] = jnp.where(diff >= 0, jnp.exp(lgc_f_ref[...] * jnp.maximum(diff, 0.0)), 0.0)
        dec_b[...] = jnp.where(diff <= 0, jnp.exp(lgc_b_ref[...] * jnp.maximum(-diff, 0.0)), 0.0)

    lane_head = jnp.right_shift(lax.broadcasted_iota(I32, (C, W), 1), 6)
    pos = lax.broadcasted_iota(I32, (C, W), 0).astype(F32)
    same_head = (jnp.right_shift(lax.broadcasted_iota(I32, (W, W), 0), 6)
                 == jnp.right_shift(lax.broadcasted_iota(I32, (W, W), 1), 6))

    def direction(blk_ref, dec_ref, st_ref, lgr, forward, out_ref):
        n_sub = blk_ref.shape[0] // C
        state = st_ref[...]
        for sub in (range(n_sub) if forward else reversed(range(n_sub))):
            state = chunk(blk_ref, dec_ref, state, lgr, forward, out_ref, sub * C)
        st_ref[...] = state

    def chunk(blk_ref, dec_ref, state, lgr, forward, out_ref, r0):
        q = blk_ref[r0:r0 + C, 0:W].astype(F32)
        kf = blk_ref[r0:r0 + C, W:2 * W].astype(F32) * (HEAD_DIM ** -0.5)
        v = blk_ref[r0:r0 + C, 2 * W:3 * W]
        if forward:
            zeta = jnp.exp(lgr * (C - 1.0 - pos))
            xi = jnp.exp(lgr * (pos + 1.0))
        else:
            zeta = jnp.exp(lgr * pos)
            xi = jnp.exp(lgr * (C - pos))
        qexp = jnp.concatenate([jnp.where(lane_head == h, q, 0.0) for h in range(RET_HEADS)],
                               axis=0).astype(BF16)
        a = lax.dot_general(qexp, kf.astype(BF16), (((1,), (1,)), ((), ())),
                            preferred_element_type=F32)
        p = (a * dec_ref[...]).astype(BF16)
        full = jnp.dot(p, v, preferred_element_type=F32)
        intra = jnp.zeros((C, W), F32)
        for h in range(RET_HEADS):
            intra = intra + jnp.where(lane_head == h, full[h * C:(h + 1) * C], 0.0)
        cross = jnp.dot((q * xi).astype(BF16), state.astype(BF16), preferred_element_type=F32)
        out_ref[r0:r0 + C, :] = intra + cross
        upd = lax.dot_general((kf * zeta).astype(BF16), v, (((0,), (0,)), ((), ())),
                              preferred_element_type=F32)
        return jnp.where(same_head, state * jnp.exp(lgr * float(C)) + upd, 0.0)

    direction(blk_f_ref, dec_f, st_f, lgr_f_ref[...], True, of_ref)
    direction(blk_b_ref, dec_b, st_b, lgr_b_ref[...], False, ob_ref)

    @pl.when(n == pl.num_programs(1) - 1)
    def _():
        sf_ref[...] = st_f[...]
        sb_ref[...] = st_b[...]


def _retention(ret, log_gamma, s0f, s0b):
    B, T, _ = ret.shape
    C = RET_CHUNK
    tm = min(RET_TILE, T)
    nc = T // tm
    lgc = lambda d: jnp.repeat(log_gamma[d], C).reshape(RET_HEADS * C, 1)
    lgr = lambda d: jnp.repeat(log_gamma[d], HEAD_DIM).reshape(1, RET_WIDTH)
    const = lambda shape: pl.BlockSpec(shape, lambda b, n: (0,) * len(shape))
    st_spec = pl.BlockSpec((None, RET_WIDTH, RET_WIDTH), lambda b, n: (b, 0, 0))
    return pl.pallas_call(
        _ret_kernel,
        out_shape=(jax.ShapeDtypeStruct((B, T, RET_WIDTH), F32),
                   jax.ShapeDtypeStruct((B, T, RET_WIDTH), F32),
                   jax.ShapeDtypeStruct((B, RET_WIDTH, RET_WIDTH), F32),
                   jax.ShapeDtypeStruct((B, RET_WIDTH, RET_WIDTH), F32)),
        grid=(B, nc),
        in_specs=[const((RET_HEADS * C, 1)), const((RET_HEADS * C, 1)),
                  const((1, RET_WIDTH)), const((1, RET_WIDTH)), st_spec, st_spec,
                  pl.BlockSpec((None, tm, 1024), lambda b, n: (b, n, 0)),
                  pl.BlockSpec((None, tm, 1024), lambda b, n: (b, nc - 1 - n, 0))],
        out_specs=(pl.BlockSpec((None, tm, RET_WIDTH), lambda b, n: (b, n, 0)),
                   pl.BlockSpec((None, tm, RET_WIDTH), lambda b, n: (b, nc - 1 - n, 0)),
                   st_spec, st_spec),
        scratch_shapes=[pltpu.VMEM((RET_WIDTH, RET_WIDTH), F32), pltpu.VMEM((RET_WIDTH, RET_WIDTH), F32),
                        pltpu.VMEM((RET_HEADS * C, C), F32), pltpu.VMEM((RET_HEADS * C, C), F32)],
        compiler_params=_cparams(("parallel", "arbitrary")),
        name="retention",
    )(lgc(0), lgc(1), lgr(0), lgr(1), s0f, s0b, ret, ret)


def _init_pool_bands(band_ref):
    tok = lax.broadcasted_iota(I32, (POOL_TILE, POOL_TILE + 2 * POOL_HALO), 0)
    src = lax.broadcasted_iota(I32, (POOL_TILE, POOL_TILE + 2 * POOL_HALO), 1) - POOL_HALO
    for gi, w in enumerate(POOL_WINDOWS):
        inside = (src >= tok - w // 2) & (src < tok + w // 2)
        band_ref[gi] = jnp.where(inside, 1.0, 0.0).astype(BF16)


def _pool_tile(prev_ref, cur_ref, next_ref, w_ref, scale_ref, band_ref, seq_len):
    tm = cur_ref.shape[0]
    i = pl.program_id(1)
    prev = jnp.where(i > 0, prev_ref[...], jnp.zeros_like(prev_ref))
    nxt = jnp.where(i < pl.num_programs(1) - 1, next_ref[...], jnp.zeros_like(next_ref))
    ext = jnp.concatenate([prev, cur_ref[...], nxt], axis=0)
    sub = min(POOL_TILE, tm)
    lane_group = jnp.right_shift(lax.broadcasted_iota(I32, (sub, POOL_WIDTH), 1), 6)
    parts = []
    for r0 in range(0, tm, sub):
        window = ext[r0:r0 + sub + 2 * POOL_HALO]
        cur = cur_ref[r0:r0 + sub, :].astype(F32)
        tcol = i * tm + r0 + lax.broadcasted_iota(I32, (sub, 1), 0)
        mixed = jnp.zeros((sub, POOL_WIDTH), F32)
        for gi, w in enumerate(POOL_WINDOWS):
            total = jnp.dot(band_ref[gi], window, preferred_element_type=F32)
            cnt = (jnp.minimum(tcol + w // 2, seq_len) - jnp.maximum(tcol - w // 2, 0)).astype(F32)
            mixed = mixed + jnp.where(lane_group == gi, total / cnt - cur, 0.0)
        y = jnp.dot(mixed.astype(BF16), w_ref[...], preferred_element_type=F32)
        parts.append((y * scale_ref[...]).astype(BF16))
    return jnp.concatenate(parts, axis=0)


def _head_mean(x, avg):
    hi = x.astype(BF16)
    lo = (x - hi.astype(F32)).astype(BF16)
    return (jnp.dot(hi, avg, preferred_element_type=F32) + jnp.dot(lo, avg, preferred_element_type=F32))


def _mix_tile(x_ref, attn_ref, of_ref, ob_ref, gate_ref, pp_prev_ref, pp_ref, pp_next_ref, pool_w_ref,
              pool_scale_ref, avg_ref, w_ref, npost_ref, g1_ref, band_ref, seq_len):
    pool = _pool_tile(pp_prev_ref, pp_ref, pp_next_ref, pool_w_ref, pool_scale_ref, band_ref, seq_len)
    o = of_ref[...] + ob_ref[...]
    avg = avg_ref[...]
    mu = _head_mean(o, avg)
    cen = o - mu
    var = _head_mean(cen * cen, avg)
    y_ret = (_silu(gate_ref[...].astype(F32)) * (cen * lax.rsqrt(var + NORM_EPS))).astype(BF16)
    mx = (jnp.dot(attn_ref[...], w_ref[0:ATTN_WIDTH, :], preferred_element_type=F32)
          + jnp.dot(y_ret, w_ref[ATTN_WIDTH:ATTN_WIDTH + RET_WIDTH, :], preferred_element_type=F32)
          + jnp.dot(pool, w_ref[ATTN_WIDTH + RET_WIDTH:, :], preferred_element_type=F32))
    return x_ref[...] + g1_ref[...] * _rms(mx, npost_ref[...])


N_MIX_INPUTS = 14
POOL_BANDS = pltpu.VMEM((len(POOL_WINDOWS), POOL_TILE, POOL_TILE + 2 * POOL_HALO), BF16)


def _mix_inputs(mix, tm, layer, ctx):
    x, attn, of, ob, ret, pp, pool_w, pool_scale, avg_bf16, w_out_bf16, npost, modr = mix
    T = x.shape[1]
    r = tm // POOL_HALO
    last = T // POOL_HALO - 1
    tok = lambda w: pl.BlockSpec((None, tm, w), lambda b, i: (b, i, 0))
    specs = [tok(D_MODEL), tok(ATTN_WIDTH), tok(RET_WIDTH), tok(RET_WIDTH),
             pl.BlockSpec((None, tm, RET_WIDTH), lambda b, i: (b, i, 3)),
             pl.BlockSpec((None, POOL_HALO, POOL_WIDTH), lambda b, i: (b, jnp.maximum(i * r - 1, 0), 0)),
             tok(POOL_WIDTH),
             pl.BlockSpec((None, POOL_HALO, POOL_WIDTH), lambda b, i: (b, jnp.minimum((i + 1) * r, last), 0)),
             pl.BlockSpec((POOL_WIDTH, POOL_WIDTH), lambda b, i: (0, 0)), _row_spec(POOL_WIDTH),
             pl.BlockSpec((RET_WIDTH, RET_WIDTH), lambda b, i: (0, 0)),
             pl.BlockSpec((D_MODEL, D_MODEL), lambda b, i: (0, 0)),
             _row_spec(D_MODEL), _mod_spec(layer, 2, ctx)]
    args = [x, attn, of, ob, ret, pp, pp, pp, pool_w, pool_scale.reshape(1, POOL_WIDTH), avg_bf16, w_out_bf16,
            npost.reshape(1, D_MODEL), modr]
    return args, specs


def _swiglu_tile(h, w1_ref, w3_ref, w2_ref):
    acc = jnp.zeros((h.shape[0], D_MODEL), F32)
    for f in range(0, D_FF, FF_CHUNK):
        a = jnp.dot(h, w1_ref[:, f:f + FF_CHUNK], preferred_element_type=F32)
        b = jnp.dot(h, w3_ref[:, f:f + FF_CHUNK], preferred_element_type=F32)
        u = (_silu(a) * b).astype(BF16)
        acc = acc + jnp.dot(u, w2_ref[f:f + FF_CHUNK, :], preferred_element_type=F32)
    return acc


def _mix_ffn_kernel(*refs, seq_len):
    gpre_ref, sc_ref, sh_ref, w1_ref, w3_ref, w2_ref, npost_ref, g2_ref, o_ref, band_ref = refs[N_MIX_INPUTS:]

    @pl.when(pl.program_id(1) == 0)
    def _():
        _init_pool_bands(band_ref)

    x = _mix_tile(*refs[:N_MIX_INPUTS], band_ref, seq_len)
    h = (_rms(x, gpre_ref[...]) * (1.0 + sc_ref[...]) + sh_ref[...]).astype(BF16)
    y = _swiglu_tile(h, w1_ref, w3_ref, w2_ref)
    o_ref[...] = x + g2_ref[...] * _rms(y, npost_ref[...])


def _mix_ffn(mix, gpre, npost, w1, w3, w2, layer, ctx):
    x, modr = mix[0], mix[-1]
    B, T, _ = x.shape
    tm = min(TOKEN_TILE, T)
    args, specs = _mix_inputs(mix, tm, layer, ctx)
    wspec = lambda shape: pl.BlockSpec(shape, lambda b, i: (0, 0), pipeline_mode=pl.Buffered(1))
    return pl.pallas_call(
        functools.partial(_mix_ffn_kernel, seq_len=T),
        out_shape=jax.ShapeDtypeStruct((B, T, D_MODEL), F32),
        grid=(B, T // tm),
        in_specs=specs + [_row_spec(D_MODEL), _mod_spec(layer, 4, ctx), _mod_spec(layer, 3, ctx),
                          wspec((D_MODEL, D_FF)), wspec((D_MODEL, D_FF)), wspec((D_FF, D_MODEL)),
                          _row_spec(D_MODEL), _mod_spec(layer, 5, ctx)],
        out_specs=pl.BlockSpec((None, tm, D_MODEL), lambda b, i: (b, i, 0)),
        scratch_shapes=[POOL_BANDS],
        compiler_params=_cparams(("parallel", "arbitrary"), VMEM_LIMIT),
        name="mix_ffn",
    )(*args, gpre.reshape(1, D_MODEL), modr, modr, w1, w3, w2, npost.reshape(1, D_MODEL), modr)


def _mix_route_kernel(*refs, seq_len):
    (gpre_ref, sc_ref, sh_ref, rt_ref, x_ref, h_ref, idx_ref, gate_ref, rank_ref, cnt_ref,
     run_ref, upper_ref, band_ref) = refs[N_MIX_INPUTS:]
    tm = x_ref.shape[0]
    first = (pl.program_id(0) == 0) & (pl.program_id(1) == 0)

    @pl.when(first)
    def _():
        run_ref[...] = jnp.zeros_like(run_ref)
        earlier = lax.broadcasted_iota(I32, (tm, tm), 0) < lax.broadcasted_iota(I32, (tm, tm), 1)
        upper_ref[...] = jnp.where(earlier, 1.0, 0.0).astype(BF16)
        _init_pool_bands(band_ref)

    x = _mix_tile(*refs[:N_MIX_INPUTS], band_ref, seq_len)
    x_ref[...] = x
    h = _rms(x, gpre_ref[...]) * (1.0 + sc_ref[...]) + sh_ref[...]
    h_ref[...] = h
    def split(v):
        hi = v.astype(BF16)
        return hi, (v - hi.astype(F32)).astype(BF16)

    nt_dot = lambda a, b: lax.dot_general(a, b, (((1,), (1,)), ((), ())), preferred_element_type=F32)
    r_hi, r_lo = split(rt_ref[...])
    h_hi, h_lo = split(h)
    by_h_hi = nt_dot(jnp.concatenate([r_hi, r_lo], axis=0), h_hi)
    logits = by_h_hi[0:N_EXPERTS] + by_h_hi[N_EXPERTS:] + nt_dot(r_hi, h_lo)
    eid = lax.broadcasted_iota(I32, (N_EXPERTS, tm), 0).astype(F32)
    m1 = jnp.max(logits, axis=0, keepdims=True)
    i1 = jnp.min(jnp.where(logits == m1, eid, float(N_EXPERTS)), axis=0, keepdims=True)
    oh1 = eid == i1
    rest = jnp.where(oh1, -jnp.inf, logits)
    m2 = jnp.max(rest, axis=0, keepdims=True)
    i2 = jnp.min(jnp.where(rest == m2, eid, float(N_EXPERTS)), axis=0, keepdims=True)
    oh2 = eid == i2
    e2 = jnp.exp(m2 - m1)
    gate_ref[0:1, :] = 1.0 / (1.0 + e2)
    gate_ref[1:2, :] = e2 / (1.0 + e2)
    idx_ref[0:1, :] = i1.astype(I32)
    idx_ref[1:2, :] = i2.astype(I32)
    upper = upper_ref[...]
    f1 = jnp.where(oh1, 1.0, 0.0)
    f2 = jnp.where(oh2, 1.0, 0.0)
    before1 = jnp.dot(f1.astype(BF16), upper, preferred_element_type=F32)
    before2 = jnp.dot(f2.astype(BF16), upper, preferred_element_type=F32)
    cnt1 = jnp.sum(f1, axis=1, keepdims=True)
    cnt2 = jnp.sum(f2, axis=1, keepdims=True)
    run = run_ref[:, 0:1]
    rank_ref[0:1, :] = jnp.sum(f1 * (run + before1), axis=0, keepdims=True).astype(I32)
    rank_ref[1:2, :] = jnp.sum(f2 * (run + cnt1 + before2), axis=0, keepdims=True).astype(I32)
    run_new = run_ref[...] + cnt1 + cnt2
    run_ref[...] = run_new
    cnt_ref[...] = run_new


def _mix_route(mix, gpre, router_t, layer):
    x, modr = mix[0], mix[-1]
    B, T, _ = x.shape
    tm = min(TOKEN_TILE, T)
    nt = T // tm
    args, specs = _mix_inputs(mix, tm, layer, False)
    tok = pl.BlockSpec((None, tm, D_MODEL), lambda b, i: (b, i, 0))
    lane = pl.BlockSpec((2, tm), lambda b, i: (0, b * nt + i))
    return pl.pallas_call(
        functools.partial(_mix_route_kernel, seq_len=T),
        out_shape=(jax.ShapeDtypeStruct((B, T, D_MODEL), F32),
                   jax.ShapeDtypeStruct((B, T, D_MODEL), F32),
                   jax.ShapeDtypeStruct((2, B * T), I32),
                   jax.ShapeDtypeStruct((2, B * T), F32),
                   jax.ShapeDtypeStruct((2, B * T), I32),
                   jax.ShapeDtypeStruct((N_EXPERTS, 128), F32)),
        grid=(B, nt),
        in_specs=specs + [_row_spec(D_MODEL), _mod_spec(layer, 4, False), _mod_spec(layer, 3, False),
                          pl.BlockSpec((N_EXPERTS, D_MODEL), lambda b, i: (0, 0))],
        out_specs=(tok, tok, lane, lane, lane,
                   pl.BlockSpec((N_EXPERTS, 128), lambda b, i: (0, 0))),
        scratch_shapes=[pltpu.VMEM((N_EXPERTS, 128), F32), pltpu.VMEM((tm, tm), BF16), POOL_BANDS],
        compiler_params=_cparams(("arbitrary", "arbitrary"), VMEM_LIMIT),
        name="mix_route",
    )(*args, gpre.reshape(1, D_MODEL), modr, modr, router_t)


def _dispatch_kernel(zero_tiles_ref, slot_ref, h_ref, hs_ref, zero_ref, sem, zero_sem):
    tm = h_ref.shape[0]

    @pl.when(pl.program_id(0) == 0)
    def _():
        zero_ref[...] = jnp.zeros_like(zero_ref)
        for j in range(2 * N_EXPERTS):
            start = pl.multiple_of(zero_tiles_ref[j] * EXPERT_TILE, EXPERT_TILE)
            fill = pltpu.make_async_copy(zero_ref, hs_ref.at[pl.ds(start, EXPERT_TILE)], zero_sem)
            fill.start()
            fill.wait()

    def body(i, carry):
        for u in range(DMA_UNROLL):
            r = i * DMA_UNROLL + u
            for k in range(2):
                s = slot_ref[0, k, r]
                pltpu.make_async_copy(h_ref.at[pl.ds(r, 1)], hs_ref.at[pl.ds(s, 1)], sem).start(priority=k)
        return carry

    lax.fori_loop(0, tm // DMA_UNROLL, body, 0)
    for _ in range(2):
        pltpu.make_async_copy(h_ref, hs_ref.at[pl.ds(0, tm)], sem).wait()


def _dispatch(h, slot_tiles, zero_tiles, n_slots):
    N = h.shape[0]
    tm = slot_tiles.shape[2]
    grid_spec = pltpu.PrefetchScalarGridSpec(
        num_scalar_prefetch=1,
        grid=(N // tm,),
        in_specs=[pl.BlockSpec((1, 2, tm), lambda i, zt: (i, 0, 0), memory_space=pltpu.SMEM),
                  pl.BlockSpec((tm, D_MODEL), lambda i, zt: (i, 0))],
        out_specs=pl.BlockSpec(memory_space=pl.ANY),
        scratch_shapes=[pltpu.VMEM((EXPERT_TILE, D_MODEL), F32),
                        pltpu.SemaphoreType.DMA(()), pltpu.SemaphoreType.DMA(())],
    )
    return pl.pallas_call(
        _dispatch_kernel,
        out_shape=jax.ShapeDtypeStruct((n_slots, D_MODEL), F32),
        grid_spec=grid_spec,
        compiler_params=_cparams(("arbitrary",)),
        name="moe_dispatch",
    )(zero_tiles, slot_tiles, h)


def _expert_kernel(te_ref, tv_ref, h_ref, w1_ref, w3_ref, w2_ref, y_ref):
    t = pl.program_id(0)

    @pl.when(tv_ref[t] == 1)
    def _():
        y_ref[...] = _swiglu_tile(h_ref[...].astype(BF16), w1_ref, w3_ref, w2_ref)

    @pl.when(tv_ref[t] == 0)
    def _():
        y_ref[...] = jnp.zeros_like(y_ref)


def _experts(hs, tile_expert, tile_valid, w1, w3, w2):
    n_slots = hs.shape[0]
    tm = EXPERT_TILE
    grid_spec = pltpu.PrefetchScalarGridSpec(
        num_scalar_prefetch=2,
        grid=(n_slots // tm,),
        in_specs=[pl.BlockSpec((tm, D_MODEL), lambda t, te, tv: (t, 0)),
                  pl.BlockSpec((None, D_MODEL, D_FF), lambda t, te, tv: (te[t], 0, 0)),
                  pl.BlockSpec((None, D_MODEL, D_FF), lambda t, te, tv: (te[t], 0, 0)),
                  pl.BlockSpec((None, D_FF, D_MODEL), lambda t, te, tv: (te[t], 0, 0))],
        out_specs=pl.BlockSpec((tm, D_MODEL), lambda t, te, tv: (t, 0)),
    )
    return pl.pallas_call(
        _expert_kernel,
        out_shape=jax.ShapeDtypeStruct((n_slots, D_MODEL), F32),
        grid_spec=grid_spec,
        compiler_params=_cparams(("arbitrary",), VMEM_LIMIT),
        name="moe_experts",
    )(tile_expert, tile_valid, hs, w1, w3, w2)


def _combine_kernel(slot_ref, gate_ref, x_ref, npost_ref, g2_ref, ys_ref, o_ref, buf0, buf1, sem):
    tm = x_ref.shape[0]
    bufs = (buf0, buf1)

    def body(i, carry):
        for u in range(DMA_UNROLL):
            r = i * DMA_UNROLL + u
            for k in range(2):
                s = slot_ref[0, k, r]
                pltpu.make_async_copy(ys_ref.at[pl.ds(s, 1)], bufs[k].at[pl.ds(r, 1)], sem).start(priority=k)
        return carry

    lax.fori_loop(0, tm // DMA_UNROLL, body, 0)
    for k in range(2):
        pltpu.make_async_copy(ys_ref.at[pl.ds(0, tm)], bufs[k], sem).wait()
    y = gate_ref[:, 0:1] * buf0[...] + gate_ref[:, 1:2] * buf1[...]
    o_ref[...] = x_ref[...] + g2_ref[...] * _rms(y, npost_ref[...])


def _combine(x, ys, slot_tiles, gate_tok, npost, modr, layer):
    B, T, _ = x.shape
    tm = slot_tiles.shape[2]
    nt = T // tm
    tok = pl.BlockSpec((None, tm, D_MODEL), lambda b, i: (b, i, 0))
    return pl.pallas_call(
        _combine_kernel,
        out_shape=jax.ShapeDtypeStruct((B, T, D_MODEL), F32),
        grid=(B, nt),
        in_specs=[pl.BlockSpec((1, 2, tm), lambda b, i: (b * nt + i, 0, 0), memory_space=pltpu.SMEM),
                  pl.BlockSpec((tm, 2), lambda b, i: (b * nt + i, 0)),
                  tok, _row_spec(D_MODEL), _mod_spec(layer, 5, False),
                  pl.BlockSpec(memory_space=pl.ANY)],
        out_specs=tok,
        scratch_shapes=[pltpu.VMEM((tm, D_MODEL), F32), pltpu.VMEM((tm, D_MODEL), F32),
                        pltpu.SemaphoreType.DMA(())],
        compiler_params=_cparams(("arbitrary", "arbitrary")),
        name="moe_combine",
    )(slot_tiles, gate_tok, x, npost.reshape(1, D_MODEL), modr, ys)


def _mix_moe_ffn(mix, gpre, npost, router, w1, w3, w2, layer):
    modr = mix[-1]
    B, T, _ = mix[0].shape
    N = B * T
    tm = min(TOKEN_TILE, T)
    x, h, idx, gate, rank, cnt = _mix_route(mix, gpre, router.T, layer)
    n_slots = 2 * N + N_EXPERTS * EXPERT_TILE
    n_tiles = n_slots // EXPERT_TILE
    counts = cnt[:, 0].astype(I32)
    padded = ((counts + EXPERT_TILE - 1) // EXPERT_TILE) * EXPERT_TILE
    ends = jnp.cumsum(padded)
    starts = ends - padded
    slot = rank
    for e in range(N_EXPERTS):
        slot = slot + jnp.where(idx == e, starts[e], 0)
    slot_tiles = slot.reshape(2, N // tm, tm).transpose(1, 0, 2)
    tile_start = jnp.arange(n_tiles, dtype=I32) * EXPERT_TILE
    tile_valid = (tile_start < ends[-1]).astype(I32)
    tile_expert = jnp.minimum(jnp.sum((tile_start[:, None] >= ends[None, :]).astype(I32), axis=1), N_EXPERTS - 1)
    last_expert = jnp.max(jnp.where(tile_valid == 1, tile_expert, 0))
    tile_expert = jnp.where(tile_valid == 1, tile_expert, last_expert)
    last_tile = jnp.where(padded > 0, ends // EXPERT_TILE - 1, n_tiles - 1)
    tail_tile = jnp.minimum(ends[-1] // EXPERT_TILE + jnp.arange(N_EXPERTS, dtype=I32), n_tiles - 1)
    zero_tiles = jnp.concatenate([last_tile, tail_tile]).astype(I32)
    hs = _dispatch(h.reshape(N, D_MODEL), slot_tiles, zero_tiles, n_slots)
    ys = _experts(hs, tile_expert, tile_valid, w1, w3, w2)
    return _combine(x, ys, slot_tiles, gate.T, npost, modr, layer)


def kernel(x, c, ctx, c_ctx, w_mod, b_mod, norm_pre_mix, norm_post_mix, norm_pre_ffn, norm_post_ffn, w_in, w_out, q_norm, k_norm, ret_decay_logit, pool_w, pool_scale, ffn_w1, ffn_w3, ffn_w2, moe_router, moe_w1, moe_w3, moe_w2):
    B, T, _ = x.shape
    cvec = jnp.zeros((MOD_ROWS, D_MODEL), F32).at[0:B].set(c).at[2].set(c_ctx)
    modr = _modulation(cvec, w_mod, b_mod)
    cos_t, sin_t = _rope_tables(T)
    avg = jnp.kron(jnp.eye(RET_HEADS, dtype=F32), jnp.full((HEAD_DIM, HEAD_DIM), 1.0 / HEAD_DIM, F32)).astype(BF16)
    zero_state = jnp.zeros((B, RET_WIDTH, RET_WIDTH), F32)
    xc = ctx
    for i in range(DEPTH):
        need_ctx = i < DEPTH - 1
        w_in_b = w_in[i].astype(BF16)
        w_out_b = w_out[i].astype(BF16)
        pool_bd = jax.scipy.linalg.block_diag(*[pool_w[i, g] for g in range(len(POOL_WINDOWS))]).astype(BF16)
        log_gamma = jax.nn.log_sigmoid(ret_decay_logit[i].astype(F32))

        qkv_c, ret_c, pp_c = _in_projection(xc, modr, norm_pre_mix[i], w_in_b, i, True)
        qt_c, kn_c, vt_c, qn_c, st_c = _attn_prep(qkv_c, q_norm[i], k_norm[i], cos_t, sin_t, False)
        of_c, ob_c, s_fwd, s_bwd = _retention(ret_c, log_gamma, zero_state, zero_state)

        qkv_x, ret_x, pp_x = _in_projection(x, modr, norm_pre_mix[i], w_in_b, i, False)
        qt_x, kn_x, vt_x, qn_x, st_x = _attn_prep(qkv_x, q_norm[i], k_norm[i], cos_t, sin_t, True)
        keys_x = [(kn_x, vt_x), (kn_c, vt_c)]
        attn_x = _attention(qt_x, _softmax_shift(qn_x, st_x, [st_x, st_c], qt_x, keys_x), keys_x)
        of_x, ob_x, _, _ = _retention(ret_x, log_gamma, s_fwd, s_bwd)
        mix_x = (x, attn_x, of_x, ob_x, ret_x, pp_x, pool_bd, pool_scale[i], avg, w_out_b, norm_post_mix[i], modr)

        j = i // 2
        if i % 2 == 0:
            w1, w3, w2 = ffn_w1[j].astype(BF16), ffn_w3[j].astype(BF16), ffn_w2[j].astype(BF16)
            x = _mix_ffn(mix_x, norm_pre_ffn[i], norm_post_ffn[i], w1, w3, w2, i, False)
            if need_ctx:
                keys_c = [(kn_c, vt_c)]
                attn_c = _attention(qt_c, _softmax_shift(qn_c, st_c, [st_c], qt_c, keys_c), keys_c)
                mix_c = (xc, attn_c, of_c, ob_c, ret_c, pp_c, pool_bd, pool_scale[i], avg, w_out_b,
                         norm_post_mix[i], modr)
                xc = _mix_ffn(mix_c, norm_pre_ffn[i], norm_post_ffn[i], w1, w3, w2, i, True)
        else:
            assert not need_ctx
            x = _mix_moe_ffn(mix_x, norm_pre_ffn[i], norm_post_ffn[i], moe_router[j], moe_w1[j].astype(BF16),
                             moe_w3[j].astype(BF16), moe_w2[j].astype(BF16), i)
    return x
```

```python
import functools

import jax
import jax.numpy as jnp
from jax import lax
from jax.experimental import pallas as pl
from jax.experimental.pallas import tpu as pltpu

F32 = jnp.float32
BF16 = jnp.bfloat16
I32 = jnp.int32

D_MODEL = 1024
GRID_W = 64
HEAD_DIM = 64
ATTN_WIDTH = 512
KV_HEADS = 2
ATTN_GROUP = 4
KV_WIDTH = 128
RET_WIDTH = 256
RET_HEADS = 4
POOL_WIDTH = 256
POOL_WINDOWS = (2, 4, 8, 16)
IN_WIDTH = 2048
RET_CHUNK = 128
ROPE_THETA = 10000.0
D_FF = 2816
N_EXPERTS = 8
NORM_EPS = 1e-6
DEPTH = 2

TOKEN_TILE = 1024
ATTN_Q_TILE = 512
ATTN_K_CHUNK = 2048
PREP_TILE = 2048
RET_TILE = 512
POOL_TILE = 256
POOL_HALO = 16
FF_CHUNK = 256
EXPERT_TILE = 512
DMA_UNROLL = 8
BF16_SUBLANES = 16
MXU_DIM = 256
K_COLS = MXU_DIM
KEY_BLOCK = MXU_DIM
ATTN_UNROLL = 9
N_MXU = 2
SCORE_REG, VALUE_REG = 1, 0
SCORE_BASE = 64
VT_ROWS = HEAD_DIM + BF16_SUBLANES
MAX_BOUND_SHIFT = 40.0
SHIFT_MARGIN = 1.01
MOD_ROWS = 8
VMEM_LIMIT = 56 * 1024 * 1024


def _cparams(sem, vmem=None):
    return pltpu.CompilerParams(dimension_semantics=sem, vmem_limit_bytes=vmem)


def _rms(x, gain):
    ms = jnp.mean(x * x, axis=-1, keepdims=True)
    return x * lax.rsqrt(ms + NORM_EPS) * gain


def _silu(x):
    return x * jax.nn.sigmoid(x)


def _mod_kernel(c_ref, w_ref, b_ref, o_ref):
    s = _silu(c_ref[...])
    o_ref[...] = jnp.dot(s, w_ref[...], precision=lax.Precision.HIGHEST,
                         preferred_element_type=F32) + b_ref[...]


def _modulation(cvec, w_mod, b_mod):
    nchunk = 6
    out = pl.pallas_call(
        _mod_kernel,
        out_shape=jax.ShapeDtypeStruct((DEPTH, MOD_ROWS, 6 * D_MODEL), F32),
        grid=(DEPTH, nchunk),
        in_specs=[
            pl.BlockSpec((MOD_ROWS, D_MODEL), lambda l, j: (0, 0)),
            pl.BlockSpec((None, D_MODEL, D_MODEL), lambda l, j: (l, 0, j)),
            pl.BlockSpec((None, 1, D_MODEL), lambda l, j: (l, 0, j)),
        ],
        out_specs=pl.BlockSpec((None, MOD_ROWS, D_MODEL), lambda l, j: (l, 0, j)),
        compiler_params=_cparams(("parallel", "parallel")),
        name="modulation",
    )(cvec, w_mod, b_mod.reshape(DEPTH, 1, 6 * D_MODEL))
    return out.reshape(DEPTH * MOD_ROWS * nchunk, 1, D_MODEL)


def _mod_spec(layer, chunk, ctx):
    base = layer * MOD_ROWS * 6
    if ctx:
        return pl.BlockSpec((None, 1, D_MODEL), lambda b, i: (base + 2 * 6 + chunk, 0, 0))
    return pl.BlockSpec((None, 1, D_MODEL), lambda b, i: (base + b * 6 + chunk, 0, 0))


def _row_spec(width):
    return pl.BlockSpec((1, width), lambda b, i: (0, 0))


def _inproj_kernel(x_ref, g_ref, sc_ref, sh_ref, w_ref, qkv_ref, ret_ref, pp_ref):
    h = _rms(x_ref[...], g_ref[...]) * (1.0 + sc_ref[...]) + sh_ref[...]
    p = jnp.dot(h.astype(BF16), w_ref[...], preferred_element_type=F32)
    qkv_ref[...] = p[:, :768].astype(BF16)
    ret_ref[...] = p[:, 768:1792].astype(BF16)
    pp_ref[...] = p[:, 1792:].astype(BF16)


def _in_projection(x, modr, gain, w_in_bf16, layer, ctx):
    B, T, _ = x.shape
    tm = min(TOKEN_TILE, T)
    tok = lambda w: pl.BlockSpec((None, tm, w), lambda b, i: (b, i, 0))
    return pl.pallas_call(
        _inproj_kernel,
        out_shape=(jax.ShapeDtypeStruct((B, T, 768), BF16),
                   jax.ShapeDtypeStruct((B, T, 1024), BF16),
                   jax.ShapeDtypeStruct((B, T, POOL_WIDTH), BF16)),
        grid=(B, T // tm),
        in_specs=[tok(D_MODEL), _row_spec(D_MODEL), _mod_spec(layer, 1, ctx), _mod_spec(layer, 0, ctx),
                  pl.BlockSpec((D_MODEL, IN_WIDTH), lambda b, i: (0, 0))],
        out_specs=(tok(768), tok(1024), tok(POOL_WIDTH)),
        compiler_params=_cparams(("parallel", "parallel"), VMEM_LIMIT),
        name="in_projection",
    )(x, gain.reshape(1, D_MODEL), modr, modr, w_in_bf16)


def _prep_kernel(qkv_ref, qg_ref, kg_ref, cos_ref, sin_ref, qt_ref, kn_ref, vt_ref, qn_ref, stat_ref, *, rope):
    t = qkv_ref[...].astype(F32).T

    def norm_rope(blk, gain):
        ms = jnp.mean(blk * blk, axis=0, keepdims=True)
        y = blk * lax.rsqrt(ms + NORM_EPS) * gain
        if rope:
            partner = jnp.concatenate([y[16:32], y[0:16], y[48:64], y[32:48]], axis=0)
            y = y * cos_ref[...] + partner * sin_ref[...]
        return y

    def sq_norm(y):
        return jnp.sum(y * y, axis=0, keepdims=True)

    def row_max(n2):
        return jnp.broadcast_to(jnp.max(n2, axis=1, keepdims=True), (1, 128))

    q_stats = []
    for h in range(ATTN_WIDTH // HEAD_DIM):
        lo = h * HEAD_DIM
        q = norm_rope(t[lo:lo + HEAD_DIM], qg_ref[...]) * (HEAD_DIM ** -0.5)
        qt_ref[lo:lo + HEAD_DIM, :] = q.astype(BF16)
        n2 = sq_norm(q)
        qn_ref[h // ATTN_GROUP, h % ATTN_GROUP:h % ATTN_GROUP + 1, :] = jnp.sqrt(n2)
        q_stats.append(row_max(n2))
    tm = t.shape[1]
    k_pad = jnp.where(lax.broadcasted_iota(I32, (K_COLS - HEAD_DIM, tm), 0) < 2, 1.0, 0.0)
    v_pad = jnp.where(lax.broadcasted_iota(I32, (VT_ROWS - HEAD_DIM, tm), 0) < 1, 1.0, 0.0)
    k_stats = []
    for kv in range(KV_HEADS):
        lo = ATTN_WIDTH + kv * HEAD_DIM
        k = norm_rope(t[lo:lo + HEAD_DIM], kg_ref[...])
        k_stats.append(row_max(sq_norm(k)))
        kn_ref[kv] = jnp.concatenate([k, k_pad], axis=0).T.astype(BF16)
        lo = ATTN_WIDTH + KV_WIDTH + kv * HEAD_DIM
        vt = jnp.concatenate([t[lo:lo + HEAD_DIM], v_pad], axis=0).astype(BF16)
        for j in range(tm // KEY_BLOCK):
            vt_ref[kv, j] = vt[:, j * KEY_BLOCK:(j + 1) * KEY_BLOCK]
    q_group = [functools.reduce(jnp.maximum, q_stats[kv * ATTN_GROUP:(kv + 1) * ATTN_GROUP])
               for kv in range(KV_HEADS)]
    stat_ref[...] = jnp.concatenate(k_stats + q_group + [jnp.zeros((8 - 2 * KV_HEADS, 128), F32)], axis=0)


def _attn_prep(qkv, q_gain, k_gain, cos_t, sin_t, rope):
    B, T, _ = qkv.shape
    tm = min(PREP_TILE, T)
    nc = T // tm
    return pl.pallas_call(
        functools.partial(_prep_kernel, rope=rope),
        out_shape=(jax.ShapeDtypeStruct((B, ATTN_WIDTH, T), BF16),
                   jax.ShapeDtypeStruct((B, KV_HEADS, T, K_COLS), BF16),
                   jax.ShapeDtypeStruct((B, KV_HEADS, T // KEY_BLOCK, VT_ROWS, KEY_BLOCK), BF16),
                   jax.ShapeDtypeStruct((B, KV_HEADS, ATTN_GROUP, T), F32),
                   jax.ShapeDtypeStruct((B, nc, 8, 128), F32)),
        grid=(B, nc),
        in_specs=[pl.BlockSpec((None, tm, 768), lambda b, i: (b, i, 0)),
                  pl.BlockSpec((HEAD_DIM, 1), lambda b, i: (0, 0)),
                  pl.BlockSpec((HEAD_DIM, 1), lambda b, i: (0, 0)),
                  pl.BlockSpec((HEAD_DIM, tm), lambda b, i: (0, i)),
                  pl.BlockSpec((HEAD_DIM, tm), lambda b, i: (0, i))],
        out_specs=(pl.BlockSpec((None, ATTN_WIDTH, tm), lambda b, i: (b, 0, i)),
                   pl.BlockSpec((None, KV_HEADS, tm, K_COLS), lambda b, i: (b, 0, i, 0)),
                   pl.BlockSpec((None, KV_HEADS, tm // KEY_BLOCK, VT_ROWS, KEY_BLOCK), lambda b, i: (b, 0, i, 0, 0)),
                   pl.BlockSpec((None, KV_HEADS, ATTN_GROUP, tm), lambda b, i: (b, 0, 0, i)),
                   pl.BlockSpec((None, None, 8, 128), lambda b, i: (b, i, 0, 0))),
        compiler_params=_cparams(("parallel", "parallel")),
        name="attn_prep",
    )(qkv, q_gain.reshape(HEAD_DIM, 1), k_gain.reshape(HEAD_DIM, 1), cos_t, sin_t)


def _rope_tables(T):
    t = jnp.arange(T)
    row = (t // GRID_W).astype(F32)
    col = (t % GRID_W).astype(F32)
    n_freq = HEAD_DIM // 4
    inv = ROPE_THETA ** (-jnp.arange(n_freq, dtype=F32) / n_freq)
    ang_r = row[None, :] * inv[:, None]
    ang_c = col[None, :] * inv[:, None]
    cos_t = jnp.concatenate([jnp.cos(ang_r), jnp.cos(ang_r), jnp.cos(ang_c), jnp.cos(ang_c)], axis=0)
    sin_t = jnp.concatenate([-jnp.sin(ang_r), jnp.sin(ang_r), -jnp.sin(ang_c), jnp.sin(ang_c)], axis=0)
    return cos_t, sin_t


def _score_max_kernel(qt_ref, k_ref, m_ref, qa_ref):
    tq = qt_ref.shape[1]
    nq = ATTN_GROUP * tq
    for g in range(ATTN_GROUP):
        qa_ref[0:HEAD_DIM, g * tq:(g + 1) * tq] = qt_ref[g * HEAD_DIM:(g + 1) * HEAD_DIM, :]
    qa_ref[HEAD_DIM:K_COLS, :] = jnp.zeros((K_COLS - HEAD_DIM, nq), BF16)
    tk = min(ATTN_K_CHUNK, k_ref.shape[0])

    def colmax8(j):
        k = k_ref[pl.ds(pl.multiple_of(j * tk, tk), tk), :]
        s = jnp.dot(k, qa_ref[...], preferred_element_type=F32)
        return jnp.max(s.reshape(tk // 8, 8, nq), axis=0)

    mx = lax.fori_loop(1, k_ref.shape[0] // tk, lambda j, mx: jnp.maximum(mx, colmax8(j)), colmax8(0))
    m = jnp.max(mx, axis=0, keepdims=True)
    for g in range(ATTN_GROUP):
        m_ref[g:g + 1, :] = m[:, g * tq:(g + 1) * tq]


def _score_max(qt, k_all):
    B, _, Tq = qt.shape
    Tk = k_all.shape[2]
    tq = min(ATTN_Q_TILE, Tq)
    return pl.pallas_call(
        _score_max_kernel,
        out_shape=jax.ShapeDtypeStruct((B, KV_HEADS, ATTN_GROUP, Tq), F32),
        grid=(B, KV_HEADS, Tq // tq),
        in_specs=[pl.BlockSpec((None, ATTN_GROUP * HEAD_DIM, tq), lambda b, h, i: (b, h, i)),
                  pl.BlockSpec((None, None, Tk, K_COLS), lambda b, h, i: (b, h, 0, 0))],
        out_specs=pl.BlockSpec((None, None, ATTN_GROUP, tq), lambda b, h, i: (b, h, 0, i)),
        scratch_shapes=[pltpu.VMEM((K_COLS, ATTN_GROUP * tq), BF16)],
        compiler_params=_cparams(("parallel", "parallel", "parallel"), VMEM_LIMIT),
        name="score_max",
    )(qt, k_all)


def _softmax_shift(q_norm, q_stats, k_stats_list, qt, key_sets):
    B = qt.shape[0]
    k2 = functools.reduce(jnp.maximum, [st[:, :, 0:KV_HEADS, 0].max(axis=1) for st in k_stats_list])
    q2 = q_stats[:, :, KV_HEADS:2 * KV_HEADS, 0].max(axis=1)
    use_bound = (jnp.sqrt(k2 * q2) * SHIFT_MARGIN <= MAX_BOUND_SHIFT).reshape(B, KV_HEADS, 1, 1)
    bound = q_norm * (jnp.sqrt(k2) * SHIFT_MARGIN).reshape(B, KV_HEADS, 1, 1)
    exact = lambda: _score_max(qt, jnp.concatenate([k for k, _ in key_sets], axis=2))
    return lax.cond(jnp.all(use_bound), lambda: bound, lambda: jnp.where(use_bound, bound, exact()))


def _attn_kernel(qt_ref, shift_ref, *refs):
    *kv_refs, o_ref, qa_ref = refs
    sources = [(kv_refs[i], kv_refs[i + 1]) for i in range(0, len(kv_refs), 2)]
    counts = [k_ref.shape[0] // KEY_BLOCK for k_ref, _ in sources]
    tq = qt_ref.shape[1]
    n_blocks = sum(counts)
    heads_per_mxu = ATTN_GROUP // N_MXU

    def locate(blk):
        if not isinstance(blk, int):
            return 0, blk
        src = 0
        while blk >= counts[src]:
            blk -= counts[src]
            src += 1
        return src, blk

    row = lax.broadcasted_iota(I32, (BF16_SUBLANES, tq), 0)
    for g in range(ATTN_GROUP):
        m = shift_ref[g:g + 1, :]
        m_hi = m.astype(BF16).astype(F32)
        qa_ref[g, 0:HEAD_DIM, :] = qt_ref[g * HEAD_DIM:(g + 1) * HEAD_DIM, :]
        qa_ref[g, HEAD_DIM:HEAD_DIM + BF16_SUBLANES, :] = jnp.where(
            row == 0, -m_hi, jnp.where(row == 1, m_hi - m, 0.0)).astype(BF16)
        qa_ref[g, HEAD_DIM + BF16_SUBLANES:K_COLS, :] = jnp.zeros((K_COLS - HEAD_DIM - BF16_SUBLANES, tq), BF16)

    def score_addr(slot):
        return SCORE_BASE + slot * (KEY_BLOCK // 4)

    def out_addr(slot):
        return slot * (VT_ROWS // 4)

    def keys(blk):
        src, j = locate(blk)
        return sources[src][0][pl.ds(pl.multiple_of(j * KEY_BLOCK, KEY_BLOCK), KEY_BLOCK), :]

    def stage_q(slot):
        for mxu in range(N_MXU):
            pltpu.matmul_push_rhs(qa_ref[mxu * heads_per_mxu + slot], SCORE_REG, mxu)

    def issue_scores(blk, slot):
        k = keys(blk)
        for mxu in range(N_MXU):
            pltpu.matmul_acc_lhs(score_addr(slot), k, mxu, load_staged_rhs=SCORE_REG)

    def pop_probs(slot):
        return [jnp.exp(pltpu.matmul_pop(score_addr(slot), (KEY_BLOCK, tq), F32, mxu)).astype(BF16)
                for mxu in range(N_MXU)]

    def push_probs(blk, slot, p):
        src, j = locate(blk)
        vt = sources[src][1][j]
        for mxu in range(N_MXU):
            pltpu.matmul_push_rhs(p[mxu], VALUE_REG, mxu)
        return vt

    def unit(blk, slot, stage_next, first):
        if not first:
            p = pop_probs(slot)
        issue_scores(blk, slot)
        if not first:
            vt = push_probs(blk - 1, slot, p)
        if stage_next:
            stage_q((slot + 1) % heads_per_mxu)
        if not first:
            for mxu in range(N_MXU):
                pltpu.matmul_acc_lhs(out_addr(slot), vt, mxu, load_staged_rhs=VALUE_REG)

    def block(blk, last=False, first=False):
        for slot in range(heads_per_mxu):
            unit(blk, slot, stage_next=not (last and slot == heads_per_mxu - 1), first=first)

    stage_q(0)
    block(0, last=n_blocks == 1, first=True)
    if n_blocks > 1:
        n_loop = n_blocks - 2
        assert n_loop + 1 <= counts[0]
        unroll = max([u for u in range(1, ATTN_UNROLL + 1) if n_loop % u == 0] or [1])

        def body(i, carry):
            for u in range(unroll):
                block(1 + i * unroll + u)
            return carry

        if n_loop:
            lax.fori_loop(0, n_loop // unroll, body, 0)
        block(n_blocks - 1, last=True)
    for slot in range(heads_per_mxu):
        vt = push_probs(n_blocks - 1, slot, pop_probs(slot))
        for mxu in range(N_MXU):
            pltpu.matmul_acc_lhs(out_addr(slot), vt, mxu, load_staged_rhs=VALUE_REG)

    outs = []
    for mxu in range(N_MXU):
        for slot in range(heads_per_mxu):
            acc = pltpu.matmul_pop(out_addr(slot), (VT_ROWS, tq), F32, mxu)
            outs.append(acc[0:HEAD_DIM, :] / acc[HEAD_DIM:HEAD_DIM + 1, :])
    o_ref[...] = jnp.concatenate(outs, axis=0).T.astype(BF16)


def _attention(qt, shift, key_sets):
    B, _, Tq = qt.shape
    tq = MXU_DIM
    in_specs = [pl.BlockSpec((None, ATTN_GROUP * HEAD_DIM, tq), lambda b, h, i: (b, h, i)),
                pl.BlockSpec((None, None, ATTN_GROUP, tq), lambda b, h, i: (b, h, 0, i))]
    args = [qt, shift]
    for k, vt in key_sets:
        Tk = k.shape[2]
        in_specs += [pl.BlockSpec((None, None, Tk, K_COLS), lambda b, h, i: (b, h, 0, 0)),
                     pl.BlockSpec((None, None, Tk // KEY_BLOCK, VT_ROWS, KEY_BLOCK), lambda b, h, i: (b, h, 0, 0, 0))]
        args += [k, vt]
    return pl.pallas_call(
        _attn_kernel,
        out_shape=jax.ShapeDtypeStruct((B, Tq, ATTN_WIDTH), BF16),
        grid=(B, KV_HEADS, Tq // tq),
        in_specs=in_specs,
        out_specs=pl.BlockSpec((None, tq, ATTN_GROUP * HEAD_DIM), lambda b, h, i: (b, i, h)),
        scratch_shapes=[pltpu.VMEM((ATTN_GROUP, K_COLS, tq), BF16)],
        compiler_params=_cparams(("parallel", "parallel", "parallel"), VMEM_LIMIT),
        name="attention",
    )(*args)


def _ret_kernel(lgc_f_ref, lgc_b_ref, lgr_f_ref, lgr_b_ref, s0f_ref, s0b_ref, blk_f_ref, blk_b_ref,
                of_ref, ob_ref, sf_ref, sb_ref, st_f, st_b, dec):
    C = RET_CHUNK
    W = RET_WIDTH
    n = pl.program_id(1)

    @pl.when(n == 0)
    def _():
        st_f[...] = s0f_ref[...]
        st_b[...] = s0b_ref[...]
        c = jnp.bitwise_and(lax.broadcasted_iota(I32, (RET_HEADS * C, C), 0), C - 1)
        m = lax.broadcasted_iota(I32, (RET_HEADS * C, C), 1)
        diff = (c - m).astype(F32)
        dec[...] = (jnp.where(diff >= 0, jnp.exp(lgc_f_ref[...] * jnp.maximum(diff, 0.0)), 0.0)
                    + jnp.where(diff <= 0, jnp.exp(lgc_b_ref[...] * jnp.maximum(-diff, 0.0)), 0.0))

    lane_head = jnp.right_shift(lax.broadcasted_iota(I32, (C, W), 1), 6)
    pos = lax.broadcasted_iota(I32, (C, W), 0).astype(F32)
    same_head = (jnp.right_shift(lax.broadcasted_iota(I32, (W, W), 0), 6)
                 == jnp.right_shift(lax.broadcasted_iota(I32, (W, W), 1), 6))

    def direction(blk_ref, st_ref, lgr, forward, out_ref):
        n_sub = blk_ref.shape[0] // C
        state = st_ref[...]
        for sub in (range(n_sub) if forward else reversed(range(n_sub))):
            state = chunk(blk_ref, state, lgr, forward, out_ref, sub * C)
        st_ref[...] = state

    def chunk(blk_ref, state, lgr, forward, out_ref, r0):
        q = blk_ref[r0:r0 + C, 0:W].astype(F32)
        kf = blk_ref[r0:r0 + C, W:2 * W].astype(F32) * (HEAD_DIM ** -0.5)
        v = blk_ref[r0:r0 + C, 2 * W:3 * W]
        if forward:
            zeta = jnp.exp(lgr * (C - 1.0 - pos))
            xi = jnp.exp(lgr * (pos + 1.0))
        else:
            zeta = jnp.exp(lgr * pos)
            xi = jnp.exp(lgr * (C - pos))
        out = jnp.dot((q * xi).astype(BF16), state.astype(BF16), preferred_element_type=F32)
        if forward:
            qexp = jnp.concatenate([jnp.where(lane_head == h, q, 0.0) for h in range(RET_HEADS)],
                                   axis=0).astype(BF16)
            a = lax.dot_general(qexp, kf.astype(BF16), (((1,), (1,)), ((), ())),
                                preferred_element_type=F32)
            p = (a * dec[...]).astype(BF16)
            full = jnp.dot(p, v, preferred_element_type=F32)
            for h in range(RET_HEADS):
                out = out + jnp.where(lane_head == h, full[h * C:(h + 1) * C], 0.0)
        out_ref[r0:r0 + C, :] = out
        upd = lax.dot_general((kf * zeta).astype(BF16), v, (((0,), (0,)), ((), ())),
                              preferred_element_type=F32)
        return jnp.where(same_head, state * jnp.exp(lgr * float(C)) + upd, 0.0)

    direction(blk_f_ref, st_f, lgr_f_ref[...], True, of_ref)
    direction(blk_b_ref, st_b, lgr_b_ref[...], False, ob_ref)

    @pl.when(n == pl.num_programs(1) - 1)
    def _():
        sf_ref[...] = st_f[...]
        sb_ref[...] = st_b[...]


def _retention(ret, log_gamma, s0f, s0b):
    B, T, _ = ret.shape
    C = RET_CHUNK
    tm = min(RET_TILE, T)
    nc = T // tm
    lgc = lambda d: jnp.repeat(log_gamma[d], C).reshape(RET_HEADS * C, 1)
    lgr = lambda d: jnp.repeat(log_gamma[d], HEAD_DIM).reshape(1, RET_WIDTH)
    const = lambda shape: pl.BlockSpec(shape, lambda b, n: (0,) * len(shape))
    st_spec = pl.BlockSpec((None, RET_WIDTH, RET_WIDTH), lambda b, n: (b, 0, 0))
    return pl.pallas_call(
        _ret_kernel,
        out_shape=(jax.ShapeDtypeStruct((B, T, RET_WIDTH), F32),
                   jax.ShapeDtypeStruct((B, T, RET_WIDTH), F32),
                   jax.ShapeDtypeStruct((B, RET_WIDTH, RET_WIDTH), F32),
                   jax.ShapeDtypeStruct((B, RET_WIDTH, RET_WIDTH), F32)),
        grid=(B, nc),
        in_specs=[const((RET_HEADS * C, 1)), const((RET_HEADS * C, 1)),
                  const((1, RET_WIDTH)), const((1, RET_WIDTH)), st_spec, st_spec,
                  pl.BlockSpec((None, tm, 1024), lambda b, n: (b, n, 0)),
                  pl.BlockSpec((None, tm, 1024), lambda b, n: (b, nc - 1 - n, 0))],
        out_specs=(pl.BlockSpec((None, tm, RET_WIDTH), lambda b, n: (b, n, 0)),
                   pl.BlockSpec((None, tm, RET_WIDTH), lambda b, n: (b, nc - 1 - n, 0)),
                   st_spec, st_spec),
        scratch_shapes=[pltpu.VMEM((RET_WIDTH, RET_WIDTH), F32), pltpu.VMEM((RET_WIDTH, RET_WIDTH), F32),
                        pltpu.VMEM((RET_HEADS * C, C), F32)],
        compiler_params=_cparams(("parallel", "arbitrary")),
        name="retention",
    )(lgc(0), lgc(1), lgr(0), lgr(1), s0f, s0b, ret, ret)


def _init_pool_bands(band_ref):
    tok = lax.broadcasted_iota(I32, (POOL_TILE, POOL_TILE + 2 * POOL_HALO), 0)
    src = lax.broadcasted_iota(I32, (POOL_TILE, POOL_TILE + 2 * POOL_HALO), 1) - POOL_HALO
    for gi, w in enumerate(POOL_WINDOWS):
        inside = (src >= tok - w // 2) & (src < tok + w // 2)
        band_ref[gi] = jnp.where(inside, 1.0, 0.0).astype(BF16)


def _pool_tile(prev_ref, cur_ref, next_ref, w_ref, scale_ref, band_ref, seq_len):
    tm = cur_ref.shape[0]
    i = pl.program_id(1)
    prev = jnp.where(i > 0, prev_ref[...], jnp.zeros_like(prev_ref))
    nxt = jnp.where(i < pl.num_programs(1) - 1, next_ref[...], jnp.zeros_like(next_ref))
    ext = jnp.concatenate([prev, cur_ref[...], nxt], axis=0)
    sub = min(POOL_TILE, tm)
    lane_group = jnp.right_shift(lax.broadcasted_iota(I32, (sub, POOL_WIDTH), 1), 6)
    parts = []
    for r0 in range(0, tm, sub):
        window = ext[r0:r0 + sub + 2 * POOL_HALO]
        cur = cur_ref[r0:r0 + sub, :].astype(F32)
        tcol = i * tm + r0 + lax.broadcasted_iota(I32, (sub, 1), 0)
        mixed = jnp.zeros((sub, POOL_WIDTH), F32)
        for gi, w in enumerate(POOL_WINDOWS):
            total = jnp.dot(band_ref[gi], window, preferred_element_type=F32)
            cnt = (jnp.minimum(tcol + w // 2, seq_len) - jnp.maximum(tcol - w // 2, 0)).astype(F32)
            mixed = mixed + jnp.where(lane_group == gi, total / cnt - cur, 0.0)
        y = jnp.dot(mixed.astype(BF16), w_ref[...], preferred_element_type=F32)
        parts.append((y * scale_ref[...]).astype(BF16))
    return jnp.concatenate(parts, axis=0)


def _head_mean(x, avg):
    hi = x.astype(BF16)
    lo = (x - hi.astype(F32)).astype(BF16)
    return (jnp.dot(hi, avg, preferred_element_type=F32) + jnp.dot(lo, avg, preferred_element_type=F32))


def _mix_tile(x_ref, attn_ref, of_ref, ob_ref, gate_ref, pp_prev_ref, pp_ref, pp_next_ref, pool_w_ref,
              pool_scale_ref, avg_ref, w_ref, npost_ref, g1_ref, band_ref, seq_len):
    pool = _pool_tile(pp_prev_ref, pp_ref, pp_next_ref, pool_w_ref, pool_scale_ref, band_ref, seq_len)
    o = of_ref[...] + ob_ref[...]
    avg = avg_ref[...]
    mu = _head_mean(o, avg)
    cen = o - mu
    var = _head_mean(cen * cen, avg)
    y_ret = (_silu(gate_ref[...].astype(F32)) * (cen * lax.rsqrt(var + NORM_EPS))).astype(BF16)
    mx = (jnp.dot(attn_ref[...], w_ref[0:ATTN_WIDTH, :], preferred_element_type=F32)
          + jnp.dot(y_ret, w_ref[ATTN_WIDTH:ATTN_WIDTH + RET_WIDTH, :], preferred_element_type=F32)
          + jnp.dot(pool, w_ref[ATTN_WIDTH + RET_WIDTH:, :], preferred_element_type=F32))
    return x_ref[...] + g1_ref[...] * _rms(mx, npost_ref[...])


N_MIX_INPUTS = 14
POOL_BANDS = pltpu.VMEM((len(POOL_WINDOWS), POOL_TILE, POOL_TILE + 2 * POOL_HALO), BF16)


def _mix_inputs(mix, tm, layer, ctx):
    x, attn, of, ob, ret, pp, pool_w, pool_scale, avg_bf16, w_out_bf16, npost, modr = mix
    T = x.shape[1]
    r = tm // POOL_HALO
    last = T // POOL_HALO - 1
    tok = lambda w: pl.BlockSpec((None, tm, w), lambda b, i: (b, i, 0))
    specs = [tok(D_MODEL), tok(ATTN_WIDTH), tok(RET_WIDTH), tok(RET_WIDTH),
             pl.BlockSpec((None, tm, RET_WIDTH), lambda b, i: (b, i, 3)),
             pl.BlockSpec((None, POOL_HALO, POOL_WIDTH), lambda b, i: (b, jnp.maximum(i * r - 1, 0), 0)),
             tok(POOL_WIDTH),
             pl.BlockSpec((None, POOL_HALO, POOL_WIDTH), lambda b, i: (b, jnp.minimum((i + 1) * r, last), 0)),
             pl.BlockSpec((POOL_WIDTH, POOL_WIDTH), lambda b, i: (0, 0)), _row_spec(POOL_WIDTH),
             pl.BlockSpec((RET_WIDTH, RET_WIDTH), lambda b, i: (0, 0)),
             pl.BlockSpec((D_MODEL, D_MODEL), lambda b, i: (0, 0)),
             _row_spec(D_MODEL), _mod_spec(layer, 2, ctx)]
    args = [x, attn, of, ob, ret, pp, pp, pp, pool_w, pool_scale.reshape(1, POOL_WIDTH), avg_bf16, w_out_bf16,
            npost.reshape(1, D_MODEL), modr]
    return args, specs


def _swiglu_tile(h, w1_ref, w3_ref, w2_ref):
    acc = jnp.zeros((h.shape[0], D_MODEL), F32)
    for f in range(0, D_FF, FF_CHUNK):
        a = jnp.dot(h, w1_ref[:, f:f + FF_CHUNK], preferred_element_type=F32)
        b = jnp.dot(h, w3_ref[:, f:f + FF_CHUNK], preferred_element_type=F32)
        u = (_silu(a) * b).astype(BF16)
        acc = acc + jnp.dot(u, w2_ref[f:f + FF_CHUNK, :], preferred_element_type=F32)
    return acc


def _mix_ffn_kernel(*refs, seq_len):
    gpre_ref, sc_ref, sh_ref, w1_ref, w3_ref, w2_ref, npost_ref, g2_ref, o_ref, band_ref = refs[N_MIX_INPUTS:]

    @pl.when(pl.program_id(1) == 0)
    def _():
        _init_pool_bands(band_ref)

    x = _mix_tile(*refs[:N_MIX_INPUTS], band_ref, seq_len)
    h = (_rms(x, gpre_ref[...]) * (1.0 + sc_ref[...]) + sh_ref[...]).astype(BF16)
    y = _swiglu_tile(h, w1_ref, w3_ref, w2_ref)
    o_ref[...] = x + g2_ref[...] * _rms(y, npost_ref[...])


def _mix_ffn(mix, gpre, npost, w1, w3, w2, layer, ctx):
    x, modr = mix[0], mix[-1]
    B, T, _ = x.shape
    tm = min(TOKEN_TILE, T)
    args, specs = _mix_inputs(mix, tm, layer, ctx)
    wspec = lambda shape: pl.BlockSpec(shape, lambda b, i: (0, 0), pipeline_mode=pl.Buffered(1))
    return pl.pallas_call(
        functools.partial(_mix_ffn_kernel, seq_len=T),
        out_shape=jax.ShapeDtypeStruct((B, T, D_MODEL), F32),
        grid=(B, T // tm),
        in_specs=specs + [_row_spec(D_MODEL), _mod_spec(layer, 4, ctx), _mod_spec(layer, 3, ctx),
                          wspec((D_MODEL, D_FF)), wspec((D_MODEL, D_FF)), wspec((D_FF, D_MODEL)),
                          _row_spec(D_MODEL), _mod_spec(layer, 5, ctx)],
        out_specs=pl.BlockSpec((None, tm, D_MODEL), lambda b, i: (b, i, 0)),
        scratch_shapes=[POOL_BANDS],
        compiler_params=_cparams(("parallel", "arbitrary"), VMEM_LIMIT),
        name="mix_ffn",
    )(*args, gpre.reshape(1, D_MODEL), modr, modr, w1, w3, w2, npost.reshape(1, D_MODEL), modr)


def _mix_route_kernel(*refs, seq_len):
    (gpre_ref, sc_ref, sh_ref, rt_ref, x_ref, h_ref, idx_ref, gate_ref, rank_ref, cnt_ref,
     run_ref, upper_ref, band_ref) = refs[N_MIX_INPUTS:]
    tm = x_ref.shape[0]
    first = (pl.program_id(0) == 0) & (pl.program_id(1) == 0)

    @pl.when(first)
    def _():
        run_ref[...] = jnp.zeros_like(run_ref)
        earlier = lax.broadcasted_iota(I32, (tm, tm), 0) < lax.broadcasted_iota(I32, (tm, tm), 1)
        upper_ref[...] = jnp.where(earlier, 1.0, 0.0).astype(BF16)
        _init_pool_bands(band_ref)

    x = _mix_tile(*refs[:N_MIX_INPUTS], band_ref, seq_len)
    x_ref[...] = x
    h = _rms(x, gpre_ref[...]) * (1.0 + sc_ref[...]) + sh_ref[...]
    h_ref[...] = h
    def split(v):
        hi = v.astype(BF16)
        return hi, (v - hi.astype(F32)).astype(BF16)

    nt_dot = lambda a, b: lax.dot_general(a, b, (((1,), (1,)), ((), ())), preferred_element_type=F32)
    r_hi, r_lo = split(rt_ref[...])
    h_hi, h_lo = split(h)
    by_h_hi = nt_dot(jnp.concatenate([r_hi, r_lo], axis=0), h_hi)
    logits = by_h_hi[0:N_EXPERTS] + by_h_hi[N_EXPERTS:] + nt_dot(r_hi, h_lo)
    eid = lax.broadcasted_iota(I32, (N_EXPERTS, tm), 0).astype(F32)
    m1 = jnp.max(logits, axis=0, keepdims=True)
    i1 = jnp.min(jnp.where(logits == m1, eid, float(N_EXPERTS)), axis=0, keepdims=True)
    oh1 = eid == i1
    rest = jnp.where(oh1, -jnp.inf, logits)
    m2 = jnp.max(rest, axis=0, keepdims=True)
    i2 = jnp.min(jnp.where(rest == m2, eid, float(N_EXPERTS)), axis=0, keepdims=True)
    oh2 = eid == i2
    e2 = jnp.exp(m2 - m1)
    gate_ref[0:1, :] = 1.0 / (1.0 + e2)
    gate_ref[1:2, :] = e2 / (1.0 + e2)
    idx_ref[0:1, :] = i1.astype(I32)
    idx_ref[1:2, :] = i2.astype(I32)
    upper = upper_ref[...]
    f1 = jnp.where(oh1, 1.0, 0.0)
    f2 = jnp.where(oh2, 1.0, 0.0)
    before1 = jnp.dot(f1.astype(BF16), upper, preferred_element_type=F32)
    before2 = jnp.dot(f2.astype(BF16), upper, preferred_element_type=F32)
    cnt1 = jnp.sum(f1, axis=1, keepdims=True)
    cnt2 = jnp.sum(f2, axis=1, keepdims=True)
    run = run_ref[:, 0:1]
    rank_ref[0:1, :] = jnp.sum(f1 * (run + before1), axis=0, keepdims=True).astype(I32)
    rank_ref[1:2, :] = jnp.sum(f2 * (run + cnt1 + before2), axis=0, keepdims=True).astype(I32)
    run_new = run_ref[...] + cnt1 + cnt2
    run_ref[...] = run_new
    cnt_ref[...] = run_new


def _mix_route(mix, gpre, router_t, layer):
    x, modr = mix[0], mix[-1]
    B, T, _ = x.shape
    tm = min(TOKEN_TILE, T)
    nt = T // tm
    args, specs = _mix_inputs(mix, tm, layer, False)
    tok = pl.BlockSpec((None, tm, D_MODEL), lambda b, i: (b, i, 0))
    lane = pl.BlockSpec((2, tm), lambda b, i: (0, b * nt + i))
    return pl.pallas_call(
        functools.partial(_mix_route_kernel, seq_len=T),
        out_shape=(jax.ShapeDtypeStruct((B, T, D_MODEL), F32),
                   jax.ShapeDtypeStruct((B, T, D_MODEL), F32),
                   jax.ShapeDtypeStruct((2, B * T), I32),
                   jax.ShapeDtypeStruct((2, B * T), F32),
                   jax.ShapeDtypeStruct((2, B * T), I32),
                   jax.ShapeDtypeStruct((N_EXPERTS, 128), F32)),
        grid=(B, nt),
        in_specs=specs + [_row_spec(D_MODEL), _mod_spec(layer, 4, False), _mod_spec(layer, 3, False),
                          pl.BlockSpec((N_EXPERTS, D_MODEL), lambda b, i: (0, 0))],
        out_specs=(tok, tok, lane, lane, lane,
                   pl.BlockSpec((N_EXPERTS, 128), lambda b, i: (0, 0))),
        scratch_shapes=[pltpu.VMEM((N_EXPERTS, 128), F32), pltpu.VMEM((tm, tm), BF16), POOL_BANDS],
        compiler_params=_cparams(("arbitrary", "arbitrary"), VMEM_LIMIT),
        name="mix_route",
    )(*args, gpre.reshape(1, D_MODEL), modr, modr, router_t)


def _dispatch_kernel(zero_tiles_ref, slot_ref, h_ref, hs_ref, zero_ref, sem, zero_sem):
    tm = h_ref.shape[0]

    @pl.when(pl.program_id(0) == 0)
    def _():
        zero_ref[...] = jnp.zeros_like(zero_ref)
        for j in range(2 * N_EXPERTS):
            start = pl.multiple_of(zero_tiles_ref[j] * EXPERT_TILE, EXPERT_TILE)
            fill = pltpu.make_async_copy(zero_ref, hs_ref.at[pl.ds(start, EXPERT_TILE)], zero_sem)
            fill.start()
            fill.wait()

    def body(i, carry):
        for u in range(DMA_UNROLL):
            r = pl.multiple_of(i * DMA_UNROLL, DMA_UNROLL) + u
            for k in range(2):
                s = slot_ref[2 * r + k]
                pltpu.make_async_copy(h_ref.at[pl.ds(r, 1)], hs_ref.at[pl.ds(s, 1)], sem).start(priority=k)
        return carry

    lax.fori_loop(0, tm // DMA_UNROLL, body, 0)
    for _ in range(2):
        pltpu.make_async_copy(h_ref, hs_ref.at[pl.ds(0, tm)], sem).wait()


def _dispatch(h, slot_flat, zero_tiles, n_slots):
    N = h.shape[0]
    tm = min(TOKEN_TILE, N)
    grid_spec = pltpu.PrefetchScalarGridSpec(
        num_scalar_prefetch=1,
        grid=(N // tm,),
        in_specs=[pl.BlockSpec((2 * tm,), lambda i, zt: (i,), memory_space=pltpu.SMEM),
                  pl.BlockSpec((tm, D_MODEL), lambda i, zt: (i, 0))],
        out_specs=pl.BlockSpec(memory_space=pl.ANY),
        scratch_shapes=[pltpu.VMEM((EXPERT_TILE, D_MODEL), F32),
                        pltpu.SemaphoreType.DMA(()), pltpu.SemaphoreType.DMA(())],
    )
    return pl.pallas_call(
        _dispatch_kernel,
        out_shape=jax.ShapeDtypeStruct((n_slots, D_MODEL), F32),
        grid_spec=grid_spec,
        compiler_params=_cparams(("arbitrary",)),
        name="moe_dispatch",
    )(zero_tiles, slot_flat, h)


def _expert_kernel(te_ref, tv_ref, h_ref, w1_ref, w3_ref, w2_ref, y_ref):
    t = pl.program_id(0)

    @pl.when(tv_ref[t] == 1)
    def _():
        y_ref[...] = _swiglu_tile(h_ref[...].astype(BF16), w1_ref, w3_ref, w2_ref)

    @pl.when(tv_ref[t] == 0)
    def _():
        y_ref[...] = jnp.zeros_like(y_ref)


def _experts(hs, tile_expert, tile_valid, w1, w3, w2):
    n_slots = hs.shape[0]
    tm = EXPERT_TILE
    grid_spec = pltpu.PrefetchScalarGridSpec(
        num_scalar_prefetch=2,
        grid=(n_slots // tm,),
        in_specs=[pl.BlockSpec((tm, D_MODEL), lambda t, te, tv: (t, 0)),
                  pl.BlockSpec((None, D_MODEL, D_FF), lambda t, te, tv: (te[t], 0, 0)),
                  pl.BlockSpec((None, D_MODEL, D_FF), lambda t, te, tv: (te[t], 0, 0)),
                  pl.BlockSpec((None, D_FF, D_MODEL), lambda t, te, tv: (te[t], 0, 0))],
        out_specs=pl.BlockSpec((tm, D_MODEL), lambda t, te, tv: (t, 0)),
    )
    return pl.pallas_call(
        _expert_kernel,
        out_shape=jax.ShapeDtypeStruct((n_slots, D_MODEL), F32),
        grid_spec=grid_spec,
        compiler_params=_cparams(("arbitrary",), VMEM_LIMIT),
        name="moe_experts",
    )(tile_expert, tile_valid, hs, w1, w3, w2)


def _combine_kernel(slot_ref, gate_ref, x_ref, npost_ref, g2_ref, ys_ref, o_ref, buf0, buf1, sem):
    tm = x_ref.shape[0]
    bufs = (buf0, buf1)

    def body(i, carry):
        for u in range(DMA_UNROLL):
            r = pl.multiple_of(i * DMA_UNROLL, DMA_UNROLL) + u
            for k in range(2):
                s = slot_ref[2 * r + k]
                pltpu.make_async_copy(ys_ref.at[pl.ds(s, 1)], bufs[k].at[pl.ds(r, 1)], sem).start(priority=k)
        return carry

    lax.fori_loop(0, tm // DMA_UNROLL, body, 0)
    for k in range(2):
        pltpu.make_async_copy(ys_ref.at[pl.ds(0, tm)], bufs[k], sem).wait()
    y = gate_ref[:, 0:1] * buf0[...] + gate_ref[:, 1:2] * buf1[...]
    o_ref[...] = x_ref[...] + g2_ref[...] * _rms(y, npost_ref[...])


def _combine(x, ys, slot_flat, gate_tok, npost, modr, layer):
    B, T, _ = x.shape
    tm = min(TOKEN_TILE, T)
    nt = T // tm
    tok = pl.BlockSpec((None, tm, D_MODEL), lambda b, i: (b, i, 0))
    return pl.pallas_call(
        _combine_kernel,
        out_shape=jax.ShapeDtypeStruct((B, T, D_MODEL), F32),
        grid=(B, nt),
        in_specs=[pl.BlockSpec((2 * tm,), lambda b, i: (b * nt + i,), memory_space=pltpu.SMEM),
                  pl.BlockSpec((tm, 2), lambda b, i: (b * nt + i, 0)),
                  tok, _row_spec(D_MODEL), _mod_spec(layer, 5, False),
                  pl.BlockSpec(memory_space=pl.ANY)],
        out_specs=tok,
        scratch_shapes=[pltpu.VMEM((tm, D_MODEL), F32), pltpu.VMEM((tm, D_MODEL), F32),
                        pltpu.SemaphoreType.DMA(())],
        compiler_params=_cparams(("arbitrary", "arbitrary")),
        name="moe_combine",
    )(slot_flat, gate_tok, x, npost.reshape(1, D_MODEL), modr, ys)


def _mix_moe_ffn(mix, gpre, npost, router, w1, w3, w2, layer):
    modr = mix[-1]
    B, T, _ = mix[0].shape
    N = B * T
    tm = min(TOKEN_TILE, T)
    x, h, idx, gate, rank, cnt = _mix_route(mix, gpre, router.T, layer)
    n_slots = 2 * N + N_EXPERTS * EXPERT_TILE
    n_tiles = n_slots // EXPERT_TILE
    counts = cnt[:, 0].astype(I32)
    padded = ((counts + EXPERT_TILE - 1) // EXPERT_TILE) * EXPERT_TILE
    ends = jnp.cumsum(padded)
    starts = ends - padded
    slot = rank
    for e in range(N_EXPERTS):
        slot = slot + jnp.where(idx == e, starts[e], 0)
    slot_flat = slot.T.reshape(2 * N)
    tile_start = jnp.arange(n_tiles, dtype=I32) * EXPERT_TILE
    tile_valid = (tile_start < ends[-1]).astype(I32)
    tile_expert = jnp.minimum(jnp.sum((tile_start[:, None] >= ends[None, :]).astype(I32), axis=1), N_EXPERTS - 1)
    last_expert = jnp.max(jnp.where(tile_valid == 1, tile_expert, 0))
    tile_expert = jnp.where(tile_valid == 1, tile_expert, last_expert)
    last_tile = jnp.where(padded > 0, ends // EXPERT_TILE - 1, n_tiles - 1)
    tail_tile = jnp.minimum(ends[-1] // EXPERT_TILE + jnp.arange(N_EXPERTS, dtype=I32), n_tiles - 1)
    zero_tiles = jnp.concatenate([last_tile, tail_tile]).astype(I32)
    hs = _dispatch(h.reshape(N, D_MODEL), slot_flat, zero_tiles, n_slots)
    ys = _experts(hs, tile_expert, tile_valid, w1, w3, w2)
    return _combine(x, ys, slot_flat, gate.T, npost, modr, layer)


def kernel(x, c, ctx, c_ctx, w_mod, b_mod, norm_pre_mix, norm_post_mix, norm_pre_ffn, norm_post_ffn, w_in, w_out, q_norm, k_norm, ret_decay_logit, pool_w, pool_scale, ffn_w1, ffn_w3, ffn_w2, moe_router, moe_w1, moe_w3, moe_w2):
    B, T, _ = x.shape
    cvec = jnp.zeros((MOD_ROWS, D_MODEL), F32).at[0:B].set(c).at[2].set(c_ctx)
    modr = _modulation(cvec, w_mod, b_mod)
    cos_t, sin_t = _rope_tables(T)
    avg = jnp.kron(jnp.eye(RET_HEADS, dtype=F32), jnp.full((HEAD_DIM, HEAD_DIM), 1.0 / HEAD_DIM, F32)).astype(BF16)
    zero_state = jnp.zeros((B, RET_WIDTH, RET_WIDTH), F32)
    xc = ctx
    for i in range(DEPTH):
        need_ctx = i < DEPTH - 1
        w_in_b = w_in[i].astype(BF16)
        w_out_b = w_out[i].astype(BF16)
        pool_bd = jax.scipy.linalg.block_diag(*[pool_w[i, g] for g in range(len(POOL_WINDOWS))]).astype(BF16)
        log_gamma = jax.nn.log_sigmoid(ret_decay_logit[i].astype(F32))

        qkv_c, ret_c, pp_c = _in_projection(xc, modr, norm_pre_mix[i], w_in_b, i, True)
        qt_c, kn_c, vt_c, qn_c, st_c = _attn_prep(qkv_c, q_norm[i], k_norm[i], cos_t, sin_t, False)
        of_c, ob_c, s_fwd, s_bwd = _retention(ret_c, log_gamma, zero_state, zero_state)

        qkv_x, ret_x, pp_x = _in_projection(x, modr, norm_pre_mix[i], w_in_b, i, False)
        qt_x, kn_x, vt_x, qn_x, st_x = _attn_prep(qkv_x, q_norm[i], k_norm[i], cos_t, sin_t, True)
        keys_x = [(kn_x, vt_x), (kn_c, vt_c)]
        attn_x = _attention(qt_x, _softmax_shift(qn_x, st_x, [st_x, st_c], qt_x, keys_x), keys_x)
        of_x, ob_x, _, _ = _retention(ret_x, log_gamma, s_fwd, s_bwd)
        mix_x = (x, attn_x, of_x, ob_x, ret_x, pp_x, pool_bd, pool_scale[i], avg, w_out_b, norm_post_mix[i], modr)

        j = i // 2
        if i % 2 == 0:
            w1, w3, w2 = ffn_w1[j].astype(BF16), ffn_w3[j].astype(BF16), ffn_w2[j].astype(BF16)
            x = _mix_ffn(mix_x, norm_pre_ffn[i], norm_post_ffn[i], w1, w3, w2, i, False)
            if need_ctx:
                keys_c = [(kn_c, vt_c)]
                attn_c = _attention(qt_c, _softmax_shift(qn_c, st_c, [st_c], qt_c, keys_c), keys_c)
                mix_c = (xc, attn_c, of_c, ob_c, ret_c, pp_c, pool_bd, pool_scale[i], avg, w_out_b,
                         norm_post_mix[i], modr)
                xc = _mix_ffn(mix_c, norm_pre_ffn[i], norm_post_ffn[i], w1, w3, w2, i, True)
        else:
            assert not need_ctx
            x = _mix_moe_ffn(mix_x, norm_pre_ffn[i], norm_post_ffn[i], moe_router[j], moe_w1[j].astype(BF16),
                             moe_w3[j].astype(BF16), moe_w2[j].astype(BF16), i)
    return x
```

```python
import functools

import jax
import jax.numpy as jnp
from jax import lax
from jax.experimental import pallas as pl
from jax.experimental.pallas import tpu as pltpu

F32 = jnp.float32
BF16 = jnp.bfloat16
I32 = jnp.int32

D_MODEL = 1024
GRID_W = 64
HEAD_DIM = 64
ATTN_WIDTH = 512
KV_HEADS = 2
ATTN_GROUP = 4
KV_WIDTH = 128
RET_WIDTH = 256
RET_HEADS = 4
POOL_WIDTH = 256
POOL_WINDOWS = (2, 4, 8, 16)
QKV_WIDTH = ATTN_WIDTH + 2 * KV_WIDTH
RET_PROJ_WIDTH = 4 * RET_WIDTH
IN_WIDTH = QKV_WIDTH + RET_PROJ_WIDTH + POOL_WIDTH
HEAD_SHIFT = HEAD_DIM.bit_length() - 1
LANES = 128
RET_CHUNK = 128
ROPE_THETA = 10000.0
D_FF = 2816
N_EXPERTS = 8
NORM_EPS = 1e-6
DEPTH = 2

TOKEN_TILE = 1024
ATTN_Q_TILE = 512
ATTN_K_CHUNK = 2048
PREP_TILE = 2048
RET_TILE = 512
POOL_TILE = 256
POOL_HALO = 16
FF_CHUNK = 256
EXPERT_TILE = 512
DMA_UNROLL = 8
BF16_SUBLANES = 16
MXU_DIM = 256
K_COLS = MXU_DIM
KEY_BLOCK = MXU_DIM
ATTN_UNROLL = 9
N_MXU = 2
SCORE_REG, VALUE_REG = 1, 0
SCORE_BASE = 64
F32_SUBLANES = 8
VT_ROWS = HEAD_DIM + F32_SUBLANES
MAX_BOUND_SHIFT = 40.0
SHIFT_MARGIN = 1.01
MOD_ROWS = 8
CTX_MOD_ROW = 2
MOD_CHUNKS = 6
VMEM_LIMIT = 56 * 1024 * 1024


def _cparams(sem, vmem=None):
    return pltpu.CompilerParams(dimension_semantics=sem, vmem_limit_bytes=vmem)


def _rms(x, gain):
    ms = jnp.mean(x * x, axis=-1, keepdims=True)
    return x * lax.rsqrt(ms + NORM_EPS) * gain


def _silu(x):
    return x * jax.nn.sigmoid(x)


def _mod_kernel(c_ref, w_ref, b_ref, o_ref):
    s = _silu(c_ref[...])
    o_ref[...] = jnp.dot(s, w_ref[...], precision=lax.Precision.HIGHEST,
                         preferred_element_type=F32) + b_ref[...]


def _modulation(cvec, w_mod, b_mod):
    out = pl.pallas_call(
        _mod_kernel,
        out_shape=jax.ShapeDtypeStruct((DEPTH, MOD_ROWS, MOD_CHUNKS * D_MODEL), F32),
        grid=(DEPTH, MOD_CHUNKS),
        in_specs=[
            pl.BlockSpec((MOD_ROWS, D_MODEL), lambda l, j: (0, 0)),
            pl.BlockSpec((None, D_MODEL, D_MODEL), lambda l, j: (l, 0, j)),
            pl.BlockSpec((None, 1, D_MODEL), lambda l, j: (l, 0, j)),
        ],
        out_specs=pl.BlockSpec((None, MOD_ROWS, D_MODEL), lambda l, j: (l, 0, j)),
        compiler_params=_cparams(("parallel", "parallel")),
        name="modulation",
    )(cvec, w_mod, b_mod.reshape(DEPTH, 1, MOD_CHUNKS * D_MODEL))
    return out.reshape(DEPTH * MOD_ROWS * MOD_CHUNKS, 1, D_MODEL)


def _mod_spec(layer, chunk, ctx):
    base = layer * MOD_ROWS * MOD_CHUNKS
    if ctx:
        return pl.BlockSpec((None, 1, D_MODEL), lambda b, i: (base + CTX_MOD_ROW * MOD_CHUNKS + chunk, 0, 0))
    return pl.BlockSpec((None, 1, D_MODEL), lambda b, i: (base + b * MOD_CHUNKS + chunk, 0, 0))


def _row_spec(width):
    return pl.BlockSpec((1, width), lambda b, i: (0, 0))


def _inproj_kernel(x_ref, g_ref, sc_ref, sh_ref, w_ref, qkv_ref, ret_ref, pp_ref):
    h = _rms(x_ref[...], g_ref[...]) * (1.0 + sc_ref[...]) + sh_ref[...]
    p = jnp.dot(h.astype(BF16), w_ref[...], preferred_element_type=F32)
    qkv_ref[...] = p[:, :QKV_WIDTH].astype(BF16)
    ret_ref[...] = p[:, QKV_WIDTH:QKV_WIDTH + RET_PROJ_WIDTH].astype(BF16)
    pp_ref[...] = p[:, QKV_WIDTH + RET_PROJ_WIDTH:].astype(BF16)


def _in_projection(x, modr, gain, w_in_bf16, layer, ctx):
    B, T, _ = x.shape
    tm = min(TOKEN_TILE, T)
    tok = lambda w: pl.BlockSpec((None, tm, w), lambda b, i: (b, i, 0))
    return pl.pallas_call(
        _inproj_kernel,
        out_shape=(jax.ShapeDtypeStruct((B, T, QKV_WIDTH), BF16),
                   jax.ShapeDtypeStruct((B, T, RET_PROJ_WIDTH), BF16),
                   jax.ShapeDtypeStruct((B, T, POOL_WIDTH), BF16)),
        grid=(B, T // tm),
        in_specs=[tok(D_MODEL), _row_spec(D_MODEL), _mod_spec(layer, 1, ctx), _mod_spec(layer, 0, ctx),
                  pl.BlockSpec((D_MODEL, IN_WIDTH), lambda b, i: (0, 0))],
        out_specs=(tok(QKV_WIDTH), tok(RET_PROJ_WIDTH), tok(POOL_WIDTH)),
        compiler_params=_cparams(("parallel", "parallel"), VMEM_LIMIT),
        name="in_projection",
    )(x, gain.reshape(1, D_MODEL), modr, modr, w_in_bf16)


def _prep_kernel(qkv_ref, qg_ref, kg_ref, cos_ref, sin_ref, qt_ref, kn_ref, vt_ref, qn_ref, stat_ref, *, rope):
    t = qkv_ref[...].astype(F32).T

    def norm_rope(blk, gain):
        ms = jnp.mean(blk * blk, axis=0, keepdims=True)
        y = blk * lax.rsqrt(ms + NORM_EPS) * gain
        if rope:
            partner = jnp.concatenate([y[16:32], y[0:16], y[48:64], y[32:48]], axis=0)
            y = y * cos_ref[...] + partner * sin_ref[...]
        return y

    def sq_norm(y):
        return jnp.sum(y * y, axis=0, keepdims=True)

    def row_max(n2):
        return jnp.broadcast_to(jnp.max(n2, axis=1, keepdims=True), (1, LANES))

    q_stats = []
    for h in range(ATTN_WIDTH // HEAD_DIM):
        lo = h * HEAD_DIM
        q = norm_rope(t[lo:lo + HEAD_DIM], qg_ref[...]) * (HEAD_DIM ** -0.5)
        qt_ref[lo:lo + HEAD_DIM, :] = q.astype(BF16)
        n2 = sq_norm(q)
        qn_ref[h // ATTN_GROUP, h % ATTN_GROUP:h % ATTN_GROUP + 1, :] = jnp.sqrt(n2)
        q_stats.append(row_max(n2))
    tm = t.shape[1]
    k_pad = jnp.where(lax.broadcasted_iota(I32, (K_COLS - HEAD_DIM, tm), 0) < 2, 1.0, 0.0)
    v_pad = jnp.where(lax.broadcasted_iota(I32, (VT_ROWS - HEAD_DIM, tm), 0) < 1, 1.0, 0.0)
    k_stats = []
    for kv in range(KV_HEADS):
        lo = ATTN_WIDTH + kv * HEAD_DIM
        k = norm_rope(t[lo:lo + HEAD_DIM], kg_ref[...])
        k_stats.append(row_max(sq_norm(k)))
        kn_ref[kv] = jnp.concatenate([k, k_pad], axis=0).T.astype(BF16)
        lo = ATTN_WIDTH + KV_WIDTH + kv * HEAD_DIM
        vt = jnp.concatenate([t[lo:lo + HEAD_DIM], v_pad], axis=0)
        for j in range(tm // KEY_BLOCK):
            vt_ref[kv, j] = vt[:, j * KEY_BLOCK:(j + 1) * KEY_BLOCK]
    q_group = [functools.reduce(jnp.maximum, q_stats[kv * ATTN_GROUP:(kv + 1) * ATTN_GROUP])
               for kv in range(KV_HEADS)]
    stat_ref[...] = jnp.concatenate(k_stats + q_group + [jnp.zeros((F32_SUBLANES - 2 * KV_HEADS, LANES), F32)], axis=0)


def _attn_prep(qkv, q_gain, k_gain, cos_t, sin_t, rope):
    B, T, _ = qkv.shape
    tm = min(PREP_TILE, T)
    nc = T // tm
    return pl.pallas_call(
        functools.partial(_prep_kernel, rope=rope),
        out_shape=(jax.ShapeDtypeStruct((B, ATTN_WIDTH, T), BF16),
                   jax.ShapeDtypeStruct((B, KV_HEADS, T, K_COLS), BF16),
                   jax.ShapeDtypeStruct((B, KV_HEADS, T // KEY_BLOCK, VT_ROWS, KEY_BLOCK), F32),
                   jax.ShapeDtypeStruct((B, KV_HEADS, ATTN_GROUP, T), F32),
                   jax.ShapeDtypeStruct((B, nc, F32_SUBLANES, LANES), F32)),
        grid=(B, nc),
        in_specs=[pl.BlockSpec((None, tm, QKV_WIDTH), lambda b, i: (b, i, 0)),
                  pl.BlockSpec((HEAD_DIM, 1), lambda b, i: (0, 0)),
                  pl.BlockSpec((HEAD_DIM, 1), lambda b, i: (0, 0)),
                  pl.BlockSpec((HEAD_DIM, tm), lambda b, i: (0, i)),
                  pl.BlockSpec((HEAD_DIM, tm), lambda b, i: (0, i))],
        out_specs=(pl.BlockSpec((None, ATTN_WIDTH, tm), lambda b, i: (b, 0, i)),
                   pl.BlockSpec((None, KV_HEADS, tm, K_COLS), lambda b, i: (b, 0, i, 0)),
                   pl.BlockSpec((None, KV_HEADS, tm // KEY_BLOCK, VT_ROWS, KEY_BLOCK), lambda b, i: (b, 0, i, 0, 0)),
                   pl.BlockSpec((None, KV_HEADS, ATTN_GROUP, tm), lambda b, i: (b, 0, 0, i)),
                   pl.BlockSpec((None, None, F32_SUBLANES, LANES), lambda b, i: (b, i, 0, 0))),
        compiler_params=_cparams(("parallel", "parallel")),
        name="attn_prep",
    )(qkv, q_gain.reshape(HEAD_DIM, 1), k_gain.reshape(HEAD_DIM, 1), cos_t, sin_t)


def _rope_tables(T):
    t = jnp.arange(T)
    row = (t // GRID_W).astype(F32)
    col = (t % GRID_W).astype(F32)
    n_freq = HEAD_DIM // 4
    inv = ROPE_THETA ** (-jnp.arange(n_freq, dtype=F32) / n_freq)
    ang_r = row[None, :] * inv[:, None]
    ang_c = col[None, :] * inv[:, None]
    cos_t = jnp.concatenate([jnp.cos(ang_r), jnp.cos(ang_r), jnp.cos(ang_c), jnp.cos(ang_c)], axis=0)
    sin_t = jnp.concatenate([-jnp.sin(ang_r), jnp.sin(ang_r), -jnp.sin(ang_c), jnp.sin(ang_c)], axis=0)
    return cos_t, sin_t


def _score_max_kernel(qt_ref, k_ref, m_ref, qa_ref):
    tq = qt_ref.shape[1]
    nq = ATTN_GROUP * tq
    for g in range(ATTN_GROUP):
        qa_ref[0:HEAD_DIM, g * tq:(g + 1) * tq] = qt_ref[g * HEAD_DIM:(g + 1) * HEAD_DIM, :]
    qa_ref[HEAD_DIM:K_COLS, :] = jnp.zeros((K_COLS - HEAD_DIM, nq), BF16)
    tk = min(ATTN_K_CHUNK, k_ref.shape[0])

    def colmax8(j):
        k = k_ref[pl.ds(pl.multiple_of(j * tk, tk), tk), :]
        s = jnp.dot(k, qa_ref[...], preferred_element_type=F32)
        return jnp.max(s.reshape(tk // 8, 8, nq), axis=0)

    mx = lax.fori_loop(1, k_ref.shape[0] // tk, lambda j, mx: jnp.maximum(mx, colmax8(j)), colmax8(0))
    m = jnp.max(mx, axis=0, keepdims=True)
    for g in range(ATTN_GROUP):
        m_ref[g:g + 1, :] = m[:, g * tq:(g + 1) * tq]


def _score_max(qt, k_all):
    B, _, Tq = qt.shape
    Tk = k_all.shape[2]
    tq = min(ATTN_Q_TILE, Tq)
    return pl.pallas_call(
        _score_max_kernel,
        out_shape=jax.ShapeDtypeStruct((B, KV_HEADS, ATTN_GROUP, Tq), F32),
        grid=(B, KV_HEADS, Tq // tq),
        in_specs=[pl.BlockSpec((None, ATTN_GROUP * HEAD_DIM, tq), lambda b, h, i: (b, h, i)),
                  pl.BlockSpec((None, None, Tk, K_COLS), lambda b, h, i: (b, h, 0, 0))],
        out_specs=pl.BlockSpec((None, None, ATTN_GROUP, tq), lambda b, h, i: (b, h, 0, i)),
        scratch_shapes=[pltpu.VMEM((K_COLS, ATTN_GROUP * tq), BF16)],
        compiler_params=_cparams(("parallel", "parallel", "parallel"), VMEM_LIMIT),
        name="score_max",
    )(qt, k_all)


def _softmax_shift(q_norm, q_stats, k_stats_list, qt, key_sets):
    B = qt.shape[0]
    k2 = functools.reduce(jnp.maximum, [st[:, :, 0:KV_HEADS, 0].max(axis=1) for st in k_stats_list])
    q2 = q_stats[:, :, KV_HEADS:2 * KV_HEADS, 0].max(axis=1)
    use_bound = (jnp.sqrt(k2 * q2) * SHIFT_MARGIN <= MAX_BOUND_SHIFT).reshape(B, KV_HEADS, 1, 1)
    bound = q_norm * (jnp.sqrt(k2) * SHIFT_MARGIN).reshape(B, KV_HEADS, 1, 1)
    exact = lambda: _score_max(qt, jnp.concatenate([k for k, _ in key_sets], axis=2))
    return lax.cond(jnp.all(use_bound), lambda: bound, lambda: jnp.where(use_bound, bound, exact()))


def _attn_kernel(qt_ref, shift_ref, *refs):
    *kv_refs, o_ref, qa_ref = refs
    sources = [(kv_refs[i], kv_refs[i + 1]) for i in range(0, len(kv_refs), 2)]
    counts = [k_ref.shape[0] // KEY_BLOCK for k_ref, _ in sources]
    tq = qt_ref.shape[1]
    n_blocks = sum(counts)
    heads_per_mxu = ATTN_GROUP // N_MXU

    def locate(blk):
        if not isinstance(blk, int):
            return 0, blk
        src = 0
        while blk >= counts[src]:
            blk -= counts[src]
            src += 1
        return src, blk

    row = lax.broadcasted_iota(I32, (BF16_SUBLANES, tq), 0)
    for g in range(ATTN_GROUP):
        m = shift_ref[g:g + 1, :]
        m_hi = m.astype(BF16).astype(F32)
        qa_ref[g, 0:HEAD_DIM, :] = qt_ref[g * HEAD_DIM:(g + 1) * HEAD_DIM, :]
        qa_ref[g, HEAD_DIM:HEAD_DIM + BF16_SUBLANES, :] = jnp.where(
            row == 0, -m_hi, jnp.where(row == 1, m_hi - m, 0.0)).astype(BF16)
        qa_ref[g, HEAD_DIM + BF16_SUBLANES:K_COLS, :] = jnp.zeros((K_COLS - HEAD_DIM - BF16_SUBLANES, tq), BF16)

    def score_addr(slot):
        return SCORE_BASE + slot * (KEY_BLOCK // 4)

    def out_addr(slot):
        return slot * (VT_ROWS // 4)

    def keys(blk):
        src, j = locate(blk)
        return sources[src][0][pl.ds(pl.multiple_of(j * KEY_BLOCK, KEY_BLOCK), KEY_BLOCK), :]

    def stage_q(slot):
        for mxu in range(N_MXU):
            pltpu.matmul_push_rhs(qa_ref[mxu * heads_per_mxu + slot], SCORE_REG, mxu)

    def issue_scores(blk, slot):
        k = keys(blk)
        for mxu in range(N_MXU):
            pltpu.matmul_acc_lhs(score_addr(slot), k, mxu, load_staged_rhs=SCORE_REG)

    def pop_probs(slot):
        return [jnp.exp(pltpu.matmul_pop(score_addr(slot), (KEY_BLOCK, tq), F32, mxu)).astype(BF16)
                for mxu in range(N_MXU)]

    def push_probs(blk, slot, p):
        src, j = locate(blk)
        vt = sources[src][1][j]
        for mxu in range(N_MXU):
            pltpu.matmul_push_rhs(p[mxu], VALUE_REG, mxu)
        return vt

    def unit(blk, slot, stage_next, first):
        if not first:
            p = pop_probs(slot)
        issue_scores(blk, slot)
        if not first:
            vt = push_probs(blk - 1, slot, p)
        if stage_next:
            stage_q((slot + 1) % heads_per_mxu)
        if not first:
            for mxu in range(N_MXU):
                pltpu.matmul_acc_lhs(out_addr(slot), vt, mxu, load_staged_rhs=VALUE_REG)

    def block(blk, last=False, first=False):
        for slot in range(heads_per_mxu):
            unit(blk, slot, stage_next=not (last and slot == heads_per_mxu - 1), first=first)

    stage_q(0)
    block(0, last=n_blocks == 1, first=True)
    if n_blocks > 1:
        n_loop = n_blocks - 2
        assert n_loop + 1 <= counts[0]
        unroll = max([u for u in range(1, ATTN_UNROLL + 1) if n_loop % u == 0] or [1])

        def body(i, carry):
            for u in range(unroll):
                block(1 + i * unroll + u)
            return carry

        if n_loop:
            lax.fori_loop(0, n_loop // unroll, body, 0)
        block(n_blocks - 1, last=True)
    for slot in range(heads_per_mxu):
        vt = push_probs(n_blocks - 1, slot, pop_probs(slot))
        for mxu in range(N_MXU):
            pltpu.matmul_acc_lhs(out_addr(slot), vt, mxu, load_staged_rhs=VALUE_REG)

    outs = []
    for mxu in range(N_MXU):
        for slot in range(heads_per_mxu):
            acc = pltpu.matmul_pop(out_addr(slot), (VT_ROWS, tq), F32, mxu)
            outs.append(acc[0:HEAD_DIM, :] / acc[HEAD_DIM:HEAD_DIM + 1, :])
    o_ref[...] = jnp.concatenate(outs, axis=0).astype(BF16)


def _attention(qt, shift, key_sets):
    B, _, Tq = qt.shape
    tq = MXU_DIM
    in_specs = [pl.BlockSpec((None, ATTN_GROUP * HEAD_DIM, tq), lambda b, h, i: (b, h, i)),
                pl.BlockSpec((None, None, ATTN_GROUP, tq), lambda b, h, i: (b, h, 0, i))]
    args = [qt, shift]
    for k, vt in key_sets:
        Tk = k.shape[2]
        in_specs += [pl.BlockSpec((None, None, Tk, K_COLS), lambda b, h, i: (b, h, 0, 0)),
                     pl.BlockSpec((None, None, Tk // KEY_BLOCK, VT_ROWS, KEY_BLOCK), lambda b, h, i: (b, h, 0, 0, 0))]
        args += [k, vt]
    return pl.pallas_call(
        _attn_kernel,
        out_shape=jax.ShapeDtypeStruct((B, ATTN_WIDTH, Tq), BF16),
        grid=(B, KV_HEADS, Tq // tq),
        in_specs=in_specs,
        out_specs=pl.BlockSpec((None, ATTN_GROUP * HEAD_DIM, tq), lambda b, h, i: (b, h, i)),
        scratch_shapes=[pltpu.VMEM((ATTN_GROUP, K_COLS, tq), BF16)],
        compiler_params=_cparams(("parallel", "parallel", "parallel"), VMEM_LIMIT),
        name="attention",
    )(*args)


def _ret_kernel(lgc_f_ref, lgc_b_ref, lgr_f_ref, lgr_b_ref, s0f_ref, s0b_ref, blk_f_ref, blk_b_ref,
                of_ref, ob_ref, sf_ref, sb_ref, st_f, st_b, dec):
    C = RET_CHUNK
    W = RET_WIDTH
    n = pl.program_id(1)

    @pl.when(n == 0)
    def _():
        st_f[...] = s0f_ref[...]
        st_b[...] = s0b_ref[...]
        c = jnp.bitwise_and(lax.broadcasted_iota(I32, (RET_HEADS * C, C), 0), C - 1)
        m = lax.broadcasted_iota(I32, (RET_HEADS * C, C), 1)
        diff = (c - m).astype(F32)
        dec[...] = (jnp.where(diff >= 0, jnp.exp(lgc_f_ref[...] * jnp.maximum(diff, 0.0)), 0.0)
                    + jnp.where(diff <= 0, jnp.exp(lgc_b_ref[...] * jnp.maximum(-diff, 0.0)), 0.0))

    lane_head = jnp.right_shift(lax.broadcasted_iota(I32, (C, W), 1), HEAD_SHIFT)
    pos = lax.broadcasted_iota(I32, (C, W), 0).astype(F32)
    same_head = (jnp.right_shift(lax.broadcasted_iota(I32, (W, W), 0), HEAD_SHIFT)
                 == jnp.right_shift(lax.broadcasted_iota(I32, (W, W), 1), HEAD_SHIFT))

    def direction(blk_ref, st_ref, lgr, forward, out_ref):
        n_sub = blk_ref.shape[0] // C
        state = st_ref[...]
        for sub in (range(n_sub) if forward else reversed(range(n_sub))):
            state = chunk(blk_ref, state, lgr, forward, out_ref, sub * C)
        st_ref[...] = state

    def chunk(blk_ref, state, lgr, forward, out_ref, r0):
        q = blk_ref[r0:r0 + C, 0:W].astype(F32)
        kf = blk_ref[r0:r0 + C, W:2 * W].astype(F32) * (HEAD_DIM ** -0.5)
        v = blk_ref[r0:r0 + C, 2 * W:3 * W]
        if forward:
            zeta = jnp.exp(lgr * (C - 1.0 - pos))
            xi = jnp.exp(lgr * (pos + 1.0))
        else:
            zeta = jnp.exp(lgr * pos)
            xi = jnp.exp(lgr * (C - pos))
        out = jnp.dot((q * xi).astype(BF16), state.astype(BF16), preferred_element_type=F32)
        if forward:
            qexp = jnp.concatenate([jnp.where(lane_head == h, q, 0.0) for h in range(RET_HEADS)],
                                   axis=0).astype(BF16)
            a = lax.dot_general(qexp, kf.astype(BF16), (((1,), (1,)), ((), ())),
                                preferred_element_type=F32)
            p = (a * dec[...]).astype(BF16)
            full = jnp.dot(p, v, preferred_element_type=F32)
            for h in range(RET_HEADS):
                out = out + jnp.where(lane_head == h, full[h * C:(h + 1) * C], 0.0)
        out_ref[r0:r0 + C, :] = out
        upd = lax.dot_general((kf * zeta).astype(BF16), v, (((0,), (0,)), ((), ())),
                              preferred_element_type=F32)
        return jnp.where(same_head, state * jnp.exp(lgr * float(C)) + upd, 0.0)

    direction(blk_f_ref, st_f, lgr_f_ref[...], True, of_ref)
    direction(blk_b_ref, st_b, lgr_b_ref[...], False, ob_ref)

    @pl.when(n == pl.num_programs(1) - 1)
    def _():
        sf_ref[...] = st_f[...]
        sb_ref[...] = st_b[...]


def _retention(ret, log_gamma, s0f, s0b):
    B, T, _ = ret.shape
    C = RET_CHUNK
    tm = min(RET_TILE, T)
    nc = T // tm
    lgc = lambda d: jnp.repeat(log_gamma[d], C).reshape(RET_HEADS * C, 1)
    lgr = lambda d: jnp.repeat(log_gamma[d], HEAD_DIM).reshape(1, RET_WIDTH)
    const = lambda shape: pl.BlockSpec(shape, lambda b, n: (0,) * len(shape))
    st_spec = pl.BlockSpec((None, RET_WIDTH, RET_WIDTH), lambda b, n: (b, 0, 0))
    return pl.pallas_call(
        _ret_kernel,
        out_shape=(jax.ShapeDtypeStruct((B, T, RET_WIDTH), F32),
                   jax.ShapeDtypeStruct((B, T, RET_WIDTH), F32),
                   jax.ShapeDtypeStruct((B, RET_WIDTH, RET_WIDTH), F32),
                   jax.ShapeDtypeStruct((B, RET_WIDTH, RET_WIDTH), F32)),
        grid=(B, nc),
        in_specs=[const((RET_HEADS * C, 1)), const((RET_HEADS * C, 1)),
                  const((1, RET_WIDTH)), const((1, RET_WIDTH)), st_spec, st_spec,
                  pl.BlockSpec((None, tm, RET_PROJ_WIDTH), lambda b, n: (b, n, 0)),
                  pl.BlockSpec((None, tm, RET_PROJ_WIDTH), lambda b, n: (b, nc - 1 - n, 0))],
        out_specs=(pl.BlockSpec((None, tm, RET_WIDTH), lambda b, n: (b, n, 0)),
                   pl.BlockSpec((None, tm, RET_WIDTH), lambda b, n: (b, nc - 1 - n, 0)),
                   st_spec, st_spec),
        scratch_shapes=[pltpu.VMEM((RET_WIDTH, RET_WIDTH), F32), pltpu.VMEM((RET_WIDTH, RET_WIDTH), F32),
                        pltpu.VMEM((RET_HEADS * C, C), F32)],
        compiler_params=_cparams(("parallel", "arbitrary")),
        name="retention",
    )(lgc(0), lgc(1), lgr(0), lgr(1), s0f, s0b, ret, ret)


def _init_pool_bands(band_ref):
    tok = lax.broadcasted_iota(I32, (POOL_TILE, POOL_TILE + 2 * POOL_HALO), 0)
    src = lax.broadcasted_iota(I32, (POOL_TILE, POOL_TILE + 2 * POOL_HALO), 1) - POOL_HALO
    for gi, w in enumerate(POOL_WINDOWS):
        inside = (src >= tok - w // 2) & (src < tok + w // 2)
        band_ref[gi] = jnp.where(inside, 1.0, 0.0).astype(BF16)


def _pool_tile(prev_ref, cur_ref, next_ref, w_ref, scale_ref, band_ref, seq_len):
    tm = cur_ref.shape[0]
    i = pl.program_id(1)
    prev = jnp.where(i > 0, prev_ref[...], jnp.zeros_like(prev_ref))
    nxt = jnp.where(i < pl.num_programs(1) - 1, next_ref[...], jnp.zeros_like(next_ref))
    ext = jnp.concatenate([prev, cur_ref[...], nxt], axis=0)
    sub = min(POOL_TILE, tm)
    lane_group = jnp.right_shift(lax.broadcasted_iota(I32, (sub, POOL_WIDTH), 1), HEAD_SHIFT)
    parts = []
    for r0 in range(0, tm, sub):
        window = ext[r0:r0 + sub + 2 * POOL_HALO]
        cur = cur_ref[r0:r0 + sub, :].astype(F32)
        tcol = i * tm + r0 + lax.broadcasted_iota(I32, (sub, 1), 0)
        mixed = jnp.zeros((sub, POOL_WIDTH), F32)
        for gi, w in enumerate(POOL_WINDOWS):
            total = jnp.dot(band_ref[gi], window, preferred_element_type=F32)
            cnt = (jnp.minimum(tcol + w // 2, seq_len) - jnp.maximum(tcol - w // 2, 0)).astype(F32)
            mixed = mixed + jnp.where(lane_group == gi, total / cnt - cur, 0.0)
        y = jnp.dot(mixed.astype(BF16), w_ref[...], preferred_element_type=F32)
        parts.append((y * scale_ref[...]).astype(BF16))
    return jnp.concatenate(parts, axis=0)


def _head_mean(x, avg):
    hi = x.astype(BF16)
    lo = (x - hi.astype(F32)).astype(BF16)
    return (jnp.dot(hi, avg, preferred_element_type=F32) + jnp.dot(lo, avg, preferred_element_type=F32))


def _mix_tile(x_ref, attn_ref, of_ref, ob_ref, gate_ref, pp_prev_ref, pp_ref, pp_next_ref, pool_w_ref,
              pool_scale_ref, avg_ref, w_ref, npost_ref, g1_ref, band_ref, seq_len):
    pool = _pool_tile(pp_prev_ref, pp_ref, pp_next_ref, pool_w_ref, pool_scale_ref, band_ref, seq_len)
    o = of_ref[...] + ob_ref[...]
    avg = avg_ref[...]
    mu = _head_mean(o, avg)
    cen = o - mu
    var = _head_mean(cen * cen, avg)
    y_ret = (_silu(gate_ref[...].astype(F32)) * (cen * lax.rsqrt(var + NORM_EPS))).astype(BF16)
    mx = (lax.dot_general(attn_ref[...], w_ref[0:ATTN_WIDTH, :], (((0,), (0,)), ((), ())),
                          preferred_element_type=F32)
          + jnp.dot(y_ret, w_ref[ATTN_WIDTH:ATTN_WIDTH + RET_WIDTH, :], preferred_element_type=F32)
          + jnp.dot(pool, w_ref[ATTN_WIDTH + RET_WIDTH:, :], preferred_element_type=F32))
    return x_ref[...] + g1_ref[...] * _rms(mx, npost_ref[...])


N_MIX_INPUTS = 14
POOL_BANDS = pltpu.VMEM((len(POOL_WINDOWS), POOL_TILE, POOL_TILE + 2 * POOL_HALO), BF16)


def _mix_inputs(mix, tm, layer, ctx):
    x, attn, of, ob, ret, pp, pool_w, pool_scale, avg_bf16, w_out_bf16, npost, modr = mix
    T = x.shape[1]
    r = tm // POOL_HALO
    last = T // POOL_HALO - 1
    tok = lambda w: pl.BlockSpec((None, tm, w), lambda b, i: (b, i, 0))
    specs = [tok(D_MODEL), pl.BlockSpec((None, ATTN_WIDTH, tm), lambda b, i: (b, 0, i)),
             tok(RET_WIDTH), tok(RET_WIDTH),
             pl.BlockSpec((None, tm, RET_WIDTH), lambda b, i: (b, i, 3)),
             pl.BlockSpec((None, POOL_HALO, POOL_WIDTH), lambda b, i: (b, jnp.maximum(i * r - 1, 0), 0)),
             tok(POOL_WIDTH),
             pl.BlockSpec((None, POOL_HALO, POOL_WIDTH), lambda b, i: (b, jnp.minimum((i + 1) * r, last), 0)),
             pl.BlockSpec((POOL_WIDTH, POOL_WIDTH), lambda b, i: (0, 0)), _row_spec(POOL_WIDTH),
             pl.BlockSpec((RET_WIDTH, RET_WIDTH), lambda b, i: (0, 0)),
             pl.BlockSpec((D_MODEL, D_MODEL), lambda b, i: (0, 0)),
             _row_spec(D_MODEL), _mod_spec(layer, 2, ctx)]
    args = [x, attn, of, ob, ret, pp, pp, pp, pool_w, pool_scale.reshape(1, POOL_WIDTH), avg_bf16, w_out_bf16,
            npost.reshape(1, D_MODEL), modr]
    return args, specs


def _swiglu_tile(h, w1_ref, w3_ref, w2_ref):
    acc = jnp.zeros((h.shape[0], D_MODEL), F32)
    for f in range(0, D_FF, FF_CHUNK):
        a = jnp.dot(h, w1_ref[:, f:f + FF_CHUNK], preferred_element_type=F32)
        b = jnp.dot(h, w3_ref[:, f:f + FF_CHUNK], preferred_element_type=F32)
        u = (_silu(a) * b).astype(BF16)
        acc = acc + jnp.dot(u, w2_ref[f:f + FF_CHUNK, :], preferred_element_type=F32)
    return acc


def _mix_ffn_kernel(*refs, seq_len):
    gpre_ref, sc_ref, sh_ref, w1_ref, w3_ref, w2_ref, npost_ref, g2_ref, o_ref, band_ref = refs[N_MIX_INPUTS:]

    @pl.when(pl.program_id(1) == 0)
    def _():
        _init_pool_bands(band_ref)

    x = _mix_tile(*refs[:N_MIX_INPUTS], band_ref, seq_len)
    h = (_rms(x, gpre_ref[...]) * (1.0 + sc_ref[...]) + sh_ref[...]).astype(BF16)
    y = _swiglu_tile(h, w1_ref, w3_ref, w2_ref)
    o_ref[...] = x + g2_ref[...] * _rms(y, npost_ref[...])


def _mix_ffn(mix, gpre, npost, w1, w3, w2, layer, ctx):
    x, modr = mix[0], mix[-1]
    B, T, _ = x.shape
    tm = min(TOKEN_TILE, T)
    args, specs = _mix_inputs(mix, tm, layer, ctx)
    wspec = lambda shape: pl.BlockSpec(shape, lambda b, i: (0, 0), pipeline_mode=pl.Buffered(1))
    return pl.pallas_call(
        functools.partial(_mix_ffn_kernel, seq_len=T),
        out_shape=jax.ShapeDtypeStruct((B, T, D_MODEL), F32),
        grid=(B, T // tm),
        in_specs=specs + [_row_spec(D_MODEL), _mod_spec(layer, 4, ctx), _mod_spec(layer, 3, ctx),
                          wspec((D_MODEL, D_FF)), wspec((D_MODEL, D_FF)), wspec((D_FF, D_MODEL)),
                          _row_spec(D_MODEL), _mod_spec(layer, 5, ctx)],
        out_specs=pl.BlockSpec((None, tm, D_MODEL), lambda b, i: (b, i, 0)),
        scratch_shapes=[POOL_BANDS],
        compiler_params=_cparams(("parallel", "arbitrary"), VMEM_LIMIT),
        name="mix_ffn",
    )(*args, gpre.reshape(1, D_MODEL), modr, modr, w1, w3, w2, npost.reshape(1, D_MODEL), modr)


def _mix_route_kernel(*refs, seq_len):
    (gpre_ref, sc_ref, sh_ref, rt_ref, x_ref, h_ref, idx_ref, gate_ref, rank_ref, cnt_ref,
     run_ref, upper_ref, band_ref) = refs[N_MIX_INPUTS:]
    tm = x_ref.shape[0]
    first = (pl.program_id(0) == 0) & (pl.program_id(1) == 0)

    @pl.when(first)
    def _():
        run_ref[...] = jnp.zeros_like(run_ref)
        earlier = lax.broadcasted_iota(I32, (tm, tm), 0) < lax.broadcasted_iota(I32, (tm, tm), 1)
        upper_ref[...] = jnp.where(earlier, 1.0, 0.0).astype(BF16)
        _init_pool_bands(band_ref)

    x = _mix_tile(*refs[:N_MIX_INPUTS], band_ref, seq_len)
    x_ref[...] = x
    h = _rms(x, gpre_ref[...]) * (1.0 + sc_ref[...]) + sh_ref[...]
    h_ref[...] = h
    def split(v):
        hi = v.astype(BF16)
        return hi, (v - hi.astype(F32)).astype(BF16)

    nt_dot = lambda a, b: lax.dot_general(a, b, (((1,), (1,)), ((), ())), preferred_element_type=F32)
    r_hi, r_lo = split(rt_ref[...])
    h_hi, h_lo = split(h)
    by_h_hi = nt_dot(jnp.concatenate([r_hi, r_lo], axis=0), h_hi)
    logits = by_h_hi[0:N_EXPERTS] + by_h_hi[N_EXPERTS:] + nt_dot(r_hi, h_lo)
    eid = lax.broadcasted_iota(I32, (N_EXPERTS, tm), 0).astype(F32)
    m1 = jnp.max(logits, axis=0, keepdims=True)
    i1 = jnp.min(jnp.where(logits == m1, eid, float(N_EXPERTS)), axis=0, keepdims=True)
    oh1 = eid == i1
    rest = jnp.where(oh1, -jnp.inf, logits)
    m2 = jnp.max(rest, axis=0, keepdims=True)
    i2 = jnp.min(jnp.where(rest == m2, eid, float(N_EXPERTS)), axis=0, keepdims=True)
    oh2 = eid == i2
    e2 = jnp.exp(m2 - m1)
    gate_ref[0:1, :] = 1.0 / (1.0 + e2)
    gate_ref[1:2, :] = e2 / (1.0 + e2)
    idx_ref[0:1, :] = i1.astype(I32)
    idx_ref[1:2, :] = i2.astype(I32)
    upper = upper_ref[...]
    f1 = jnp.where(oh1, 1.0, 0.0)
    f2 = jnp.where(oh2, 1.0, 0.0)
    before1 = jnp.dot(f1.astype(BF16), upper, preferred_element_type=F32)
    before2 = jnp.dot(f2.astype(BF16), upper, preferred_element_type=F32)
    cnt1 = jnp.sum(f1, axis=1, keepdims=True)
    cnt2 = jnp.sum(f2, axis=1, keepdims=True)
    run = run_ref[:, 0:1]
    rank_ref[0:1, :] = jnp.sum(f1 * (run + before1), axis=0, keepdims=True).astype(I32)
    rank_ref[1:2, :] = jnp.sum(f2 * (run + cnt1 + before2), axis=0, keepdims=True).astype(I32)
    run_new = run_ref[...] + cnt1 + cnt2
    run_ref[...] = run_new
    cnt_ref[...] = run_new


def _mix_route(mix, gpre, router_t, layer):
    x, modr = mix[0], mix[-1]
    B, T, _ = x.shape
    tm = min(TOKEN_TILE, T)
    nt = T // tm
    args, specs = _mix_inputs(mix, tm, layer, False)
    tok = pl.BlockSpec((None, tm, D_MODEL), lambda b, i: (b, i, 0))
    lane = pl.BlockSpec((2, tm), lambda b, i: (0, b * nt + i))
    return pl.pallas_call(
        functools.partial(_mix_route_kernel, seq_len=T),
        out_shape=(jax.ShapeDtypeStruct((B, T, D_MODEL), F32),
                   jax.ShapeDtypeStruct((B, T, D_MODEL), F32),
                   jax.ShapeDtypeStruct((2, B * T), I32),
                   jax.ShapeDtypeStruct((2, B * T), F32),
                   jax.ShapeDtypeStruct((2, B * T), I32),
                   jax.ShapeDtypeStruct((N_EXPERTS, LANES), F32)),
        grid=(B, nt),
        in_specs=specs + [_row_spec(D_MODEL), _mod_spec(layer, 4, False), _mod_spec(layer, 3, False),
                          pl.BlockSpec((N_EXPERTS, D_MODEL), lambda b, i: (0, 0))],
        out_specs=(tok, tok, lane, lane, lane,
                   pl.BlockSpec((N_EXPERTS, LANES), lambda b, i: (0, 0))),
        scratch_shapes=[pltpu.VMEM((N_EXPERTS, LANES), F32), pltpu.VMEM((tm, tm), BF16), POOL_BANDS],
        compiler_params=_cparams(("arbitrary", "arbitrary"), VMEM_LIMIT),
        name="mix_route",
    )(*args, gpre.reshape(1, D_MODEL), modr, modr, router_t)


def _dispatch_kernel(zero_tiles_ref, slot_ref, h_ref, hs_ref, zero_ref, sem, zero_sem):
    tm = h_ref.shape[0]

    @pl.when(pl.program_id(0) == 0)
    def _():
        zero_ref[...] = jnp.zeros_like(zero_ref)
        for j in range(2 * N_EXPERTS):
            start = pl.multiple_of(zero_tiles_ref[j] * EXPERT_TILE, EXPERT_TILE)
            fill = pltpu.make_async_copy(zero_ref, hs_ref.at[pl.ds(start, EXPERT_TILE)], zero_sem)
            fill.start()
            fill.wait()

    def body(i, carry):
        for u in range(DMA_UNROLL):
            r = pl.multiple_of(i * DMA_UNROLL, DMA_UNROLL) + u
            for k in range(2):
                s = slot_ref[2 * r + k]
                pltpu.make_async_copy(h_ref.at[pl.ds(r, 1)], hs_ref.at[pl.ds(s, 1)], sem).start(priority=k)
        return carry

    lax.fori_loop(0, tm // DMA_UNROLL, body, 0)
    for _ in range(2):
        pltpu.make_async_copy(h_ref, hs_ref.at[pl.ds(0, tm)], sem).wait()


def _dispatch(h, slot_flat, zero_tiles, n_slots):
    N = h.shape[0]
    tm = min(TOKEN_TILE, N)
    grid_spec = pltpu.PrefetchScalarGridSpec(
        num_scalar_prefetch=1,
        grid=(N // tm,),
        in_specs=[pl.BlockSpec((2 * tm,), lambda i, zt: (i,), memory_space=pltpu.SMEM),
                  pl.BlockSpec((tm, D_MODEL), lambda i, zt: (i, 0))],
        out_specs=pl.BlockSpec(memory_space=pl.ANY),
        scratch_shapes=[pltpu.VMEM((EXPERT_TILE, D_MODEL), F32),
                        pltpu.SemaphoreType.DMA(()), pltpu.SemaphoreType.DMA(())],
    )
    return pl.pallas_call(
        _dispatch_kernel,
        out_shape=jax.ShapeDtypeStruct((n_slots, D_MODEL), F32),
        grid_spec=grid_spec,
        compiler_params=_cparams(("arbitrary",)),
        name="moe_dispatch",
    )(zero_tiles, slot_flat, h)


def _expert_kernel(te_ref, tv_ref, h_ref, w1_ref, w3_ref, w2_ref, y_ref):
    t = pl.program_id(0)

    @pl.when(tv_ref[t] == 1)
    def _():
        y_ref[...] = _swiglu_tile(h_ref[...].astype(BF16), w1_ref, w3_ref, w2_ref)

    @pl.when(tv_ref[t] == 0)
    def _():
        y_ref[...] = jnp.zeros_like(y_ref)


def _experts(hs, tile_expert, tile_valid, w1, w3, w2):
    n_slots = hs.shape[0]
    tm = EXPERT_TILE
    grid_spec = pltpu.PrefetchScalarGridSpec(
        num_scalar_prefetch=2,
        grid=(n_slots // tm,),
        in_specs=[pl.BlockSpec((tm, D_MODEL), lambda t, te, tv: (t, 0)),
                  pl.BlockSpec((None, D_MODEL, D_FF), lambda t, te, tv: (te[t], 0, 0)),
                  pl.BlockSpec((None, D_MODEL, D_FF), lambda t, te, tv: (te[t], 0, 0)),
                  pl.BlockSpec((None, D_FF, D_MODEL), lambda t, te, tv: (te[t], 0, 0))],
        out_specs=pl.BlockSpec((tm, D_MODEL), lambda t, te, tv: (t, 0)),
    )
    return pl.pallas_call(
        _expert_kernel,
        out_shape=jax.ShapeDtypeStruct((n_slots, D_MODEL), F32),
        grid_spec=grid_spec,
        compiler_params=_cparams(("arbitrary",), VMEM_LIMIT),
        name="moe_experts",
    )(tile_expert, tile_valid, hs, w1, w3, w2)


def _combine_kernel(slot_ref, gate_ref, x_ref, npost_ref, g2_ref, ys_ref, o_ref, buf0, buf1, sem):
    tm = x_ref.shape[0]
    bufs = (buf0, buf1)

    def body(i, carry):
        for u in range(DMA_UNROLL):
            r = pl.multiple_of(i * DMA_UNROLL, DMA_UNROLL) + u
            for k in range(2):
                s = slot_ref[2 * r + k]
                pltpu.make_async_copy(ys_ref.at[pl.ds(s, 1)], bufs[k].at[pl.ds(r, 1)], sem).start(priority=k)
        return carry

    lax.fori_loop(0, tm // DMA_UNROLL, body, 0)
    for k in range(2):
        pltpu.make_async_copy(ys_ref.at[pl.ds(0, tm)], bufs[k], sem).wait()
    y = gate_ref[:, 0:1] * buf0[...] + gate_ref[:, 1:2] * buf1[...]
    o_ref[...] = x_ref[...] + g2_ref[...] * _rms(y, npost_ref[...])


def _combine(x, ys, slot_flat, gate_tok, npost, modr, layer):
    B, T, _ = x.shape
    tm = min(TOKEN_TILE, T)
    nt = T // tm
    tok = pl.BlockSpec((None, tm, D_MODEL), lambda b, i: (b, i, 0))
    return pl.pallas_call(
        _combine_kernel,
        out_shape=jax.ShapeDtypeStruct((B, T, D_MODEL), F32),
        grid=(B, nt),
        in_specs=[pl.BlockSpec((2 * tm,), lambda b, i: (b * nt + i,), memory_space=pltpu.SMEM),
                  pl.BlockSpec((tm, 2), lambda b, i: (b * nt + i, 0)),
                  tok, _row_spec(D_MODEL), _mod_spec(layer, 5, False),
                  pl.BlockSpec(memory_space=pl.ANY)],
        out_specs=tok,
        scratch_shapes=[pltpu.VMEM((tm, D_MODEL), F32), pltpu.VMEM((tm, D_MODEL), F32),
                        pltpu.SemaphoreType.DMA(())],
        compiler_params=_cparams(("arbitrary", "arbitrary")),
        name="moe_combine",
    )(slot_flat, gate_tok, x, npost.reshape(1, D_MODEL), modr, ys)


def _mix_moe_ffn(mix, gpre, npost, router, w1, w3, w2, layer):
    modr = mix[-1]
    B, T, _ = mix[0].shape
    N = B * T
    x, h, idx, gate, rank, cnt = _mix_route(mix, gpre, router.T, layer)
    n_slots = 2 * N + N_EXPERTS * EXPERT_TILE
    n_tiles = n_slots // EXPERT_TILE
    counts = cnt[:, 0].astype(I32)
    padded = ((counts + EXPERT_TILE - 1) // EXPERT_TILE) * EXPERT_TILE
    ends = jnp.cumsum(padded)
    starts = ends - padded
    slot = rank
    for e in range(N_EXPERTS):
        slot = slot + jnp.where(idx == e, starts[e], 0)
    slot_flat = slot.T.reshape(2 * N)
    tile_start = jnp.arange(n_tiles, dtype=I32) * EXPERT_TILE
    tile_valid = (tile_start < ends[-1]).astype(I32)
    tile_expert = jnp.minimum(jnp.sum((tile_start[:, None] >= ends[None, :]).astype(I32), axis=1), N_EXPERTS - 1)
    last_expert = jnp.max(jnp.where(tile_valid == 1, tile_expert, 0))
    tile_expert = jnp.where(tile_valid == 1, tile_expert, last_expert)
    last_tile = jnp.where(padded > 0, ends // EXPERT_TILE - 1, n_tiles - 1)
    tail_tile = jnp.minimum(ends[-1] // EXPERT_TILE + jnp.arange(N_EXPERTS, dtype=I32), n_tiles - 1)
    zero_tiles = jnp.concatenate([last_tile, tail_tile]).astype(I32)
    hs = _dispatch(h.reshape(N, D_MODEL), slot_flat, zero_tiles, n_slots)
    ys = _experts(hs, tile_expert, tile_valid, w1, w3, w2)
    return _combine(x, ys, slot_flat, gate.T, npost, modr, layer)


def kernel(x, c, ctx, c_ctx, w_mod, b_mod, norm_pre_mix, norm_post_mix, norm_pre_ffn, norm_post_ffn, w_in, w_out, q_norm, k_norm, ret_decay_logit, pool_w, pool_scale, ffn_w1, ffn_w3, ffn_w2, moe_router, moe_w1, moe_w3, moe_w2):
    B, T, _ = x.shape
    assert B == CTX_MOD_ROW and x.shape[2] == D_MODEL
    cvec = jnp.zeros((MOD_ROWS, D_MODEL), F32).at[0:B].set(c).at[CTX_MOD_ROW].set(c_ctx)
    modr = _modulation(cvec, w_mod, b_mod)
    cos_t, sin_t = _rope_tables(T)
    avg = jnp.kron(jnp.eye(RET_HEADS, dtype=F32), jnp.full((HEAD_DIM, HEAD_DIM), 1.0 / HEAD_DIM, F32)).astype(BF16)
    zero_state = jnp.zeros((B, RET_WIDTH, RET_WIDTH), F32)
    xc = ctx
    for i in range(DEPTH):
        need_ctx = i < DEPTH - 1
        w_in_b = w_in[i].astype(BF16)
        w_out_b = w_out[i].astype(BF16)
        pool_bd = jax.scipy.linalg.block_diag(*[pool_w[i, g] for g in range(len(POOL_WINDOWS))]).astype(BF16)
        log_gamma = jax.nn.log_sigmoid(ret_decay_logit[i].astype(F32))

        qkv_c, ret_c, pp_c = _in_projection(xc, modr, norm_pre_mix[i], w_in_b, i, True)
        qt_c, kn_c, vt_c, qn_c, st_c = _attn_prep(qkv_c, q_norm[i], k_norm[i], cos_t, sin_t, False)
        of_c, ob_c, s_fwd, s_bwd = _retention(ret_c, log_gamma, zero_state, zero_state)

        qkv_x, ret_x, pp_x = _in_projection(x, modr, norm_pre_mix[i], w_in_b, i, False)
        qt_x, kn_x, vt_x, qn_x, st_x = _attn_prep(qkv_x, q_norm[i], k_norm[i], cos_t, sin_t, True)
        keys_x = [(kn_x, vt_x), (kn_c, vt_c)]
        attn_x = _attention(qt_x, _softmax_shift(qn_x, st_x, [st_x, st_c], qt_x, keys_x), keys_x)
        of_x, ob_x, _, _ = _retention(ret_x, log_gamma, s_fwd, s_bwd)
        mix_x = (x, attn_x, of_x, ob_x, ret_x, pp_x, pool_bd, pool_scale[i], avg, w_out_b, norm_post_mix[i], modr)

        j = i // 2
        if i % 2 == 0:
            w1, w3, w2 = ffn_w1[j].astype(BF16), ffn_w3[j].astype(BF16), ffn_w2[j].astype(BF16)
            x = _mix_ffn(mix_x, norm_pre_ffn[i], norm_post_ffn[i], w1, w3, w2, i, False)
            if need_ctx:
                keys_c = [(kn_c, vt_c)]
                attn_c = _attention(qt_c, _softmax_shift(qn_c, st_c, [st_c], qt_c, keys_c), keys_c)
                mix_c = (xc, attn_c, of_c, ob_c, ret_c, pp_c, pool_bd, pool_scale[i], avg, w_out_b,
                         norm_post_mix[i], modr)
                xc = _mix_ffn(mix_c, norm_pre_ffn[i], norm_post_ffn[i], w1, w3, w2, i, True)
        else:
            assert not need_ctx
            x = _mix_moe_ffn(mix_x, norm_pre_ffn[i], norm_post_ffn[i], moe_router[j], moe_w1[j].astype(BF16),
                             moe_w3[j].astype(BF16), moe_w2[j].astype(BF16), i)
    return x
```

```python
import functools

import jax
import jax.numpy as jnp
from jax import lax
from jax.experimental import pallas as pl
from jax.experimental.pallas import tpu as pltpu

F32 = jnp.float32
BF16 = jnp.bfloat16
I32 = jnp.int32

D_MODEL = 1024
GRID_W = 64
HEAD_DIM = 64
ATTN_WIDTH = 512
KV_HEADS = 2
ATTN_GROUP = 4
KV_WIDTH = 128
RET_WIDTH = 256
RET_HEADS = 4
POOL_WIDTH = 256
POOL_WINDOWS = (2, 4, 8, 16)
QKV_WIDTH = ATTN_WIDTH + 2 * KV_WIDTH
RET_PROJ_WIDTH = 4 * RET_WIDTH
IN_WIDTH = QKV_WIDTH + RET_PROJ_WIDTH + POOL_WIDTH
HEAD_SHIFT = HEAD_DIM.bit_length() - 1
LANES = 128
RET_CHUNK = 128
ROPE_THETA = 10000.0
D_FF = 2816
N_EXPERTS = 8
NORM_EPS = 1e-6
DEPTH = 2

TOKEN_TILE = 1024
ATTN_Q_TILE = 512
ATTN_K_CHUNK = 2048
RET_TILE = 512
POOL_TILE = 256
POOL_HALO = 16
FF_CHUNK = 256
EXPERT_TILE = 512
DMA_UNROLL = 8
BF16_SUBLANES = 16
MXU_DIM = 256
K_COLS = MXU_DIM
KEY_BLOCK = MXU_DIM
ATTN_UNROLL = 9
N_MXU = 2
SCORE_REG, VALUE_REG = 1, 0
SCORE_BASE = 64
F32_SUBLANES = 8
VT_ROWS = HEAD_DIM + F32_SUBLANES
MAX_BOUND_SHIFT = 40.0
SHIFT_MARGIN = 1.01
MOD_ROWS = 8
CTX_MOD_ROW = 2
MOD_CHUNKS = 6
VMEM_LIMIT = 56 * 1024 * 1024


def _cparams(sem, vmem=None):
    return pltpu.CompilerParams(dimension_semantics=sem, vmem_limit_bytes=vmem)


def _rms(x, gain):
    ms = jnp.mean(x * x, axis=-1, keepdims=True)
    return x * lax.rsqrt(ms + NORM_EPS) * gain


def _silu(x):
    return x * jax.nn.sigmoid(x)


def _mod_kernel(c_ref, w_ref, b_ref, o_ref):
    s = _silu(c_ref[...])
    o_ref[...] = jnp.dot(s, w_ref[...], precision=lax.Precision.HIGHEST,
                         preferred_element_type=F32) + b_ref[...]


def _modulation(cvec, w_mod, b_mod):
    out = pl.pallas_call(
        _mod_kernel,
        out_shape=jax.ShapeDtypeStruct((DEPTH, MOD_ROWS, MOD_CHUNKS * D_MODEL), F32),
        grid=(DEPTH, MOD_CHUNKS),
        in_specs=[
            pl.BlockSpec((MOD_ROWS, D_MODEL), lambda l, j: (0, 0)),
            pl.BlockSpec((None, D_MODEL, D_MODEL), lambda l, j: (l, 0, j)),
            pl.BlockSpec((None, 1, D_MODEL), lambda l, j: (l, 0, j)),
        ],
        out_specs=pl.BlockSpec((None, MOD_ROWS, D_MODEL), lambda l, j: (l, 0, j)),
        compiler_params=_cparams(("parallel", "parallel")),
        name="modulation",
    )(cvec, w_mod, b_mod.reshape(DEPTH, 1, MOD_CHUNKS * D_MODEL))
    return out.reshape(DEPTH * MOD_ROWS * MOD_CHUNKS, 1, D_MODEL)


def _mod_spec(layer, chunk, ctx):
    base = layer * MOD_ROWS * MOD_CHUNKS
    if ctx:
        return pl.BlockSpec((None, 1, D_MODEL), lambda b, i: (base + CTX_MOD_ROW * MOD_CHUNKS + chunk, 0, 0))
    return pl.BlockSpec((None, 1, D_MODEL), lambda b, i: (base + b * MOD_CHUNKS + chunk, 0, 0))


def _row_spec(width):
    return pl.BlockSpec((1, width), lambda b, i: (0, 0))


def _inproj_kernel(x_ref, g_ref, sc_ref, sh_ref, w_ref, qg_ref, kg_ref, cos_ref, sin_ref,
                   qt_ref, kn_ref, vt_ref, qn_ref, stat_ref, ret_ref, pp_ref, *, rope):
    h = _rms(x_ref[...], g_ref[...]) * (1.0 + sc_ref[...]) + sh_ref[...]
    p = jnp.dot(h.astype(BF16), w_ref[...], preferred_element_type=F32)
    ret_ref[...] = p[:, QKV_WIDTH:QKV_WIDTH + RET_PROJ_WIDTH].astype(BF16)
    pp_ref[...] = p[:, QKV_WIDTH + RET_PROJ_WIDTH:].astype(BF16)
    _prep_tile(p[:, :QKV_WIDTH], qg_ref, kg_ref, cos_ref, sin_ref, qt_ref, kn_ref, vt_ref, qn_ref, stat_ref, rope)


def _prep_tile(qkv, qg_ref, kg_ref, cos_ref, sin_ref, qt_ref, kn_ref, vt_ref, qn_ref, stat_ref, rope):
    t = qkv.T

    def norm_rope(blk, gain):
        ms = jnp.mean(blk * blk, axis=0, keepdims=True)
        y = blk * lax.rsqrt(ms + NORM_EPS) * gain
        if rope:
            partner = jnp.concatenate([y[16:32], y[0:16], y[48:64], y[32:48]], axis=0)
            y = y * cos_ref[...] + partner * sin_ref[...]
        return y

    def sq_norm(y):
        return jnp.sum(y * y, axis=0, keepdims=True)

    def row_max(n2):
        return jnp.broadcast_to(jnp.max(n2, axis=1, keepdims=True), (1, LANES))

    q_stats = []
    for h in range(ATTN_WIDTH // HEAD_DIM):
        lo = h * HEAD_DIM
        q = norm_rope(t[lo:lo + HEAD_DIM], qg_ref[...]) * (HEAD_DIM ** -0.5)
        qt_ref[lo:lo + HEAD_DIM, :] = q.astype(BF16)
        n2 = sq_norm(q)
        qn_ref[h // ATTN_GROUP, h % ATTN_GROUP:h % ATTN_GROUP + 1, :] = jnp.sqrt(n2)
        q_stats.append(row_max(n2))
    tm = t.shape[1]
    k_pad = jnp.where(lax.broadcasted_iota(I32, (K_COLS - HEAD_DIM, tm), 0) < 2, 1.0, 0.0)
    v_pad = jnp.where(lax.broadcasted_iota(I32, (VT_ROWS - HEAD_DIM, tm), 0) < 1, 1.0, 0.0)
    k_stats = []
    for kv in range(KV_HEADS):
        lo = ATTN_WIDTH + kv * HEAD_DIM
        k = norm_rope(t[lo:lo + HEAD_DIM], kg_ref[...])
        k_stats.append(row_max(sq_norm(k)))
        kn_ref[kv] = jnp.concatenate([k, k_pad], axis=0).T.astype(BF16)
        lo = ATTN_WIDTH + KV_WIDTH + kv * HEAD_DIM
        vt = jnp.concatenate([t[lo:lo + HEAD_DIM], v_pad], axis=0)
        for j in range(tm // KEY_BLOCK):
            vt_ref[kv, j] = vt[:, j * KEY_BLOCK:(j + 1) * KEY_BLOCK]
    q_group = [functools.reduce(jnp.maximum, q_stats[kv * ATTN_GROUP:(kv + 1) * ATTN_GROUP])
               for kv in range(KV_HEADS)]
    stat_ref[...] = jnp.concatenate(k_stats + q_group + [jnp.zeros((F32_SUBLANES - 2 * KV_HEADS, LANES), F32)], axis=0)


def _in_projection(x, modr, gain, w_in_bf16, q_gain, k_gain, cos_t, sin_t, layer, ctx):
    B, T, _ = x.shape
    tm = min(TOKEN_TILE, T)
    nc = T // tm
    tok = lambda w: pl.BlockSpec((None, tm, w), lambda b, i: (b, i, 0))
    return pl.pallas_call(
        functools.partial(_inproj_kernel, rope=not ctx),
        out_shape=(jax.ShapeDtypeStruct((B, ATTN_WIDTH, T), BF16),
                   jax.ShapeDtypeStruct((B, KV_HEADS, T, K_COLS), BF16),
                   jax.ShapeDtypeStruct((B, KV_HEADS, T // KEY_BLOCK, VT_ROWS, KEY_BLOCK), F32),
                   jax.ShapeDtypeStruct((B, KV_HEADS, ATTN_GROUP, T), F32),
                   jax.ShapeDtypeStruct((B, nc, F32_SUBLANES, LANES), F32),
                   jax.ShapeDtypeStruct((B, T, RET_PROJ_WIDTH), BF16),
                   jax.ShapeDtypeStruct((B, T, POOL_WIDTH), BF16)),
        grid=(B, nc),
        in_specs=[tok(D_MODEL), _row_spec(D_MODEL), _mod_spec(layer, 1, ctx), _mod_spec(layer, 0, ctx),
                  pl.BlockSpec((D_MODEL, IN_WIDTH), lambda b, i: (0, 0)),
                  pl.BlockSpec((HEAD_DIM, 1), lambda b, i: (0, 0)),
                  pl.BlockSpec((HEAD_DIM, 1), lambda b, i: (0, 0)),
                  pl.BlockSpec((HEAD_DIM, tm), lambda b, i: (0, i)),
                  pl.BlockSpec((HEAD_DIM, tm), lambda b, i: (0, i))],
        out_specs=(pl.BlockSpec((None, ATTN_WIDTH, tm), lambda b, i: (b, 0, i)),
                   pl.BlockSpec((None, KV_HEADS, tm, K_COLS), lambda b, i: (b, 0, i, 0)),
                   pl.BlockSpec((None, KV_HEADS, tm // KEY_BLOCK, VT_ROWS, KEY_BLOCK), lambda b, i: (b, 0, i, 0, 0)),
                   pl.BlockSpec((None, KV_HEADS, ATTN_GROUP, tm), lambda b, i: (b, 0, 0, i)),
                   pl.BlockSpec((None, None, F32_SUBLANES, LANES), lambda b, i: (b, i, 0, 0)),
                   tok(RET_PROJ_WIDTH), tok(POOL_WIDTH)),
        compiler_params=_cparams(("parallel", "parallel"), VMEM_LIMIT),
        name="in_projection",
    )(x, gain.reshape(1, D_MODEL), modr, modr, w_in_bf16, q_gain.reshape(HEAD_DIM, 1), k_gain.reshape(HEAD_DIM, 1),
      cos_t, sin_t)


def _rope_tables(T):
    t = jnp.arange(T)
    row = (t // GRID_W).astype(F32)
    col = (t % GRID_W).astype(F32)
    n_freq = HEAD_DIM // 4
    inv = ROPE_THETA ** (-jnp.arange(n_freq, dtype=F32) / n_freq)
    ang_r = row[None, :] * inv[:, None]
    ang_c = col[None, :] * inv[:, None]
    cos_t = jnp.concatenate([jnp.cos(ang_r), jnp.cos(ang_r), jnp.cos(ang_c), jnp.cos(ang_c)], axis=0)
    sin_t = jnp.concatenate([-jnp.sin(ang_r), jnp.sin(ang_r), -jnp.sin(ang_c), jnp.sin(ang_c)], axis=0)
    return cos_t, sin_t


def _score_max_kernel(qt_ref, k_ref, m_ref, qa_ref):
    tq = qt_ref.shape[1]
    nq = ATTN_GROUP * tq
    for g in range(ATTN_GROUP):
        qa_ref[0:HEAD_DIM, g * tq:(g + 1) * tq] = qt_ref[g * HEAD_DIM:(g + 1) * HEAD_DIM, :]
    qa_ref[HEAD_DIM:K_COLS, :] = jnp.zeros((K_COLS - HEAD_DIM, nq), BF16)
    tk = min(ATTN_K_CHUNK, k_ref.shape[0])

    def colmax8(j):
        k = k_ref[pl.ds(pl.multiple_of(j * tk, tk), tk), :]
        s = jnp.dot(k, qa_ref[...], preferred_element_type=F32)
        return jnp.max(s.reshape(tk // 8, 8, nq), axis=0)

    mx = lax.fori_loop(1, k_ref.shape[0] // tk, lambda j, mx: jnp.maximum(mx, colmax8(j)), colmax8(0))
    m = jnp.max(mx, axis=0, keepdims=True)
    for g in range(ATTN_GROUP):
        m_ref[g:g + 1, :] = m[:, g * tq:(g + 1) * tq]


def _score_max(qt, k_all):
    B, _, Tq = qt.shape
    Tk = k_all.shape[2]
    tq = min(ATTN_Q_TILE, Tq)
    return pl.pallas_call(
        _score_max_kernel,
        out_shape=jax.ShapeDtypeStruct((B, KV_HEADS, ATTN_GROUP, Tq), F32),
        grid=(B, KV_HEADS, Tq // tq),
        in_specs=[pl.BlockSpec((None, ATTN_GROUP * HEAD_DIM, tq), lambda b, h, i: (b, h, i)),
                  pl.BlockSpec((None, None, Tk, K_COLS), lambda b, h, i: (b, h, 0, 0))],
        out_specs=pl.BlockSpec((None, None, ATTN_GROUP, tq), lambda b, h, i: (b, h, 0, i)),
        scratch_shapes=[pltpu.VMEM((K_COLS, ATTN_GROUP * tq), BF16)],
        compiler_params=_cparams(("parallel", "parallel", "parallel"), VMEM_LIMIT),
        name="score_max",
    )(qt, k_all)


def _softmax_shift(q_norm, q_stats, k_stats_list, qt, key_sets):
    B = qt.shape[0]
    k2 = functools.reduce(jnp.maximum, [st[:, :, 0:KV_HEADS, 0].max(axis=1) for st in k_stats_list])
    q2 = q_stats[:, :, KV_HEADS:2 * KV_HEADS, 0].max(axis=1)
    use_bound = (jnp.sqrt(k2 * q2) * SHIFT_MARGIN <= MAX_BOUND_SHIFT).reshape(B, KV_HEADS, 1, 1)
    bound = q_norm * (jnp.sqrt(k2) * SHIFT_MARGIN).reshape(B, KV_HEADS, 1, 1)
    exact = lambda: _score_max(qt, jnp.concatenate([k for k, _ in key_sets], axis=2))
    return lax.cond(jnp.all(use_bound), lambda: bound, lambda: jnp.where(use_bound, bound, exact()))


def _attn_kernel(qt_ref, shift_ref, *refs):
    *kv_refs, o_ref, qa_ref = refs
    sources = [(kv_refs[i], kv_refs[i + 1]) for i in range(0, len(kv_refs), 2)]
    counts = [k_ref.shape[0] // KEY_BLOCK for k_ref, _ in sources]
    tq = qt_ref.shape[1]
    n_blocks = sum(counts)
    heads_per_mxu = ATTN_GROUP // N_MXU

    def locate(blk):
        if not isinstance(blk, int):
            return 0, blk
        src = 0
        while blk >= counts[src]:
            blk -= counts[src]
            src += 1
        return src, blk

    row = lax.broadcasted_iota(I32, (BF16_SUBLANES, tq), 0)
    for g in range(ATTN_GROUP):
        m = shift_ref[g:g + 1, :]
        m_hi = m.astype(BF16).astype(F32)
        qa_ref[g, 0:HEAD_DIM, :] = qt_ref[g * HEAD_DIM:(g + 1) * HEAD_DIM, :]
        qa_ref[g, HEAD_DIM:HEAD_DIM + BF16_SUBLANES, :] = jnp.where(
            row == 0, -m_hi, jnp.where(row == 1, m_hi - m, 0.0)).astype(BF16)
        qa_ref[g, HEAD_DIM + BF16_SUBLANES:K_COLS, :] = jnp.zeros((K_COLS - HEAD_DIM - BF16_SUBLANES, tq), BF16)

    def score_addr(slot):
        return SCORE_BASE + slot * (KEY_BLOCK // 4)

    def out_addr(slot):
        return slot * (VT_ROWS // 4)

    def keys(blk):
        src, j = locate(blk)
        return sources[src][0][pl.ds(pl.multiple_of(j * KEY_BLOCK, KEY_BLOCK), KEY_BLOCK), :]

    def stage_q(slot):
        for mxu in range(N_MXU):
            pltpu.matmul_push_rhs(qa_ref[mxu * heads_per_mxu + slot], SCORE_REG, mxu)

    def issue_scores(blk, slot):
        k = keys(blk)
        for mxu in range(N_MXU):
            pltpu.matmul_acc_lhs(score_addr(slot), k, mxu, load_staged_rhs=SCORE_REG)

    def pop_probs(slot):
        return [jnp.exp(pltpu.matmul_pop(score_addr(slot), (KEY_BLOCK, tq), F32, mxu)).astype(BF16)
                for mxu in range(N_MXU)]

    def push_probs(blk, slot, p):
        src, j = locate(blk)
        vt = sources[src][1][j]
        for mxu in range(N_MXU):
            pltpu.matmul_push_rhs(p[mxu], VALUE_REG, mxu)
        return vt

    def unit(blk, slot, stage_next, first):
        if not first:
            p = pop_probs(slot)
        issue_scores(blk, slot)
        if not first:
            vt = push_probs(blk - 1, slot, p)
        if stage_next:
            stage_q((slot + 1) % heads_per_mxu)
        if not first:
            for mxu in range(N_MXU):
                pltpu.matmul_acc_lhs(out_addr(slot), vt, mxu, load_staged_rhs=VALUE_REG)

    def block(blk, last=False, first=False):
        for slot in range(heads_per_mxu):
            unit(blk, slot, stage_next=not (last and slot == heads_per_mxu - 1), first=first)

    stage_q(0)
    block(0, last=n_blocks == 1, first=True)
    if n_blocks > 1:
        n_loop = n_blocks - 2
        assert n_loop + 1 <= counts[0]
        unroll = max([u for u in range(1, ATTN_UNROLL + 1) if n_loop % u == 0] or [1])

        def body(i, carry):
            for u in range(unroll):
                block(1 + i * unroll + u)
            return carry

        if n_loop:
            lax.fori_loop(0, n_loop // unroll, body, 0)
        block(n_blocks - 1, last=True)
    for slot in range(heads_per_mxu):
        vt = push_probs(n_blocks - 1, slot, pop_probs(slot))
        for mxu in range(N_MXU):
            pltpu.matmul_acc_lhs(out_addr(slot), vt, mxu, load_staged_rhs=VALUE_REG)

    outs = []
    for mxu in range(N_MXU):
        for slot in range(heads_per_mxu):
            acc = pltpu.matmul_pop(out_addr(slot), (VT_ROWS, tq), F32, mxu)
            outs.append(acc[0:HEAD_DIM, :] / acc[HEAD_DIM:HEAD_DIM + 1, :])
    o_ref[...] = jnp.concatenate(outs, axis=0).astype(BF16)


def _attention(qt, shift, key_sets):
    B, _, Tq = qt.shape
    tq = MXU_DIM
    in_specs = [pl.BlockSpec((None, ATTN_GROUP * HEAD_DIM, tq), lambda b, h, i: (b, h, i)),
                pl.BlockSpec((None, None, ATTN_GROUP, tq), lambda b, h, i: (b, h, 0, i))]
    args = [qt, shift]
    for k, vt in key_sets:
        Tk = k.shape[2]
        in_specs += [pl.BlockSpec((None, None, Tk, K_COLS), lambda b, h, i: (b, h, 0, 0)),
                     pl.BlockSpec((None, None, Tk // KEY_BLOCK, VT_ROWS, KEY_BLOCK), lambda b, h, i: (b, h, 0, 0, 0))]
        args += [k, vt]
    return pl.pallas_call(
        _attn_kernel,
        out_shape=jax.ShapeDtypeStruct((B, ATTN_WIDTH, Tq), BF16),
        grid=(B, KV_HEADS, Tq // tq),
        in_specs=in_specs,
        out_specs=pl.BlockSpec((None, ATTN_GROUP * HEAD_DIM, tq), lambda b, h, i: (b, h, i)),
        scratch_shapes=[pltpu.VMEM((ATTN_GROUP, K_COLS, tq), BF16)],
        compiler_params=_cparams(("parallel", "parallel", "parallel"), VMEM_LIMIT),
        name="attention",
    )(*args)


def _ret_kernel(lgc_f_ref, lgc_b_ref, lgr_f_ref, lgr_b_ref, s0f_ref, s0b_ref, blk_f_ref, blk_b_ref,
                of_ref, ob_ref, sf_ref, sb_ref, st_f, st_b, dec):
    C = RET_CHUNK
    W = RET_WIDTH
    n = pl.program_id(1)

    @pl.when(n == 0)
    def _():
        st_f[...] = s0f_ref[...]
        st_b[...] = s0b_ref[...]
        c = jnp.bitwise_and(lax.broadcasted_iota(I32, (RET_HEADS * C, C), 0), C - 1)
        m = lax.broadcasted_iota(I32, (RET_HEADS * C, C), 1)
        diff = (c - m).astype(F32)
        dec[...] = (jnp.where(diff >= 0, jnp.exp(lgc_f_ref[...] * jnp.maximum(diff, 0.0)), 0.0)
                    + jnp.where(diff <= 0, jnp.exp(lgc_b_ref[...] * jnp.maximum(-diff, 0.0)), 0.0))

    lane_head = jnp.right_shift(lax.broadcasted_iota(I32, (C, W), 1), HEAD_SHIFT)
    pos = lax.broadcasted_iota(I32, (C, W), 0).astype(F32)
    same_head = (jnp.right_shift(lax.broadcasted_iota(I32, (W, W), 0), HEAD_SHIFT)
                 == jnp.right_shift(lax.broadcasted_iota(I32, (W, W), 1), HEAD_SHIFT))

    def direction(blk_ref, st_ref, lgr, forward, out_ref):
        n_sub = blk_ref.shape[0] // C
        state = st_ref[...]
        for sub in (range(n_sub) if forward else reversed(range(n_sub))):
            state = chunk(blk_ref, state, lgr, forward, out_ref, sub * C)
        st_ref[...] = state

    def chunk(blk_ref, state, lgr, forward, out_ref, r0):
        q = blk_ref[r0:r0 + C, 0:W].astype(F32)
        kf = blk_ref[r0:r0 + C, W:2 * W].astype(F32) * (HEAD_DIM ** -0.5)
        v = blk_ref[r0:r0 + C, 2 * W:3 * W]
        if forward:
            zeta = jnp.exp(lgr * (C - 1.0 - pos))
            xi = jnp.exp(lgr * (pos + 1.0))
        else:
            zeta = jnp.exp(lgr * pos)
            xi = jnp.exp(lgr * (C - pos))
        out = jnp.dot((q * xi).astype(BF16), state.astype(BF16), preferred_element_type=F32)
        if forward:
            qexp = jnp.concatenate([jnp.where(lane_head == h, q, 0.0) for h in range(RET_HEADS)],
                                   axis=0).astype(BF16)
            a = lax.dot_general(qexp, kf.astype(BF16), (((1,), (1,)), ((), ())),
                                preferred_element_type=F32)
            p = (a * dec[...]).astype(BF16)
            full = jnp.dot(p, v, preferred_element_type=F32)
            for h in range(RET_HEADS):
                out = out + jnp.where(lane_head == h, full[h * C:(h + 1) * C], 0.0)
        out_ref[r0:r0 + C, :] = out
        upd = lax.dot_general((kf * zeta).astype(BF16), v, (((0,), (0,)), ((), ())),
                              preferred_element_type=F32)
        return jnp.where(same_head, state * jnp.exp(lgr * float(C)) + upd, 0.0)

    direction(blk_f_ref, st_f, lgr_f_ref[...], True, of_ref)
    direction(blk_b_ref, st_b, lgr_b_ref[...], False, ob_ref)

    @pl.when(n == pl.num_programs(1) - 1)
    def _():
        sf_ref[...] = st_f[...]
        sb_ref[...] = st_b[...]


def _retention(ret, log_gamma, s0f, s0b):
    B, T, _ = ret.shape
    C = RET_CHUNK
    tm = min(RET_TILE, T)
    nc = T // tm
    lgc = lambda d: jnp.repeat(log_gamma[d], C).reshape(RET_HEADS * C, 1)
    lgr = lambda d: jnp.repeat(log_gamma[d], HEAD_DIM).reshape(1, RET_WIDTH)
    const = lambda shape: pl.BlockSpec(shape, lambda b, n: (0,) * len(shape))
    st_spec = pl.BlockSpec((None, RET_WIDTH, RET_WIDTH), lambda b, n: (b, 0, 0))
    return pl.pallas_call(
        _ret_kernel,
        out_shape=(jax.ShapeDtypeStruct((B, T, RET_WIDTH), F32),
                   jax.ShapeDtypeStruct((B, T, RET_WIDTH), F32),
                   jax.ShapeDtypeStruct((B, RET_WIDTH, RET_WIDTH), F32),
                   jax.ShapeDtypeStruct((B, RET_WIDTH, RET_WIDTH), F32)),
        grid=(B, nc),
        in_specs=[const((RET_HEADS * C, 1)), const((RET_HEADS * C, 1)),
                  const((1, RET_WIDTH)), const((1, RET_WIDTH)), st_spec, st_spec,
                  pl.BlockSpec((None, tm, RET_PROJ_WIDTH), lambda b, n: (b, n, 0)),
                  pl.BlockSpec((None, tm, RET_PROJ_WIDTH), lambda b, n: (b, nc - 1 - n, 0))],
        out_specs=(pl.BlockSpec((None, tm, RET_WIDTH), lambda b, n: (b, n, 0)),
                   pl.BlockSpec((None, tm, RET_WIDTH), lambda b, n: (b, nc - 1 - n, 0)),
                   st_spec, st_spec),
        scratch_shapes=[pltpu.VMEM((RET_WIDTH, RET_WIDTH), F32), pltpu.VMEM((RET_WIDTH, RET_WIDTH), F32),
                        pltpu.VMEM((RET_HEADS * C, C), F32)],
        compiler_params=_cparams(("parallel", "arbitrary")),
        name="retention",
    )(lgc(0), lgc(1), lgr(0), lgr(1), s0f, s0b, ret, ret)


def _init_pool_bands(band_ref):
    tok = lax.broadcasted_iota(I32, (POOL_TILE, POOL_TILE + 2 * POOL_HALO), 0)
    src = lax.broadcasted_iota(I32, (POOL_TILE, POOL_TILE + 2 * POOL_HALO), 1) - POOL_HALO
    for gi, w in enumerate(POOL_WINDOWS):
        inside = (src >= tok - w // 2) & (src < tok + w // 2)
        band_ref[gi] = jnp.where(inside, 1.0, 0.0).astype(BF16)


def _pool_tile(prev_ref, cur_ref, next_ref, w_ref, scale_ref, band_ref, seq_len):
    tm = cur_ref.shape[0]
    i = pl.program_id(1)
    prev = jnp.where(i > 0, prev_ref[...], jnp.zeros_like(prev_ref))
    nxt = jnp.where(i < pl.num_programs(1) - 1, next_ref[...], jnp.zeros_like(next_ref))
    ext = jnp.concatenate([prev, cur_ref[...], nxt], axis=0)
    sub = min(POOL_TILE, tm)
    lane_group = jnp.right_shift(lax.broadcasted_iota(I32, (sub, POOL_WIDTH), 1), HEAD_SHIFT)
    parts = []
    for r0 in range(0, tm, sub):
        window = ext[r0:r0 + sub + 2 * POOL_HALO]
        cur = cur_ref[r0:r0 + sub, :].astype(F32)
        tcol = i * tm + r0 + lax.broadcasted_iota(I32, (sub, 1), 0)
        mixed = jnp.zeros((sub, POOL_WIDTH), F32)
        for gi, w in enumerate(POOL_WINDOWS):
            total = jnp.dot(band_ref[gi], window, preferred_element_type=F32)
            cnt = (jnp.minimum(tcol + w // 2, seq_len) - jnp.maximum(tcol - w // 2, 0)).astype(F32)
            mixed = mixed + jnp.where(lane_group == gi, total / cnt - cur, 0.0)
        y = jnp.dot(mixed.astype(BF16), w_ref[...], preferred_element_type=F32)
        parts.append((y * scale_ref[...]).astype(BF16))
    return jnp.concatenate(parts, axis=0)


def _head_mean(x, avg):
    hi = x.astype(BF16)
    lo = (x - hi.astype(F32)).astype(BF16)
    return (jnp.dot(hi, avg, preferred_element_type=F32) + jnp.dot(lo, avg, preferred_element_type=F32))


def _mix_tile(x_ref, attn_ref, of_ref, ob_ref, gate_ref, pp_prev_ref, pp_ref, pp_next_ref, pool_w_ref,
              pool_scale_ref, avg_ref, w_ref, npost_ref, g1_ref, band_ref, seq_len):
    pool = _pool_tile(pp_prev_ref, pp_ref, pp_next_ref, pool_w_ref, pool_scale_ref, band_ref, seq_len)
    o = of_ref[...] + ob_ref[...]
    avg = avg_ref[...]
    mu = _head_mean(o, avg)
    cen = o - mu
    var = _head_mean(cen * cen, avg)
    y_ret = (_silu(gate_ref[...].astype(F32)) * (cen * lax.rsqrt(var + NORM_EPS))).astype(BF16)
    mx = (lax.dot_general(attn_ref[...], w_ref[0:ATTN_WIDTH, :], (((0,), (0,)), ((), ())),
                          preferred_element_type=F32)
          + jnp.dot(y_ret, w_ref[ATTN_WIDTH:ATTN_WIDTH + RET_WIDTH, :], preferred_element_type=F32)
          + jnp.dot(pool, w_ref[ATTN_WIDTH + RET_WIDTH:, :], preferred_element_type=F32))
    return x_ref[...] + g1_ref[...] * _rms(mx, npost_ref[...])


N_MIX_INPUTS = 14
POOL_BANDS = pltpu.VMEM((len(POOL_WINDOWS), POOL_TILE, POOL_TILE + 2 * POOL_HALO), BF16)


def _mix_inputs(mix, tm, layer, ctx):
    x, attn, of, ob, ret, pp, pool_w, pool_scale, avg_bf16, w_out_bf16, npost, modr = mix
    T = x.shape[1]
    r = tm // POOL_HALO
    last = T // POOL_HALO - 1
    tok = lambda w: pl.BlockSpec((None, tm, w), lambda b, i: (b, i, 0))
    specs = [tok(D_MODEL), pl.BlockSpec((None, ATTN_WIDTH, tm), lambda b, i: (b, 0, i)),
             tok(RET_WIDTH), tok(RET_WIDTH),
             pl.BlockSpec((None, tm, RET_WIDTH), lambda b, i: (b, i, 3)),
             pl.BlockSpec((None, POOL_HALO, POOL_WIDTH), lambda b, i: (b, jnp.maximum(i * r - 1, 0), 0)),
             tok(POOL_WIDTH),
             pl.BlockSpec((None, POOL_HALO, POOL_WIDTH), lambda b, i: (b, jnp.minimum((i + 1) * r, last), 0)),
             pl.BlockSpec((POOL_WIDTH, POOL_WIDTH), lambda b, i: (0, 0)), _row_spec(POOL_WIDTH),
             pl.BlockSpec((RET_WIDTH, RET_WIDTH), lambda b, i: (0, 0)),
             pl.BlockSpec((D_MODEL, D_MODEL), lambda b, i: (0, 0)),
             _row_spec(D_MODEL), _mod_spec(layer, 2, ctx)]
    args = [x, attn, of, ob, ret, pp, pp, pp, pool_w, pool_scale.reshape(1, POOL_WIDTH), avg_bf16, w_out_bf16,
            npost.reshape(1, D_MODEL), modr]
    return args, specs


def _swiglu_tile(h, w1_ref, w3_ref, w2_ref):
    acc = jnp.zeros((h.shape[0], D_MODEL), F32)
    for f in range(0, D_FF, FF_CHUNK):
        a = jnp.dot(h, w1_ref[:, f:f + FF_CHUNK], preferred_element_type=F32)
        b = jnp.dot(h, w3_ref[:, f:f + FF_CHUNK], preferred_element_type=F32)
        u = (_silu(a) * b).astype(BF16)
        acc = acc + jnp.dot(u, w2_ref[f:f + FF_CHUNK, :], preferred_element_type=F32)
    return acc


def _mix_ffn_kernel(*refs, seq_len):
    gpre_ref, sc_ref, sh_ref, w1_ref, w3_ref, w2_ref, npost_ref, g2_ref, o_ref, band_ref = refs[N_MIX_INPUTS:]

    @pl.when(pl.program_id(1) == 0)
    def _():
        _init_pool_bands(band_ref)

    x = _mix_tile(*refs[:N_MIX_INPUTS], band_ref, seq_len)
    h = (_rms(x, gpre_ref[...]) * (1.0 + sc_ref[...]) + sh_ref[...]).astype(BF16)
    y = _swiglu_tile(h, w1_ref, w3_ref, w2_ref)
    o_ref[...] = x + g2_ref[...] * _rms(y, npost_ref[...])


def _mix_ffn(mix, gpre, npost, w1, w3, w2, layer, ctx):
    x, modr = mix[0], mix[-1]
    B, T, _ = x.shape
    tm = min(TOKEN_TILE, T)
    args, specs = _mix_inputs(mix, tm, layer, ctx)
    wspec = lambda shape: pl.BlockSpec(shape, lambda b, i: (0, 0), pipeline_mode=pl.Buffered(1))
    return pl.pallas_call(
        functools.partial(_mix_ffn_kernel, seq_len=T),
        out_shape=jax.ShapeDtypeStruct((B, T, D_MODEL), F32),
        grid=(B, T // tm),
        in_specs=specs + [_row_spec(D_MODEL), _mod_spec(layer, 4, ctx), _mod_spec(layer, 3, ctx),
                          wspec((D_MODEL, D_FF)), wspec((D_MODEL, D_FF)), wspec((D_FF, D_MODEL)),
                          _row_spec(D_MODEL), _mod_spec(layer, 5, ctx)],
        out_specs=pl.BlockSpec((None, tm, D_MODEL), lambda b, i: (b, i, 0)),
        scratch_shapes=[POOL_BANDS],
        compiler_params=_cparams(("parallel", "arbitrary"), VMEM_LIMIT),
        name="mix_ffn",
    )(*args, gpre.reshape(1, D_MODEL), modr, modr, w1, w3, w2, npost.reshape(1, D_MODEL), modr)


def _mix_route_kernel(*refs, seq_len):
    (gpre_ref, sc_ref, sh_ref, rt_ref, x_ref, h_ref, idx_ref, gate_ref, rank_ref, cnt_ref,
     run_ref, upper_ref, band_ref) = refs[N_MIX_INPUTS:]
    tm = x_ref.shape[0]
    first = (pl.program_id(0) == 0) & (pl.program_id(1) == 0)

    @pl.when(first)
    def _():
        run_ref[...] = jnp.zeros_like(run_ref)
        earlier = lax.broadcasted_iota(I32, (tm, tm), 0) < lax.broadcasted_iota(I32, (tm, tm), 1)
        upper_ref[...] = jnp.where(earlier, 1.0, 0.0).astype(BF16)
        _init_pool_bands(band_ref)

    x = _mix_tile(*refs[:N_MIX_INPUTS], band_ref, seq_len)
    x_ref[...] = x
    h = _rms(x, gpre_ref[...]) * (1.0 + sc_ref[...]) + sh_ref[...]
    h_ref[...] = h
    def split(v):
        hi = v.astype(BF16)
        return hi, (v - hi.astype(F32)).astype(BF16)

    nt_dot = lambda a, b: lax.dot_general(a, b, (((1,), (1,)), ((), ())), preferred_element_type=F32)
    r_hi, r_lo = split(rt_ref[...])
    h_hi, h_lo = split(h)
    by_h_hi = nt_dot(jnp.concatenate([r_hi, r_lo], axis=0), h_hi)
    logits = by_h_hi[0:N_EXPERTS] + by_h_hi[N_EXPERTS:] + nt_dot(r_hi, h_lo)
    eid = lax.broadcasted_iota(I32, (N_EXPERTS, tm), 0).astype(F32)
    m1 = jnp.max(logits, axis=0, keepdims=True)
    i1 = jnp.min(jnp.where(logits == m1, eid, float(N_EXPERTS)), axis=0, keepdims=True)
    oh1 = eid == i1
    rest = jnp.where(oh1, -jnp.inf, logits)
    m2 = jnp.max(rest, axis=0, keepdims=True)
    i2 = jnp.min(jnp.where(rest == m2, eid, float(N_EXPERTS)), axis=0, keepdims=True)
    oh2 = eid == i2
    e2 = jnp.exp(m2 - m1)
    gate_ref[0:1, :] = 1.0 / (1.0 + e2)
    gate_ref[1:2, :] = e2 / (1.0 + e2)
    idx_ref[0:1, :] = i1.astype(I32)
    idx_ref[1:2, :] = i2.astype(I32)
    upper = upper_ref[...]
    f1 = jnp.where(oh1, 1.0, 0.0)
    f2 = jnp.where(oh2, 1.0, 0.0)
    before1 = jnp.dot(f1.astype(BF16), upper, preferred_element_type=F32)
    before2 = jnp.dot(f2.astype(BF16), upper, preferred_element_type=F32)
    cnt1 = jnp.sum(f1, axis=1, keepdims=True)
    cnt2 = jnp.sum(f2, axis=1, keepdims=True)
    run = run_ref[:, 0:1]
    rank_ref[0:1, :] = jnp.sum(f1 * (run + before1), axis=0, keepdims=True).astype(I32)
    rank_ref[1:2, :] = jnp.sum(f2 * (run + cnt1 + before2), axis=0, keepdims=True).astype(I32)
    run_new = run_ref[...] + cnt1 + cnt2
    run_ref[...] = run_new
    cnt_ref[...] = run_new


def _mix_route(mix, gpre, router_t, layer):
    x, modr = mix[0], mix[-1]
    B, T, _ = x.shape
    tm = min(TOKEN_TILE, T)
    nt = T // tm
    args, specs = _mix_inputs(mix, tm, layer, False)
    tok = pl.BlockSpec((None, tm, D_MODEL), lambda b, i: (b, i, 0))
    lane = pl.BlockSpec((2, tm), lambda b, i: (0, b * nt + i))
    return pl.pallas_call(
        functools.partial(_mix_route_kernel, seq_len=T),
        out_shape=(jax.ShapeDtypeStruct((B, T, D_MODEL), F32),
                   jax.ShapeDtypeStruct((B, T, D_MODEL), F32),
                   jax.ShapeDtypeStruct((2, B * T), I32),
                   jax.ShapeDtypeStruct((2, B * T), F32),
                   jax.ShapeDtypeStruct((2, B * T), I32),
                   jax.ShapeDtypeStruct((N_EXPERTS, LANES), F32)),
        grid=(B, nt),
        in_specs=specs + [_row_spec(D_MODEL), _mod_spec(layer, 4, False), _mod_spec(layer, 3, False),
                          pl.BlockSpec((N_EXPERTS, D_MODEL), lambda b, i: (0, 0))],
        out_specs=(tok, tok, lane, lane, lane,
                   pl.BlockSpec((N_EXPERTS, LANES), lambda b, i: (0, 0))),
        scratch_shapes=[pltpu.VMEM((N_EXPERTS, LANES), F32), pltpu.VMEM((tm, tm), BF16), POOL_BANDS],
        compiler_params=_cparams(("arbitrary", "arbitrary"), VMEM_LIMIT),
        name="mix_route",
    )(*args, gpre.reshape(1, D_MODEL), modr, modr, router_t)


def _dispatch_kernel(zero_tiles_ref, slot_ref, h_ref, hs_ref, zero_ref, sem, zero_sem):
    tm = h_ref.shape[0]

    @pl.when(pl.program_id(0) == 0)
    def _():
        zero_ref[...] = jnp.zeros_like(zero_ref)
        for j in range(2 * N_EXPERTS):
            start = pl.multiple_of(zero_tiles_ref[j] * EXPERT_TILE, EXPERT_TILE)
            fill = pltpu.make_async_copy(zero_ref, hs_ref.at[pl.ds(start, EXPERT_TILE)], zero_sem)
            fill.start()
            fill.wait()

    def body(i, carry):
        for u in range(DMA_UNROLL):
            r = pl.multiple_of(i * DMA_UNROLL, DMA_UNROLL) + u
            for k in range(2):
                s = slot_ref[2 * r + k]
                pltpu.make_async_copy(h_ref.at[pl.ds(r, 1)], hs_ref.at[pl.ds(s, 1)], sem).start(priority=k)
        return carry

    lax.fori_loop(0, tm // DMA_UNROLL, body, 0)
    for _ in range(2):
        pltpu.make_async_copy(h_ref, hs_ref.at[pl.ds(0, tm)], sem).wait()


def _dispatch(h, slot_flat, zero_tiles, n_slots):
    N = h.shape[0]
    tm = min(TOKEN_TILE, N)
    grid_spec = pltpu.PrefetchScalarGridSpec(
        num_scalar_prefetch=1,
        grid=(N // tm,),
        in_specs=[pl.BlockSpec((2 * tm,), lambda i, zt: (i,), memory_space=pltpu.SMEM),
                  pl.BlockSpec((tm, D_MODEL), lambda i, zt: (i, 0))],
        out_specs=pl.BlockSpec(memory_space=pl.ANY),
        scratch_shapes=[pltpu.VMEM((EXPERT_TILE, D_MODEL), F32),
                        pltpu.SemaphoreType.DMA(()), pltpu.SemaphoreType.DMA(())],
    )
    return pl.pallas_call(
        _dispatch_kernel,
        out_shape=jax.ShapeDtypeStruct((n_slots, D_MODEL), F32),
        grid_spec=grid_spec,
        compiler_params=_cparams(("arbitrary",)),
        name="moe_dispatch",
    )(zero_tiles, slot_flat, h)


def _expert_kernel(te_ref, tv_ref, h_ref, w1_ref, w3_ref, w2_ref, y_ref):
    t = pl.program_id(0)

    @pl.when(tv_ref[t] == 1)
    def _():
        y_ref[...] = _swiglu_tile(h_ref[...].astype(BF16), w1_ref, w3_ref, w2_ref)

    @pl.when(tv_ref[t] == 0)
    def _():
        y_ref[...] = jnp.zeros_like(y_ref)


def _experts(hs, tile_expert, tile_valid, w1, w3, w2):
    n_slots = hs.shape[0]
    tm = EXPERT_TILE
    grid_spec = pltpu.PrefetchScalarGridSpec(
        num_scalar_prefetch=2,
        grid=(n_slots // tm,),
        in_specs=[pl.BlockSpec((tm, D_MODEL), lambda t, te, tv: (t, 0)),
                  pl.BlockSpec((None, D_MODEL, D_FF), lambda t, te, tv: (te[t], 0, 0)),
                  pl.BlockSpec((None, D_MODEL, D_FF), lambda t, te, tv: (te[t], 0, 0)),
                  pl.BlockSpec((None, D_FF, D_MODEL), lambda t, te, tv: (te[t], 0, 0))],
        out_specs=pl.BlockSpec((tm, D_MODEL), lambda t, te, tv: (t, 0)),
    )
    return pl.pallas_call(
        _expert_kernel,
        out_shape=jax.ShapeDtypeStruct((n_slots, D_MODEL), F32),
        grid_spec=grid_spec,
        compiler_params=_cparams(("arbitrary",), VMEM_LIMIT),
        name="moe_experts",
    )(tile_expert, tile_valid, hs, w1, w3, w2)


def _combine_kernel(slot_ref, gate_ref, x_ref, npost_ref, g2_ref, ys_ref, o_ref, buf0, buf1, sem):
    tm = x_ref.shape[0]
    bufs = (buf0, buf1)

    def body(i, carry):
        for u in range(DMA_UNROLL):
            r = pl.multiple_of(i * DMA_UNROLL, DMA_UNROLL) + u
            for k in range(2):
                s = slot_ref[2 * r + k]
                pltpu.make_async_copy(ys_ref.at[pl.ds(s, 1)], bufs[k].at[pl.ds(r, 1)], sem).start(priority=k)
        return carry

    lax.fori_loop(0, tm // DMA_UNROLL, body, 0)
    for k in range(2):
        pltpu.make_async_copy(ys_ref.at[pl.ds(0, tm)], bufs[k], sem).wait()
    y = gate_ref[:, 0:1] * buf0[...] + gate_ref[:, 1:2] * buf1[...]
    o_ref[...] = x_ref[...] + g2_ref[...] * _rms(y, npost_ref[...])


def _combine(x, ys, slot_flat, gate_tok, npost, modr, layer):
    B, T, _ = x.shape
    tm = min(TOKEN_TILE, T)
    nt = T // tm
    tok = pl.BlockSpec((None, tm, D_MODEL), lambda b, i: (b, i, 0))
    return pl.pallas_call(
        _combine_kernel,
        out_shape=jax.ShapeDtypeStruct((B, T, D_MODEL), F32),
        grid=(B, nt),
        in_specs=[pl.BlockSpec((2 * tm,), lambda b, i: (b * nt + i,), memory_space=pltpu.SMEM),
                  pl.BlockSpec((tm, 2), lambda b, i: (b * nt + i, 0)),
                  tok, _row_spec(D_MODEL), _mod_spec(layer, 5, False),
                  pl.BlockSpec(memory_space=pl.ANY)],
        out_specs=tok,
        scratch_shapes=[pltpu.VMEM((tm, D_MODEL), F32), pltpu.VMEM((tm, D_MODEL), F32),
                        pltpu.SemaphoreType.DMA(())],
        compiler_params=_cparams(("arbitrary", "arbitrary")),
        name="moe_combine",
    )(slot_flat, gate_tok, x, npost.reshape(1, D_MODEL), modr, ys)


def _mix_moe_ffn(mix, gpre, npost, router, w1, w3, w2, layer):
    modr = mix[-1]
    B, T, _ = mix[0].shape
    N = B * T
    x, h, idx, gate, rank, cnt = _mix_route(mix, gpre, router.T, layer)
    n_slots = 2 * N + N_EXPERTS * EXPERT_TILE
    n_tiles = n_slots // EXPERT_TILE
    counts = cnt[:, 0].astype(I32)
    padded = ((counts + EXPERT_TILE - 1) // EXPERT_TILE) * EXPERT_TILE
    ends = jnp.cumsum(padded)
    starts = ends - padded
    slot = rank
    for e in range(N_EXPERTS):
        slot = slot + jnp.where(idx == e, starts[e], 0)
    slot_flat = slot.T.reshape(2 * N)
    tile_start = jnp.arange(n_tiles, dtype=I32) * EXPERT_TILE
    tile_valid = (tile_start < ends[-1]).astype(I32)
    tile_expert = jnp.minimum(jnp.sum((tile_start[:, None] >= ends[None, :]).astype(I32), axis=1), N_EXPERTS - 1)
    last_expert = jnp.max(jnp.where(tile_valid == 1, tile_expert, 0))
    tile_expert = jnp.where(tile_valid == 1, tile_expert, last_expert)
    last_tile = jnp.where(padded > 0, ends // EXPERT_TILE - 1, n_tiles - 1)
    tail_tile = jnp.minimum(ends[-1] // EXPERT_TILE + jnp.arange(N_EXPERTS, dtype=I32), n_tiles - 1)
    zero_tiles = jnp.concatenate([last_tile, tail_tile]).astype(I32)
    hs = _dispatch(h.reshape(N, D_MODEL), slot_flat, zero_tiles, n_slots)
    ys = _experts(hs, tile_expert, tile_valid, w1, w3, w2)
    return _combine(x, ys, slot_flat, gate.T, npost, modr, layer)


def kernel(x, c, ctx, c_ctx, w_mod, b_mod, norm_pre_mix, norm_post_mix, norm_pre_ffn, norm_post_ffn, w_in, w_out, q_norm, k_norm, ret_decay_logit, pool_w, pool_scale, ffn_w1, ffn_w3, ffn_w2, moe_router, moe_w1, moe_w3, moe_w2):
    B, T, _ = x.shape
    assert B == CTX_MOD_ROW and x.shape[2] == D_MODEL
    cvec = jnp.zeros((MOD_ROWS, D_MODEL), F32).at[0:B].set(c).at[CTX_MOD_ROW].set(c_ctx)
    modr = _modulation(cvec, w_mod, b_mod)
    cos_t, sin_t = _rope_tables(T)
    avg = jnp.kron(jnp.eye(RET_HEADS, dtype=F32), jnp.full((HEAD_DIM, HEAD_DIM), 1.0 / HEAD_DIM, F32)).astype(BF16)
    zero_state = jnp.zeros((B, RET_WIDTH, RET_WIDTH), F32)
    xc = ctx
    for i in range(DEPTH):
        need_ctx = i < DEPTH - 1
        w_in_b = w_in[i].astype(BF16)
        w_out_b = w_out[i].astype(BF16)
        pool_bd = jax.scipy.linalg.block_diag(*[pool_w[i, g] for g in range(len(POOL_WINDOWS))]).astype(BF16)
        log_gamma = jax.nn.log_sigmoid(ret_decay_logit[i].astype(F32))

        qt_c, kn_c, vt_c, qn_c, st_c, ret_c, pp_c = _in_projection(
            xc, modr, norm_pre_mix[i], w_in_b, q_norm[i], k_norm[i], cos_t, sin_t, i, True)
        of_c, ob_c, s_fwd, s_bwd = _retention(ret_c, log_gamma, zero_state, zero_state)

        qt_x, kn_x, vt_x, qn_x, st_x, ret_x, pp_x = _in_projection(
            x, modr, norm_pre_mix[i], w_in_b, q_norm[i], k_norm[i], cos_t, sin_t, i, False)
        keys_x = [(kn_x, vt_x), (kn_c, vt_c)]
        attn_x = _attention(qt_x, _softmax_shift(qn_x, st_x, [st_x, st_c], qt_x, keys_x), keys_x)
        of_x, ob_x, _, _ = _retention(ret_x, log_gamma, s_fwd, s_bwd)
        mix_x = (x, attn_x, of_x, ob_x, ret_x, pp_x, pool_bd, pool_scale[i], avg, w_out_b, norm_post_mix[i], modr)

        j = i // 2
        if i % 2 == 0:
            w1, w3, w2 = ffn_w1[j].astype(BF16), ffn_w3[j].astype(BF16), ffn_w2[j].astype(BF16)
            x = _mix_ffn(mix_x, norm_pre_ffn[i], norm_post_ffn[i], w1, w3, w2, i, False)
            if need_ctx:
                keys_c = [(kn_c, vt_c)]
                attn_c = _attention(qt_c, _softmax_shift(qn_c, st_c, [st_c], qt_c, keys_c), keys_c)
                mix_c = (xc, attn_c, of_c, ob_c, ret_c, pp_c, pool_bd, pool_scale[i], avg, w_out_b,
                         norm_post_mix[i], modr)
                xc = _mix_ffn(mix_c, norm_pre_ffn[i], norm_post_ffn[i], w1, w3, w2, i, True)
        else:
            assert not need_ctx
            x = _mix_moe_ffn(mix_x, norm_pre_ffn[i], norm_post_ffn[i], moe_router[j], moe_w1[j].astype(BF16),
                             moe_w3[j].astype(BF16), moe_w2[j].astype(BF16), i)
    return x
```

```python
import functools

import jax
import jax.numpy as jnp
from jax import lax
from jax.experimental import pallas as pl
from jax.experimental.pallas import tpu as pltpu

F32 = jnp.float32
BF16 = jnp.bfloat16
I32 = jnp.int32

D_MODEL = 1024
GRID_W = 64
HEAD_DIM = 64
ATTN_WIDTH = 512
KV_HEADS = 2
ATTN_GROUP = 4
KV_WIDTH = 128
RET_WIDTH = 256
RET_HEADS = 4
POOL_WIDTH = 256
POOL_WINDOWS = (2, 4, 8, 16)
QKV_WIDTH = ATTN_WIDTH + 2 * KV_WIDTH
RET_PROJ_WIDTH = 4 * RET_WIDTH
IN_WIDTH = QKV_WIDTH + RET_PROJ_WIDTH + POOL_WIDTH
HEAD_SHIFT = HEAD_DIM.bit_length() - 1
LANES = 128
RET_CHUNK = 128
ROPE_THETA = 10000.0
D_FF = 2816
N_EXPERTS = 8
NORM_EPS = 1e-6
DEPTH = 2

TOKEN_TILE = 1024
ATTN_Q_TILE = 512
ATTN_K_CHUNK = 2048
RET_TILE = 512
POOL_TILE = 256
POOL_HALO = 16
FF_CHUNK = 256
EXPERT_TILE = 512
DMA_UNROLL = 8
BF16_SUBLANES = 16
MXU_DIM = 256
K_COLS = MXU_DIM
KEY_BLOCK = MXU_DIM
ATTN_UNROLL = 9
N_MXU = 2
SCORE_REG, VALUE_REG = 1, 0
SCORE_BASE = 64
F32_SUBLANES = 8
VT_ROWS = HEAD_DIM + F32_SUBLANES
MAX_BOUND_SHIFT = 40.0
SHIFT_MARGIN = 1.01
MOD_ROWS = 8
CTX_MOD_ROW = 2
MOD_CHUNKS = 6
VMEM_LIMIT = 56 * 1024 * 1024


def _cparams(sem, vmem=None):
    return pltpu.CompilerParams(dimension_semantics=sem, vmem_limit_bytes=vmem)


def _rms(x, gain):
    ms = jnp.mean(x * x, axis=-1, keepdims=True)
    return x * lax.rsqrt(ms + NORM_EPS) * gain


def _silu(x):
    return x * jax.nn.sigmoid(x)


def _mod_kernel(c_ref, w_ref, b_ref, o_ref):
    s = _silu(c_ref[...])
    o_ref[...] = jnp.dot(s, w_ref[...], precision=lax.Precision.HIGHEST,
                         preferred_element_type=F32) + b_ref[...]


def _modulation(cvec, w_mod, b_mod):
    out = pl.pallas_call(
        _mod_kernel,
        out_shape=jax.ShapeDtypeStruct((DEPTH, MOD_ROWS, MOD_CHUNKS * D_MODEL), F32),
        grid=(DEPTH, MOD_CHUNKS),
        in_specs=[
            pl.BlockSpec((MOD_ROWS, D_MODEL), lambda l, j: (0, 0)),
            pl.BlockSpec((None, D_MODEL, D_MODEL), lambda l, j: (l, 0, j)),
            pl.BlockSpec((None, 1, D_MODEL), lambda l, j: (l, 0, j)),
        ],
        out_specs=pl.BlockSpec((None, MOD_ROWS, D_MODEL), lambda l, j: (l, 0, j)),
        compiler_params=_cparams(("parallel", "parallel")),
        name="modulation",
    )(cvec, w_mod, b_mod.reshape(DEPTH, 1, MOD_CHUNKS * D_MODEL))
    return out.reshape(DEPTH * MOD_ROWS * MOD_CHUNKS, 1, D_MODEL)


def _mod_spec(layer, chunk, ctx):
    base = layer * MOD_ROWS * MOD_CHUNKS
    if ctx:
        return pl.BlockSpec((None, 1, D_MODEL), lambda b, i: (base + CTX_MOD_ROW * MOD_CHUNKS + chunk, 0, 0))
    return pl.BlockSpec((None, 1, D_MODEL), lambda b, i: (base + b * MOD_CHUNKS + chunk, 0, 0))


def _row_spec(width):
    return pl.BlockSpec((1, width), lambda b, i: (0, 0))


def _inproj_kernel(x_ref, g_ref, sc_ref, sh_ref, w_ref, qg_ref, kg_ref, cos_ref, sin_ref,
                   qt_ref, kn_ref, vt_ref, qn_ref, stat_ref, ret_ref, pp_ref, *, rope):
    h = _rms(x_ref[...], g_ref[...]) * (1.0 + sc_ref[...]) + sh_ref[...]
    p = jnp.dot(h.astype(BF16), w_ref[...], preferred_element_type=F32)
    ret_ref[...] = p[:, QKV_WIDTH:QKV_WIDTH + RET_PROJ_WIDTH].astype(BF16)
    pp_ref[...] = p[:, QKV_WIDTH + RET_PROJ_WIDTH:].astype(BF16)
    _prep_tile(p[:, :QKV_WIDTH], qg_ref, kg_ref, cos_ref, sin_ref, qt_ref, kn_ref, vt_ref, qn_ref, stat_ref, rope)


def _prep_tile(qkv, qg_ref, kg_ref, cos_ref, sin_ref, qt_ref, kn_ref, vt_ref, qn_ref, stat_ref, rope):
    t = qkv.T

    def norm_rope(blk, gain):
        ms = jnp.mean(blk * blk, axis=0, keepdims=True)
        y = blk * lax.rsqrt(ms + NORM_EPS) * gain
        if rope:
            partner = jnp.concatenate([y[16:32], y[0:16], y[48:64], y[32:48]], axis=0)
            y = y * cos_ref[...] + partner * sin_ref[...]
        return y

    def sq_norm(y):
        return jnp.sum(y * y, axis=0, keepdims=True)

    def row_max(n2):
        return jnp.broadcast_to(jnp.max(n2, axis=1, keepdims=True), (1, LANES))

    q_stats = []
    for h in range(ATTN_WIDTH // HEAD_DIM):
        lo = h * HEAD_DIM
        q = norm_rope(t[lo:lo + HEAD_DIM], qg_ref[...]) * (HEAD_DIM ** -0.5)
        qt_ref[lo:lo + HEAD_DIM, :] = q.astype(BF16)
        n2 = sq_norm(q)
        qn_ref[h // ATTN_GROUP, h % ATTN_GROUP:h % ATTN_GROUP + 1, :] = jnp.sqrt(n2)
        q_stats.append(row_max(n2))
    tm = t.shape[1]
    k_pad = jnp.where(lax.broadcasted_iota(I32, (K_COLS - HEAD_DIM, tm), 0) < 2, 1.0, 0.0)
    v_pad = jnp.where(lax.broadcasted_iota(I32, (VT_ROWS - HEAD_DIM, tm), 0) < 1, 1.0, 0.0)
    k_stats = []
    for kv in range(KV_HEADS):
        lo = ATTN_WIDTH + kv * HEAD_DIM
        k = norm_rope(t[lo:lo + HEAD_DIM], kg_ref[...])
        k_stats.append(row_max(sq_norm(k)))
        kn_ref[kv] = jnp.concatenate([k, k_pad], axis=0).T.astype(BF16)
        lo = ATTN_WIDTH + KV_WIDTH + kv * HEAD_DIM
        vt = jnp.concatenate([t[lo:lo + HEAD_DIM], v_pad], axis=0)
        for j in range(tm // KEY_BLOCK):
            vt_ref[kv, j] = vt[:, j * KEY_BLOCK:(j + 1) * KEY_BLOCK]
    q_group = [functools.reduce(jnp.maximum, q_stats[kv * ATTN_GROUP:(kv + 1) * ATTN_GROUP])
               for kv in range(KV_HEADS)]
    stat_ref[...] = jnp.concatenate(k_stats + q_group + [jnp.zeros((F32_SUBLANES - 2 * KV_HEADS, LANES), F32)], axis=0)


def _in_projection(x, modr, gain, w_in_bf16, q_gain, k_gain, cos_t, sin_t, layer, ctx):
    B, T, _ = x.shape
    tm = min(TOKEN_TILE, T)
    nc = T // tm
    tok = lambda w: pl.BlockSpec((None, tm, w), lambda b, i: (b, i, 0))
    return pl.pallas_call(
        functools.partial(_inproj_kernel, rope=not ctx),
        out_shape=(jax.ShapeDtypeStruct((B, ATTN_WIDTH, T), BF16),
                   jax.ShapeDtypeStruct((B, KV_HEADS, T, K_COLS), BF16),
                   jax.ShapeDtypeStruct((B, KV_HEADS, T // KEY_BLOCK, VT_ROWS, KEY_BLOCK), F32),
                   jax.ShapeDtypeStruct((B, KV_HEADS, ATTN_GROUP, T), F32),
                   jax.ShapeDtypeStruct((B, nc, F32_SUBLANES, LANES), F32),
                   jax.ShapeDtypeStruct((B, T, RET_PROJ_WIDTH), BF16),
                   jax.ShapeDtypeStruct((B, T, POOL_WIDTH), BF16)),
        grid=(B, nc),
        in_specs=[tok(D_MODEL), _row_spec(D_MODEL), _mod_spec(layer, 1, ctx), _mod_spec(layer, 0, ctx),
                  pl.BlockSpec((D_MODEL, IN_WIDTH), lambda b, i: (0, 0)),
                  pl.BlockSpec((HEAD_DIM, 1), lambda b, i: (0, 0)),
                  pl.BlockSpec((HEAD_DIM, 1), lambda b, i: (0, 0)),
                  pl.BlockSpec((HEAD_DIM, tm), lambda b, i: (0, i)),
                  pl.BlockSpec((HEAD_DIM, tm), lambda b, i: (0, i))],
        out_specs=(pl.BlockSpec((None, ATTN_WIDTH, tm), lambda b, i: (b, 0, i)),
                   pl.BlockSpec((None, KV_HEADS, tm, K_COLS), lambda b, i: (b, 0, i, 0)),
                   pl.BlockSpec((None, KV_HEADS, tm // KEY_BLOCK, VT_ROWS, KEY_BLOCK), lambda b, i: (b, 0, i, 0, 0)),
                   pl.BlockSpec((None, KV_HEADS, ATTN_GROUP, tm), lambda b, i: (b, 0, 0, i)),
                   pl.BlockSpec((None, None, F32_SUBLANES, LANES), lambda b, i: (b, i, 0, 0)),
                   tok(RET_PROJ_WIDTH), tok(POOL_WIDTH)),
        compiler_params=_cparams(("parallel", "parallel"), VMEM_LIMIT),
        name="in_projection",
    )(x, gain.reshape(1, D_MODEL), modr, modr, w_in_bf16, q_gain.reshape(HEAD_DIM, 1), k_gain.reshape(HEAD_DIM, 1),
      cos_t, sin_t)


def _rope_tables(T):
    t = jnp.arange(T)
    row = (t // GRID_W).astype(F32)
    col = (t % GRID_W).astype(F32)
    n_freq = HEAD_DIM // 4
    inv = ROPE_THETA ** (-jnp.arange(n_freq, dtype=F32) / n_freq)
    ang_r = row[None, :] * inv[:, None]
    ang_c = col[None, :] * inv[:, None]
    cos_t = jnp.concatenate([jnp.cos(ang_r), jnp.cos(ang_r), jnp.cos(ang_c), jnp.cos(ang_c)], axis=0)
    sin_t = jnp.concatenate([-jnp.sin(ang_r), jnp.sin(ang_r), -jnp.sin(ang_c), jnp.sin(ang_c)], axis=0)
    return cos_t, sin_t


def _score_max_kernel(qt_ref, k_ref, m_ref, qa_ref):
    tq = qt_ref.shape[1]
    nq = ATTN_GROUP * tq
    for g in range(ATTN_GROUP):
        qa_ref[0:HEAD_DIM, g * tq:(g + 1) * tq] = qt_ref[g * HEAD_DIM:(g + 1) * HEAD_DIM, :]
    qa_ref[HEAD_DIM:K_COLS, :] = jnp.zeros((K_COLS - HEAD_DIM, nq), BF16)
    tk = min(ATTN_K_CHUNK, k_ref.shape[0])

    def colmax8(j):
        k = k_ref[pl.ds(pl.multiple_of(j * tk, tk), tk), :]
        s = jnp.dot(k, qa_ref[...], preferred_element_type=F32)
        return jnp.max(s.reshape(tk // 8, 8, nq), axis=0)

    mx = lax.fori_loop(1, k_ref.shape[0] // tk, lambda j, mx: jnp.maximum(mx, colmax8(j)), colmax8(0))
    m = jnp.max(mx, axis=0, keepdims=True)
    for g in range(ATTN_GROUP):
        m_ref[g:g + 1, :] = m[:, g * tq:(g + 1) * tq]


def _score_max(qt, k_all):
    B, _, Tq = qt.shape
    Tk = k_all.shape[2]
    tq = min(ATTN_Q_TILE, Tq)
    return pl.pallas_call(
        _score_max_kernel,
        out_shape=jax.ShapeDtypeStruct((B, KV_HEADS, ATTN_GROUP, Tq), F32),
        grid=(B, KV_HEADS, Tq // tq),
        in_specs=[pl.BlockSpec((None, ATTN_GROUP * HEAD_DIM, tq), lambda b, h, i: (b, h, i)),
                  pl.BlockSpec((None, None, Tk, K_COLS), lambda b, h, i: (b, h, 0, 0))],
        out_specs=pl.BlockSpec((None, None, ATTN_GROUP, tq), lambda b, h, i: (b, h, 0, i)),
        scratch_shapes=[pltpu.VMEM((K_COLS, ATTN_GROUP * tq), BF16)],
        compiler_params=_cparams(("parallel", "parallel", "parallel"), VMEM_LIMIT),
        name="score_max",
    )(qt, k_all)


def _softmax_shift(q_norm, q_stats, k_stats_list, qt, key_sets):
    B = qt.shape[0]
    k2 = functools.reduce(jnp.maximum, [st[:, :, 0:KV_HEADS, 0].max(axis=1) for st in k_stats_list])
    q2 = q_stats[:, :, KV_HEADS:2 * KV_HEADS, 0].max(axis=1)
    use_bound = (jnp.sqrt(k2 * q2) * SHIFT_MARGIN <= MAX_BOUND_SHIFT).reshape(B, KV_HEADS, 1, 1)
    bound = q_norm * (jnp.sqrt(k2) * SHIFT_MARGIN).reshape(B, KV_HEADS, 1, 1)
    exact = lambda: _score_max(qt, jnp.concatenate([k for k, _ in key_sets], axis=2))
    return lax.cond(jnp.all(use_bound), lambda: bound, lambda: jnp.where(use_bound, bound, exact()))


def _attn_kernel(qt_ref, shift_ref, *refs, n_key_sets, n_casts):
    kv_refs = refs[:2 * n_key_sets]
    cast_in = refs[2 * n_key_sets:2 * n_key_sets + n_casts]
    o_ref = refs[2 * n_key_sets + n_casts]
    cast_out = refs[2 * n_key_sets + n_casts + 1:2 * n_key_sets + 2 * n_casts + 1]
    qa_ref = refs[-1]
    for src_ref, dst_ref in zip(cast_in, cast_out):
        dst_ref[...] = src_ref[...].astype(BF16)
    sources = [(kv_refs[i], kv_refs[i + 1]) for i in range(0, len(kv_refs), 2)]
    counts = [k_ref.shape[0] // KEY_BLOCK for k_ref, _ in sources]
    tq = qt_ref.shape[1]
    n_blocks = sum(counts)
    heads_per_mxu = ATTN_GROUP // N_MXU

    def locate(blk):
        if not isinstance(blk, int):
            return 0, blk
        src = 0
        while blk >= counts[src]:
            blk -= counts[src]
            src += 1
        return src, blk

    row = lax.broadcasted_iota(I32, (BF16_SUBLANES, tq), 0)
    for g in range(ATTN_GROUP):
        m = shift_ref[g:g + 1, :]
        m_hi = m.astype(BF16).astype(F32)
        qa_ref[g, 0:HEAD_DIM, :] = qt_ref[g * HEAD_DIM:(g + 1) * HEAD_DIM, :]
        qa_ref[g, HEAD_DIM:HEAD_DIM + BF16_SUBLANES, :] = jnp.where(
            row == 0, -m_hi, jnp.where(row == 1, m_hi - m, 0.0)).astype(BF16)
        qa_ref[g, HEAD_DIM + BF16_SUBLANES:K_COLS, :] = jnp.zeros((K_COLS - HEAD_DIM - BF16_SUBLANES, tq), BF16)

    def score_addr(slot):
        return SCORE_BASE + slot * (KEY_BLOCK // 4)

    def out_addr(slot):
        return slot * (VT_ROWS // 4)

    def keys(blk):
        src, j = locate(blk)
        return sources[src][0][pl.ds(pl.multiple_of(j * KEY_BLOCK, KEY_BLOCK), KEY_BLOCK), :]

    def stage_q(slot):
        for mxu in range(N_MXU):
            pltpu.matmul_push_rhs(qa_ref[mxu * heads_per_mxu + slot], SCORE_REG, mxu)

    def issue_scores(blk, slot):
        k = keys(blk)
        for mxu in range(N_MXU):
            pltpu.matmul_acc_lhs(score_addr(slot), k, mxu, load_staged_rhs=SCORE_REG)

    def pop_probs(slot):
        return [jnp.exp(pltpu.matmul_pop(score_addr(slot), (KEY_BLOCK, tq), F32, mxu)).astype(BF16)
                for mxu in range(N_MXU)]

    def push_probs(blk, slot, p):
        src, j = locate(blk)
        vt = sources[src][1][j]
        for mxu in range(N_MXU):
            pltpu.matmul_push_rhs(p[mxu], VALUE_REG, mxu)
        return vt

    def unit(blk, slot, stage_next, first):
        if not first:
            p = pop_probs(slot)
        issue_scores(blk, slot)
        if not first:
            vt = push_probs(blk - 1, slot, p)
        if stage_next:
            stage_q((slot + 1) % heads_per_mxu)
        if not first:
            for mxu in range(N_MXU):
                pltpu.matmul_acc_lhs(out_addr(slot), vt, mxu, load_staged_rhs=VALUE_REG)

    def block(blk, last=False, first=False):
        for slot in range(heads_per_mxu):
            unit(blk, slot, stage_next=not (last and slot == heads_per_mxu - 1), first=first)

    stage_q(0)
    block(0, last=n_blocks == 1, first=True)
    if n_blocks > 1:
        n_loop = n_blocks - 2
        assert n_loop + 1 <= counts[0]
        unroll = max([u for u in range(1, ATTN_UNROLL + 1) if n_loop % u == 0] or [1])

        def body(i, carry):
            for u in range(unroll):
                block(1 + i * unroll + u)
            return carry

        if n_loop:
            lax.fori_loop(0, n_loop // unroll, body, 0)
        block(n_blocks - 1, last=True)
    for slot in range(heads_per_mxu):
        vt = push_probs(n_blocks - 1, slot, pop_probs(slot))
        for mxu in range(N_MXU):
            pltpu.matmul_acc_lhs(out_addr(slot), vt, mxu, load_staged_rhs=VALUE_REG)

    outs = []
    for mxu in range(N_MXU):
        for slot in range(heads_per_mxu):
            acc = pltpu.matmul_pop(out_addr(slot), (VT_ROWS, tq), F32, mxu)
            outs.append(acc[0:HEAD_DIM, :] / acc[HEAD_DIM:HEAD_DIM + 1, :])
    o_ref[...] = jnp.concatenate(outs, axis=0).astype(BF16)


def _attention(qt, shift, key_sets, casts=()):
    B, _, Tq = qt.shape
    tq = MXU_DIM
    nq = Tq // tq
    n_steps = B * KV_HEADS * nq
    in_specs = [pl.BlockSpec((None, ATTN_GROUP * HEAD_DIM, tq), lambda b, h, i: (b, h, i)),
                pl.BlockSpec((None, None, ATTN_GROUP, tq), lambda b, h, i: (b, h, 0, i))]
    args = [qt, shift]
    for k, vt in key_sets:
        Tk = k.shape[2]
        in_specs += [pl.BlockSpec((None, None, Tk, K_COLS), lambda b, h, i: (b, h, 0, 0)),
                     pl.BlockSpec((None, None, Tk // KEY_BLOCK, VT_ROWS, KEY_BLOCK), lambda b, h, i: (b, h, 0, 0, 0))]
        args += [k, vt]
    cast_specs = []
    for w in casts:
        rows, cols = w.shape
        rep = next(r for r in (1, 2, 4, 8) if n_steps % r == 0 and rows % (n_steps // r) == 0
                   and (rows // (n_steps // r)) % BF16_SUBLANES == 0)
        block = (rows // (n_steps // rep), cols)
        cast_specs.append(pl.BlockSpec(block, lambda b, h, i, rep=rep: (((b * KV_HEADS + h) * nq + i) // rep, 0)))
    outs = pl.pallas_call(
        functools.partial(_attn_kernel, n_key_sets=len(key_sets), n_casts=len(casts)),
        out_shape=[jax.ShapeDtypeStruct((B, ATTN_WIDTH, Tq), BF16)]
                  + [jax.ShapeDtypeStruct(w.shape, BF16) for w in casts],
        grid=(B, KV_HEADS, nq),
        in_specs=in_specs + cast_specs,
        out_specs=[pl.BlockSpec((None, ATTN_GROUP * HEAD_DIM, tq), lambda b, h, i: (b, h, i))] + cast_specs,
        scratch_shapes=[pltpu.VMEM((ATTN_GROUP, K_COLS, tq), BF16)],
        compiler_params=_cparams(("arbitrary", "arbitrary", "arbitrary"), VMEM_LIMIT),
        name="attention",
    )(*args, *casts)
    return outs[0], outs[1:]


def _ret_kernel(lgc_f_ref, lgc_b_ref, lgr_f_ref, lgr_b_ref, s0f_ref, s0b_ref, blk_f_ref, blk_b_ref,
                of_ref, ob_ref, sf_ref, sb_ref, st_f, st_b, dec):
    C = RET_CHUNK
    W = RET_WIDTH
    n = pl.program_id(1)

    @pl.when(n == 0)
    def _():
        st_f[...] = s0f_ref[...]
        st_b[...] = s0b_ref[...]
        c = jnp.bitwise_and(lax.broadcasted_iota(I32, (RET_HEADS * C, C), 0), C - 1)
        m = lax.broadcasted_iota(I32, (RET_HEADS * C, C), 1)
        diff = (c - m).astype(F32)
        dec[...] = (jnp.where(diff >= 0, jnp.exp(lgc_f_ref[...] * jnp.maximum(diff, 0.0)), 0.0)
                    + jnp.where(diff <= 0, jnp.exp(lgc_b_ref[...] * jnp.maximum(-diff, 0.0)), 0.0))

    lane_head = jnp.right_shift(lax.broadcasted_iota(I32, (C, W), 1), HEAD_SHIFT)
    pos = lax.broadcasted_iota(I32, (C, W), 0).astype(F32)
    same_head = (jnp.right_shift(lax.broadcasted_iota(I32, (W, W), 0), HEAD_SHIFT)
                 == jnp.right_shift(lax.broadcasted_iota(I32, (W, W), 1), HEAD_SHIFT))

    def direction(blk_ref, st_ref, lgr, forward, out_ref):
        n_sub = blk_ref.shape[0] // C
        state = st_ref[...]
        for sub in (range(n_sub) if forward else reversed(range(n_sub))):
            state = chunk(blk_ref, state, lgr, forward, out_ref, sub * C)
        st_ref[...] = state

    def chunk(blk_ref, state, lgr, forward, out_ref, r0):
        q = blk_ref[r0:r0 + C, 0:W].astype(F32)
        kf = blk_ref[r0:r0 + C, W:2 * W].astype(F32) * (HEAD_DIM ** -0.5)
        v = blk_ref[r0:r0 + C, 2 * W:3 * W]
        if forward:
            zeta = jnp.exp(lgr * (C - 1.0 - pos))
            xi = jnp.exp(lgr * (pos + 1.0))
        else:
            zeta = jnp.exp(lgr * pos)
            xi = jnp.exp(lgr * (C - pos))
        out = jnp.dot((q * xi).astype(BF16), state.astype(BF16), preferred_element_type=F32)
        if forward:
            qexp = jnp.concatenate([jnp.where(lane_head == h, q, 0.0) for h in range(RET_HEADS)],
                                   axis=0).astype(BF16)
            a = lax.dot_general(qexp, kf.astype(BF16), (((1,), (1,)), ((), ())),
                                preferred_element_type=F32)
            p = (a * dec[...]).astype(BF16)
            full = jnp.dot(p, v, preferred_element_type=F32)
            for h in range(RET_HEADS):
                out = out + jnp.where(lane_head == h, full[h * C:(h + 1) * C], 0.0)
        out_ref[r0:r0 + C, :] = out
        upd = lax.dot_general((kf * zeta).astype(BF16), v, (((0,), (0,)), ((), ())),
                              preferred_element_type=F32)
        return jnp.where(same_head, state * jnp.exp(lgr * float(C)) + upd, 0.0)

    direction(blk_f_ref, st_f, lgr_f_ref[...], True, of_ref)
    direction(blk_b_ref, st_b, lgr_b_ref[...], False, ob_ref)

    @pl.when(n == pl.num_programs(1) - 1)
    def _():
        sf_ref[...] = st_f[...]
        sb_ref[...] = st_b[...]


def _retention(ret, log_gamma, s0f, s0b):
    B, T, _ = ret.shape
    C = RET_CHUNK
    tm = min(RET_TILE, T)
    nc = T // tm
    lgc = lambda d: jnp.repeat(log_gamma[d], C).reshape(RET_HEADS * C, 1)
    lgr = lambda d: jnp.repeat(log_gamma[d], HEAD_DIM).reshape(1, RET_WIDTH)
    const = lambda shape: pl.BlockSpec(shape, lambda b, n: (0,) * len(shape))
    st_spec = pl.BlockSpec((None, RET_WIDTH, RET_WIDTH), lambda b, n: (b, 0, 0))
    return pl.pallas_call(
        _ret_kernel,
        out_shape=(jax.ShapeDtypeStruct((B, T, RET_WIDTH), F32),
                   jax.ShapeDtypeStruct((B, T, RET_WIDTH), F32),
                   jax.ShapeDtypeStruct((B, RET_WIDTH, RET_WIDTH), F32),
                   jax.ShapeDtypeStruct((B, RET_WIDTH, RET_WIDTH), F32)),
        grid=(B, nc),
        in_specs=[const((RET_HEADS * C, 1)), const((RET_HEADS * C, 1)),
                  const((1, RET_WIDTH)), const((1, RET_WIDTH)), st_spec, st_spec,
                  pl.BlockSpec((None, tm, RET_PROJ_WIDTH), lambda b, n: (b, n, 0)),
                  pl.BlockSpec((None, tm, RET_PROJ_WIDTH), lambda b, n: (b, nc - 1 - n, 0))],
        out_specs=(pl.BlockSpec((None, tm, RET_WIDTH), lambda b, n: (b, n, 0)),
                   pl.BlockSpec((None, tm, RET_WIDTH), lambda b, n: (b, nc - 1 - n, 0)),
                   st_spec, st_spec),
        scratch_shapes=[pltpu.VMEM((RET_WIDTH, RET_WIDTH), F32), pltpu.VMEM((RET_WIDTH, RET_WIDTH), F32),
                        pltpu.VMEM((RET_HEADS * C, C), F32)],
        compiler_params=_cparams(("parallel", "arbitrary")),
        name="retention",
    )(lgc(0), lgc(1), lgr(0), lgr(1), s0f, s0b, ret, ret)


def _init_pool_bands(band_ref):
    tok = lax.broadcasted_iota(I32, (POOL_TILE, POOL_TILE + 2 * POOL_HALO), 0)
    src = lax.broadcasted_iota(I32, (POOL_TILE, POOL_TILE + 2 * POOL_HALO), 1) - POOL_HALO
    for gi, w in enumerate(POOL_WINDOWS):
        inside = (src >= tok - w // 2) & (src < tok + w // 2)
        band_ref[gi] = jnp.where(inside, 1.0, 0.0).astype(BF16)


def _pool_tile(prev_ref, cur_ref, next_ref, w_ref, scale_ref, band_ref, seq_len):
    tm = cur_ref.shape[0]
    i = pl.program_id(1)
    prev = jnp.where(i > 0, prev_ref[...], jnp.zeros_like(prev_ref))
    nxt = jnp.where(i < pl.num_programs(1) - 1, next_ref[...], jnp.zeros_like(next_ref))
    ext = jnp.concatenate([prev, cur_ref[...], nxt], axis=0)
    sub = min(POOL_TILE, tm)
    lane_group = jnp.right_shift(lax.broadcasted_iota(I32, (sub, POOL_WIDTH), 1), HEAD_SHIFT)
    parts = []
    for r0 in range(0, tm, sub):
        window = ext[r0:r0 + sub + 2 * POOL_HALO]
        cur = cur_ref[r0:r0 + sub, :].astype(F32)
        tcol = i * tm + r0 + lax.broadcasted_iota(I32, (sub, 1), 0)
        mixed = jnp.zeros((sub, POOL_WIDTH), F32)
        for gi, w in enumerate(POOL_WINDOWS):
            total = jnp.dot(band_ref[gi], window, preferred_element_type=F32)
            cnt = (jnp.minimum(tcol + w // 2, seq_len) - jnp.maximum(tcol - w // 2, 0)).astype(F32)
            mixed = mixed + jnp.where(lane_group == gi, total / cnt - cur, 0.0)
        y = jnp.dot(mixed.astype(BF16), w_ref[...], preferred_element_type=F32)
        parts.append((y * scale_ref[...]).astype(BF16))
    return jnp.concatenate(parts, axis=0)


def _head_mean(x, avg):
    hi = x.astype(BF16)
    lo = (x - hi.astype(F32)).astype(BF16)
    return (jnp.dot(hi, avg, preferred_element_type=F32) + jnp.dot(lo, avg, preferred_element_type=F32))


def _mix_tile(x_ref, attn_ref, of_ref, ob_ref, gate_ref, pp_prev_ref, pp_ref, pp_next_ref, pool_w_ref,
              pool_scale_ref, avg_ref, w_ref, npost_ref, g1_ref, band_ref, seq_len):
    pool = _pool_tile(pp_prev_ref, pp_ref, pp_next_ref, pool_w_ref, pool_scale_ref, band_ref, seq_len)
    o = of_ref[...] + ob_ref[...]
    avg = avg_ref[...]
    mu = _head_mean(o, avg)
    cen = o - mu
    var = _head_mean(cen * cen, avg)
    y_ret = (_silu(gate_ref[...].astype(F32)) * (cen * lax.rsqrt(var + NORM_EPS))).astype(BF16)
    mx = (lax.dot_general(attn_ref[...], w_ref[0:ATTN_WIDTH, :], (((0,), (0,)), ((), ())),
                          preferred_element_type=F32)
          + jnp.dot(y_ret, w_ref[ATTN_WIDTH:ATTN_WIDTH + RET_WIDTH, :], preferred_element_type=F32)
          + jnp.dot(pool, w_ref[ATTN_WIDTH + RET_WIDTH:, :], preferred_element_type=F32))
    return x_ref[...] + g1_ref[...] * _rms(mx, npost_ref[...])


N_MIX_INPUTS = 14
POOL_BANDS = pltpu.VMEM((len(POOL_WINDOWS), POOL_TILE, POOL_TILE + 2 * POOL_HALO), BF16)


def _mix_inputs(mix, tm, layer, ctx):
    x, attn, of, ob, ret, pp, pool_w, pool_scale, avg_bf16, w_out_bf16, npost, modr = mix
    T = x.shape[1]
    r = tm // POOL_HALO
    last = T // POOL_HALO - 1
    tok = lambda w: pl.BlockSpec((None, tm, w), lambda b, i: (b, i, 0))
    specs = [tok(D_MODEL), pl.BlockSpec((None, ATTN_WIDTH, tm), lambda b, i: (b, 0, i)),
             tok(RET_WIDTH), tok(RET_WIDTH),
             pl.BlockSpec((None, tm, RET_WIDTH), lambda b, i: (b, i, 3)),
             pl.BlockSpec((None, POOL_HALO, POOL_WIDTH), lambda b, i: (b, jnp.maximum(i * r - 1, 0), 0)),
             tok(POOL_WIDTH),
             pl.BlockSpec((None, POOL_HALO, POOL_WIDTH), lambda b, i: (b, jnp.minimum((i + 1) * r, last), 0)),
             pl.BlockSpec((POOL_WIDTH, POOL_WIDTH), lambda b, i: (0, 0)), _row_spec(POOL_WIDTH),
             pl.BlockSpec((RET_WIDTH, RET_WIDTH), lambda b, i: (0, 0)),
             pl.BlockSpec((D_MODEL, D_MODEL), lambda b, i: (0, 0)),
             _row_spec(D_MODEL), _mod_spec(layer, 2, ctx)]
    args = [x, attn, of, ob, ret, pp, pp, pp, pool_w, pool_scale.reshape(1, POOL_WIDTH), avg_bf16, w_out_bf16,
            npost.reshape(1, D_MODEL), modr]
    return args, specs


def _swiglu_tile(h, w1_ref, w3_ref, w2_ref):
    acc = jnp.zeros((h.shape[0], D_MODEL), F32)
    for f in range(0, D_FF, FF_CHUNK):
        a = jnp.dot(h, w1_ref[:, f:f + FF_CHUNK], preferred_element_type=F32)
        b = jnp.dot(h, w3_ref[:, f:f + FF_CHUNK], preferred_element_type=F32)
        u = (_silu(a) * b).astype(BF16)
        acc = acc + jnp.dot(u, w2_ref[f:f + FF_CHUNK, :], preferred_element_type=F32)
    return acc


def _mix_ffn_kernel(*refs, seq_len):
    gpre_ref, sc_ref, sh_ref, w1_ref, w3_ref, w2_ref, npost_ref, g2_ref, o_ref, band_ref = refs[N_MIX_INPUTS:]

    @pl.when(pl.program_id(1) == 0)
    def _():
        _init_pool_bands(band_ref)

    x = _mix_tile(*refs[:N_MIX_INPUTS], band_ref, seq_len)
    h = (_rms(x, gpre_ref[...]) * (1.0 + sc_ref[...]) + sh_ref[...]).astype(BF16)
    y = _swiglu_tile(h, w1_ref, w3_ref, w2_ref)
    o_ref[...] = x + g2_ref[...] * _rms(y, npost_ref[...])


def _mix_ffn(mix, gpre, npost, w1, w3, w2, layer, ctx):
    x, modr = mix[0], mix[-1]
    B, T, _ = x.shape
    tm = min(TOKEN_TILE, T)
    args, specs = _mix_inputs(mix, tm, layer, ctx)
    wspec = lambda shape: pl.BlockSpec(shape, lambda b, i: (0, 0), pipeline_mode=pl.Buffered(1))
    return pl.pallas_call(
        functools.partial(_mix_ffn_kernel, seq_len=T),
        out_shape=jax.ShapeDtypeStruct((B, T, D_MODEL), F32),
        grid=(B, T // tm),
        in_specs=specs + [_row_spec(D_MODEL), _mod_spec(layer, 4, ctx), _mod_spec(layer, 3, ctx),
                          wspec((D_MODEL, D_FF)), wspec((D_MODEL, D_FF)), wspec((D_FF, D_MODEL)),
                          _row_spec(D_MODEL), _mod_spec(layer, 5, ctx)],
        out_specs=pl.BlockSpec((None, tm, D_MODEL), lambda b, i: (b, i, 0)),
        scratch_shapes=[POOL_BANDS],
        compiler_params=_cparams(("parallel", "arbitrary"), VMEM_LIMIT),
        name="mix_ffn",
    )(*args, gpre.reshape(1, D_MODEL), modr, modr, w1, w3, w2, npost.reshape(1, D_MODEL), modr)


def _mix_route_kernel(*refs, seq_len):
    (gpre_ref, sc_ref, sh_ref, rt_ref, x_ref, h_ref, idx_ref, gate_ref, rank_ref, cnt_ref,
     run_ref, upper_ref, band_ref) = refs[N_MIX_INPUTS:]
    tm = x_ref.shape[0]
    first = (pl.program_id(0) == 0) & (pl.program_id(1) == 0)

    @pl.when(first)
    def _():
        run_ref[...] = jnp.zeros_like(run_ref)
        earlier = lax.broadcasted_iota(I32, (tm, tm), 0) < lax.broadcasted_iota(I32, (tm, tm), 1)
        upper_ref[...] = jnp.where(earlier, 1.0, 0.0).astype(BF16)
        _init_pool_bands(band_ref)

    x = _mix_tile(*refs[:N_MIX_INPUTS], band_ref, seq_len)
    x_ref[...] = x
    h = _rms(x, gpre_ref[...]) * (1.0 + sc_ref[...]) + sh_ref[...]
    h_ref[...] = h
    def split(v):
        hi = v.astype(BF16)
        return hi, (v - hi.astype(F32)).astype(BF16)

    nt_dot = lambda a, b: lax.dot_general(a, b, (((1,), (1,)), ((), ())), preferred_element_type=F32)
    r_hi, r_lo = split(rt_ref[...])
    h_hi, h_lo = split(h)
    by_h_hi = nt_dot(jnp.concatenate([r_hi, r_lo], axis=0), h_hi)
    logits = by_h_hi[0:N_EXPERTS] + by_h_hi[N_EXPERTS:] + nt_dot(r_hi, h_lo)
    eid = lax.broadcasted_iota(I32, (N_EXPERTS, tm), 0).astype(F32)
    m1 = jnp.max(logits, axis=0, keepdims=True)
    i1 = jnp.min(jnp.where(logits == m1, eid, float(N_EXPERTS)), axis=0, keepdims=True)
    oh1 = eid == i1
    rest = jnp.where(oh1, -jnp.inf, logits)
    m2 = jnp.max(rest, axis=0, keepdims=True)
    i2 = jnp.min(jnp.where(rest == m2, eid, float(N_EXPERTS)), axis=0, keepdims=True)
    oh2 = eid == i2
    e2 = jnp.exp(m2 - m1)
    gate_ref[0:1, :] = 1.0 / (1.0 + e2)
    gate_ref[1:2, :] = e2 / (1.0 + e2)
    idx_ref[0:1, :] = i1.astype(I32)
    idx_ref[1:2, :] = i2.astype(I32)
    upper = upper_ref[...]
    f1 = jnp.where(oh1, 1.0, 0.0)
    f2 = jnp.where(oh2, 1.0, 0.0)
    before1 = jnp.dot(f1.astype(BF16), upper, preferred_element_type=F32)
    before2 = jnp.dot(f2.astype(BF16), upper, preferred_element_type=F32)
    cnt1 = jnp.sum(f1, axis=1, keepdims=True)
    cnt2 = jnp.sum(f2, axis=1, keepdims=True)
    run = run_ref[:, 0:1]
    rank_ref[0:1, :] = jnp.sum(f1 * (run + before1), axis=0, keepdims=True).astype(I32)
    rank_ref[1:2, :] = jnp.sum(f2 * (run + cnt1 + before2), axis=0, keepdims=True).astype(I32)
    run_new = run_ref[...] + cnt1 + cnt2
    run_ref[...] = run_new
    cnt_ref[...] = run_new


def _mix_route(mix, gpre, router_t, layer):
    x, modr = mix[0], mix[-1]
    B, T, _ = x.shape
    tm = min(TOKEN_TILE, T)
    nt = T // tm
    args, specs = _mix_inputs(mix, tm, layer, False)
    tok = pl.BlockSpec((None, tm, D_MODEL), lambda b, i: (b, i, 0))
    lane = pl.BlockSpec((2, tm), lambda b, i: (0, b * nt + i))
    return pl.pallas_call(
        functools.partial(_mix_route_kernel, seq_len=T),
        out_shape=(jax.ShapeDtypeStruct((B, T, D_MODEL), F32),
                   jax.ShapeDtypeStruct((B, T, D_MODEL), F32),
                   jax.ShapeDtypeStruct((2, B * T), I32),
                   jax.ShapeDtypeStruct((2, B * T), F32),
                   jax.ShapeDtypeStruct((2, B * T), I32),
                   jax.ShapeDtypeStruct((N_EXPERTS, LANES), F32)),
        grid=(B, nt),
        in_specs=specs + [_row_spec(D_MODEL), _mod_spec(layer, 4, False), _mod_spec(layer, 3, False),
                          pl.BlockSpec((N_EXPERTS, D_MODEL), lambda b, i: (0, 0))],
        out_specs=(tok, tok, lane, lane, lane,
                   pl.BlockSpec((N_EXPERTS, LANES), lambda b, i: (0, 0))),
        scratch_shapes=[pltpu.VMEM((N_EXPERTS, LANES), F32), pltpu.VMEM((tm, tm), BF16), POOL_BANDS],
        compiler_params=_cparams(("arbitrary", "arbitrary"), VMEM_LIMIT),
        name="mix_route",
    )(*args, gpre.reshape(1, D_MODEL), modr, modr, router_t)


def _dispatch_kernel(zero_tiles_ref, slot_ref, h_ref, hs_ref, zero_ref, sem, zero_sem):
    tm = h_ref.shape[0]

    @pl.when(pl.program_id(0) == 0)
    def _():
        zero_ref[...] = jnp.zeros_like(zero_ref)
        for j in range(2 * N_EXPERTS):
            start = pl.multiple_of(zero_tiles_ref[j] * EXPERT_TILE, EXPERT_TILE)
            fill = pltpu.make_async_copy(zero_ref, hs_ref.at[pl.ds(start, EXPERT_TILE)], zero_sem)
            fill.start()
            fill.wait()

    def body(i, carry):
        for u in range(DMA_UNROLL):
            r = pl.multiple_of(i * DMA_UNROLL, DMA_UNROLL) + u
            for k in range(2):
                s = slot_ref[2 * r + k]
                pltpu.make_async_copy(h_ref.at[pl.ds(r, 1)], hs_ref.at[pl.ds(s, 1)], sem).start(priority=k)
        return carry

    lax.fori_loop(0, tm // DMA_UNROLL, body, 0)
    for _ in range(2):
        pltpu.make_async_copy(h_ref, hs_ref.at[pl.ds(0, tm)], sem).wait()


def _dispatch(h, slot_flat, zero_tiles, n_slots):
    N = h.shape[0]
    tm = min(TOKEN_TILE, N)
    grid_spec = pltpu.PrefetchScalarGridSpec(
        num_scalar_prefetch=1,
        grid=(N // tm,),
        in_specs=[pl.BlockSpec((2 * tm,), lambda i, zt: (i,), memory_space=pltpu.SMEM),
                  pl.BlockSpec((tm, D_MODEL), lambda i, zt: (i, 0))],
        out_specs=pl.BlockSpec(memory_space=pl.ANY),
        scratch_shapes=[pltpu.VMEM((EXPERT_TILE, D_MODEL), F32),
                        pltpu.SemaphoreType.DMA(()), pltpu.SemaphoreType.DMA(())],
    )
    return pl.pallas_call(
        _dispatch_kernel,
        out_shape=jax.ShapeDtypeStruct((n_slots, D_MODEL), F32),
        grid_spec=grid_spec,
        compiler_params=_cparams(("arbitrary",)),
        name="moe_dispatch",
    )(zero_tiles, slot_flat, h)


def _expert_kernel(te_ref, tv_ref, h_ref, w1_ref, w3_ref, w2_ref, y_ref):
    t = pl.program_id(0)

    @pl.when(tv_ref[t] == 1)
    def _():
        y_ref[...] = _swiglu_tile(h_ref[...].astype(BF16), w1_ref, w3_ref, w2_ref)

    @pl.when(tv_ref[t] == 0)
    def _():
        y_ref[...] = jnp.zeros_like(y_ref)


def _experts(hs, tile_expert, tile_valid, w1, w3, w2):
    n_slots = hs.shape[0]
    tm = EXPERT_TILE
    grid_spec = pltpu.PrefetchScalarGridSpec(
        num_scalar_prefetch=2,
        grid=(n_slots // tm,),
        in_specs=[pl.BlockSpec((tm, D_MODEL), lambda t, te, tv: (t, 0)),
                  pl.BlockSpec((None, D_MODEL, D_FF), lambda t, te, tv: (te[t], 0, 0)),
                  pl.BlockSpec((None, D_MODEL, D_FF), lambda t, te, tv: (te[t], 0, 0)),
                  pl.BlockSpec((None, D_FF, D_MODEL), lambda t, te, tv: (te[t], 0, 0))],
        out_specs=pl.BlockSpec((tm, D_MODEL), lambda t, te, tv: (t, 0)),
    )
    return pl.pallas_call(
        _expert_kernel,
        out_shape=jax.ShapeDtypeStruct((n_slots, D_MODEL), F32),
        grid_spec=grid_spec,
        compiler_params=_cparams(("arbitrary",), VMEM_LIMIT),
        name="moe_experts",
    )(tile_expert, tile_valid, hs, w1, w3, w2)


def _combine_kernel(slot_ref, gate_ref, x_ref, npost_ref, g2_ref, ys_ref, o_ref, buf0, buf1, sem):
    tm = x_ref.shape[0]
    bufs = (buf0, buf1)

    def body(i, carry):
        for u in range(DMA_UNROLL):
            r = pl.multiple_of(i * DMA_UNROLL, DMA_UNROLL) + u
            for k in range(2):
                s = slot_ref[2 * r + k]
                pltpu.make_async_copy(ys_ref.at[pl.ds(s, 1)], bufs[k].at[pl.ds(r, 1)], sem).start(priority=k)
        return carry

    lax.fori_loop(0, tm // DMA_UNROLL, body, 0)
    for k in range(2):
        pltpu.make_async_copy(ys_ref.at[pl.ds(0, tm)], bufs[k], sem).wait()
    y = gate_ref[:, 0:1] * buf0[...] + gate_ref[:, 1:2] * buf1[...]
    o_ref[...] = x_ref[...] + g2_ref[...] * _rms(y, npost_ref[...])


def _combine(x, ys, slot_flat, gate_tok, npost, modr, layer):
    B, T, _ = x.shape
    tm = min(TOKEN_TILE, T)
    nt = T // tm
    tok = pl.BlockSpec((None, tm, D_MODEL), lambda b, i: (b, i, 0))
    return pl.pallas_call(
        _combine_kernel,
        out_shape=jax.ShapeDtypeStruct((B, T, D_MODEL), F32),
        grid=(B, nt),
        in_specs=[pl.BlockSpec((2 * tm,), lambda b, i: (b * nt + i,), memory_space=pltpu.SMEM),
                  pl.BlockSpec((tm, 2), lambda b, i: (b * nt + i, 0)),
                  tok, _row_spec(D_MODEL), _mod_spec(layer, 5, False),
                  pl.BlockSpec(memory_space=pl.ANY)],
        out_specs=tok,
        scratch_shapes=[pltpu.VMEM((tm, D_MODEL), F32), pltpu.VMEM((tm, D_MODEL), F32),
                        pltpu.SemaphoreType.DMA(())],
        compiler_params=_cparams(("arbitrary", "arbitrary")),
        name="moe_combine",
    )(slot_flat, gate_tok, x, npost.reshape(1, D_MODEL), modr, ys)


def _mix_moe_ffn(mix, gpre, npost, router, w1, w3, w2, layer):
    modr = mix[-1]
    B, T, _ = mix[0].shape
    N = B * T
    x, h, idx, gate, rank, cnt = _mix_route(mix, gpre, router.T, layer)
    n_slots = 2 * N + N_EXPERTS * EXPERT_TILE
    n_tiles = n_slots // EXPERT_TILE
    counts = cnt[:, 0].astype(I32)
    padded = ((counts + EXPERT_TILE - 1) // EXPERT_TILE) * EXPERT_TILE
    ends = jnp.cumsum(padded)
    starts = ends - padded
    slot = rank
    for e in range(N_EXPERTS):
        slot = slot + jnp.where(idx == e, starts[e], 0)
    slot_flat = slot.T.reshape(2 * N)
    tile_start = jnp.arange(n_tiles, dtype=I32) * EXPERT_TILE
    tile_valid = (tile_start < ends[-1]).astype(I32)
    tile_expert = jnp.minimum(jnp.sum((tile_start[:, None] >= ends[None, :]).astype(I32), axis=1), N_EXPERTS - 1)
    last_expert = jnp.max(jnp.where(tile_valid == 1, tile_expert, 0))
    tile_expert = jnp.where(tile_valid == 1, tile_expert, last_expert)
    last_tile = jnp.where(padded > 0, ends // EXPERT_TILE - 1, n_tiles - 1)
    tail_tile = jnp.minimum(ends[-1] // EXPERT_TILE + jnp.arange(N_EXPERTS, dtype=I32), n_tiles - 1)
    zero_tiles = jnp.concatenate([last_tile, tail_tile]).astype(I32)
    hs = _dispatch(h.reshape(N, D_MODEL), slot_flat, zero_tiles, n_slots)
    ys = _experts(hs, tile_expert, tile_valid, w1, w3, w2)
    return _combine(x, ys, slot_flat, gate.T, npost, modr, layer)


def kernel(x, c, ctx, c_ctx, w_mod, b_mod, norm_pre_mix, norm_post_mix, norm_pre_ffn, norm_post_ffn, w_in, w_out, q_norm, k_norm, ret_decay_logit, pool_w, pool_scale, ffn_w1, ffn_w3, ffn_w2, moe_router, moe_w1, moe_w3, moe_w2):
    B, T, _ = x.shape
    assert B == CTX_MOD_ROW and x.shape[2] == D_MODEL
    cvec = jnp.zeros((MOD_ROWS, D_MODEL), F32).at[0:B].set(c).at[CTX_MOD_ROW].set(c_ctx)
    modr = _modulation(cvec, w_mod, b_mod)
    cos_t, sin_t = _rope_tables(T)
    avg = jnp.kron(jnp.eye(RET_HEADS, dtype=F32), jnp.full((HEAD_DIM, HEAD_DIM), 1.0 / HEAD_DIM, F32)).astype(BF16)
    zero_state = jnp.zeros((B, RET_WIDTH, RET_WIDTH), F32)
    moe_bf16 = {}
    xc = ctx
    for i in range(DEPTH):
        need_ctx = i < DEPTH - 1
        w_in_b = w_in[i].astype(BF16)
        w_out_b = w_out[i].astype(BF16)
        pool_bd = jax.scipy.linalg.block_diag(*[pool_w[i, g] for g in range(len(POOL_WINDOWS))]).astype(BF16)
        log_gamma = jax.nn.log_sigmoid(ret_decay_logit[i].astype(F32))

        qt_c, kn_c, vt_c, qn_c, st_c, ret_c, pp_c = _in_projection(
            xc, modr, norm_pre_mix[i], w_in_b, q_norm[i], k_norm[i], cos_t, sin_t, i, True)
        of_c, ob_c, s_fwd, s_bwd = _retention(ret_c, log_gamma, zero_state, zero_state)

        qt_x, kn_x, vt_x, qn_x, st_x, ret_x, pp_x = _in_projection(
            x, modr, norm_pre_mix[i], w_in_b, q_norm[i], k_norm[i], cos_t, sin_t, i, False)
        keys_x = [(kn_x, vt_x), (kn_c, vt_c)]
        next_moe = (i + 1) // 2 if i + 1 < DEPTH and (i + 1) % 2 == 1 else None
        casts = [] if next_moe is None else [w[next_moe].reshape(-1, w.shape[-1]) for w in (moe_w1, moe_w3, moe_w2)]
        attn_x, cast = _attention(qt_x, _softmax_shift(qn_x, st_x, [st_x, st_c], qt_x, keys_x), keys_x, casts)
        if cast:
            moe_bf16[next_moe] = [c.reshape(w.shape[1:]) for c, w in zip(cast, (moe_w1, moe_w3, moe_w2))]
        of_x, ob_x, _, _ = _retention(ret_x, log_gamma, s_fwd, s_bwd)
        mix_x = (x, attn_x, of_x, ob_x, ret_x, pp_x, pool_bd, pool_scale[i], avg, w_out_b, norm_post_mix[i], modr)

        j = i // 2
        if i % 2 == 0:
            w1, w3, w2 = ffn_w1[j].astype(BF16), ffn_w3[j].astype(BF16), ffn_w2[j].astype(BF16)
            x = _mix_ffn(mix_x, norm_pre_ffn[i], norm_post_ffn[i], w1, w3, w2, i, False)
            if need_ctx:
                keys_c = [(kn_c, vt_c)]
                attn_c, _ = _attention(qt_c, _softmax_shift(qn_c, st_c, [st_c], qt_c, keys_c), keys_c)
                mix_c = (xc, attn_c, of_c, ob_c, ret_c, pp_c, pool_bd, pool_scale[i], avg, w_out_b,
                         norm_post_mix[i], modr)
                xc = _mix_ffn(mix_c, norm_pre_ffn[i], norm_post_ffn[i], w1, w3, w2, i, True)
        else:
            assert not need_ctx
            w1, w3, w2 = moe_bf16.get(j) or [w[j].astype(BF16) for w in (moe_w1, moe_w3, moe_w2)]
            x = _mix_moe_ffn(mix_x, norm_pre_ffn[i], norm_post_ffn[i], moe_router[j], w1, w3, w2, i)
    return x
```

```python
import functools

import jax
import jax.numpy as jnp
from jax import lax
from jax.experimental import pallas as pl
from jax.experimental.pallas import tpu as pltpu

F32 = jnp.float32
BF16 = jnp.bfloat16
I32 = jnp.int32

D_MODEL = 1024
GRID_W = 64
HEAD_DIM = 64
ATTN_WIDTH = 512
KV_HEADS = 2
ATTN_GROUP = 4
KV_WIDTH = 128
RET_WIDTH = 256
RET_HEADS = 4
POOL_WIDTH = 256
POOL_WINDOWS = (2, 4, 8, 16)
QKV_WIDTH = ATTN_WIDTH + 2 * KV_WIDTH
RET_PROJ_WIDTH = 4 * RET_WIDTH
IN_WIDTH = QKV_WIDTH + RET_PROJ_WIDTH + POOL_WIDTH
HEAD_SHIFT = HEAD_DIM.bit_length() - 1
LANES = 128
RET_CHUNK = 128
ROPE_THETA = 10000.0
D_FF = 2816
N_EXPERTS = 8
NORM_EPS = 1e-6
DEPTH = 2

TOKEN_TILE = 1024
ATTN_Q_TILE = 512
ATTN_K_CHUNK = 2048
RET_TILE = 1024
POOL_TILE = 256
POOL_HALO = 16
FF_CHUNK = 256
EXPERT_TILE = 512
DMA_UNROLL = 8
BF16_SUBLANES = 16
MXU_DIM = 256
K_COLS = MXU_DIM
KEY_BLOCK = MXU_DIM
ATTN_UNROLL = 9
N_MXU = 2
SCORE_REG, VALUE_REG = 1, 0
SCORE_BASE = 64
F32_SUBLANES = 8
VT_ROWS = HEAD_DIM + F32_SUBLANES
MAX_BOUND_SHIFT = 40.0
SHIFT_MARGIN = 1.01
MOD_ROWS = 8
CTX_MOD_ROW = 2
MOD_CHUNKS = 6
VMEM_LIMIT = 56 * 1024 * 1024


def _cparams(sem, vmem=None):
    return pltpu.CompilerParams(dimension_semantics=sem, vmem_limit_bytes=vmem)


def _rms(x, gain):
    ms = jnp.mean(x * x, axis=-1, keepdims=True)
    return x * lax.rsqrt(ms + NORM_EPS) * gain


def _silu(x):
    return x * jax.nn.sigmoid(x)


def _mod_kernel(c_ref, w_ref, b_ref, o_ref):
    s = _silu(c_ref[...])
    o_ref[...] = jnp.dot(s, w_ref[...], precision=lax.Precision.HIGHEST,
                         preferred_element_type=F32) + b_ref[...]


def _modulation(cvec, w_mod, b_mod):
    out = pl.pallas_call(
        _mod_kernel,
        out_shape=jax.ShapeDtypeStruct((DEPTH, MOD_ROWS, MOD_CHUNKS * D_MODEL), F32),
        grid=(DEPTH, MOD_CHUNKS),
        in_specs=[
            pl.BlockSpec((MOD_ROWS, D_MODEL), lambda l, j: (0, 0)),
            pl.BlockSpec((None, D_MODEL, D_MODEL), lambda l, j: (l, 0, j)),
            pl.BlockSpec((None, 1, D_MODEL), lambda l, j: (l, 0, j)),
        ],
        out_specs=pl.BlockSpec((None, MOD_ROWS, D_MODEL), lambda l, j: (l, 0, j)),
        compiler_params=_cparams(("parallel", "parallel")),
        name="modulation",
    )(cvec, w_mod, b_mod.reshape(DEPTH, 1, MOD_CHUNKS * D_MODEL))
    return out.reshape(DEPTH * MOD_ROWS * MOD_CHUNKS, 1, D_MODEL)


def _mod_spec(layer, chunk, ctx):
    base = layer * MOD_ROWS * MOD_CHUNKS
    if ctx:
        return pl.BlockSpec((None, 1, D_MODEL), lambda b, i: (base + CTX_MOD_ROW * MOD_CHUNKS + chunk, 0, 0))
    return pl.BlockSpec((None, 1, D_MODEL), lambda b, i: (base + b * MOD_CHUNKS + chunk, 0, 0))


def _row_spec(width):
    return pl.BlockSpec((1, width), lambda b, i: (0, 0))


def _inproj_kernel(x_ref, g_ref, sc_ref, sh_ref, w_ref, qg_ref, kg_ref, cos_ref, sin_ref,
                   qt_ref, kn_ref, vt_ref, qn_ref, stat_ref, ret_ref, pp_ref, *, rope):
    h = _rms(x_ref[...], g_ref[...]) * (1.0 + sc_ref[...]) + sh_ref[...]
    p = jnp.dot(h.astype(BF16), w_ref[...], preferred_element_type=F32)
    ret_ref[...] = p[:, QKV_WIDTH:QKV_WIDTH + RET_PROJ_WIDTH].astype(BF16)
    pp_ref[...] = p[:, QKV_WIDTH + RET_PROJ_WIDTH:].astype(BF16)
    _prep_tile(p[:, :QKV_WIDTH], qg_ref, kg_ref, cos_ref, sin_ref, qt_ref, kn_ref, vt_ref, qn_ref, stat_ref, rope)


def _prep_tile(qkv, qg_ref, kg_ref, cos_ref, sin_ref, qt_ref, kn_ref, vt_ref, qn_ref, stat_ref, rope):
    t = qkv.T

    def norm_rope(blk, gain):
        ms = jnp.mean(blk * blk, axis=0, keepdims=True)
        y = blk * lax.rsqrt(ms + NORM_EPS) * gain
        if rope:
            partner = jnp.concatenate([y[16:32], y[0:16], y[48:64], y[32:48]], axis=0)
            y = y * cos_ref[...] + partner * sin_ref[...]
        return y

    def sq_norm(y):
        return jnp.sum(y * y, axis=0, keepdims=True)

    def row_max(n2):
        return jnp.broadcast_to(jnp.max(n2, axis=1, keepdims=True), (1, LANES))

    q_stats = []
    for h in range(ATTN_WIDTH // HEAD_DIM):
        lo = h * HEAD_DIM
        q = norm_rope(t[lo:lo + HEAD_DIM], qg_ref[...]) * (HEAD_DIM ** -0.5)
        qt_ref[lo:lo + HEAD_DIM, :] = q.astype(BF16)
        n2 = sq_norm(q)
        qn_ref[h // ATTN_GROUP, h % ATTN_GROUP:h % ATTN_GROUP + 1, :] = jnp.sqrt(n2)
        q_stats.append(row_max(n2))
    tm = t.shape[1]
    k_pad = jnp.where(lax.broadcasted_iota(I32, (K_COLS - HEAD_DIM, tm), 0) < 2, 1.0, 0.0)
    v_pad = jnp.where(lax.broadcasted_iota(I32, (VT_ROWS - HEAD_DIM, tm), 0) < 1, 1.0, 0.0)
    k_stats = []
    for kv in range(KV_HEADS):
        lo = ATTN_WIDTH + kv * HEAD_DIM
        k = norm_rope(t[lo:lo + HEAD_DIM], kg_ref[...])
        k_stats.append(row_max(sq_norm(k)))
        kn_ref[kv] = jnp.concatenate([k, k_pad], axis=0).T.astype(BF16)
        lo = ATTN_WIDTH + KV_WIDTH + kv * HEAD_DIM
        vt = jnp.concatenate([t[lo:lo + HEAD_DIM], v_pad], axis=0)
        for j in range(tm // KEY_BLOCK):
            vt_ref[kv, j] = vt[:, j * KEY_BLOCK:(j + 1) * KEY_BLOCK]
    q_group = [functools.reduce(jnp.maximum, q_stats[kv * ATTN_GROUP:(kv + 1) * ATTN_GROUP])
               for kv in range(KV_HEADS)]
    stat_ref[...] = jnp.concatenate(k_stats + q_group + [jnp.zeros((F32_SUBLANES - 2 * KV_HEADS, LANES), F32)], axis=0)


def _in_projection(x, modr, gain, w_in_bf16, q_gain, k_gain, cos_t, sin_t, layer, ctx):
    B, T, _ = x.shape
    tm = min(TOKEN_TILE, T)
    nc = T // tm
    tok = lambda w: pl.BlockSpec((None, tm, w), lambda b, i: (b, i, 0))
    return pl.pallas_call(
        functools.partial(_inproj_kernel, rope=not ctx),
        out_shape=(jax.ShapeDtypeStruct((B, ATTN_WIDTH, T), BF16),
                   jax.ShapeDtypeStruct((B, KV_HEADS, T, K_COLS), BF16),
                   jax.ShapeDtypeStruct((B, KV_HEADS, T // KEY_BLOCK, VT_ROWS, KEY_BLOCK), F32),
                   jax.ShapeDtypeStruct((B, KV_HEADS, ATTN_GROUP, T), F32),
                   jax.ShapeDtypeStruct((B, nc, F32_SUBLANES, LANES), F32),
                   jax.ShapeDtypeStruct((B, T, RET_PROJ_WIDTH), BF16),
                   jax.ShapeDtypeStruct((B, T, POOL_WIDTH), BF16)),
        grid=(B, nc),
        in_specs=[tok(D_MODEL), _row_spec(D_MODEL), _mod_spec(layer, 1, ctx), _mod_spec(layer, 0, ctx),
                  pl.BlockSpec((D_MODEL, IN_WIDTH), lambda b, i: (0, 0)),
                  pl.BlockSpec((HEAD_DIM, 1), lambda b, i: (0, 0)),
                  pl.BlockSpec((HEAD_DIM, 1), lambda b, i: (0, 0)),
                  pl.BlockSpec((HEAD_DIM, tm), lambda b, i: (0, i)),
                  pl.BlockSpec((HEAD_DIM, tm), lambda b, i: (0, i))],
        out_specs=(pl.BlockSpec((None, ATTN_WIDTH, tm), lambda b, i: (b, 0, i)),
                   pl.BlockSpec((None, KV_HEADS, tm, K_COLS), lambda b, i: (b, 0, i, 0)),
                   pl.BlockSpec((None, KV_HEADS, tm // KEY_BLOCK, VT_ROWS, KEY_BLOCK), lambda b, i: (b, 0, i, 0, 0)),
                   pl.BlockSpec((None, KV_HEADS, ATTN_GROUP, tm), lambda b, i: (b, 0, 0, i)),
                   pl.BlockSpec((None, None, F32_SUBLANES, LANES), lambda b, i: (b, i, 0, 0)),
                   tok(RET_PROJ_WIDTH), tok(POOL_WIDTH)),
        compiler_params=_cparams(("parallel", "parallel"), VMEM_LIMIT),
        name="in_projection",
    )(x, gain.reshape(1, D_MODEL), modr, modr, w_in_bf16, q_gain.reshape(HEAD_DIM, 1), k_gain.reshape(HEAD_DIM, 1),
      cos_t, sin_t)


def _rope_tables(T):
    t = jnp.arange(T)
    row = (t // GRID_W).astype(F32)
    col = (t % GRID_W).astype(F32)
    n_freq = HEAD_DIM // 4
    inv = ROPE_THETA ** (-jnp.arange(n_freq, dtype=F32) / n_freq)
    ang_r = row[None, :] * inv[:, None]
    ang_c = col[None, :] * inv[:, None]
    cos_t = jnp.concatenate([jnp.cos(ang_r), jnp.cos(ang_r), jnp.cos(ang_c), jnp.cos(ang_c)], axis=0)
    sin_t = jnp.concatenate([-jnp.sin(ang_r), jnp.sin(ang_r), -jnp.sin(ang_c), jnp.sin(ang_c)], axis=0)
    return cos_t, sin_t


def _score_max_kernel(qt_ref, k_ref, m_ref, qa_ref):
    tq = qt_ref.shape[1]
    nq = ATTN_GROUP * tq
    for g in range(ATTN_GROUP):
        qa_ref[0:HEAD_DIM, g * tq:(g + 1) * tq] = qt_ref[g * HEAD_DIM:(g + 1) * HEAD_DIM, :]
    qa_ref[HEAD_DIM:K_COLS, :] = jnp.zeros((K_COLS - HEAD_DIM, nq), BF16)
    tk = min(ATTN_K_CHUNK, k_ref.shape[0])

    def colmax8(j):
        k = k_ref[pl.ds(pl.multiple_of(j * tk, tk), tk), :]
        s = jnp.dot(k, qa_ref[...], preferred_element_type=F32)
        return jnp.max(s.reshape(tk // 8, 8, nq), axis=0)

    mx = lax.fori_loop(1, k_ref.shape[0] // tk, lambda j, mx: jnp.maximum(mx, colmax8(j)), colmax8(0))
    m = jnp.max(mx, axis=0, keepdims=True)
    for g in range(ATTN_GROUP):
        m_ref[g:g + 1, :] = m[:, g * tq:(g + 1) * tq]


def _score_max(qt, k_all):
    B, _, Tq = qt.shape
    Tk = k_all.shape[2]
    tq = min(ATTN_Q_TILE, Tq)
    return pl.pallas_call(
        _score_max_kernel,
        out_shape=jax.ShapeDtypeStruct((B, KV_HEADS, ATTN_GROUP, Tq), F32),
        grid=(B, KV_HEADS, Tq // tq),
        in_specs=[pl.BlockSpec((None, ATTN_GROUP * HEAD_DIM, tq), lambda b, h, i: (b, h, i)),
                  pl.BlockSpec((None, None, Tk, K_COLS), lambda b, h, i: (b, h, 0, 0))],
        out_specs=pl.BlockSpec((None, None, ATTN_GROUP, tq), lambda b, h, i: (b, h, 0, i)),
        scratch_shapes=[pltpu.VMEM((K_COLS, ATTN_GROUP * tq), BF16)],
        compiler_params=_cparams(("parallel", "parallel", "parallel"), VMEM_LIMIT),
        name="score_max",
    )(qt, k_all)


def _softmax_shift(q_norm, q_stats, k_stats_list, qt, key_sets):
    B = qt.shape[0]
    k2 = functools.reduce(jnp.maximum, [st[:, :, 0:KV_HEADS, 0].max(axis=1) for st in k_stats_list])
    q2 = q_stats[:, :, KV_HEADS:2 * KV_HEADS, 0].max(axis=1)
    use_bound = (jnp.sqrt(k2 * q2) * SHIFT_MARGIN <= MAX_BOUND_SHIFT).reshape(B, KV_HEADS, 1, 1)
    bound = q_norm * (jnp.sqrt(k2) * SHIFT_MARGIN).reshape(B, KV_HEADS, 1, 1)
    exact = lambda: _score_max(qt, jnp.concatenate([k for k, _ in key_sets], axis=2))
    return lax.cond(jnp.all(use_bound), lambda: bound, lambda: jnp.where(use_bound, bound, exact()))


def _attn_kernel(qt_ref, shift_ref, *refs, n_key_sets, n_casts):
    kv_refs = refs[:2 * n_key_sets]
    cast_in = refs[2 * n_key_sets:2 * n_key_sets + n_casts]
    o_ref = refs[2 * n_key_sets + n_casts]
    cast_out = refs[2 * n_key_sets + n_casts + 1:2 * n_key_sets + 2 * n_casts + 1]
    qa_ref = refs[-1]
    for src_ref, dst_ref in zip(cast_in, cast_out):
        dst_ref[...] = src_ref[...].astype(BF16)
    sources = [(kv_refs[i], kv_refs[i + 1]) for i in range(0, len(kv_refs), 2)]
    counts = [k_ref.shape[0] // KEY_BLOCK for k_ref, _ in sources]
    tq = qt_ref.shape[1]
    n_blocks = sum(counts)
    heads_per_mxu = ATTN_GROUP // N_MXU

    def locate(blk):
        if not isinstance(blk, int):
            return 0, blk
        src = 0
        while blk >= counts[src]:
            blk -= counts[src]
            src += 1
        return src, blk

    row = lax.broadcasted_iota(I32, (BF16_SUBLANES, tq), 0)
    for g in range(ATTN_GROUP):
        m = shift_ref[g:g + 1, :]
        m_hi = m.astype(BF16).astype(F32)
        qa_ref[g, 0:HEAD_DIM, :] = qt_ref[g * HEAD_DIM:(g + 1) * HEAD_DIM, :]
        qa_ref[g, HEAD_DIM:HEAD_DIM + BF16_SUBLANES, :] = jnp.where(
            row == 0, -m_hi, jnp.where(row == 1, m_hi - m, 0.0)).astype(BF16)
        qa_ref[g, HEAD_DIM + BF16_SUBLANES:K_COLS, :] = jnp.zeros((K_COLS - HEAD_DIM - BF16_SUBLANES, tq), BF16)

    def score_addr(slot):
        return SCORE_BASE + slot * (KEY_BLOCK // 4)

    def out_addr(slot):
        return slot * (VT_ROWS // 4)

    def keys(blk):
        src, j = locate(blk)
        return sources[src][0][pl.ds(pl.multiple_of(j * KEY_BLOCK, KEY_BLOCK), KEY_BLOCK), :]

    def stage_q(slot):
        for mxu in range(N_MXU):
            pltpu.matmul_push_rhs(qa_ref[mxu * heads_per_mxu + slot], SCORE_REG, mxu)

    def issue_scores(blk, slot):
        k = keys(blk)
        for mxu in range(N_MXU):
            pltpu.matmul_acc_lhs(score_addr(slot), k, mxu, load_staged_rhs=SCORE_REG)

    def pop_probs(slot):
        return [jnp.exp(pltpu.matmul_pop(score_addr(slot), (KEY_BLOCK, tq), F32, mxu)).astype(BF16)
                for mxu in range(N_MXU)]

    def push_probs(blk, slot, p):
        src, j = locate(blk)
        vt = sources[src][1][j]
        for mxu in range(N_MXU):
            pltpu.matmul_push_rhs(p[mxu], VALUE_REG, mxu)
        return vt

    def unit(blk, slot, stage_next, first):
        if not first:
            p = pop_probs(slot)
        issue_scores(blk, slot)
        if not first:
            vt = push_probs(blk - 1, slot, p)
        if stage_next:
            stage_q((slot + 1) % heads_per_mxu)
        if not first:
            for mxu in range(N_MXU):
                pltpu.matmul_acc_lhs(out_addr(slot), vt, mxu, load_staged_rhs=VALUE_REG)

    def block(blk, last=False, first=False):
        for slot in range(heads_per_mxu):
            unit(blk, slot, stage_next=not (last and slot == heads_per_mxu - 1), first=first)

    stage_q(0)
    block(0, last=n_blocks == 1, first=True)
    if n_blocks > 1:
        n_loop = n_blocks - 2
        assert n_loop + 1 <= counts[0]
        unroll = max([u for u in range(1, ATTN_UNROLL + 1) if n_loop % u == 0] or [1])

        def body(i, carry):
            for u in range(unroll):
                block(1 + i * unroll + u)
            return carry

        if n_loop:
            lax.fori_loop(0, n_loop // unroll, body, 0)
        block(n_blocks - 1, last=True)
    for slot in range(heads_per_mxu):
        vt = push_probs(n_blocks - 1, slot, pop_probs(slot))
        for mxu in range(N_MXU):
            pltpu.matmul_acc_lhs(out_addr(slot), vt, mxu, load_staged_rhs=VALUE_REG)

    outs = []
    for mxu in range(N_MXU):
        for slot in range(heads_per_mxu):
            acc = pltpu.matmul_pop(out_addr(slot), (VT_ROWS, tq), F32, mxu)
            outs.append(acc[0:HEAD_DIM, :] / acc[HEAD_DIM:HEAD_DIM + 1, :])
    o_ref[...] = jnp.concatenate(outs, axis=0).astype(BF16)


def _attention(qt, shift, key_sets, casts=()):
    B, _, Tq = qt.shape
    tq = MXU_DIM
    nq = Tq // tq
    n_steps = B * KV_HEADS * nq
    in_specs = [pl.BlockSpec((None, ATTN_GROUP * HEAD_DIM, tq), lambda b, h, i: (b, h, i)),
                pl.BlockSpec((None, None, ATTN_GROUP, tq), lambda b, h, i: (b, h, 0, i))]
    args = [qt, shift]
    for k, vt in key_sets:
        Tk = k.shape[2]
        in_specs += [pl.BlockSpec((None, None, Tk, K_COLS), lambda b, h, i: (b, h, 0, 0)),
                     pl.BlockSpec((None, None, Tk // KEY_BLOCK, VT_ROWS, KEY_BLOCK), lambda b, h, i: (b, h, 0, 0, 0))]
        args += [k, vt]
    cast_specs = []
    for w in casts:
        rows, cols = w.shape
        rep = next(r for r in (1, 2, 4, 8, 16, 32) if n_steps % r == 0 and rows % (n_steps // r) == 0
                   and (rows // (n_steps // r)) % BF16_SUBLANES == 0)
        block = (rows // (n_steps // rep), cols)
        cast_specs.append(pl.BlockSpec(block, lambda b, h, i, rep=rep: (((b * KV_HEADS + h) * nq + i) // rep, 0)))
    outs = pl.pallas_call(
        functools.partial(_attn_kernel, n_key_sets=len(key_sets), n_casts=len(casts)),
        out_shape=[jax.ShapeDtypeStruct((B, ATTN_WIDTH, Tq), BF16)]
                  + [jax.ShapeDtypeStruct(w.shape, BF16) for w in casts],
        grid=(B, KV_HEADS, nq),
        in_specs=in_specs + cast_specs,
        out_specs=[pl.BlockSpec((None, ATTN_GROUP * HEAD_DIM, tq), lambda b, h, i: (b, h, i))] + cast_specs,
        scratch_shapes=[pltpu.VMEM((ATTN_GROUP, K_COLS, tq), BF16)],
        compiler_params=_cparams(("arbitrary", "arbitrary", "arbitrary"), VMEM_LIMIT),
        name="attention",
    )(*args, *casts)
    return outs[0], outs[1:]


def _ret_kernel(lgc_f_ref, lgc_b_ref, lgr_f_ref, lgr_b_ref, s0f_ref, s0b_ref, blk_f_ref, blk_b_ref,
                of_ref, ob_ref, sf_ref, sb_ref, st_f, st_b, dec):
    C = RET_CHUNK
    W = RET_WIDTH
    n = pl.program_id(1)

    @pl.when(n == 0)
    def _():
        st_f[...] = s0f_ref[...]
        st_b[...] = s0b_ref[...]
        c = jnp.bitwise_and(lax.broadcasted_iota(I32, (RET_HEADS * C, C), 0), C - 1)
        m = lax.broadcasted_iota(I32, (RET_HEADS * C, C), 1)
        diff = (c - m).astype(F32)
        dec[...] = (jnp.where(diff >= 0, jnp.exp(lgc_f_ref[...] * jnp.maximum(diff, 0.0)), 0.0)
                    + jnp.where(diff <= 0, jnp.exp(lgc_b_ref[...] * jnp.maximum(-diff, 0.0)), 0.0))

    lane_head = jnp.right_shift(lax.broadcasted_iota(I32, (C, W), 1), HEAD_SHIFT)
    pos = lax.broadcasted_iota(I32, (C, W), 0).astype(F32)
    same_head = (jnp.right_shift(lax.broadcasted_iota(I32, (W, W), 0), HEAD_SHIFT)
                 == jnp.right_shift(lax.broadcasted_iota(I32, (W, W), 1), HEAD_SHIFT))

    def direction(blk_ref, st_ref, lgr, forward, out_ref):
        n_sub = blk_ref.shape[0] // C
        state = st_ref[...]
        for sub in (range(n_sub) if forward else reversed(range(n_sub))):
            state = chunk(blk_ref, state, lgr, forward, out_ref, sub * C)
        st_ref[...] = state

    def chunk(blk_ref, state, lgr, forward, out_ref, r0):
        q = blk_ref[r0:r0 + C, 0:W].astype(F32)
        kf = blk_ref[r0:r0 + C, W:2 * W].astype(F32) * (HEAD_DIM ** -0.5)
        v = blk_ref[r0:r0 + C, 2 * W:3 * W]
        if forward:
            zeta = jnp.exp(lgr * (C - 1.0 - pos))
            xi = jnp.exp(lgr * (pos + 1.0))
        else:
            zeta = jnp.exp(lgr * pos)
            xi = jnp.exp(lgr * (C - pos))
        out = jnp.dot((q * xi).astype(BF16), state.astype(BF16), preferred_element_type=F32)
        if forward:
            qexp = jnp.concatenate([jnp.where(lane_head == h, q, 0.0) for h in range(RET_HEADS)],
                                   axis=0).astype(BF16)
            a = lax.dot_general(qexp, kf.astype(BF16), (((1,), (1,)), ((), ())),
                                preferred_element_type=F32)
            p = (a * dec[...]).astype(BF16)
            full = jnp.dot(p, v, preferred_element_type=F32)
            for h in range(RET_HEADS):
                out = out + jnp.where(lane_head == h, full[h * C:(h + 1) * C], 0.0)
        out_ref[r0:r0 + C, :] = out
        upd = lax.dot_general((kf * zeta).astype(BF16), v, (((0,), (0,)), ((), ())),
                              preferred_element_type=F32)
        return jnp.where(same_head, state * jnp.exp(lgr * float(C)) + upd, 0.0)

    direction(blk_f_ref, st_f, lgr_f_ref[...], True, of_ref)
    direction(blk_b_ref, st_b, lgr_b_ref[...], False, ob_ref)

    @pl.when(n == pl.num_programs(1) - 1)
    def _():
        sf_ref[...] = st_f[...]
        sb_ref[...] = st_b[...]


def _retention(ret, log_gamma, s0f, s0b):
    B, T, _ = ret.shape
    C = RET_CHUNK
    tm = min(RET_TILE, T)
    nc = T // tm
    lgc = lambda d: jnp.repeat(log_gamma[d], C).reshape(RET_HEADS * C, 1)
    lgr = lambda d: jnp.repeat(log_gamma[d], HEAD_DIM).reshape(1, RET_WIDTH)
    const = lambda shape: pl.BlockSpec(shape, lambda b, n: (0,) * len(shape))
    st_spec = pl.BlockSpec((None, RET_WIDTH, RET_WIDTH), lambda b, n: (b, 0, 0))
    return pl.pallas_call(
        _ret_kernel,
        out_shape=(jax.ShapeDtypeStruct((B, T, RET_WIDTH), F32),
                   jax.ShapeDtypeStruct((B, T, RET_WIDTH), F32),
                   jax.ShapeDtypeStruct((B, RET_WIDTH, RET_WIDTH), F32),
                   jax.ShapeDtypeStruct((B, RET_WIDTH, RET_WIDTH), F32)),
        grid=(B, nc),
        in_specs=[const((RET_HEADS * C, 1)), const((RET_HEADS * C, 1)),
                  const((1, RET_WIDTH)), const((1, RET_WIDTH)), st_spec, st_spec,
                  pl.BlockSpec((None, tm, RET_PROJ_WIDTH), lambda b, n: (b, n, 0)),
                  pl.BlockSpec((None, tm, RET_PROJ_WIDTH), lambda b, n: (b, nc - 1 - n, 0))],
        out_specs=(pl.BlockSpec((None, tm, RET_WIDTH), lambda b, n: (b, n, 0)),
                   pl.BlockSpec((None, tm, RET_WIDTH), lambda b, n: (b, nc - 1 - n, 0)),
                   st_spec, st_spec),
        scratch_shapes=[pltpu.VMEM((RET_WIDTH, RET_WIDTH), F32), pltpu.VMEM((RET_WIDTH, RET_WIDTH), F32),
                        pltpu.VMEM((RET_HEADS * C, C), F32)],
        compiler_params=_cparams(("parallel", "arbitrary")),
        name="retention",
    )(lgc(0), lgc(1), lgr(0), lgr(1), s0f, s0b, ret, ret)


def _init_pool_bands(band_ref):
    tok = lax.broadcasted_iota(I32, (POOL_TILE, POOL_TILE + 2 * POOL_HALO), 0)
    src = lax.broadcasted_iota(I32, (POOL_TILE, POOL_TILE + 2 * POOL_HALO), 1) - POOL_HALO
    for gi, w in enumerate(POOL_WINDOWS):
        inside = (src >= tok - w // 2) & (src < tok + w // 2)
        band_ref[gi] = jnp.where(inside, 1.0, 0.0).astype(BF16)


def _pool_tile(prev_ref, cur_ref, next_ref, w_ref, scale_ref, band_ref, seq_len):
    tm = cur_ref.shape[0]
    i = pl.program_id(1)
    prev = jnp.where(i > 0, prev_ref[...], jnp.zeros_like(prev_ref))
    nxt = jnp.where(i < pl.num_programs(1) - 1, next_ref[...], jnp.zeros_like(next_ref))
    ext = jnp.concatenate([prev, cur_ref[...], nxt], axis=0)
    sub = min(POOL_TILE, tm)
    lane_group = jnp.right_shift(lax.broadcasted_iota(I32, (sub, POOL_WIDTH), 1), HEAD_SHIFT)
    parts = []
    for r0 in range(0, tm, sub):
        window = ext[r0:r0 + sub + 2 * POOL_HALO]
        cur = cur_ref[r0:r0 + sub, :].astype(F32)
        tcol = i * tm + r0 + lax.broadcasted_iota(I32, (sub, 1), 0)
        mixed = jnp.zeros((sub, POOL_WIDTH), F32)
        for gi, w in enumerate(POOL_WINDOWS):
            total = jnp.dot(band_ref[gi], window, preferred_element_type=F32)
            cnt = (jnp.minimum(tcol + w // 2, seq_len) - jnp.maximum(tcol - w // 2, 0)).astype(F32)
            mixed = mixed + jnp.where(lane_group == gi, total / cnt - cur, 0.0)
        y = jnp.dot(mixed.astype(BF16), w_ref[...], preferred_element_type=F32)
        parts.append((y * scale_ref[...]).astype(BF16))
    return jnp.concatenate(parts, axis=0)


def _head_mean(x, avg):
    hi = x.astype(BF16)
    lo = (x - hi.astype(F32)).astype(BF16)
    return (jnp.dot(hi, avg, preferred_element_type=F32) + jnp.dot(lo, avg, preferred_element_type=F32))


def _mix_tile(x_ref, attn_ref, of_ref, ob_ref, gate_ref, pp_prev_ref, pp_ref, pp_next_ref, pool_w_ref,
              pool_scale_ref, avg_ref, w_ref, npost_ref, g1_ref, band_ref, seq_len):
    pool = _pool_tile(pp_prev_ref, pp_ref, pp_next_ref, pool_w_ref, pool_scale_ref, band_ref, seq_len)
    o = of_ref[...] + ob_ref[...]
    avg = avg_ref[...]
    mu = _head_mean(o, avg)
    cen = o - mu
    var = _head_mean(cen * cen, avg)
    y_ret = (_silu(gate_ref[...].astype(F32)) * (cen * lax.rsqrt(var + NORM_EPS))).astype(BF16)
    mx = (lax.dot_general(attn_ref[...], w_ref[0:ATTN_WIDTH, :], (((0,), (0,)), ((), ())),
                          preferred_element_type=F32)
          + jnp.dot(y_ret, w_ref[ATTN_WIDTH:ATTN_WIDTH + RET_WIDTH, :], preferred_element_type=F32)
          + jnp.dot(pool, w_ref[ATTN_WIDTH + RET_WIDTH:, :], preferred_element_type=F32))
    return x_ref[...] + g1_ref[...] * _rms(mx, npost_ref[...])


N_MIX_INPUTS = 14
POOL_BANDS = pltpu.VMEM((len(POOL_WINDOWS), POOL_TILE, POOL_TILE + 2 * POOL_HALO), BF16)


def _mix_inputs(mix, tm, layer, ctx):
    x, attn, of, ob, ret, pp, pool_w, pool_scale, avg_bf16, w_out_bf16, npost, modr = mix
    T = x.shape[1]
    r = tm // POOL_HALO
    last = T // POOL_HALO - 1
    tok = lambda w: pl.BlockSpec((None, tm, w), lambda b, i: (b, i, 0))
    specs = [tok(D_MODEL), pl.BlockSpec((None, ATTN_WIDTH, tm), lambda b, i: (b, 0, i)),
             tok(RET_WIDTH), tok(RET_WIDTH),
             pl.BlockSpec((None, tm, RET_WIDTH), lambda b, i: (b, i, 3)),
             pl.BlockSpec((None, POOL_HALO, POOL_WIDTH), lambda b, i: (b, jnp.maximum(i * r - 1, 0), 0)),
             tok(POOL_WIDTH),
             pl.BlockSpec((None, POOL_HALO, POOL_WIDTH), lambda b, i: (b, jnp.minimum((i + 1) * r, last), 0)),
             pl.BlockSpec((POOL_WIDTH, POOL_WIDTH), lambda b, i: (0, 0)), _row_spec(POOL_WIDTH),
             pl.BlockSpec((RET_WIDTH, RET_WIDTH), lambda b, i: (0, 0)),
             pl.BlockSpec((D_MODEL, D_MODEL), lambda b, i: (0, 0)),
             _row_spec(D_MODEL), _mod_spec(layer, 2, ctx)]
    args = [x, attn, of, ob, ret, pp, pp, pp, pool_w, pool_scale.reshape(1, POOL_WIDTH), avg_bf16, w_out_bf16,
            npost.reshape(1, D_MODEL), modr]
    return args, specs


def _swiglu_tile(h, w1_ref, w3_ref, w2_ref):
    acc = jnp.zeros((h.shape[0], D_MODEL), F32)
    for f in range(0, D_FF, FF_CHUNK):
        a = jnp.dot(h, w1_ref[:, f:f + FF_CHUNK], preferred_element_type=F32)
        b = jnp.dot(h, w3_ref[:, f:f + FF_CHUNK], preferred_element_type=F32)
        u = (_silu(a) * b).astype(BF16)
        acc = acc + jnp.dot(u, w2_ref[f:f + FF_CHUNK, :], preferred_element_type=F32)
    return acc


def _mix_ffn_kernel(*refs, seq_len):
    gpre_ref, sc_ref, sh_ref, w1_ref, w3_ref, w2_ref, npost_ref, g2_ref, o_ref, band_ref = refs[N_MIX_INPUTS:]

    @pl.when(pl.program_id(1) == 0)
    def _():
        _init_pool_bands(band_ref)

    x = _mix_tile(*refs[:N_MIX_INPUTS], band_ref, seq_len)
    h = (_rms(x, gpre_ref[...]) * (1.0 + sc_ref[...]) + sh_ref[...]).astype(BF16)
    y = _swiglu_tile(h, w1_ref, w3_ref, w2_ref)
    o_ref[...] = x + g2_ref[...] * _rms(y, npost_ref[...])


def _mix_ffn(mix, gpre, npost, w1, w3, w2, layer, ctx):
    x, modr = mix[0], mix[-1]
    B, T, _ = x.shape
    tm = min(TOKEN_TILE, T)
    args, specs = _mix_inputs(mix, tm, layer, ctx)
    wspec = lambda shape: pl.BlockSpec(shape, lambda b, i: (0, 0), pipeline_mode=pl.Buffered(1))
    return pl.pallas_call(
        functools.partial(_mix_ffn_kernel, seq_len=T),
        out_shape=jax.ShapeDtypeStruct((B, T, D_MODEL), F32),
        grid=(B, T // tm),
        in_specs=specs + [_row_spec(D_MODEL), _mod_spec(layer, 4, ctx), _mod_spec(layer, 3, ctx),
                          wspec((D_MODEL, D_FF)), wspec((D_MODEL, D_FF)), wspec((D_FF, D_MODEL)),
                          _row_spec(D_MODEL), _mod_spec(layer, 5, ctx)],
        out_specs=pl.BlockSpec((None, tm, D_MODEL), lambda b, i: (b, i, 0)),
        scratch_shapes=[POOL_BANDS],
        compiler_params=_cparams(("parallel", "arbitrary"), VMEM_LIMIT),
        name="mix_ffn",
    )(*args, gpre.reshape(1, D_MODEL), modr, modr, w1, w3, w2, npost.reshape(1, D_MODEL), modr)


def _mix_route_kernel(*refs, seq_len):
    (gpre_ref, sc_ref, sh_ref, rt_ref, x_ref, h_ref, idx_ref, gate_ref, rank_ref, cnt_ref,
     run_ref, upper_ref, band_ref) = refs[N_MIX_INPUTS:]
    tm = x_ref.shape[0]
    first = (pl.program_id(0) == 0) & (pl.program_id(1) == 0)

    @pl.when(first)
    def _():
        run_ref[...] = jnp.zeros_like(run_ref)
        earlier = lax.broadcasted_iota(I32, (tm, tm), 0) < lax.broadcasted_iota(I32, (tm, tm), 1)
        upper_ref[...] = jnp.where(earlier, 1.0, 0.0).astype(BF16)
        _init_pool_bands(band_ref)

    x = _mix_tile(*refs[:N_MIX_INPUTS], band_ref, seq_len)
    x_ref[...] = x
    h = _rms(x, gpre_ref[...]) * (1.0 + sc_ref[...]) + sh_ref[...]
    h_ref[...] = h
    def split(v):
        hi = v.astype(BF16)
        return hi, (v - hi.astype(F32)).astype(BF16)

    nt_dot = lambda a, b: lax.dot_general(a, b, (((1,), (1,)), ((), ())), preferred_element_type=F32)
    r_hi, r_lo = split(rt_ref[...])
    h_hi, h_lo = split(h)
    by_h_hi = nt_dot(jnp.concatenate([r_hi, r_lo], axis=0), h_hi)
    logits = by_h_hi[0:N_EXPERTS] + by_h_hi[N_EXPERTS:] + nt_dot(r_hi, h_lo)
    eid = lax.broadcasted_iota(I32, (N_EXPERTS, tm), 0).astype(F32)
    m1 = jnp.max(logits, axis=0, keepdims=True)
    i1 = jnp.min(jnp.where(logits == m1, eid, float(N_EXPERTS)), axis=0, keepdims=True)
    oh1 = eid == i1
    rest = jnp.where(oh1, -jnp.inf, logits)
    m2 = jnp.max(rest, axis=0, keepdims=True)
    i2 = jnp.min(jnp.where(rest == m2, eid, float(N_EXPERTS)), axis=0, keepdims=True)
    oh2 = eid == i2
    e2 = jnp.exp(m2 - m1)
    gate_ref[0:1, :] = 1.0 / (1.0 + e2)
    gate_ref[1:2, :] = e2 / (1.0 + e2)
    idx_ref[0:1, :] = i1.astype(I32)
    idx_ref[1:2, :] = i2.astype(I32)
    upper = upper_ref[...]
    f1 = jnp.where(oh1, 1.0, 0.0)
    f2 = jnp.where(oh2, 1.0, 0.0)
    before1 = jnp.dot(f1.astype(BF16), upper, preferred_element_type=F32)
    before2 = jnp.dot(f2.astype(BF16), upper, preferred_element_type=F32)
    cnt1 = jnp.sum(f1, axis=1, keepdims=True)
    cnt2 = jnp.sum(f2, axis=1, keepdims=True)
    run = run_ref[:, 0:1]
    rank_ref[0:1, :] = jnp.sum(f1 * (run + before1), axis=0, keepdims=True).astype(I32)
    rank_ref[1:2, :] = jnp.sum(f2 * (run + cnt1 + before2), axis=0, keepdims=True).astype(I32)
    run_new = run_ref[...] + cnt1 + cnt2
    run_ref[...] = run_new
    cnt_ref[...] = run_new


def _mix_route(mix, gpre, router_t, layer):
    x, modr = mix[0], mix[-1]
    B, T, _ = x.shape
    tm = min(TOKEN_TILE, T)
    nt = T // tm
    args, specs = _mix_inputs(mix, tm, layer, False)
    tok = pl.BlockSpec((None, tm, D_MODEL), lambda b, i: (b, i, 0))
    lane = pl.BlockSpec((2, tm), lambda b, i: (0, b * nt + i))
    return pl.pallas_call(
        functools.partial(_mix_route_kernel, seq_len=T),
        out_shape=(jax.ShapeDtypeStruct((B, T, D_MODEL), F32),
                   jax.ShapeDtypeStruct((B, T, D_MODEL), F32),
                   jax.ShapeDtypeStruct((2, B * T), I32),
                   jax.ShapeDtypeStruct((2, B * T), F32),
                   jax.ShapeDtypeStruct((2, B * T), I32),
                   jax.ShapeDtypeStruct((N_EXPERTS, LANES), F32)),
        grid=(B, nt),
        in_specs=specs + [_row_spec(D_MODEL), _mod_spec(layer, 4, False), _mod_spec(layer, 3, False),
                          pl.BlockSpec((N_EXPERTS, D_MODEL), lambda b, i: (0, 0))],
        out_specs=(tok, tok, lane, lane, lane,
                   pl.BlockSpec((N_EXPERTS, LANES), lambda b, i: (0, 0))),
        scratch_shapes=[pltpu.VMEM((N_EXPERTS, LANES), F32), pltpu.VMEM((tm, tm), BF16), POOL_BANDS],
        compiler_params=_cparams(("arbitrary", "arbitrary"), VMEM_LIMIT),
        name="mix_route",
    )(*args, gpre.reshape(1, D_MODEL), modr, modr, router_t)


def _dispatch_kernel(zero_tiles_ref, slot_ref, h_ref, hs_ref, zero_ref, sem, zero_sem):
    tm = h_ref.shape[0]

    @pl.when(pl.program_id(0) == 0)
    def _():
        zero_ref[...] = jnp.zeros_like(zero_ref)
        for j in range(2 * N_EXPERTS):
            start = pl.multiple_of(zero_tiles_ref[j] * EXPERT_TILE, EXPERT_TILE)
            fill = pltpu.make_async_copy(zero_ref, hs_ref.at[pl.ds(start, EXPERT_TILE)], zero_sem)
            fill.start()
            fill.wait()

    def body(i, carry):
        for u in range(DMA_UNROLL):
            r = pl.multiple_of(i * DMA_UNROLL, DMA_UNROLL) + u
            for k in range(2):
                s = slot_ref[2 * r + k]
                pltpu.make_async_copy(h_ref.at[pl.ds(r, 1)], hs_ref.at[pl.ds(s, 1)], sem).start(priority=k)
        return carry

    lax.fori_loop(0, tm // DMA_UNROLL, body, 0)
    for _ in range(2):
        pltpu.make_async_copy(h_ref, hs_ref.at[pl.ds(0, tm)], sem).wait()


def _dispatch(h, slot_flat, zero_tiles, n_slots):
    N = h.shape[0]
    tm = min(TOKEN_TILE, N)
    grid_spec = pltpu.PrefetchScalarGridSpec(
        num_scalar_prefetch=1,
        grid=(N // tm,),
        in_specs=[pl.BlockSpec((2 * tm,), lambda i, zt: (i,), memory_space=pltpu.SMEM),
                  pl.BlockSpec((tm, D_MODEL), lambda i, zt: (i, 0))],
        out_specs=pl.BlockSpec(memory_space=pl.ANY),
        scratch_shapes=[pltpu.VMEM((EXPERT_TILE, D_MODEL), F32),
                        pltpu.SemaphoreType.DMA(()), pltpu.SemaphoreType.DMA(())],
    )
    return pl.pallas_call(
        _dispatch_kernel,
        out_shape=jax.ShapeDtypeStruct((n_slots, D_MODEL), F32),
        grid_spec=grid_spec,
        compiler_params=_cparams(("arbitrary",)),
        name="moe_dispatch",
    )(zero_tiles, slot_flat, h)


def _expert_kernel(te_ref, tv_ref, h_ref, w1_ref, w3_ref, w2_ref, y_ref):
    t = pl.program_id(0)

    @pl.when(tv_ref[t] == 1)
    def _():
        y_ref[...] = _swiglu_tile(h_ref[...].astype(BF16), w1_ref, w3_ref, w2_ref)

    @pl.when(tv_ref[t] == 0)
    def _():
        y_ref[...] = jnp.zeros_like(y_ref)


def _experts(hs, tile_expert, tile_valid, w1, w3, w2):
    n_slots = hs.shape[0]
    tm = EXPERT_TILE
    grid_spec = pltpu.PrefetchScalarGridSpec(
        num_scalar_prefetch=2,
        grid=(n_slots // tm,),
        in_specs=[pl.BlockSpec((tm, D_MODEL), lambda t, te, tv: (t, 0)),
                  pl.BlockSpec((None, D_MODEL, D_FF), lambda t, te, tv: (te[t], 0, 0)),
                  pl.BlockSpec((None, D_MODEL, D_FF), lambda t, te, tv: (te[t], 0, 0)),
                  pl.BlockSpec((None, D_FF, D_MODEL), lambda t, te, tv: (te[t], 0, 0))],
        out_specs=pl.BlockSpec((tm, D_MODEL), lambda t, te, tv: (t, 0)),
    )
    return pl.pallas_call(
        _expert_kernel,
        out_shape=jax.ShapeDtypeStruct((n_slots, D_MODEL), F32),
        grid_spec=grid_spec,
        compiler_params=_cparams(("arbitrary",), VMEM_LIMIT),
        name="moe_experts",
    )(tile_expert, tile_valid, hs, w1, w3, w2)


def _combine_kernel(slot_ref, gate_ref, x_ref, npost_ref, g2_ref, ys_ref, o_ref, buf0, buf1, sem):
    tm = x_ref.shape[0]
    bufs = (buf0, buf1)

    def body(i, carry):
        for u in range(DMA_UNROLL):
            r = pl.multiple_of(i * DMA_UNROLL, DMA_UNROLL) + u
            for k in range(2):
                s = slot_ref[2 * r + k]
                pltpu.make_async_copy(ys_ref.at[pl.ds(s, 1)], bufs[k].at[pl.ds(r, 1)], sem).start(priority=k)
        return carry

    lax.fori_loop(0, tm // DMA_UNROLL, body, 0)
    for k in range(2):
        pltpu.make_async_copy(ys_ref.at[pl.ds(0, tm)], bufs[k], sem).wait()
    y = gate_ref[:, 0:1] * buf0[...] + gate_ref[:, 1:2] * buf1[...]
    o_ref[...] = x_ref[...] + g2_ref[...] * _rms(y, npost_ref[...])


def _combine(x, ys, slot_flat, gate_tok, npost, modr, layer):
    B, T, _ = x.shape
    tm = min(TOKEN_TILE, T)
    nt = T // tm
    tok = pl.BlockSpec((None, tm, D_MODEL), lambda b, i: (b, i, 0))
    return pl.pallas_call(
        _combine_kernel,
        out_shape=jax.ShapeDtypeStruct((B, T, D_MODEL), F32),
        grid=(B, nt),
        in_specs=[pl.BlockSpec((2 * tm,), lambda b, i: (b * nt + i,), memory_space=pltpu.SMEM),
                  pl.BlockSpec((tm, 2), lambda b, i: (b * nt + i, 0)),
                  tok, _row_spec(D_MODEL), _mod_spec(layer, 5, False),
                  pl.BlockSpec(memory_space=pl.ANY)],
        out_specs=tok,
        scratch_shapes=[pltpu.VMEM((tm, D_MODEL), F32), pltpu.VMEM((tm, D_MODEL), F32),
                        pltpu.SemaphoreType.DMA(())],
        compiler_params=_cparams(("arbitrary", "arbitrary")),
        name="moe_combine",
    )(slot_flat, gate_tok, x, npost.reshape(1, D_MODEL), modr, ys)


def _mix_moe_ffn(mix, gpre, npost, router, w1, w3, w2, layer):
    modr = mix[-1]
    B, T, _ = mix[0].shape
    N = B * T
    x, h, idx, gate, rank, cnt = _mix_route(mix, gpre, router.T, layer)
    n_slots = 2 * N + N_EXPERTS * EXPERT_TILE
    n_tiles = n_slots // EXPERT_TILE
    counts = cnt[:, 0].astype(I32)
    padded = ((counts + EXPERT_TILE - 1) // EXPERT_TILE) * EXPERT_TILE
    ends = jnp.cumsum(padded)
    starts = ends - padded
    slot = rank
    for e in range(N_EXPERTS):
        slot = slot + jnp.where(idx == e, starts[e], 0)
    slot_flat = slot.T.reshape(2 * N)
    tile_start = jnp.arange(n_tiles, dtype=I32) * EXPERT_TILE
    tile_valid = (tile_start < ends[-1]).astype(I32)
    tile_expert = jnp.minimum(jnp.sum((tile_start[:, None] >= ends[None, :]).astype(I32), axis=1), N_EXPERTS - 1)
    last_expert = jnp.max(jnp.where(tile_valid == 1, tile_expert, 0))
    tile_expert = jnp.where(tile_valid == 1, tile_expert, last_expert)
    last_tile = jnp.where(padded > 0, ends // EXPERT_TILE - 1, n_tiles - 1)
    tail_tile = jnp.minimum(ends[-1] // EXPERT_TILE + jnp.arange(N_EXPERTS, dtype=I32), n_tiles - 1)
    zero_tiles = jnp.concatenate([last_tile, tail_tile]).astype(I32)
    hs = _dispatch(h.reshape(N, D_MODEL), slot_flat, zero_tiles, n_slots)
    ys = _experts(hs, tile_expert, tile_valid, w1, w3, w2)
    return _combine(x, ys, slot_flat, gate.T, npost, modr, layer)


def kernel(x, c, ctx, c_ctx, w_mod, b_mod, norm_pre_mix, norm_post_mix, norm_pre_ffn, norm_post_ffn, w_in, w_out, q_norm, k_norm, ret_decay_logit, pool_w, pool_scale, ffn_w1, ffn_w3, ffn_w2, moe_router, moe_w1, moe_w3, moe_w2):
    B, T, _ = x.shape
    assert B == CTX_MOD_ROW and x.shape[2] == D_MODEL
    cvec = jnp.zeros((MOD_ROWS, D_MODEL), F32).at[0:B].set(c).at[CTX_MOD_ROW].set(c_ctx)
    modr = _modulation(cvec, w_mod, b_mod)
    cos_t, sin_t = _rope_tables(T)
    avg = jnp.kron(jnp.eye(RET_HEADS, dtype=F32), jnp.full((HEAD_DIM, HEAD_DIM), 1.0 / HEAD_DIM, F32)).astype(BF16)
    zero_state = jnp.zeros((B, RET_WIDTH, RET_WIDTH), F32)
    bf16_of = {}

    def as_bf16(key, w):
        return bf16_of[key] if key in bf16_of else w.astype(BF16)

    xc = ctx
    for i in range(DEPTH):
        need_ctx = i < DEPTH - 1
        j = i // 2
        w_in_b = as_bf16(("w_in", i), w_in[i])
        pool_bd = jax.scipy.linalg.block_diag(*[pool_w[i, g] for g in range(len(POOL_WINDOWS))]).astype(BF16)
        log_gamma = jax.nn.log_sigmoid(ret_decay_logit[i].astype(F32))

        qt_c, kn_c, vt_c, qn_c, st_c, ret_c, pp_c = _in_projection(
            xc, modr, norm_pre_mix[i], w_in_b, q_norm[i], k_norm[i], cos_t, sin_t, i, True)
        of_c, ob_c, s_fwd, s_bwd = _retention(ret_c, log_gamma, zero_state, zero_state)

        qt_x, kn_x, vt_x, qn_x, st_x, ret_x, pp_x = _in_projection(
            x, modr, norm_pre_mix[i], w_in_b, q_norm[i], k_norm[i], cos_t, sin_t, i, False)
        keys_x = [(kn_x, vt_x), (kn_c, vt_c)]
        todo = [(("w_out", i), w_out[i])]
        if i % 2 == 0:
            todo += [(("ffn_w1", j), ffn_w1[j]), (("ffn_w3", j), ffn_w3[j]), (("ffn_w2", j), ffn_w2[j])]
        if i + 1 < DEPTH:
            todo.append((("w_in", i + 1), w_in[i + 1]))
            if (i + 1) % 2 == 1:
                jn = (i + 1) // 2
                todo += [(("moe_w1", jn), moe_w1[jn]), (("moe_w3", jn), moe_w3[jn]), (("moe_w2", jn), moe_w2[jn])]
        attn_x, done = _attention(qt_x, _softmax_shift(qn_x, st_x, [st_x, st_c], qt_x, keys_x), keys_x,
                                  [w.reshape(-1, w.shape[-1]) for _, w in todo])
        for (key, w), w_bf16 in zip(todo, done):
            bf16_of[key] = w_bf16.reshape(w.shape)
        w_out_b = bf16_of[("w_out", i)]
        of_x, ob_x, _, _ = _retention(ret_x, log_gamma, s_fwd, s_bwd)
        mix_x = (x, attn_x, of_x, ob_x, ret_x, pp_x, pool_bd, pool_scale[i], avg, w_out_b, norm_post_mix[i], modr)

        if i % 2 == 0:
            w1, w3, w2 = [as_bf16((n, j), w[j]) for n, w in (("ffn_w1", ffn_w1), ("ffn_w3", ffn_w3), ("ffn_w2", ffn_w2))]
            x = _mix_ffn(mix_x, norm_pre_ffn[i], norm_post_ffn[i], w1, w3, w2, i, False)
            if need_ctx:
                keys_c = [(kn_c, vt_c)]
                attn_c, _ = _attention(qt_c, _softmax_shift(qn_c, st_c, [st_c], qt_c, keys_c), keys_c)
                mix_c = (xc, attn_c, of_c, ob_c, ret_c, pp_c, pool_bd, pool_scale[i], avg, w_out_b,
                         norm_post_mix[i], modr)
                xc = _mix_ffn(mix_c, norm_pre_ffn[i], norm_post_ffn[i], w1, w3, w2, i, True)
        else:
            assert not need_ctx
            w1, w3, w2 = [as_bf16((n, j), w[j]) for n, w in (("moe_w1", moe_w1), ("moe_w3", moe_w3), ("moe_w2", moe_w2))]
            x = _mix_moe_ffn(mix_x, norm_pre_ffn[i], norm_post_ffn[i], moe_router[j], w1, w3, w2, i)
    return x
```
